```python
import math
import jax, jax.numpy as jnp
from jax import lax
import numpy as np

D_MODEL = 1024
BATCH = 8
SEQ = 2048
DEPTH = 2

CTX_LEN = 256
GRID_W = 64
MIX_WIDTH = D_MODEL

GLA_HEADS = 4
GLA_DV = MIX_WIDTH // 2 // GLA_HEADS
GLA_DK = GLA_DV // 2
GLA_RANK = 16
GLA_TAU = 16.0
GLA_CHUNK = 64
GLA_K_W = GLA_HEADS * GLA_DK
GLA_V_W = GLA_HEADS * GLA_DV

FOURIER_GROUPS = 4
FOURIER_WIDTH = MIX_WIDTH // 4

CONV_GROUPS = 4
CONV_WIDTH = MIX_WIDTH - GLA_V_W - FOURIER_WIDTH
CONV_K = 3

FFN_HIDDEN = -(-8 * D_MODEL // (3 * 256)) * 256
RMS_EPS = 1e-6
N_MOD = 6

STATE_COLS = GLA_K_W + GLA_V_W + 2 * GLA_RANK
IN_COLS = STATE_COLS + GLA_K_W + GLA_V_W + FOURIER_WIDTH + 3 * CONV_WIDTH
STATE_SPLITS = [GLA_K_W, GLA_K_W + GLA_V_W, GLA_K_W + GLA_V_W + GLA_RANK]
MAIN_SPLITS = [
    STATE_COLS,
    STATE_COLS + GLA_K_W,
    STATE_COLS + GLA_K_W + GLA_V_W,
    STATE_COLS + GLA_K_W + GLA_V_W + FOURIER_WIDTH,
    STATE_COLS + GLA_K_W + GLA_V_W + FOURIER_WIDTH + CONV_WIDTH,
    STATE_COLS + GLA_K_W + GLA_V_W + FOURIER_WIDTH + 2 * CONV_WIDTH,
]

kernel_name = "hybrid_gla_fourier_conv_dit_block"


def rmsnorm(x, g):
    xf = x.astype(jnp.float32)
    y = xf * lax.rsqrt(jnp.mean(xf * xf, axis=-1, keepdims=True) + RMS_EPS)
    return (y * g.astype(jnp.float32)).astype(x.dtype)


def modulate(h, shift, scale):
    return h * (1 + scale) + shift


def flip(t):
    return jnp.flip(t, axis=1)


def grid_pos_embed(n_tokens, dim, dtype):
    rows = n_tokens // GRID_W
    row = jnp.repeat(jnp.arange(rows, dtype=jnp.float32), GRID_W)
    col = jnp.tile(jnp.arange(GRID_W, dtype=jnp.float32), rows)
    quarter = dim // 4
    freqs = 1.0 / (10000.0 ** (jnp.arange(quarter, dtype=jnp.float32) / quarter))

    def enc(p):
        ang = p[:, None] * freqs[None, :]
        return jnp.concatenate([jnp.sin(ang), jnp.cos(ang)], axis=-1)

    return jnp.concatenate([enc(row), enc(col)], axis=-1).astype(dtype)


def gla_chunked(q, k, v, logg, s0, with_output=True):
    f32 = jnp.float32
    b, l, h, dk = k.shape
    dv = v.shape[-1]
    n = l // GLA_CHUNK
    k = k.astype(f32).reshape(b, n, GLA_CHUNK, h, dk)
    v = v.astype(f32).reshape(b, n, GLA_CHUNK, h, dv)
    g = jnp.cumsum(logg.astype(f32).reshape(b, n, GLA_CHUNK, h, dk), axis=2)
    g_last = g[:, :, -1]
    k_upd = k * jnp.exp(g_last[:, :, None] - g)
    ds = jnp.einsum('bnjhd,bnjhv->nbhdv', k_upd, v)
    decay = jnp.moveaxis(jnp.exp(g_last), 1, 0)

    def step(s, inp):
        dec, d = inp
        return dec[..., None] * s + d, s

    s_final, s_start = lax.scan(step, s0.astype(f32), (decay, ds))
    if not with_output:
        return None, s_final
    q = q.astype(f32).reshape(b, n, GLA_CHUNK, h, dk)
    g_mid = g[:, :, GLA_CHUNK // 2 - 1][:, :, None]
    scores = jnp.einsum('bnihd,bnjhd->bnhij', q * jnp.exp(g - g_mid), k * jnp.exp(g_mid - g))
    mask = jnp.tril(jnp.ones((GLA_CHUNK, GLA_CHUNK), dtype=bool))
    scores = jnp.where(mask, scores, 0.0)
    o_intra = jnp.einsum('bnhij,bnjhv->bnihv', scores, v)
    o_inter = jnp.einsum('bnihd,nbhdv->bnihv', q * jnp.exp(g), s_start)
    return (o_intra + o_inter).reshape(b, l, h, dv), s_final


def gla_state_inputs(p_state, w_dec_f, b_dec_f, w_dec_b, b_dec_b):
    k, v, lr_f, lr_b = jnp.split(p_state, STATE_SPLITS, axis=-1)
    b, l = k.shape[:2]
    k = k.reshape(b, l, GLA_HEADS, GLA_DK)
    v = v.reshape(b, l, GLA_HEADS, GLA_DV)
    logg_f = jax.nn.log_sigmoid((lr_f @ w_dec_f + b_dec_f).astype(jnp.float32)) / GLA_TAU
    logg_b = jax.nn.log_sigmoid((lr_b @ w_dec_b + b_dec_b).astype(jnp.float32)) / GLA_TAU
    return (k, v, logg_f.reshape(b, l, GLA_HEADS, GLA_DK), logg_b.reshape(b, l, GLA_HEADS, GLA_DK))


def zero_state(b):
    return jnp.zeros((b, GLA_HEADS, GLA_DK, GLA_DV), jnp.float32)


def fourier_mix(u):
    b, l, w = u.shape
    uf = u.astype(jnp.float32).reshape(b, l, FOURIER_GROUPS, w // FOURIER_GROUPS)
    y = jnp.fft.fft2(uf, axes=(1, 3), norm="ortho").real
    return y.reshape(b, l, w).astype(u.dtype)


def short_conv(b_gate, c_gate, u, conv_w, conv_b):
    z = c_gate * u
    zp = jnp.pad(z, ((0, 0), (1, 1), (0, 0)))
    y = conv_w[0] * zp[:, :-2] + conv_w[1] * zp[:, 1:-1] + conv_w[2] * zp[:, 2:] + conv_b
    return b_gate * y


def context_states(h, w_in, w_dec_f, b_dec_f, w_dec_b, b_dec_b):
    p_state = h @ w_in[:, :STATE_COLS]
    k, v, gf, gb = gla_state_inputs(p_state, w_dec_f, b_dec_f, w_dec_b, b_dec_b)
    s0 = zero_state(h.shape[0])
    _, s_f = gla_chunked(None, k, v, gf, s0, with_output=False)
    _, s_b = gla_chunked(None, flip(k), flip(v), flip(gb), s0, with_output=False)
    return s_f, s_b


def token_mixer(h, w_in, w_dec_f, b_dec_f, w_dec_b, b_dec_b, gla_norm_w, conv_w, conv_b, w_out, s_f0, s_b0):
    b, l, _ = h.shape
    p = h @ w_in
    p_state, q, r, u_four, b_gate, c_gate, u_conv = jnp.split(p, MAIN_SPLITS, axis=-1)
    k, v, gf, gb = gla_state_inputs(p_state, w_dec_f, b_dec_f, w_dec_b, b_dec_b)
    q = q.reshape(b, l, GLA_HEADS, GLA_DK) * (GLA_DK ** -0.5)
    o_f, s_f = gla_chunked(q, k, v, gf, s_f0)
    o_b, s_b = gla_chunked(flip(q), flip(k), flip(v), flip(gb), s_b0)
    o = o_f + flip(o_b)
    o = o * lax.rsqrt(jnp.mean(o * o, axis=-1, keepdims=True) + RMS_EPS)
    o = o * gla_norm_w.astype(jnp.float32).reshape(GLA_HEADS, GLA_DV)
    y_gla = o.reshape(b, l, GLA_V_W).astype(h.dtype) * jax.nn.silu(r)
    y_four = fourier_mix(u_four)
    y_conv = short_conv(b_gate, c_gate, u_conv, conv_w, conv_b)
    y = jnp.concatenate([y_gla, y_four, y_conv], axis=-1) @ w_out
    return y, s_f, s_b


def ffn_sublayer(x, shift, scale, gate, g_pre, g_post, w_ffn_in, w_ffn_out):
    h = modulate(rmsnorm(x, g_pre), shift, scale)
    a, u = jnp.split(h @ w_ffn_in, 2, axis=-1)
    y = (jax.nn.silu(a) * u) @ w_ffn_out
    return x + gate * rmsnorm(y, g_post)


def setup_inputs(seed: int = 0) -> dict:
    key = jax.random.key(seed)
    ks = jax.random.split(key, 24)
    f32 = jnp.float32
    D = D_MODEL

    def nrm(k, shape, scale):
        return jax.random.normal(k, shape, f32) * scale

    def gain(k, shape):
        return 1.0 + 0.05 * jax.random.normal(k, shape, f32)

    return {
        "x": nrm(ks[0], (BATCH, SEQ, D), 1.0),
        "c": nrm(ks[1], (BATCH, D), 1.0),
        "ctx": nrm(ks[2], (BATCH, CTX_LEN, D), 1.0),
        "c_ctx": nrm(ks[3], (D,), 1.0),
        "w_mod": nrm(ks[4], (DEPTH, D, N_MOD * D), 0.5 * D ** -0.5),
        "b_mod": nrm(ks[5], (DEPTH, N_MOD * D), 0.02),
        "g_mix_pre": gain(ks[6], (DEPTH, D)),
        "g_mix_post": gain(ks[7], (DEPTH, D)),
        "g_ffn_pre": gain(ks[8], (DEPTH, D)),
        "g_ffn_post": gain(ks[9], (DEPTH, D)),
        "w_in": nrm(ks[10], (DEPTH, D, IN_COLS), D ** -0.5),
        "w_dec_f": nrm(ks[11], (DEPTH, GLA_RANK, GLA_K_W), GLA_RANK ** -0.5),
        "b_dec_f": nrm(ks[12], (DEPTH, GLA_K_W), 0.1),
        "w_dec_b": nrm(ks[13], (DEPTH, GLA_RANK, GLA_K_W), GLA_RANK ** -0.5),
        "b_dec_b": nrm(ks[14], (DEPTH, GLA_K_W), 0.1),
        "gla_norm_w": gain(ks[15], (DEPTH, GLA_V_W)),
        "conv_w": nrm(ks[16], (DEPTH, CONV_K, CONV_WIDTH), CONV_K ** -0.5),
        "conv_b": nrm(ks[17], (DEPTH, CONV_WIDTH), 0.02),
        "w_out": nrm(ks[18], (DEPTH, MIX_WIDTH, D), MIX_WIDTH ** -0.5),
        "w_ffn_in": nrm(ks[19], (DEPTH, D, 2 * FFN_HIDDEN), D ** -0.5),
        "w_ffn_out": nrm(ks[20], (DEPTH, FFN_HIDDEN, D), FFN_HIDDEN ** -0.5),
    }


def reference(x, c, ctx, c_ctx, w_mod, b_mod, g_mix_pre, g_mix_post, g_ffn_pre, g_ffn_post,
              w_in, w_dec_f, b_dec_f, w_dec_b, b_dec_b, gla_norm_w, conv_w, conv_b, w_out,
              w_ffn_in, w_ffn_out):
    n_lat = x.shape[1]
    x = x + grid_pos_embed(n_lat, x.shape[-1], x.dtype)[None]
    xc = ctx
    for i in range(DEPTH):
        last = i == DEPTH - 1
        mod_lat = jnp.split((jax.nn.silu(c) @ w_mod[i] + b_mod[i])[:, None, :], N_MOD, axis=-1)
        mod_ctx = jnp.split((jax.nn.silu(c_ctx) @ w_mod[i] + b_mod[i])[None, None, :], N_MOD, axis=-1)
        mix_w = (w_in[i], w_dec_f[i], b_dec_f[i], w_dec_b[i], b_dec_b[i], gla_norm_w[i], conv_w[i], conv_b[i], w_out[i])

        h_lat = modulate(rmsnorm(x, g_mix_pre[i]), mod_lat[0], mod_lat[1])
        h_ctx = modulate(rmsnorm(xc, g_mix_pre[i]), mod_ctx[0], mod_ctx[1])
        if last:
            s_f, s_b = context_states(h_ctx, w_in[i], w_dec_f[i], b_dec_f[i], w_dec_b[i], b_dec_b[i])
        else:
            s0 = zero_state(xc.shape[0])
            y_ctx, s_f, s_b = token_mixer(h_ctx, *mix_w, s0, s0)
            xc = xc + mod_ctx[2] * rmsnorm(y_ctx, g_mix_post[i])
            xc = ffn_sublayer(xc, mod_ctx[3], mod_ctx[4], mod_ctx[5], g_ffn_pre[i], g_ffn_post[i], w_ffn_in[i], w_ffn_out[i])

        y_lat, _, _ = token_mixer(h_lat, *mix_w, s_f, s_b)
        x = x + mod_lat[2] * rmsnorm(y_lat, g_mix_post[i])
        x = ffn_sublayer(x, mod_lat[3], mod_lat[4], mod_lat[5], g_ffn_pre[i], g_ffn_post[i], w_ffn_in[i], w_ffn_out[i])
    return x
```

```python
import functools

import numpy as np
import jax
import jax.numpy as jnp
from jax import lax
from jax.experimental import pallas as pl
from jax.experimental.pallas import tpu as pltpu

F32 = jnp.float32
BF16 = jnp.bfloat16

D_MODEL = 1024
DEPTH = 2
GRID_W = 64
N_MOD = 6
HEADS = 4
DK = 64
DV = 128
K_W = HEADS * DK
V_W = HEADS * DV
RANK = 16
TAU = 16.0
CHUNK = 64
FOUR_W = 256
FOUR_G = 4
CONV_W = 256
HIDDEN = 2816
EPS = 1e-6
LR_PAD = 128
MOD_ROWS = 16

_C_K, _C_V, _C_LR, _C_Q, _C_R, _C_FOUR, _C_BG, _C_CG, _C_UC, _C_END = (
    0, 256, 768, 800, 1056, 1568, 1824, 2080, 2336, 2592)
_MAIN_W = 2560
_IN_W = _MAIN_W + LR_PAD
_OUT_GROUPS_FULL = (("k", 0, 256), ("v", 256, 768), ("q", 768, 1024), ("r", 1024, 1536),
                    ("uf", 1536, 1792), ("bg", 1792, 2048), ("cg", 2048, 2304),
                    ("uc", 2304, 2560), ("lr", 2560, 2688))
_OUT_GROUPS_STATE = (("k", 0, 256), ("v", 256, 768), ("lr", 768, 896))

V7X_VMEM_BYTES = 64 * 1024 * 1024


def _params(semantics, vmem_mb):
    assert vmem_mb * 1024 * 1024 < V7X_VMEM_BYTES
    return pltpu.CompilerParams(dimension_semantics=semantics,
                                vmem_limit_bytes=vmem_mb * 1024 * 1024)


def _resident(shape):
    zeros = (0,) * len(shape)
    return pl.BlockSpec(shape, lambda *_: zeros, pipeline_mode=pl.Buffered(1))


def _silu(a):
    return a / (1.0 + jnp.exp(-a))


def _rms(x, g):
    return x * lax.rsqrt(jnp.mean(x * x, axis=-1, keepdims=True) + EPS) * g


def _dot(a, b):
    return jnp.dot(a, b, preferred_element_type=F32)


def _dot_nt(a, b):
    return lax.dot_general(a, b, (((1,), (1,)), ((), ())), preferred_element_type=F32)


def _dot_tn(a, b):
    return lax.dot_general(a, b, (((0,), (0,)), ((), ())), preferred_element_type=F32)


def _pos_embed(n_tokens, dim):
    rows = n_tokens // GRID_W
    row = np.repeat(np.arange(rows, dtype=np.float32), GRID_W)
    col = np.tile(np.arange(GRID_W, dtype=np.float32), rows)
    quarter = dim // 4
    freqs = (1.0 / (10000.0 ** (np.arange(quarter, dtype=np.float32) / quarter))).astype(np.float32)

    def enc(p):
        ang = (p[:, None] * freqs[None, :]).astype(np.float32)
        return np.concatenate([np.sin(ang), np.cos(ang)], axis=-1)

    return np.concatenate([enc(row), enc(col)], axis=-1).astype(np.float32)


def _dft_cos_sin(n):
    idx = np.arange(n, dtype=np.int64)
    ang = 2.0 * np.pi * ((idx[:, None] * idx[None, :]) % n).astype(np.float64) / n
    return np.cos(ang) / np.sqrt(n), np.sin(ang) / np.sqrt(n)


def _dft_tables(seq):
    cl, sl = _dft_cos_sin(seq)
    pos_tab = np.concatenate([cl, -sl], axis=1)
    gw = FOUR_W // FOUR_G
    cc, sc = _dft_cos_sin(gw)
    eye = np.eye(FOUR_G)
    chan_tab = np.concatenate([np.kron(eye, cc), np.kron(eye, sc)], axis=1)
    return (jnp.asarray(pos_tab, dtype=F32).astype(BF16), jnp.asarray(chan_tab, dtype=F32).astype(BF16))


def _mod_kernel(c_ref, w_ref, b_ref, o_ref):
    s = _silu(c_ref[...]).astype(BF16)
    o_ref[0] = _dot(s, w_ref[0].astype(BF16)) + b_ref[0]


def _modulation(cvec, w_mod, b_mod):
    tn = 1536
    n = N_MOD * D_MODEL
    return pl.pallas_call(
        _mod_kernel,
        grid=(DEPTH, n // tn),
        in_specs=[pl.BlockSpec((MOD_ROWS, D_MODEL), lambda i, j: (0, 0)),
                  pl.BlockSpec((1, D_MODEL, tn), lambda i, j: (i, 0, j)),
                  pl.BlockSpec((1, 1, tn), lambda i, j: (i, 0, j))],
        out_specs=pl.BlockSpec((1, MOD_ROWS, tn), lambda i, j: (i, 0, j)),
        out_shape=jax.ShapeDtypeStruct((DEPTH, MOD_ROWS, n), F32),
        compiler_params=_params(("arbitrary", "arbitrary"), 40),
        name="modulation",
    )(cvec, w_mod, b_mod.reshape(DEPTH, 1, n))


def _inproj_kernel(*refs, add_pos, groups):
    it = iter(refs)
    x_ref = next(it)
    pos_ref = next(it) if add_pos else None
    g_ref, shift_ref, scale_ref, w_ref = next(it), next(it), next(it), next(it)
    out_refs = list(it)
    x = x_ref[0]
    if add_pos:
        x = x + pos_ref[...]
    h = (_rms(x, g_ref[...]) * (1.0 + scale_ref[0]) + shift_ref[0]).astype(BF16)
    for o_ref, (_, lo, hi) in zip(out_refs, groups):
        o_ref[0] = _dot(h, w_ref[:, lo:hi]).astype(BF16)


def _inproj(x, pos, g, shift, scale, w, groups, tm):
    b, l, d = x.shape
    add_pos = pos is not None
    tok = lambda bi, ti: (bi, ti, 0)
    vec = lambda bi, ti: (bi, 0, 0)
    in_specs = [pl.BlockSpec((1, tm, d), tok)]
    args = [x]
    if add_pos:
        in_specs.append(pl.BlockSpec((tm, d), lambda bi, ti: (ti, 0)))
        args.append(pos)
    in_specs += [_resident((1, d)), pl.BlockSpec((1, 1, d), vec), pl.BlockSpec((1, 1, d), vec),
                 _resident(w.shape)]
    args += [g, shift, scale, w]
    outs = pl.pallas_call(
        functools.partial(_inproj_kernel, add_pos=add_pos, groups=groups),
        grid=(b, l // tm),
        in_specs=in_specs,
        out_specs=[pl.BlockSpec((1, tm, hi - lo), tok) for _, lo, hi in groups],
        out_shape=[jax.ShapeDtypeStruct((b, l, hi - lo), BF16) for _, lo, hi in groups],
        compiler_params=_params(("parallel", "parallel"), 48),
        name="inproj",
    )(*args)
    return {name: o for (name, _, _), o in zip(groups, outs)}


def _gla_kernel(*refs, seq, with_output, with_state):
    it = iter(refs)
    k_ref, v_ref, lr_ref, wd_ref, bd_ref, s0_ref = (next(it) for _ in range(6))
    if with_output:
        q_ref, r_ref, nw_ref = next(it), next(it), next(it)
        y_ref = next(it)
    if with_state:
        s_out_ref = next(it)
    st_ref = next(it)
    if with_output:
        o_scr = next(it)
    n_chunks = seq // CHUNK

    st_ref[...] = s0_ref[0]

    row = lax.broadcasted_iota(jnp.int32, (CHUNK, CHUNK), 0)
    col = lax.broadcasted_iota(jnp.int32, (CHUNK, CHUNK), 1)
    cum = (jnp.where(row >= col, 1.0, 0.0).astype(BF16), jnp.where(row <= col, 1.0, 0.0).astype(BF16))
    lane_head = lax.broadcasted_iota(jnp.int32, (1, K_W), 1) // DK
    head_mask = [lane_head == h for h in range(HEADS)]
    srow = lax.broadcasted_iota(jnp.int32, (HEADS * CHUNK, CHUNK), 0) % CHUNK
    scol = lax.broadcasted_iota(jnp.int32, (HEADS * CHUNK, CHUNK), 1)
    score_mask = (srow >= scol, srow <= scol)
    last_row = (CHUNK - 1, 0)
    mid_row = (CHUNK // 2 - 1, CHUNK // 2)

    def stack_heads(a):
        zero = jnp.zeros_like(a)
        return jnp.concatenate([jnp.where(head_mask[h], a, zero) for h in range(HEADS)], axis=0)

    def chunk_step(dirn, n):
        rows = pl.ds(pl.multiple_of(n * CHUNK, CHUNK), CHUNK)
        z = _dot(lr_ref[0, rows, :], wd_ref[dirn]) + bd_ref[dirn]
        logg = (jnp.minimum(z, 0.0) - jnp.log1p(jnp.exp(-jnp.abs(z)))) * (1.0 / TAU)
        hi = logg.astype(BF16)
        lo = (logg - hi.astype(F32)).astype(BF16)
        g = _dot(cum[dirn], hi) + _dot(cum[dirn], lo)
        g_last = g[last_row[dirn]:last_row[dirn] + 1]
        k = k_ref[0, rows, :].astype(F32)
        v = v_ref[0, rows, :]
        st = st_ref[dirn]
        if with_output:
            g_mid = g[mid_row[dirn]:mid_row[dirn] + 1]
            qs = q_ref[0, rows, :].astype(F32) * (DK ** -0.5)
            q_in = stack_heads((qs * jnp.exp(g - g_mid)).astype(BF16))
            k_in = (k * jnp.exp(g_mid - g)).astype(BF16)
            q_st = stack_heads((qs * jnp.exp(g)).astype(BF16))
            scores = jnp.where(score_mask[dirn], _dot_nt(q_in, k_in), 0.0).astype(BF16)
            inter = _dot_nt(q_st, st.astype(BF16))
            for h in range(HEADS):
                hs = slice(h * CHUNK, (h + 1) * CHUNK)
                vs = slice(h * DV, (h + 1) * DV)
                o_scr[dirn, rows, vs] = _dot(scores[hs], v[:, vs]) + inter[hs]
        k_upd = (k * jnp.exp(g_last - g)).astype(BF16)
        outer = _dot_tn(v, k_upd)
        upd = jnp.zeros((DV, K_W), F32)
        for h in range(HEADS):
            upd = jnp.where(head_mask[h], outer[h * DV:(h + 1) * DV], upd)
        st_ref[dirn] = jnp.exp(g_last) * st + upd

    def body(n, carry):
        chunk_step(0, n)
        chunk_step(1, n_chunks - 1 - n)
        return carry

    lax.fori_loop(0, n_chunks, body, 0)

    if with_state:
        s_out_ref[0] = st_ref[...]

    if with_output:
        eb = min(seq, 256)

        def finish(i, carry):
            rows = pl.ds(pl.multiple_of(i * eb, eb), eb)
            o = o_scr[0, rows, :] + o_scr[1, rows, :]
            gate = _silu(r_ref[0, rows, :].astype(F32))
            for h in range(HEADS):
                vs = slice(h * DV, (h + 1) * DV)
                y_ref[0, rows, vs] = (_rms(o[:, vs], nw_ref[:, vs]) * gate[:, vs]).astype(BF16)
            return carry

        lax.fori_loop(0, seq // eb, finish, 0)


def _gla(p, wd, bd, s0, nw, with_output, with_state):
    b, l, _ = p["k"].shape
    seq3 = lambda bi: (bi, 0, 0)
    in_specs = [pl.BlockSpec((1, l, K_W), seq3), pl.BlockSpec((1, l, V_W), seq3),
                pl.BlockSpec((1, l, LR_PAD), seq3), _resident(wd.shape), _resident(bd.shape),
                pl.BlockSpec((1, 2, DV, K_W), lambda bi: (bi, 0, 0, 0))]
    args = [p["k"], p["v"], p["lr"], wd, bd, s0]
    out_specs, out_shape = [], []
    scratch = [pltpu.VMEM((2, DV, K_W), F32)]
    if with_output:
        in_specs += [pl.BlockSpec((1, l, K_W), seq3), pl.BlockSpec((1, l, V_W), seq3),
                     _resident((1, V_W))]
        args += [p["q"], p["r"], nw]
        out_specs.append(pl.BlockSpec((1, l, V_W), seq3))
        out_shape.append(jax.ShapeDtypeStruct((b, l, V_W), BF16))
        scratch.append(pltpu.VMEM((2, l, V_W), F32))
    if with_state:
        out_specs.append(pl.BlockSpec((1, 2, DV, K_W), lambda bi: (bi, 0, 0, 0)))
        out_shape.append(jax.ShapeDtypeStruct((b, 2, DV, K_W), F32))
    outs = pl.pallas_call(
        functools.partial(_gla_kernel, seq=l, with_output=with_output, with_state=with_state),
        grid=(b,),
        in_specs=in_specs,
        out_specs=out_specs,
        out_shape=out_shape,
        scratch_shapes=scratch,
        compiler_params=_params(("parallel",), 48),
        name="gla",
    )(*args)
    outs = list(outs)
    y = outs.pop(0) if with_output else None
    s = outs.pop(0) if with_state else None
    return y, s


def _four_conv_kernel(uf_ref, bg_ref, cg_ref, uc_ref, cw_ref, cb_ref, chan_ref, pos_ref,
                      yf_ref, yc_ref, ab_scr, *, seq):
    ab = _dot(uf_ref[0], chan_ref[...])
    ab_scr[0:seq, :] = ab[:, :FOUR_W].astype(BF16)
    ab_scr[seq:2 * seq, :] = ab[:, FOUR_W:].astype(BF16)
    rb = min(seq, 512)
    for i in range(seq // rb):
        yf_ref[0, i * rb:(i + 1) * rb, :] = _dot(pos_ref[i * rb:(i + 1) * rb, :], ab_scr[...]).astype(BF16)

    z = cg_ref[0].astype(F32) * uc_ref[0].astype(F32)
    t = lax.broadcasted_iota(jnp.int32, (seq, CONV_W), 0)
    z_prev = jnp.where(t == 0, 0.0, pltpu.roll(z, 1, 0))
    z_next = jnp.where(t == seq - 1, 0.0, pltpu.roll(z, seq - 1, 0))
    y = cw_ref[0:1] * z_prev + cw_ref[1:2] * z + cw_ref[2:3] * z_next + cb_ref[...]
    yc_ref[0] = (bg_ref[0].astype(F32) * y).astype(BF16)


def _four_conv(p, conv_w, conv_b, chan_tab, pos_tab):
    b, l, _ = p["uf"].shape
    seq3 = lambda bi: (bi, 0, 0)
    tok = pl.BlockSpec((1, l, FOUR_W), seq3)
    return pl.pallas_call(
        functools.partial(_four_conv_kernel, seq=l),
        grid=(b,),
        in_specs=[tok, tok, tok, tok, _resident(conv_w.shape), _resident(conv_b.shape),
                  _resident(chan_tab.shape), _resident(pos_tab.shape)],
        out_specs=[tok, tok],
        out_shape=[jax.ShapeDtypeStruct((b, l, FOUR_W), BF16)] * 2,
        scratch_shapes=[pltpu.VMEM((2 * l, FOUR_W), BF16)],
        compiler_params=_params(("parallel",), 48),
        name="four_conv",
    )(p["uf"], p["bg"], p["cg"], p["uc"], conv_w, conv_b, chan_tab, pos_tab)


def _outproj_kernel(*refs, add_pos):
    it = iter(refs)
    yg_ref, yf_ref, yc_ref, w_ref, x_ref = (next(it) for _ in range(5))
    pos_ref = next(it) if add_pos else None
    g_ref, gate_ref, o_ref = next(it), next(it), next(it)
    y = (_dot(yg_ref[0], w_ref[0:V_W]) + _dot(yf_ref[0], w_ref[V_W:V_W + FOUR_W])
         + _dot(yc_ref[0], w_ref[V_W + FOUR_W:]))
    x = x_ref[0]
    if add_pos:
        x = x + pos_ref[...]
    o_ref[0] = x + gate_ref[0] * _rms(y, g_ref[...])


def _outproj(yg, yf, yc, w, x, pos, g, gate, tm):
    b, l, d = x.shape
    add_pos = pos is not None
    tok = lambda bi, ti: (bi, ti, 0)
    in_specs = [pl.BlockSpec((1, tm, V_W), tok), pl.BlockSpec((1, tm, FOUR_W), tok),
                pl.BlockSpec((1, tm, CONV_W), tok), _resident(w.shape), pl.BlockSpec((1, tm, d), tok)]
    args = [yg, yf, yc, w, x]
    if add_pos:
        in_specs.append(pl.BlockSpec((tm, d), lambda bi, ti: (ti, 0)))
        args.append(pos)
    in_specs += [_resident((1, d)), pl.BlockSpec((1, 1, d), lambda bi, ti: (bi, 0, 0))]
    args += [g, gate]
    return pl.pallas_call(
        functools.partial(_outproj_kernel, add_pos=add_pos),
        grid=(b, l // tm),
        in_specs=in_specs,
        out_specs=pl.BlockSpec((1, tm, d), tok),
        out_shape=jax.ShapeDtypeStruct((b, l, d), F32),
        compiler_params=_params(("parallel", "parallel"), 48),
        name="outproj",
    )(*args)


_FFN_TILES = ((0, 1536), (1536, HIDDEN))


def _ffn_kernel(x_ref, gpre_ref, shift_ref, scale_ref, gate_ref, gpost_ref, win_ref, wout_ref, o_ref):
    x = x_ref[0]
    h = (_rms(x, gpre_ref[...]) * (1.0 + scale_ref[0]) + shift_ref[0]).astype(BF16)
    y = jnp.zeros(x.shape, F32)
    for lo, hi in _FFN_TILES:
        a = _dot(h, win_ref[:, lo:hi])
        u = _dot(h, win_ref[:, HIDDEN + lo:HIDDEN + hi])
        y = y + _dot((_silu(a) * u).astype(BF16), wout_ref[lo:hi, :])
    o_ref[0] = x + gate_ref[0] * _rms(y, gpost_ref[...])


def _ffn(x, gpre, shift, scale, gate, gpost, w_in, w_out, tm):
    b, l, d = x.shape
    tok = lambda bi, ti: (bi, ti, 0)
    vec = pl.BlockSpec((1, 1, d), lambda bi, ti: (bi, 0, 0))
    return pl.pallas_call(
        _ffn_kernel,
        grid=(b, l // tm),
        in_specs=[pl.BlockSpec((1, tm, d), tok), _resident((1, d)), vec, vec, vec, _resident((1, d)),
                  _resident(w_in.shape), _resident(w_out.shape)],
        out_specs=pl.BlockSpec((1, tm, d), tok),
        out_shape=jax.ShapeDtypeStruct((b, l, d), F32),
        compiler_params=_params(("parallel", "parallel"), 56),
        name="ffn",
    )(x, gpre, shift, scale, gate, gpost, w_in, w_out)


def _arrange_w_in(w):
    main = jnp.concatenate([w[:, _C_K:_C_LR], w[:, _C_Q:_C_END]], axis=1)
    lr = jnp.pad(w[:, _C_LR:_C_Q], ((0, 0), (0, LR_PAD - 2 * RANK)))
    return jnp.concatenate([main, lr], axis=1).astype(BF16)


def _arrange_w_state(w_all):
    return jnp.concatenate([w_all[:, :K_W + V_W], w_all[:, _MAIN_W:]], axis=1)


def _arrange_w_dec(w_f, w_b):
    wd = jnp.zeros((2, LR_PAD, K_W), F32)
    wd = wd.at[0, 0:RANK].set(w_f).at[1, RANK:2 * RANK].set(w_b)
    return wd.astype(BF16)


def kernel(x, c, ctx, c_ctx, w_mod, b_mod, g_mix_pre, g_mix_post, g_ffn_pre, g_ffn_post,
           w_in, w_dec_f, b_dec_f, w_dec_b, b_dec_b, gla_norm_w, conv_w, conv_b, w_out,
           w_ffn_in, w_ffn_out):
    b, n_lat, d = x.shape
    n_ctx = ctx.shape[1]
    assert d == D_MODEL and w_in.shape == (DEPTH, D_MODEL, _C_END)
    assert n_lat % 512 == 0 and n_ctx % CHUNK == 0 and b + 1 <= MOD_ROWS

    pos = jnp.asarray(_pos_embed(n_lat, d))
    tabs = {n_lat: _dft_tables(n_lat), n_ctx: _dft_tables(n_ctx)}

    cvec = jnp.zeros((MOD_ROWS, d), F32).at[:b].set(c).at[b].set(c_ctx)
    mod = _modulation(cvec, w_mod, b_mod)

    zero_state = jnp.zeros((b, 2, DV, K_W), F32)
    xc = ctx
    for i in range(DEPTH):
        last = i == DEPTH - 1
        m_lat = [mod[i, :b, j * d:(j + 1) * d].reshape(b, 1, d) for j in range(N_MOD)]
        m_ctx = [jnp.broadcast_to(mod[i, b, j * d:(j + 1) * d].reshape(1, 1, d), (b, 1, d))
                 for j in range(N_MOD)]
        w_all = _arrange_w_in(w_in[i])
        wd = _arrange_w_dec(w_dec_f[i], w_dec_b[i])
        bd = jnp.stack([b_dec_f[i], b_dec_b[i]]).reshape(2, 1, K_W)
        nw = gla_norm_w[i].reshape(1, V_W)
        cb = conv_b[i].reshape(1, CONV_W)
        wo = w_out[i].astype(BF16)
        wfi = w_ffn_in[i].astype(BF16)
        wfo = w_ffn_out[i].astype(BF16)
        g_pre, g_post = g_mix_pre[i].reshape(1, d), g_mix_post[i].reshape(1, d)
        gf_pre, gf_post = g_ffn_pre[i].reshape(1, d), g_ffn_post[i].reshape(1, d)

        def mixer_and_ffn(xs, pos_s, m, s0, want_state, tm):
            p = _inproj(xs, pos_s, g_pre, m[0], m[1], w_all, _OUT_GROUPS_FULL, tm)
            yg, s = _gla(p, wd, bd, s0, nw, True, want_state)
            yf, yc = _four_conv(p, conv_w[i], cb, *reversed(tabs[xs.shape[1]]))
            xs = _outproj(yg, yf, yc, wo, xs, pos_s, g_post, m[2], tm)
            xs = _ffn(xs, gf_pre, m[3], m[4], m[5], gf_post, wfi, wfo, tm)
            return xs, s

        if last:
            p = _inproj(xc, None, g_pre, m_ctx[0], m_ctx[1], _arrange_w_state(w_all),
                        _OUT_GROUPS_STATE, n_ctx)
            _, s = _gla(p, wd, bd, zero_state, None, False, True)
        else:
            xc, s = mixer_and_ffn(xc, None, m_ctx, zero_state, True, n_ctx)
        x, _ = mixer_and_ffn(x, pos if i == 0 else None, m_lat, s, False, 512)
    return x
```

```python
import functools

import numpy as np
import jax
import jax.numpy as jnp
from jax import lax
from jax.experimental import pallas as pl
from jax.experimental.pallas import tpu as pltpu

F32 = jnp.float32
BF16 = jnp.bfloat16

D_MODEL = 1024
DEPTH = 2
GRID_W = 64
N_MOD = 6
HEADS = 4
DK = 64
DV = 128
K_W = HEADS * DK
V_W = HEADS * DV
RANK = 16
TAU = 16.0
CHUNK = 64
FOUR_W = 256
FOUR_G = 4
CONV_W = 256
HIDDEN = 2816
EPS = 1e-6
LR_PAD = 128
MOD_ROWS = 16

_C_K, _C_V, _C_LR, _C_Q, _C_R, _C_FOUR, _C_BG, _C_CG, _C_UC, _C_END = (
    0, 256, 768, 800, 1056, 1568, 1824, 2080, 2336, 2592)
_MAIN_W = 2560
_IN_W = _MAIN_W + LR_PAD
_OUT_GROUPS_FULL = (("k", 0, 256), ("v", 256, 768), ("q", 768, 1024), ("r", 1024, 1536),
                    ("uf", 1536, 1792), ("bg", 1792, 2048), ("cg", 2048, 2304),
                    ("uc", 2304, 2560), ("lr", 2560, 2688))
_OUT_GROUPS_STATE = (("k", 0, 256), ("v", 256, 768), ("lr", 768, 896))

V7X_VMEM_BYTES = 64 * 1024 * 1024


def _params(semantics, vmem_mb):
    assert vmem_mb * 1024 * 1024 < V7X_VMEM_BYTES
    return pltpu.CompilerParams(dimension_semantics=semantics,
                                vmem_limit_bytes=vmem_mb * 1024 * 1024)


def _resident(shape):
    zeros = (0,) * len(shape)
    return pl.BlockSpec(shape, lambda *_: zeros, pipeline_mode=pl.Buffered(1))


def _silu(a):
    return a / (1.0 + jnp.exp(-a))


def _rms(x, g):
    return x * lax.rsqrt(jnp.mean(x * x, axis=-1, keepdims=True) + EPS) * g


def _dot(a, b):
    return jnp.dot(a, b, preferred_element_type=F32)


def _dot_nt(a, b):
    return lax.dot_general(a, b, (((1,), (1,)), ((), ())), preferred_element_type=F32)


def _dot_tn(a, b):
    return lax.dot_general(a, b, (((0,), (0,)), ((), ())), preferred_element_type=F32)


def _pos_embed(n_tokens, dim):
    rows = n_tokens // GRID_W
    row = np.repeat(np.arange(rows, dtype=np.float32), GRID_W)
    col = np.tile(np.arange(GRID_W, dtype=np.float32), rows)
    quarter = dim // 4
    freqs = (1.0 / (10000.0 ** (np.arange(quarter, dtype=np.float32) / quarter))).astype(np.float32)

    def enc(p):
        ang = (p[:, None] * freqs[None, :]).astype(np.float32)
        return np.concatenate([np.sin(ang), np.cos(ang)], axis=-1)

    return np.concatenate([enc(row), enc(col)], axis=-1).astype(np.float32)


def _dft_cos_sin(n):
    idx = np.arange(n, dtype=np.int64)
    ang = 2.0 * np.pi * ((idx[:, None] * idx[None, :]) % n).astype(np.float64) / n
    return np.cos(ang) / np.sqrt(n), np.sin(ang) / np.sqrt(n)


def _dft_tables(seq):
    cl, sl = _dft_cos_sin(seq)
    pos_tab = np.concatenate([cl, -sl], axis=1)
    gw = FOUR_W // FOUR_G
    cc, sc = _dft_cos_sin(gw)
    eye = np.eye(FOUR_G)
    chan_tab = np.concatenate([np.kron(eye, cc), np.kron(eye, sc)], axis=1)
    return (jnp.asarray(pos_tab, dtype=F32).astype(BF16), jnp.asarray(chan_tab, dtype=F32).astype(BF16))


def _mod_kernel(c_ref, w_ref, b_ref, o_ref):
    s = _silu(c_ref[...]).astype(BF16)
    o_ref[0] = _dot(s, w_ref[0].astype(BF16)) + b_ref[0]


def _modulation(cvec, w_mod, b_mod):
    tn = 1536
    n = N_MOD * D_MODEL
    return pl.pallas_call(
        _mod_kernel,
        grid=(DEPTH, n // tn),
        in_specs=[pl.BlockSpec((MOD_ROWS, D_MODEL), lambda i, j: (0, 0)),
                  pl.BlockSpec((1, D_MODEL, tn), lambda i, j: (i, 0, j)),
                  pl.BlockSpec((1, 1, tn), lambda i, j: (i, 0, j))],
        out_specs=pl.BlockSpec((1, MOD_ROWS, tn), lambda i, j: (i, 0, j)),
        out_shape=jax.ShapeDtypeStruct((DEPTH, MOD_ROWS, n), F32),
        compiler_params=_params(("arbitrary", "arbitrary"), 40),
        name="modulation",
    )(cvec, w_mod, b_mod.reshape(DEPTH, 1, n))


def _inproj_kernel(*refs, add_pos, groups):
    it = iter(refs)
    x_ref = next(it)
    pos_ref = next(it) if add_pos else None
    g_ref, shift_ref, scale_ref, w_ref = next(it), next(it), next(it), next(it)
    out_refs = list(it)
    x = x_ref[0]
    if add_pos:
        x = x + pos_ref[...]
    h = (_rms(x, g_ref[...]) * (1.0 + scale_ref[0]) + shift_ref[0]).astype(BF16)
    for o_ref, (_, lo, hi) in zip(out_refs, groups):
        o_ref[0] = _dot(h, w_ref[:, lo:hi]).astype(BF16)


def _inproj(x, pos, g, shift, scale, w, groups, tm):
    b, l, d = x.shape
    add_pos = pos is not None
    tok = lambda bi, ti: (bi, ti, 0)
    vec = lambda bi, ti: (bi, 0, 0)
    in_specs = [pl.BlockSpec((1, tm, d), tok)]
    args = [x]
    if add_pos:
        in_specs.append(pl.BlockSpec((tm, d), lambda bi, ti: (ti, 0)))
        args.append(pos)
    in_specs += [_resident((1, d)), pl.BlockSpec((1, 1, d), vec), pl.BlockSpec((1, 1, d), vec),
                 _resident(w.shape)]
    args += [g, shift, scale, w]
    outs = pl.pallas_call(
        functools.partial(_inproj_kernel, add_pos=add_pos, groups=groups),
        grid=(b, l // tm),
        in_specs=in_specs,
        out_specs=[pl.BlockSpec((1, tm, hi - lo), tok) for _, lo, hi in groups],
        out_shape=[jax.ShapeDtypeStruct((b, l, hi - lo), BF16) for _, lo, hi in groups],
        compiler_params=_params(("parallel", "parallel"), 48),
        name="inproj",
    )(*args)
    return {name: o for (name, _, _), o in zip(groups, outs)}


def _gla_kernel(*refs, seq, with_output, with_state):
    it = iter(refs)
    k_ref, v_ref, lr_ref, wd_ref, bd_ref, s0_ref = (next(it) for _ in range(6))
    if with_output:
        q_ref, r_ref, nw_ref = next(it), next(it), next(it)
        y_ref = next(it)
    if with_state:
        s_out_ref = next(it)
    st_ref = next(it)
    lg_hi, lg_lo = next(it), next(it)
    ds_scr = next(it)
    dec_scr = next(it)
    if with_output:
        o_scr = next(it)
        qst_scr = next(it)
    n_chunks = seq // CHUNK

    st_ref[...] = s0_ref[0]

    row = lax.broadcasted_iota(jnp.int32, (CHUNK, CHUNK), 0)
    col = lax.broadcasted_iota(jnp.int32, (CHUNK, CHUNK), 1)
    cum = (jnp.where(row >= col, 1.0, 0.0).astype(BF16), jnp.where(row <= col, 1.0, 0.0).astype(BF16))
    lane_head = lax.broadcasted_iota(jnp.int32, (1, K_W), 1) // DK
    head_mask = [lane_head == h for h in range(HEADS)]
    srow = lax.broadcasted_iota(jnp.int32, (HEADS * CHUNK, CHUNK), 0) % CHUNK
    scol = lax.broadcasted_iota(jnp.int32, (HEADS * CHUNK, CHUNK), 1)
    score_mask = (srow >= scol, srow <= scol)
    last_row = (CHUNK - 1, 0)
    mid_row = (CHUNK // 2 - 1, CHUNK // 2)

    def stack_heads(a):
        zero = jnp.zeros_like(a)
        return jnp.concatenate([jnp.where(head_mask[h], a, zero) for h in range(HEADS)], axis=0)

    def chunk_rows(n):
        return pl.ds(pl.multiple_of(n * CHUNK, CHUNK), CHUNK)

    gb = min(seq, 512)

    def gates(i, carry):
        rows = pl.ds(pl.multiple_of(i * gb, gb), gb)
        z = _dot(lr_ref[0, rows, :], wd_ref[...]) + bd_ref[...]
        logg = (jnp.minimum(z, 0.0) - jnp.log(1.0 + jnp.exp(-jnp.abs(z)))) * (1.0 / TAU)
        hi = logg.astype(BF16)
        lg_hi[rows, :] = hi
        lg_lo[rows, :] = (logg - hi.astype(F32)).astype(BF16)
        return carry

    lax.fori_loop(0, seq // gb, gates, 0)

    per_iter = 4

    def chunk_local(i, carry):
        loaded = []
        for u in range(per_iter):
            n = i * per_iter + u
            rows = chunk_rows(n)
            k = k_ref[0, rows, :].astype(F32)
            v = v_ref[0, rows, :]
            qs = q_ref[0, rows, :].astype(F32) * (DK ** -0.5) if with_output else None
            lg = [(lg_hi[rows, d * K_W:(d + 1) * K_W], lg_lo[rows, d * K_W:(d + 1) * K_W]) for d in range(2)]
            loaded.append((n, rows, k, v, qs, lg))
        chains = [(n, rows, k, v, qs, lg[dirn], dirn) for n, rows, k, v, qs, lg in loaded for dirn in range(2)]
        stores = []
        gs = [_dot(cum[dirn], hi) + _dot(cum[dirn], lo) for *_, (hi, lo), dirn in chains]
        outers, score_list = [], []
        for (n, rows, k, v, qs, _, dirn), g in zip(chains, gs):
            g_last = g[last_row[dirn]:last_row[dirn] + 1]
            stores.append((dec_scr, (dirn, pl.ds(pl.multiple_of(n * 8, 8), 8), slice(None)),
                           jnp.broadcast_to(jnp.exp(g_last), (8, K_W))))
            if with_output:
                g_mid = g[mid_row[dirn]:mid_row[dirn] + 1]
                q_in = stack_heads((qs * jnp.exp(g - g_mid)).astype(BF16))
                k_in = (k * jnp.exp(g_mid - g)).astype(BF16)
                score_list.append(_dot_nt(q_in, k_in))
                stores.append((qst_scr, (dirn, rows, slice(None)), (qs * jnp.exp(g)).astype(BF16)))
            outers.append(_dot_tn(v, (k * jnp.exp(g_last - g)).astype(BF16)))
        for (n, rows, k, v, qs, _, dirn), outer in zip(chains, outers):
            upd = jnp.zeros((DV, K_W), F32)
            for h in range(HEADS):
                upd = jnp.where(head_mask[h], outer[h * DV:(h + 1) * DV], upd)
            stores.append((ds_scr, (dirn, pl.ds(pl.multiple_of(n * DV, DV), DV), slice(None)), upd))
        if with_output:
            for (n, rows, k, v, qs, _, dirn), sc in zip(chains, score_list):
                scores = jnp.where(score_mask[dirn], sc, 0.0).astype(BF16)
                for h in range(HEADS):
                    hs = slice(h * CHUNK, (h + 1) * CHUNK)
                    vs = slice(h * DV, (h + 1) * DV)
                    stores.append((o_scr, (dirn, rows, vs), _dot(scores[hs], v[:, vs])))
        for ref, idx, val in stores:
            ref[idx] = val
        return carry

    lax.fori_loop(0, n_chunks // per_iter, chunk_local, 0)

    def scan(i, carry):
        steps = ((0, i), (1, n_chunks - 1 - i))
        new_st, inters = [], []
        for dirn, n in steps:
            st = st_ref[dirn]
            if with_output:
                q_st = stack_heads(qst_scr[dirn, chunk_rows(n), :])
                inters.append(_dot_nt(q_st, st.astype(BF16)))
            dec = dec_scr[dirn, pl.ds(pl.multiple_of(n * 8, 8), 8), :]
            new_st.append(dec[0:1] * st + ds_scr[dirn, pl.ds(pl.multiple_of(n * DV, DV), DV), :])
        for (dirn, n), st in zip(steps, new_st):
            st_ref[dirn] = st
        if with_output:
            for (dirn, n), inter in zip(steps, inters):
                for h in range(HEADS):
                    vs = slice(h * DV, (h + 1) * DV)
                    o_scr[dirn, chunk_rows(n), vs] += inter[h * CHUNK:(h + 1) * CHUNK]
        return carry

    lax.fori_loop(0, n_chunks, scan, 0, unroll=2)

    if with_state:
        s_out_ref[0] = st_ref[...]

    if with_output:
        eb = min(seq, 256)

        def finish(i, carry):
            rows = pl.ds(pl.multiple_of(i * eb, eb), eb)
            o = o_scr[0, rows, :] + o_scr[1, rows, :]
            gate = _silu(r_ref[0, rows, :].astype(F32))
            for h in range(HEADS):
                vs = slice(h * DV, (h + 1) * DV)
                y_ref[0, rows, vs] = (_rms(o[:, vs], nw_ref[:, vs]) * gate[:, vs]).astype(BF16)
            return carry

        lax.fori_loop(0, seq // eb, finish, 0)


def _gla(p, wd, bd, s0, nw, with_output, with_state):
    b, l, _ = p["k"].shape
    seq3 = lambda bi: (bi, 0, 0)
    in_specs = [pl.BlockSpec((1, l, K_W), seq3), pl.BlockSpec((1, l, V_W), seq3),
                pl.BlockSpec((1, l, LR_PAD), seq3), _resident(wd.shape), _resident(bd.shape),
                pl.BlockSpec((1, 2, DV, K_W), lambda bi: (bi, 0, 0, 0))]
    args = [p["k"], p["v"], p["lr"], wd, bd, s0]
    out_specs, out_shape = [], []
    scratch = [pltpu.VMEM((2, DV, K_W), F32), pltpu.VMEM((l, 2 * K_W), BF16), pltpu.VMEM((l, 2 * K_W), BF16),
               pltpu.VMEM((2, l // CHUNK * DV, K_W), F32), pltpu.VMEM((2, l // CHUNK * 8, K_W), F32)]
    if with_output:
        in_specs += [pl.BlockSpec((1, l, K_W), seq3), pl.BlockSpec((1, l, V_W), seq3),
                     _resident((1, V_W))]
        args += [p["q"], p["r"], nw]
        out_specs.append(pl.BlockSpec((1, l, V_W), seq3))
        out_shape.append(jax.ShapeDtypeStruct((b, l, V_W), BF16))
        scratch += [pltpu.VMEM((2, l, V_W), F32), pltpu.VMEM((2, l, K_W), BF16)]
    if with_state:
        out_specs.append(pl.BlockSpec((1, 2, DV, K_W), lambda bi: (bi, 0, 0, 0)))
        out_shape.append(jax.ShapeDtypeStruct((b, 2, DV, K_W), F32))
    outs = pl.pallas_call(
        functools.partial(_gla_kernel, seq=l, with_output=with_output, with_state=with_state),
        grid=(b,),
        in_specs=in_specs,
        out_specs=out_specs,
        out_shape=out_shape,
        scratch_shapes=scratch,
        compiler_params=_params(("parallel",), 48),
        name="gla",
    )(*args)
    outs = list(outs)
    y = outs.pop(0) if with_output else None
    s = outs.pop(0) if with_state else None
    return y, s


def _four_conv_kernel(uf_ref, bg_ref, cg_ref, uc_ref, cw_ref, cb_ref, chan_ref, pos_ref,
                      yf_ref, yc_ref, ab_scr, *, seq):
    ab = _dot(uf_ref[0], chan_ref[...])
    ab_scr[0:seq, :] = ab[:, :FOUR_W].astype(BF16)
    ab_scr[seq:2 * seq, :] = ab[:, FOUR_W:].astype(BF16)
    rb = min(seq, 512)
    for i in range(seq // rb):
        yf_ref[0, i * rb:(i + 1) * rb, :] = _dot(pos_ref[i * rb:(i + 1) * rb, :], ab_scr[...]).astype(BF16)

    z = cg_ref[0].astype(F32) * uc_ref[0].astype(F32)
    t = lax.broadcasted_iota(jnp.int32, (seq, CONV_W), 0)
    z_prev = jnp.where(t == 0, 0.0, pltpu.roll(z, 1, 0))
    z_next = jnp.where(t == seq - 1, 0.0, pltpu.roll(z, seq - 1, 0))
    y = cw_ref[0:1] * z_prev + cw_ref[1:2] * z + cw_ref[2:3] * z_next + cb_ref[...]
    yc_ref[0] = (bg_ref[0].astype(F32) * y).astype(BF16)


def _four_conv(p, conv_w, conv_b, chan_tab, pos_tab):
    b, l, _ = p["uf"].shape
    seq3 = lambda bi: (bi, 0, 0)
    tok = pl.BlockSpec((1, l, FOUR_W), seq3)
    return pl.pallas_call(
        functools.partial(_four_conv_kernel, seq=l),
        grid=(b,),
        in_specs=[tok, tok, tok, tok, _resident(conv_w.shape), _resident(conv_b.shape),
                  _resident(chan_tab.shape), _resident(pos_tab.shape)],
        out_specs=[tok, tok],
        out_shape=[jax.ShapeDtypeStruct((b, l, FOUR_W), BF16)] * 2,
        scratch_shapes=[pltpu.VMEM((2 * l, FOUR_W), BF16)],
        compiler_params=_params(("parallel",), 48),
        name="four_conv",
    )(p["uf"], p["bg"], p["cg"], p["uc"], conv_w, conv_b, chan_tab, pos_tab)


def _outproj_kernel(*refs, add_pos):
    it = iter(refs)
    yg_ref, yf_ref, yc_ref, w_ref, x_ref = (next(it) for _ in range(5))
    pos_ref = next(it) if add_pos else None
    g_ref, gate_ref, o_ref = next(it), next(it), next(it)
    y = (_dot(yg_ref[0], w_ref[0:V_W]) + _dot(yf_ref[0], w_ref[V_W:V_W + FOUR_W])
         + _dot(yc_ref[0], w_ref[V_W + FOUR_W:]))
    x = x_ref[0]
    if add_pos:
        x = x + pos_ref[...]
    o_ref[0] = x + gate_ref[0] * _rms(y, g_ref[...])


def _outproj(yg, yf, yc, w, x, pos, g, gate, tm):
    b, l, d = x.shape
    add_pos = pos is not None
    tok = lambda bi, ti: (bi, ti, 0)
    in_specs = [pl.BlockSpec((1, tm, V_W), tok), pl.BlockSpec((1, tm, FOUR_W), tok),
                pl.BlockSpec((1, tm, CONV_W), tok), _resident(w.shape), pl.BlockSpec((1, tm, d), tok)]
    args = [yg, yf, yc, w, x]
    if add_pos:
        in_specs.append(pl.BlockSpec((tm, d), lambda bi, ti: (ti, 0)))
        args.append(pos)
    in_specs += [_resident((1, d)), pl.BlockSpec((1, 1, d), lambda bi, ti: (bi, 0, 0))]
    args += [g, gate]
    return pl.pallas_call(
        functools.partial(_outproj_kernel, add_pos=add_pos),
        grid=(b, l // tm),
        in_specs=in_specs,
        out_specs=pl.BlockSpec((1, tm, d), tok),
        out_shape=jax.ShapeDtypeStruct((b, l, d), F32),
        compiler_params=_params(("parallel", "parallel"), 48),
        name="outproj",
    )(*args)


_FFN_TILES = ((0, 1536), (1536, HIDDEN))


def _ffn_kernel(x_ref, gpre_ref, shift_ref, scale_ref, gate_ref, gpost_ref, win_ref, wout_ref, o_ref):
    x = x_ref[0]
    h = (_rms(x, gpre_ref[...]) * (1.0 + scale_ref[0]) + shift_ref[0]).astype(BF16)
    y = jnp.zeros(x.shape, F32)
    for lo, hi in _FFN_TILES:
        a = _dot(h, win_ref[:, lo:hi])
        u = _dot(h, win_ref[:, HIDDEN + lo:HIDDEN + hi])
        y = y + _dot((_silu(a) * u).astype(BF16), wout_ref[lo:hi, :])
    o_ref[0] = x + gate_ref[0] * _rms(y, gpost_ref[...])


def _ffn(x, gpre, shift, scale, gate, gpost, w_in, w_out, tm):
    b, l, d = x.shape
    tok = lambda bi, ti: (bi, ti, 0)
    vec = pl.BlockSpec((1, 1, d), lambda bi, ti: (bi, 0, 0))
    return pl.pallas_call(
        _ffn_kernel,
        grid=(b, l // tm),
        in_specs=[pl.BlockSpec((1, tm, d), tok), _resident((1, d)), vec, vec, vec, _resident((1, d)),
                  _resident(w_in.shape), _resident(w_out.shape)],
        out_specs=pl.BlockSpec((1, tm, d), tok),
        out_shape=jax.ShapeDtypeStruct((b, l, d), F32),
        compiler_params=_params(("parallel", "parallel"), 56),
        name="ffn",
    )(x, gpre, shift, scale, gate, gpost, w_in, w_out)


def _arrange_w_in(w):
    main = jnp.concatenate([w[:, _C_K:_C_LR], w[:, _C_Q:_C_END]], axis=1)
    lr = jnp.pad(w[:, _C_LR:_C_Q], ((0, 0), (0, LR_PAD - 2 * RANK)))
    return jnp.concatenate([main, lr], axis=1).astype(BF16)


def _arrange_w_state(w_all):
    return jnp.concatenate([w_all[:, :K_W + V_W], w_all[:, _MAIN_W:]], axis=1)


def _arrange_w_dec(w_f, w_b):
    wd = jnp.zeros((LR_PAD, 2 * K_W), F32)
    wd = wd.at[0:RANK, :K_W].set(w_f).at[RANK:2 * RANK, K_W:].set(w_b)
    return wd.astype(BF16)


def kernel(x, c, ctx, c_ctx, w_mod, b_mod, g_mix_pre, g_mix_post, g_ffn_pre, g_ffn_post,
           w_in, w_dec_f, b_dec_f, w_dec_b, b_dec_b, gla_norm_w, conv_w, conv_b, w_out,
           w_ffn_in, w_ffn_out):
    b, n_lat, d = x.shape
    n_ctx = ctx.shape[1]
    assert d == D_MODEL and w_in.shape == (DEPTH, D_MODEL, _C_END)
    assert n_lat % 512 == 0 and n_ctx % CHUNK == 0 and b + 1 <= MOD_ROWS

    pos = jnp.asarray(_pos_embed(n_lat, d))
    tabs = {n_lat: _dft_tables(n_lat), n_ctx: _dft_tables(n_ctx)}

    cvec = jnp.zeros((MOD_ROWS, d), F32).at[:b].set(c).at[b].set(c_ctx)
    mod = _modulation(cvec, w_mod, b_mod)

    zero_state = jnp.zeros((b, 2, DV, K_W), F32)
    xc = ctx
    for i in range(DEPTH):
        last = i == DEPTH - 1
        m_lat = [mod[i, :b, j * d:(j + 1) * d].reshape(b, 1, d) for j in range(N_MOD)]
        m_ctx = [jnp.broadcast_to(mod[i, b, j * d:(j + 1) * d].reshape(1, 1, d), (b, 1, d))
                 for j in range(N_MOD)]
        w_all = _arrange_w_in(w_in[i])
        wd = _arrange_w_dec(w_dec_f[i], w_dec_b[i])
        bd = jnp.concatenate([b_dec_f[i], b_dec_b[i]]).reshape(1, 2 * K_W)
        nw = gla_norm_w[i].reshape(1, V_W)
        cb = conv_b[i].reshape(1, CONV_W)
        wo = w_out[i].astype(BF16)
        wfi = w_ffn_in[i].astype(BF16)
        wfo = w_ffn_out[i].astype(BF16)
        g_pre, g_post = g_mix_pre[i].reshape(1, d), g_mix_post[i].reshape(1, d)
        gf_pre, gf_post = g_ffn_pre[i].reshape(1, d), g_ffn_post[i].reshape(1, d)

        def mixer_and_ffn(xs, pos_s, m, s0, want_state, tm):
            p = _inproj(xs, pos_s, g_pre, m[0], m[1], w_all, _OUT_GROUPS_FULL, tm)
            yg, s = _gla(p, wd, bd, s0, nw, True, want_state)
            yf, yc = _four_conv(p, conv_w[i], cb, *reversed(tabs[xs.shape[1]]))
            xs = _outproj(yg, yf, yc, wo, xs, pos_s, g_post, m[2], tm)
            xs = _ffn(xs, gf_pre, m[3], m[4], m[5], gf_post, wfi, wfo, tm)
            return xs, s

        if last:
            p = _inproj(xc, None, g_pre, m_ctx[0], m_ctx[1], _arrange_w_state(w_all),
                        _OUT_GROUPS_STATE, n_ctx)
            _, s = _gla(p, wd, bd, zero_state, None, False, True)
        else:
            xc, s = mixer_and_ffn(xc, None, m_ctx, zero_state, True, n_ctx)
        x, _ = mixer_and_ffn(x, pos if i == 0 else None, m_lat, s, False, 512)
    return x
```

```python
import functools

import numpy as np
import jax
import jax.numpy as jnp
from jax import lax
from jax.experimental import pallas as pl
from jax.experimental.pallas import tpu as pltpu

F32 = jnp.float32
BF16 = jnp.bfloat16

D_MODEL = 1024
DEPTH = 2
GRID_W = 64
N_MOD = 6
HEADS = 4
DK = 64
DV = 128
K_W = HEADS * DK
V_W = HEADS * DV
RANK = 16
TAU = 16.0
CHUNK = 64
FOUR_W = 256
FOUR_G = 4
CONV_W = 256
HIDDEN = 2816
EPS = 1e-6
LR_PAD = 128
MOD_ROWS = 16

_C_K, _C_V, _C_LR, _C_Q, _C_R, _C_FOUR, _C_BG, _C_CG, _C_UC, _C_END = (
    0, 256, 768, 800, 1056, 1568, 1824, 2080, 2336, 2592)
_MAIN_W = 2560
_IN_W = _MAIN_W + LR_PAD
_OUT_GROUPS_FULL = (("k", 0, 256), ("v", 256, 768), ("q", 768, 1024), ("r", 1024, 1536),
                    ("uf", 1536, 1792), ("bg", 1792, 2048), ("cg", 2048, 2304),
                    ("uc", 2304, 2560), ("lr", 2560, 2688))
_OUT_GROUPS_STATE = (("k", 0, 256), ("v", 256, 768), ("lr", 768, 896))

V7X_VMEM_BYTES = 64 * 1024 * 1024


def _params(semantics, vmem_mb):
    assert vmem_mb * 1024 * 1024 < V7X_VMEM_BYTES
    return pltpu.CompilerParams(dimension_semantics=semantics,
                                vmem_limit_bytes=vmem_mb * 1024 * 1024)


def _resident(shape):
    zeros = (0,) * len(shape)
    return pl.BlockSpec(shape, lambda *_: zeros, pipeline_mode=pl.Buffered(1))


def _silu(a):
    return a / (1.0 + jnp.exp(-a))


def _rms(x, g):
    return x * lax.rsqrt(jnp.mean(x * x, axis=-1, keepdims=True) + EPS) * g


def _dot(a, b):
    return jnp.dot(a, b, preferred_element_type=F32)


def _dot_nt(a, b):
    return lax.dot_general(a, b, (((1,), (1,)), ((), ())), preferred_element_type=F32)


def _dot_tn(a, b):
    return lax.dot_general(a, b, (((0,), (0,)), ((), ())), preferred_element_type=F32)


def _pos_embed(n_tokens, dim):
    rows = n_tokens // GRID_W
    row = np.repeat(np.arange(rows, dtype=np.float32), GRID_W)
    col = np.tile(np.arange(GRID_W, dtype=np.float32), rows)
    quarter = dim // 4
    freqs = (1.0 / (10000.0 ** (np.arange(quarter, dtype=np.float32) / quarter))).astype(np.float32)

    def enc(p):
        ang = (p[:, None] * freqs[None, :]).astype(np.float32)
        return np.concatenate([np.sin(ang), np.cos(ang)], axis=-1)

    return np.concatenate([enc(row), enc(col)], axis=-1).astype(np.float32)


def _dft_cos_sin(n):
    idx = np.arange(n, dtype=np.int64)
    ang = 2.0 * np.pi * ((idx[:, None] * idx[None, :]) % n).astype(np.float64) / n
    return np.cos(ang) / np.sqrt(n), np.sin(ang) / np.sqrt(n)


def _dft_tables(seq):
    cl, sl = _dft_cos_sin(seq)
    pos_tab = np.concatenate([cl, -sl], axis=1)
    gw = FOUR_W // FOUR_G
    cc, sc = _dft_cos_sin(gw)
    eye = np.eye(FOUR_G)
    chan_tab = np.concatenate([np.kron(eye, cc), np.kron(eye, sc)], axis=1)
    return (jnp.asarray(pos_tab, dtype=F32).astype(BF16), jnp.asarray(chan_tab, dtype=F32).astype(BF16))


def _mod_kernel(c_ref, w_ref, b_ref, o_ref):
    s = _silu(c_ref[...]).astype(BF16)
    o_ref[0] = _dot(s, w_ref[0].astype(BF16)) + b_ref[0]


def _modulation(cvec, w_mod, b_mod):
    tn = 1536
    n = N_MOD * D_MODEL
    return pl.pallas_call(
        _mod_kernel,
        grid=(DEPTH, n // tn),
        in_specs=[pl.BlockSpec((MOD_ROWS, D_MODEL), lambda i, j: (0, 0)),
                  pl.BlockSpec((1, D_MODEL, tn), lambda i, j: (i, 0, j)),
                  pl.BlockSpec((1, 1, tn), lambda i, j: (i, 0, j))],
        out_specs=pl.BlockSpec((1, MOD_ROWS, tn), lambda i, j: (i, 0, j)),
        out_shape=jax.ShapeDtypeStruct((DEPTH, MOD_ROWS, n), F32),
        compiler_params=_params(("arbitrary", "arbitrary"), 40),
        name="modulation",
    )(cvec, w_mod, b_mod.reshape(DEPTH, 1, n))


def _inproj_kernel(*refs, add_pos, groups):
    it = iter(refs)
    x_ref = next(it)
    pos_ref = next(it) if add_pos else None
    g_ref, shift_ref, scale_ref, w_ref = next(it), next(it), next(it), next(it)
    out_refs = list(it)
    x = x_ref[0]
    if add_pos:
        x = x + pos_ref[...]
    h = (_rms(x, g_ref[...]) * (1.0 + scale_ref[0]) + shift_ref[0]).astype(BF16)
    for o_ref, (_, lo, hi) in zip(out_refs, groups):
        o_ref[0] = _dot(h, w_ref[:, lo:hi]).astype(BF16)


def _inproj(x, pos, g, shift, scale, w, groups, tm):
    b, l, d = x.shape
    add_pos = pos is not None
    tok = lambda bi, ti: (bi, ti, 0)
    vec = lambda bi, ti: (bi, 0, 0)
    in_specs = [pl.BlockSpec((1, tm, d), tok)]
    args = [x]
    if add_pos:
        in_specs.append(pl.BlockSpec((tm, d), lambda bi, ti: (ti, 0)))
        args.append(pos)
    in_specs += [_resident((1, d)), pl.BlockSpec((1, 1, d), vec), pl.BlockSpec((1, 1, d), vec),
                 _resident(w.shape)]
    args += [g, shift, scale, w]
    outs = pl.pallas_call(
        functools.partial(_inproj_kernel, add_pos=add_pos, groups=groups),
        grid=(b, l // tm),
        in_specs=in_specs,
        out_specs=[pl.BlockSpec((1, tm, hi - lo), tok) for _, lo, hi in groups],
        out_shape=[jax.ShapeDtypeStruct((b, l, hi - lo), BF16) for _, lo, hi in groups],
        compiler_params=_params(("parallel", "parallel"), 48),
        name="inproj",
    )(*args)
    return {name: o for (name, _, _), o in zip(groups, outs)}


def _gla_kernel(*refs, seq, with_output, with_state):
    it = iter(refs)
    k_ref, v_ref, lr_ref, wd_ref, bd_ref, s0_ref = (next(it) for _ in range(6))
    if with_output:
        q_ref, r_ref, nw_ref = next(it), next(it), next(it)
        y_ref = next(it)
    if with_state:
        s_out_ref = next(it)
    st_ref = next(it)
    lg_hi, lg_lo = next(it), next(it)
    ds_scr = next(it)
    dec_scr = next(it)
    if with_output:
        o_scr = next(it)
        qst_scr = next(it)
    n_chunks = seq // CHUNK

    st_ref[...] = s0_ref[0]

    row = lax.broadcasted_iota(jnp.int32, (CHUNK, CHUNK), 0)
    col = lax.broadcasted_iota(jnp.int32, (CHUNK, CHUNK), 1)
    cum = (jnp.where(row >= col, 1.0, 0.0).astype(BF16), jnp.where(row <= col, 1.0, 0.0).astype(BF16))
    lane_head = lax.broadcasted_iota(jnp.int32, (1, K_W), 1) // DK
    head_mask = [lane_head == h for h in range(HEADS)]
    srow = lax.broadcasted_iota(jnp.int32, (HEADS * CHUNK, CHUNK), 0) % CHUNK
    scol = lax.broadcasted_iota(jnp.int32, (HEADS * CHUNK, CHUNK), 1)
    score_mask = (srow >= scol, srow <= scol)
    last_row = (CHUNK - 1, 0)
    mid_row = (CHUNK // 2 - 1, CHUNK // 2)

    def stack_heads(a):
        zero = jnp.zeros_like(a)
        return jnp.concatenate([jnp.where(head_mask[h], a, zero) for h in range(HEADS)], axis=0)

    def chunk_rows(n):
        return pl.ds(pl.multiple_of(n * CHUNK, CHUNK), CHUNK)

    gb = min(seq, 512)

    def gates(i, carry):
        rows = pl.ds(pl.multiple_of(i * gb, gb), gb)
        z = _dot(lr_ref[0, rows, :], wd_ref[...]) + bd_ref[...]
        logg = (jnp.minimum(z, 0.0) - jnp.log(1.0 + jnp.exp(-jnp.abs(z)))) * (1.0 / TAU)
        hi = logg.astype(BF16)
        lg_hi[rows, :] = hi
        lg_lo[rows, :] = (logg - hi.astype(F32)).astype(BF16)
        return carry

    lax.fori_loop(0, seq // gb, gates, 0)

    per_iter = 4

    def chunk_local(i, carry):
        loaded = []
        for u in range(per_iter):
            n = i * per_iter + u
            rows = chunk_rows(n)
            k = k_ref[0, rows, :].astype(F32)
            v = v_ref[0, rows, :]
            qs = q_ref[0, rows, :].astype(F32) * (DK ** -0.5) if with_output else None
            lg = [(lg_hi[rows, d * K_W:(d + 1) * K_W], lg_lo[rows, d * K_W:(d + 1) * K_W]) for d in range(2)]
            loaded.append((n, rows, k, v, qs, lg))
        chains = [(n, rows, k, v, qs, lg[dirn], dirn) for n, rows, k, v, qs, lg in loaded for dirn in range(2)]
        stores = []
        gs = [_dot(cum[dirn], hi) + _dot(cum[dirn], lo) for *_, (hi, lo), dirn in chains]
        outers, score_list = [], []
        for (n, rows, k, v, qs, _, dirn), g in zip(chains, gs):
            g_last = g[last_row[dirn]:last_row[dirn] + 1]
            stores.append((dec_scr, (dirn, pl.ds(pl.multiple_of(n * 8, 8), 8), slice(None)),
                           jnp.broadcast_to(jnp.exp(g_last), (8, K_W))))
            if with_output:
                g_mid = g[mid_row[dirn]:mid_row[dirn] + 1]
                q_in = stack_heads((qs * jnp.exp(g - g_mid)).astype(BF16))
                k_in = (k * jnp.exp(g_mid - g)).astype(BF16)
                score_list.append(_dot_nt(q_in, k_in))
                stores.append((qst_scr, (dirn, rows, slice(None)), (qs * jnp.exp(g)).astype(BF16)))
            outers.append(_dot_tn(v, (k * jnp.exp(g_last - g)).astype(BF16)))
        for (n, rows, k, v, qs, _, dirn), outer in zip(chains, outers):
            upd = jnp.zeros((DV, K_W), F32)
            for h in range(HEADS):
                upd = jnp.where(head_mask[h], outer[h * DV:(h + 1) * DV], upd)
            stores.append((ds_scr, (dirn, pl.ds(pl.multiple_of(n * DV, DV), DV), slice(None)), upd))
        if with_output:
            for (n, rows, k, v, qs, _, dirn), sc in zip(chains, score_list):
                scores = jnp.where(score_mask[dirn], sc, 0.0).astype(BF16)
                for h in range(HEADS):
                    hs = slice(h * CHUNK, (h + 1) * CHUNK)
                    vs = slice(h * DV, (h + 1) * DV)
                    stores.append((o_scr, (dirn, rows, vs), _dot(scores[hs], v[:, vs])))
        for ref, idx, val in stores:
            ref[idx] = val
        return carry

    lax.fori_loop(0, n_chunks // per_iter, chunk_local, 0)

    def scan(i, carry):
        steps = ((0, i), (1, n_chunks - 1 - i))
        new_st, inters = [], []
        for dirn, n in steps:
            st = st_ref[dirn]
            if with_output:
                q_st = stack_heads(qst_scr[dirn, chunk_rows(n), :])
                inters.append(_dot_nt(q_st, st.astype(BF16)))
            dec = dec_scr[dirn, pl.ds(pl.multiple_of(n * 8, 8), 8), :]
            new_st.append(dec[0:1] * st + ds_scr[dirn, pl.ds(pl.multiple_of(n * DV, DV), DV), :])
        for (dirn, n), st in zip(steps, new_st):
            st_ref[dirn] = st
        if with_output:
            for (dirn, n), inter in zip(steps, inters):
                for h in range(HEADS):
                    vs = slice(h * DV, (h + 1) * DV)
                    o_scr[dirn, chunk_rows(n), vs] += inter[h * CHUNK:(h + 1) * CHUNK]
        return carry

    lax.fori_loop(0, n_chunks, scan, 0, unroll=2)

    if with_state:
        s_out_ref[0] = st_ref[...]

    if with_output:
        eb = min(seq, 256)

        def finish(i, carry):
            rows = pl.ds(pl.multiple_of(i * eb, eb), eb)
            o = o_scr[0, rows, :] + o_scr[1, rows, :]
            gate = _silu(r_ref[0, rows, :].astype(F32))
            for h in range(HEADS):
                vs = slice(h * DV, (h + 1) * DV)
                y_ref[0, rows, vs] = (_rms(o[:, vs], nw_ref[:, vs]) * gate[:, vs]).astype(BF16)
            return carry

        lax.fori_loop(0, seq // eb, finish, 0)


def _gla(p, wd, bd, s0, nw, with_output, with_state):
    b, l, _ = p["k"].shape
    seq3 = lambda bi: (bi, 0, 0)
    in_specs = [pl.BlockSpec((1, l, K_W), seq3), pl.BlockSpec((1, l, V_W), seq3),
                pl.BlockSpec((1, l, LR_PAD), seq3), _resident(wd.shape), _resident(bd.shape),
                pl.BlockSpec((1, 2, DV, K_W), lambda bi: (bi, 0, 0, 0))]
    args = [p["k"], p["v"], p["lr"], wd, bd, s0]
    out_specs, out_shape = [], []
    scratch = [pltpu.VMEM((2, DV, K_W), F32), pltpu.VMEM((l, 2 * K_W), BF16), pltpu.VMEM((l, 2 * K_W), BF16),
               pltpu.VMEM((2, l // CHUNK * DV, K_W), F32), pltpu.VMEM((2, l // CHUNK * 8, K_W), F32)]
    if with_output:
        in_specs += [pl.BlockSpec((1, l, K_W), seq3), pl.BlockSpec((1, l, V_W), seq3),
                     _resident((1, V_W))]
        args += [p["q"], p["r"], nw]
        out_specs.append(pl.BlockSpec((1, l, V_W), seq3))
        out_shape.append(jax.ShapeDtypeStruct((b, l, V_W), BF16))
        scratch += [pltpu.VMEM((2, l, V_W), F32), pltpu.VMEM((2, l, K_W), BF16)]
    if with_state:
        out_specs.append(pl.BlockSpec((1, 2, DV, K_W), lambda bi: (bi, 0, 0, 0)))
        out_shape.append(jax.ShapeDtypeStruct((b, 2, DV, K_W), F32))
    outs = pl.pallas_call(
        functools.partial(_gla_kernel, seq=l, with_output=with_output, with_state=with_state),
        grid=(b,),
        in_specs=in_specs,
        out_specs=out_specs,
        out_shape=out_shape,
        scratch_shapes=scratch,
        compiler_params=_params(("parallel",), 48),
        name="gla",
    )(*args)
    outs = list(outs)
    y = outs.pop(0) if with_output else None
    s = outs.pop(0) if with_state else None
    return y, s


def _four_conv_kernel(uf_ref, bg_ref, cg_ref, uc_ref, cw_ref, cb_ref, chan_ref, pos_ref,
                      yf_ref, yc_ref, ab_scr, *, seq):
    ab = _dot(uf_ref[0], chan_ref[...])
    ab_scr[0:seq, :] = ab[:, :FOUR_W].astype(BF16)
    ab_scr[seq:2 * seq, :] = ab[:, FOUR_W:].astype(BF16)
    rb = min(seq, 512)
    for i in range(seq // rb):
        yf_ref[0, i * rb:(i + 1) * rb, :] = _dot(pos_ref[i * rb:(i + 1) * rb, :], ab_scr[...]).astype(BF16)

    z = cg_ref[0].astype(F32) * uc_ref[0].astype(F32)
    t = lax.broadcasted_iota(jnp.int32, (seq, CONV_W), 0)
    z_prev = jnp.where(t == 0, 0.0, pltpu.roll(z, 1, 0))
    z_next = jnp.where(t == seq - 1, 0.0, pltpu.roll(z, seq - 1, 0))
    y = cw_ref[0:1] * z_prev + cw_ref[1:2] * z + cw_ref[2:3] * z_next + cb_ref[...]
    yc_ref[0] = (bg_ref[0].astype(F32) * y).astype(BF16)


def _four_conv(p, conv_w, conv_b, chan_tab, pos_tab):
    b, l, _ = p["uf"].shape
    seq3 = lambda bi: (bi, 0, 0)
    tok = pl.BlockSpec((1, l, FOUR_W), seq3)
    return pl.pallas_call(
        functools.partial(_four_conv_kernel, seq=l),
        grid=(b,),
        in_specs=[tok, tok, tok, tok, _resident(conv_w.shape), _resident(conv_b.shape),
                  _resident(chan_tab.shape), _resident(pos_tab.shape)],
        out_specs=[tok, tok],
        out_shape=[jax.ShapeDtypeStruct((b, l, FOUR_W), BF16)] * 2,
        scratch_shapes=[pltpu.VMEM((2 * l, FOUR_W), BF16)],
        compiler_params=_params(("parallel",), 48),
        name="four_conv",
    )(p["uf"], p["bg"], p["cg"], p["uc"], conv_w, conv_b, chan_tab, pos_tab)


_FFN_TILES = ((0, 1536), (1536, HIDDEN))


def _out_ffn_kernel(*refs, add_pos):
    it = iter(refs)
    yg_ref, yf_ref, yc_ref, wo_ref, x_ref = (next(it) for _ in range(5))
    pos_ref = next(it) if add_pos else None
    gmix_ref, gatemix_ref, gpre_ref, shift_ref, scale_ref, gate_ref, gpost_ref = (next(it) for _ in range(7))
    win_ref, wout_ref, o_ref = next(it), next(it), next(it)
    y = (_dot(yg_ref[0], wo_ref[0:V_W]) + _dot(yf_ref[0], wo_ref[V_W:V_W + FOUR_W])
         + _dot(yc_ref[0], wo_ref[V_W + FOUR_W:]))
    x = x_ref[0]
    if add_pos:
        x = x + pos_ref[...]
    x = x + gatemix_ref[0] * _rms(y, gmix_ref[...])
    h = (_rms(x, gpre_ref[...]) * (1.0 + scale_ref[0]) + shift_ref[0]).astype(BF16)
    y = jnp.zeros(x.shape, F32)
    for lo, hi in _FFN_TILES:
        a = _dot(h, win_ref[:, lo:hi])
        u = _dot(h, win_ref[:, HIDDEN + lo:HIDDEN + hi])
        y = y + _dot((_silu(a) * u).astype(BF16), wout_ref[lo:hi, :])
    o_ref[0] = x + gate_ref[0] * _rms(y, gpost_ref[...])


def _out_ffn(yg, yf, yc, wo, x, pos, gmix, gatemix, gpre, shift, scale, gate, gpost, w_in, w_out, tm):
    b, l, d = x.shape
    add_pos = pos is not None
    tok = lambda bi, ti: (bi, ti, 0)
    vec = pl.BlockSpec((1, 1, d), lambda bi, ti: (bi, 0, 0))
    in_specs = [pl.BlockSpec((1, tm, V_W), tok), pl.BlockSpec((1, tm, FOUR_W), tok),
                pl.BlockSpec((1, tm, CONV_W), tok), _resident(wo.shape), pl.BlockSpec((1, tm, d), tok)]
    args = [yg, yf, yc, wo, x]
    if add_pos:
        in_specs.append(pl.BlockSpec((tm, d), lambda bi, ti: (ti, 0)))
        args.append(pos)
    in_specs += [_resident((1, d)), vec, _resident((1, d)), vec, vec, vec, _resident((1, d)),
                 _resident(w_in.shape), _resident(w_out.shape)]
    args += [gmix, gatemix, gpre, shift, scale, gate, gpost, w_in, w_out]
    return pl.pallas_call(
        functools.partial(_out_ffn_kernel, add_pos=add_pos),
        grid=(b, l // tm),
        in_specs=in_specs,
        out_specs=pl.BlockSpec((1, tm, d), tok),
        out_shape=jax.ShapeDtypeStruct((b, l, d), F32),
        compiler_params=_params(("parallel", "parallel"), 56),
        name="out_ffn",
    )(*args)


def _arrange_w_in(w):
    main = jnp.concatenate([w[:, _C_K:_C_LR], w[:, _C_Q:_C_END]], axis=1)
    lr = jnp.pad(w[:, _C_LR:_C_Q], ((0, 0), (0, LR_PAD - 2 * RANK)))
    return jnp.concatenate([main, lr], axis=1).astype(BF16)


def _arrange_w_state(w_all):
    return jnp.concatenate([w_all[:, :K_W + V_W], w_all[:, _MAIN_W:]], axis=1)


def _arrange_w_dec(w_f, w_b):
    wd = jnp.zeros((LR_PAD, 2 * K_W), F32)
    wd = wd.at[0:RANK, :K_W].set(w_f).at[RANK:2 * RANK, K_W:].set(w_b)
    return wd.astype(BF16)


def kernel(x, c, ctx, c_ctx, w_mod, b_mod, g_mix_pre, g_mix_post, g_ffn_pre, g_ffn_post,
           w_in, w_dec_f, b_dec_f, w_dec_b, b_dec_b, gla_norm_w, conv_w, conv_b, w_out,
           w_ffn_in, w_ffn_out):
    b, n_lat, d = x.shape
    n_ctx = ctx.shape[1]
    assert d == D_MODEL and w_in.shape == (DEPTH, D_MODEL, _C_END)
    assert n_lat % 512 == 0 and n_ctx % CHUNK == 0 and b + 1 <= MOD_ROWS

    pos = jnp.asarray(_pos_embed(n_lat, d))
    tabs = {n_lat: _dft_tables(n_lat), n_ctx: _dft_tables(n_ctx)}

    cvec = jnp.zeros((MOD_ROWS, d), F32).at[:b].set(c).at[b].set(c_ctx)
    mod = _modulation(cvec, w_mod, b_mod)

    zero_state = jnp.zeros((b, 2, DV, K_W), F32)
    xc = ctx
    for i in range(DEPTH):
        last = i == DEPTH - 1
        m_lat = [mod[i, :b, j * d:(j + 1) * d].reshape(b, 1, d) for j in range(N_MOD)]
        m_ctx = [jnp.broadcast_to(mod[i, b, j * d:(j + 1) * d].reshape(1, 1, d), (b, 1, d))
                 for j in range(N_MOD)]
        w_all = _arrange_w_in(w_in[i])
        wd = _arrange_w_dec(w_dec_f[i], w_dec_b[i])
        bd = jnp.concatenate([b_dec_f[i], b_dec_b[i]]).reshape(1, 2 * K_W)
        nw = gla_norm_w[i].reshape(1, V_W)
        cb = conv_b[i].reshape(1, CONV_W)
        wo = w_out[i].astype(BF16)
        wfi = w_ffn_in[i].astype(BF16)
        wfo = w_ffn_out[i].astype(BF16)
        g_pre, g_post = g_mix_pre[i].reshape(1, d), g_mix_post[i].reshape(1, d)
        gf_pre, gf_post = g_ffn_pre[i].reshape(1, d), g_ffn_post[i].reshape(1, d)

        def mixer_and_ffn(xs, pos_s, m, s0, want_state, tm):
            p = _inproj(xs, pos_s, g_pre, m[0], m[1], w_all, _OUT_GROUPS_FULL, tm)
            yg, s = _gla(p, wd, bd, s0, nw, True, want_state)
            yf, yc = _four_conv(p, conv_w[i], cb, *reversed(tabs[xs.shape[1]]))
            xs = _out_ffn(yg, yf, yc, wo, xs, pos_s, g_post, m[2], gf_pre, m[3], m[4], m[5], gf_post,
                          wfi, wfo, tm)
            return xs, s

        if last:
            p = _inproj(xc, None, g_pre, m_ctx[0], m_ctx[1], _arrange_w_state(w_all),
                        _OUT_GROUPS_STATE, n_ctx)
            _, s = _gla(p, wd, bd, zero_state, None, False, True)
        else:
            xc, s = mixer_and_ffn(xc, None, m_ctx, zero_state, True, n_ctx)
        x, _ = mixer_and_ffn(x, pos if i == 0 else None, m_lat, s, False, 512)
    return x
```

```python
import functools

import numpy as np
import jax
import jax.numpy as jnp
from jax import lax
from jax.experimental import pallas as pl
from jax.experimental.pallas import tpu as pltpu

F32 = jnp.float32
BF16 = jnp.bfloat16

D_MODEL = 1024
DEPTH = 2
GRID_W = 64
N_MOD = 6
HEADS = 4
DK = 64
DV = 128
K_W = HEADS * DK
V_W = HEADS * DV
RANK = 16
TAU = 16.0
CHUNK = 64
FOUR_W = 256
FOUR_G = 4
CONV_W = 256
HIDDEN = 2816
EPS = 1e-6
LR_PAD = 128
MOD_ROWS = 16

_C_K, _C_V, _C_LR, _C_Q, _C_R, _C_FOUR, _C_BG, _C_CG, _C_UC, _C_END = (
    0, 256, 768, 800, 1056, 1568, 1824, 2080, 2336, 2592)
_MAIN_W = 2560
_IN_W = _MAIN_W + LR_PAD
_OUT_GROUPS_FULL = (("k", 0, 256), ("v", 256, 768), ("q", 768, 1024), ("r", 1024, 1536),
                    ("uf", 1536, 1792), ("bg", 1792, 2048), ("cg", 2048, 2304),
                    ("uc", 2304, 2560), ("lr", 2560, 2688))
_OUT_GROUPS_STATE = (("k", 0, 256), ("v", 256, 768), ("lr", 768, 896))

V7X_VMEM_BYTES = 64 * 1024 * 1024


def _params(semantics, vmem_mb):
    assert vmem_mb * 1024 * 1024 < V7X_VMEM_BYTES
    return pltpu.CompilerParams(dimension_semantics=semantics,
                                vmem_limit_bytes=vmem_mb * 1024 * 1024)


def _resident(shape):
    zeros = (0,) * len(shape)
    return pl.BlockSpec(shape, lambda *_: zeros, pipeline_mode=pl.Buffered(1))


def _silu(a):
    return a / (1.0 + jnp.exp(-a))


def _rms(x, g):
    return x * lax.rsqrt(jnp.mean(x * x, axis=-1, keepdims=True) + EPS) * g


def _dot(a, b):
    return jnp.dot(a, b, preferred_element_type=F32)


def _dot_nt(a, b):
    return lax.dot_general(a, b, (((1,), (1,)), ((), ())), preferred_element_type=F32)


def _dot_tn(a, b):
    return lax.dot_general(a, b, (((0,), (0,)), ((), ())), preferred_element_type=F32)


def _pos_embed(n_tokens, dim):
    rows = n_tokens // GRID_W
    row = np.repeat(np.arange(rows, dtype=np.float32), GRID_W)
    col = np.tile(np.arange(GRID_W, dtype=np.float32), rows)
    quarter = dim // 4
    freqs = (1.0 / (10000.0 ** (np.arange(quarter, dtype=np.float32) / quarter))).astype(np.float32)

    def enc(p):
        ang = (p[:, None] * freqs[None, :]).astype(np.float32)
        return np.concatenate([np.sin(ang), np.cos(ang)], axis=-1)

    return np.concatenate([enc(row), enc(col)], axis=-1).astype(np.float32)


def _dft_cos_sin(n):
    idx = np.arange(n, dtype=np.int64)
    ang = 2.0 * np.pi * ((idx[:, None] * idx[None, :]) % n).astype(np.float64) / n
    return np.cos(ang) / np.sqrt(n), np.sin(ang) / np.sqrt(n)


def _dft_tables(seq):
    cl, sl = _dft_cos_sin(seq)
    pos_tab = np.concatenate([cl, -sl], axis=1)
    gw = FOUR_W // FOUR_G
    cc, sc = _dft_cos_sin(gw)
    eye = np.eye(FOUR_G)
    chan_tab = np.concatenate([np.kron(eye, cc), np.kron(eye, sc)], axis=1)
    return (jnp.asarray(pos_tab, dtype=F32).astype(BF16), jnp.asarray(chan_tab, dtype=F32).astype(BF16))


def _mod_kernel(c_ref, w_ref, b_ref, o_ref):
    s = _silu(c_ref[...]).astype(BF16)
    o_ref[0] = _dot(s, w_ref[0].astype(BF16)) + b_ref[0]


def _modulation(cvec, w_mod, b_mod):
    tn = 1536
    n = N_MOD * D_MODEL
    return pl.pallas_call(
        _mod_kernel,
        grid=(DEPTH, n // tn),
        in_specs=[pl.BlockSpec((MOD_ROWS, D_MODEL), lambda i, j: (0, 0)),
                  pl.BlockSpec((1, D_MODEL, tn), lambda i, j: (i, 0, j)),
                  pl.BlockSpec((1, 1, tn), lambda i, j: (i, 0, j))],
        out_specs=pl.BlockSpec((1, MOD_ROWS, tn), lambda i, j: (i, 0, j)),
        out_shape=jax.ShapeDtypeStruct((DEPTH, MOD_ROWS, n), F32),
        compiler_params=_params(("arbitrary", "arbitrary"), 40),
        name="modulation",
    )(cvec, w_mod, b_mod.reshape(DEPTH, 1, n))


def _inproj_kernel(*refs, add_pos, groups):
    it = iter(refs)
    x_ref = next(it)
    pos_ref = next(it) if add_pos else None
    g_ref, shift_ref, scale_ref, w_ref = next(it), next(it), next(it), next(it)
    out_refs = list(it)
    x = x_ref[0]
    if add_pos:
        x = x + pos_ref[...]
    h = (_rms(x, g_ref[...]) * (1.0 + scale_ref[0]) + shift_ref[0]).astype(BF16)
    for o_ref, (_, lo, hi) in zip(out_refs, groups):
        o_ref[0] = _dot(h, w_ref[:, lo:hi]).astype(BF16)


def _inproj(x, pos, g, shift, scale, w, groups, tm):
    b, l, d = x.shape
    add_pos = pos is not None
    tok = lambda bi, ti: (bi, ti, 0)
    vec = lambda bi, ti: (bi, 0, 0)
    in_specs = [pl.BlockSpec((1, tm, d), tok)]
    args = [x]
    if add_pos:
        in_specs.append(pl.BlockSpec((tm, d), lambda bi, ti: (ti, 0)))
        args.append(pos)
    in_specs += [_resident((1, d)), pl.BlockSpec((1, 1, d), vec), pl.BlockSpec((1, 1, d), vec),
                 _resident(w.shape)]
    args += [g, shift, scale, w]
    outs = pl.pallas_call(
        functools.partial(_inproj_kernel, add_pos=add_pos, groups=groups),
        grid=(b, l // tm),
        in_specs=in_specs,
        out_specs=[pl.BlockSpec((1, tm, hi - lo), tok) for _, lo, hi in groups],
        out_shape=[jax.ShapeDtypeStruct((b, l, hi - lo), BF16) for _, lo, hi in groups],
        compiler_params=_params(("parallel", "parallel"), 48),
        name="inproj",
    )(*args)
    return {name: o for (name, _, _), o in zip(groups, outs)}


def _gla_kernel(*refs, seq, with_output, with_state):
    it = iter(refs)
    k_ref, v_ref, lr_ref, wd_ref, bd_ref, s0_ref = (next(it) for _ in range(6))
    if with_output:
        q_ref, r_ref, nw_ref = next(it), next(it), next(it)
        y_ref = next(it)
    if with_state:
        s_out_ref = next(it)
    st_ref = next(it)
    lg_hi, lg_lo = next(it), next(it)
    ds_scr = next(it)
    dec_scr = next(it)
    if with_output:
        o_scr = next(it)
        qst_scr = next(it)
    n_chunks = seq // CHUNK

    st_ref[...] = s0_ref[0]

    row = lax.broadcasted_iota(jnp.int32, (CHUNK, CHUNK), 0)
    col = lax.broadcasted_iota(jnp.int32, (CHUNK, CHUNK), 1)
    cum = (jnp.where(row >= col, 1.0, 0.0).astype(BF16), jnp.where(row <= col, 1.0, 0.0).astype(BF16))
    lane_head = lax.broadcasted_iota(jnp.int32, (1, K_W), 1) // DK
    head_mask = [lane_head == h for h in range(HEADS)]
    srow = lax.broadcasted_iota(jnp.int32, (HEADS * CHUNK, CHUNK), 0) % CHUNK
    scol = lax.broadcasted_iota(jnp.int32, (HEADS * CHUNK, CHUNK), 1)
    score_mask = (srow >= scol, srow <= scol)
    last_row = (CHUNK - 1, 0)
    mid_row = (CHUNK // 2 - 1, CHUNK // 2)

    def stack_heads(a):
        zero = jnp.zeros_like(a)
        return jnp.concatenate([jnp.where(head_mask[h], a, zero) for h in range(HEADS)], axis=0)

    def chunk_rows(n):
        return pl.ds(pl.multiple_of(n * CHUNK, CHUNK), CHUNK)

    gb = min(seq, 512)

    def gates(i, carry):
        rows = pl.ds(pl.multiple_of(i * gb, gb), gb)
        z = _dot(lr_ref[0, rows, :], wd_ref[...]) + bd_ref[...]
        logg = (jnp.minimum(z, 0.0) - jnp.log(1.0 + jnp.exp(-jnp.abs(z)))) * (1.0 / TAU)
        hi = logg.astype(BF16)
        lg_hi[rows, :] = hi
        lg_lo[rows, :] = (logg - hi.astype(F32)).astype(BF16)
        return carry

    lax.fori_loop(0, seq // gb, gates, 0)

    per_iter = 4

    def chunk_local(i, carry):
        loaded = []
        for u in range(per_iter):
            n = i * per_iter + u
            rows = chunk_rows(n)
            k = k_ref[0, rows, :].astype(F32)
            v = v_ref[0, rows, :]
            qs = q_ref[0, rows, :].astype(F32) * (DK ** -0.5) if with_output else None
            lg = [(lg_hi[rows, d * K_W:(d + 1) * K_W], lg_lo[rows, d * K_W:(d + 1) * K_W]) for d in range(2)]
            loaded.append((n, rows, k, v, qs, lg))
        chains = [(n, rows, k, v, qs, lg[dirn], dirn) for n, rows, k, v, qs, lg in loaded for dirn in range(2)]
        stores = []
        gs = [_dot(cum[dirn], hi) + _dot(cum[dirn], lo) for *_, (hi, lo), dirn in chains]
        outers, score_list = [], []
        for (n, rows, k, v, qs, _, dirn), g in zip(chains, gs):
            g_last = g[last_row[dirn]:last_row[dirn] + 1]
            stores.append((dec_scr, (dirn, pl.ds(pl.multiple_of(n * 8, 8), 8), slice(None)),
                           jnp.broadcast_to(jnp.exp(g_last), (8, K_W))))
            if with_output:
                g_mid = g[mid_row[dirn]:mid_row[dirn] + 1]
                q_in = stack_heads((qs * jnp.exp(g - g_mid)).astype(BF16))
                k_in = (k * jnp.exp(g_mid - g)).astype(BF16)
                score_list.append(_dot_nt(q_in, k_in))
                stores.append((qst_scr, (dirn, rows, slice(None)), (qs * jnp.exp(g)).astype(BF16)))
            outers.append(_dot_tn(v, (k * jnp.exp(g_last - g)).astype(BF16)))
        for (n, rows, k, v, qs, _, dirn), outer in zip(chains, outers):
            upd = jnp.zeros((DV, K_W), F32)
            for h in range(HEADS):
                upd = jnp.where(head_mask[h], outer[h * DV:(h + 1) * DV], upd)
            stores.append((ds_scr, (dirn, pl.ds(pl.multiple_of(n * DV, DV), DV), slice(None)), upd))
        if with_output:
            for (n, rows, k, v, qs, _, dirn), sc in zip(chains, score_list):
                scores = jnp.where(score_mask[dirn], sc, 0.0).astype(BF16)
                for h in range(HEADS):
                    hs = slice(h * CHUNK, (h + 1) * CHUNK)
                    vs = slice(h * DV, (h + 1) * DV)
                    stores.append((o_scr, (dirn, rows, vs), _dot(scores[hs], v[:, vs])))
        for ref, idx, val in stores:
            ref[idx] = val
        return carry

    lax.fori_loop(0, n_chunks // per_iter, chunk_local, 0)

    def scan(i, carry):
        steps = ((0, i), (1, n_chunks - 1 - i))
        new_st, inters = [], []
        for dirn, n in steps:
            st = st_ref[dirn]
            if with_output:
                q_st = stack_heads(qst_scr[dirn, chunk_rows(n), :])
                inters.append(_dot_nt(q_st, st.astype(BF16)))
            dec = dec_scr[dirn, pl.ds(pl.multiple_of(n * 8, 8), 8), :]
            new_st.append(dec[0:1] * st + ds_scr[dirn, pl.ds(pl.multiple_of(n * DV, DV), DV), :])
        for (dirn, n), st in zip(steps, new_st):
            st_ref[dirn] = st
        if with_output:
            for (dirn, n), inter in zip(steps, inters):
                for h in range(HEADS):
                    vs = slice(h * DV, (h + 1) * DV)
                    o_scr[dirn, chunk_rows(n), vs] += inter[h * CHUNK:(h + 1) * CHUNK]
        return carry

    lax.fori_loop(0, n_chunks, scan, 0, unroll=2)

    if with_state:
        s_out_ref[0] = st_ref[...]

    if with_output:
        eb = min(seq, 256)

        def finish(i, carry):
            rows = pl.ds(pl.multiple_of(i * eb, eb), eb)
            o = o_scr[0, rows, :] + o_scr[1, rows, :]
            gate = _silu(r_ref[0, rows, :].astype(F32))
            for h in range(HEADS):
                vs = slice(h * DV, (h + 1) * DV)
                y_ref[0, rows, vs] = (_rms(o[:, vs], nw_ref[:, vs]) * gate[:, vs]).astype(BF16)
            return carry

        lax.fori_loop(0, seq // eb, finish, 0)


def _gla(p, wd, bd, s0, nw, with_output, with_state):
    b, l, _ = p["k"].shape
    seq3 = lambda bi: (bi, 0, 0)
    in_specs = [pl.BlockSpec((1, l, K_W), seq3), pl.BlockSpec((1, l, V_W), seq3),
                pl.BlockSpec((1, l, LR_PAD), seq3), _resident(wd.shape), _resident(bd.shape),
                pl.BlockSpec((1, 2, DV, K_W), lambda bi: (bi, 0, 0, 0))]
    args = [p["k"], p["v"], p["lr"], wd, bd, s0]
    out_specs, out_shape = [], []
    scratch = [pltpu.VMEM((2, DV, K_W), F32), pltpu.VMEM((l, 2 * K_W), BF16), pltpu.VMEM((l, 2 * K_W), BF16),
               pltpu.VMEM((2, l // CHUNK * DV, K_W), F32), pltpu.VMEM((2, l // CHUNK * 8, K_W), F32)]
    if with_output:
        in_specs += [pl.BlockSpec((1, l, K_W), seq3), pl.BlockSpec((1, l, V_W), seq3),
                     _resident((1, V_W))]
        args += [p["q"], p["r"], nw]
        out_specs.append(pl.BlockSpec((1, l, V_W), seq3))
        out_shape.append(jax.ShapeDtypeStruct((b, l, V_W), BF16))
        scratch += [pltpu.VMEM((2, l, V_W), F32), pltpu.VMEM((2, l, K_W), BF16)]
    if with_state:
        out_specs.append(pl.BlockSpec((1, 2, DV, K_W), lambda bi: (bi, 0, 0, 0)))
        out_shape.append(jax.ShapeDtypeStruct((b, 2, DV, K_W), F32))
    outs = pl.pallas_call(
        functools.partial(_gla_kernel, seq=l, with_output=with_output, with_state=with_state),
        grid=(b,),
        in_specs=in_specs,
        out_specs=out_specs,
        out_shape=out_shape,
        scratch_shapes=scratch,
        compiler_params=_params(("parallel",), 48),
        name="gla",
    )(*args)
    outs = list(outs)
    y = outs.pop(0) if with_output else None
    s = outs.pop(0) if with_state else None
    return y, s


def _four_conv_kernel(uf_ref, bg_ref, cg_ref, uc_ref, cw_ref, cb_ref, chan_ref, pos_ref,
                      yf_ref, yc_ref, ab_scr, *, seq):
    ab = _dot(uf_ref[0], chan_ref[...])
    ab_scr[0:seq, :] = ab[:, :FOUR_W].astype(BF16)
    ab_scr[seq:2 * seq, :] = ab[:, FOUR_W:].astype(BF16)
    rb = min(seq, 512)
    for i in range(seq // rb):
        yf_ref[0, i * rb:(i + 1) * rb, :] = _dot(pos_ref[i * rb:(i + 1) * rb, :], ab_scr[...]).astype(BF16)

    z = cg_ref[0].astype(F32) * uc_ref[0].astype(F32)
    t = lax.broadcasted_iota(jnp.int32, (seq, CONV_W), 0)
    z_prev = jnp.where(t == 0, 0.0, pltpu.roll(z, 1, 0))
    z_next = jnp.where(t == seq - 1, 0.0, pltpu.roll(z, seq - 1, 0))
    y = cw_ref[0:1] * z_prev + cw_ref[1:2] * z + cw_ref[2:3] * z_next + cb_ref[...]
    yc_ref[0] = (bg_ref[0].astype(F32) * y).astype(BF16)


def _four_conv(p, conv_w, conv_b, chan_tab, pos_tab):
    b, l, _ = p["uf"].shape
    seq3 = lambda bi: (bi, 0, 0)
    tok = pl.BlockSpec((1, l, FOUR_W), seq3)
    return pl.pallas_call(
        functools.partial(_four_conv_kernel, seq=l),
        grid=(b,),
        in_specs=[tok, tok, tok, tok, _resident(conv_w.shape), _resident(conv_b.shape),
                  _resident(chan_tab.shape), _resident(pos_tab.shape)],
        out_specs=[tok, tok],
        out_shape=[jax.ShapeDtypeStruct((b, l, FOUR_W), BF16)] * 2,
        scratch_shapes=[pltpu.VMEM((2 * l, FOUR_W), BF16)],
        compiler_params=_params(("parallel",), 48),
        name="four_conv",
    )(p["uf"], p["bg"], p["cg"], p["uc"], conv_w, conv_b, chan_tab, pos_tab)


_FFN_TILES = ((0, 1536), (1536, HIDDEN))


def _out_ffn_kernel(*refs, add_pos, n_sub):
    it = iter(refs)
    yg_ref, yf_ref, yc_ref, wo_ref, x_ref = (next(it) for _ in range(5))
    pos_ref = next(it) if add_pos else None
    gmix_ref, gatemix_ref, gpre_ref, shift_ref, scale_ref, gate_ref, gpost_ref = (next(it) for _ in range(7))
    win_ref, wout_ref, o_ref = next(it), next(it), next(it)
    tm = x_ref.shape[1]
    subs = [slice(i * tm // n_sub, (i + 1) * tm // n_sub) for i in range(n_sub)]
    ys = [(_dot(yg_ref[0, r, :], wo_ref[0:V_W]) + _dot(yf_ref[0, r, :], wo_ref[V_W:V_W + FOUR_W])
           + _dot(yc_ref[0, r, :], wo_ref[V_W + FOUR_W:])) for r in subs]
    xs, hs = [], []
    for r, y in zip(subs, ys):
        x = x_ref[0, r, :]
        if add_pos:
            x = x + pos_ref[r, :]
        x = x + gatemix_ref[0] * _rms(y, gmix_ref[...])
        xs.append(x)
        hs.append((_rms(x, gpre_ref[...]) * (1.0 + scale_ref[0]) + shift_ref[0]).astype(BF16))
    ys = []
    for h in hs:
        y = jnp.zeros((h.shape[0], D_MODEL), F32)
        for lo, hi in _FFN_TILES:
            a = _dot(h, win_ref[:, lo:hi])
            u = _dot(h, win_ref[:, HIDDEN + lo:HIDDEN + hi])
            y = y + _dot((_silu(a) * u).astype(BF16), wout_ref[lo:hi, :])
        ys.append(y)
    for r, x, y in zip(subs, xs, ys):
        o_ref[0, r, :] = x + gate_ref[0] * _rms(y, gpost_ref[...])


def _out_ffn(yg, yf, yc, wo, x, pos, gmix, gatemix, gpre, shift, scale, gate, gpost, w_in, w_out, tm):
    b, l, d = x.shape
    add_pos = pos is not None
    tok = lambda bi, ti: (bi, ti, 0)
    vec = pl.BlockSpec((1, 1, d), lambda bi, ti: (bi, 0, 0))
    in_specs = [pl.BlockSpec((1, tm, V_W), tok), pl.BlockSpec((1, tm, FOUR_W), tok),
                pl.BlockSpec((1, tm, CONV_W), tok), _resident(wo.shape), pl.BlockSpec((1, tm, d), tok)]
    args = [yg, yf, yc, wo, x]
    if add_pos:
        in_specs.append(pl.BlockSpec((tm, d), lambda bi, ti: (ti, 0)))
        args.append(pos)
    in_specs += [_resident((1, d)), vec, _resident((1, d)), vec, vec, vec, _resident((1, d)),
                 _resident(w_in.shape), _resident(w_out.shape)]
    args += [gmix, gatemix, gpre, shift, scale, gate, gpost, w_in, w_out]
    return pl.pallas_call(
        functools.partial(_out_ffn_kernel, add_pos=add_pos, n_sub=2 if tm >= 512 else 1),
        grid=(b, l // tm),
        in_specs=in_specs,
        out_specs=pl.BlockSpec((1, tm, d), tok),
        out_shape=jax.ShapeDtypeStruct((b, l, d), F32),
        compiler_params=_params(("parallel", "parallel"), 56),
        name="out_ffn",
    )(*args)


def _arrange_w_in(w):
    main = jnp.concatenate([w[:, _C_K:_C_LR], w[:, _C_Q:_C_END]], axis=1)
    lr = jnp.pad(w[:, _C_LR:_C_Q], ((0, 0), (0, LR_PAD - 2 * RANK)))
    return jnp.concatenate([main, lr], axis=1).astype(BF16)


def _arrange_w_state(w_all):
    return jnp.concatenate([w_all[:, :K_W + V_W], w_all[:, _MAIN_W:]], axis=1)


def _arrange_w_dec(w_f, w_b):
    wd = jnp.zeros((LR_PAD, 2 * K_W), F32)
    wd = wd.at[0:RANK, :K_W].set(w_f).at[RANK:2 * RANK, K_W:].set(w_b)
    return wd.astype(BF16)


def kernel(x, c, ctx, c_ctx, w_mod, b_mod, g_mix_pre, g_mix_post, g_ffn_pre, g_ffn_post,
           w_in, w_dec_f, b_dec_f, w_dec_b, b_dec_b, gla_norm_w, conv_w, conv_b, w_out,
           w_ffn_in, w_ffn_out):
    b, n_lat, d = x.shape
    n_ctx = ctx.shape[1]
    assert d == D_MODEL and w_in.shape == (DEPTH, D_MODEL, _C_END)
    assert n_lat % 512 == 0 and n_ctx % CHUNK == 0 and b + 1 <= MOD_ROWS

    pos = jnp.asarray(_pos_embed(n_lat, d))
    tabs = {n_lat: _dft_tables(n_lat), n_ctx: _dft_tables(n_ctx)}

    cvec = jnp.zeros((MOD_ROWS, d), F32).at[:b].set(c).at[b].set(c_ctx)
    mod = _modulation(cvec, w_mod, b_mod)

    zero_state = jnp.zeros((b, 2, DV, K_W), F32)
    xc = ctx
    for i in range(DEPTH):
        last = i == DEPTH - 1
        m_lat = [mod[i, :b, j * d:(j + 1) * d].reshape(b, 1, d) for j in range(N_MOD)]
        m_ctx = [jnp.broadcast_to(mod[i, b, j * d:(j + 1) * d].reshape(1, 1, d), (b, 1, d))
                 for j in range(N_MOD)]
        w_all = _arrange_w_in(w_in[i])
        wd = _arrange_w_dec(w_dec_f[i], w_dec_b[i])
        bd = jnp.concatenate([b_dec_f[i], b_dec_b[i]]).reshape(1, 2 * K_W)
        nw = gla_norm_w[i].reshape(1, V_W)
        cb = conv_b[i].reshape(1, CONV_W)
        wo = w_out[i].astype(BF16)
        wfi = w_ffn_in[i].astype(BF16)
        wfo = w_ffn_out[i].astype(BF16)
        g_pre, g_post = g_mix_pre[i].reshape(1, d), g_mix_post[i].reshape(1, d)
        gf_pre, gf_post = g_ffn_pre[i].reshape(1, d), g_ffn_post[i].reshape(1, d)

        def mixer_and_ffn(xs, pos_s, m, s0, want_state, tm):
            p = _inproj(xs, pos_s, g_pre, m[0], m[1], w_all, _OUT_GROUPS_FULL, tm)
            yg, s = _gla(p, wd, bd, s0, nw, True, want_state)
            yf, yc = _four_conv(p, conv_w[i], cb, *reversed(tabs[xs.shape[1]]))
            xs = _out_ffn(yg, yf, yc, wo, xs, pos_s, g_post, m[2], gf_pre, m[3], m[4], m[5], gf_post,
                          wfi, wfo, tm)
            return xs, s

        if last:
            p = _inproj(xc, None, g_pre, m_ctx[0], m_ctx[1], _arrange_w_state(w_all),
                        _OUT_GROUPS_STATE, n_ctx)
            _, s = _gla(p, wd, bd, zero_state, None, False, True)
        else:
            xc, s = mixer_and_ffn(xc, None, m_ctx, zero_state, True, n_ctx)
        x, _ = mixer_and_ffn(x, pos if i == 0 else None, m_lat, s, False, 512)
    return x
```

```python
import functools

import numpy as np
import jax
import jax.numpy as jnp
from jax import lax
from jax.experimental import pallas as pl
from jax.experimental.pallas import tpu as pltpu

F32 = jnp.float32
BF16 = jnp.bfloat16

D_MODEL = 1024
DEPTH = 2
GRID_W = 64
N_MOD = 6
HEADS = 4
DK = 64
DV = 128
K_W = HEADS * DK
V_W = HEADS * DV
RANK = 16
TAU = 16.0
CHUNK = 64
FOUR_W = 256
FOUR_G = 4
CONV_W = 256
HIDDEN = 2816
EPS = 1e-6
LR_PAD = 128
MOD_ROWS = 16

_C_K, _C_V, _C_LR, _C_Q, _C_R, _C_FOUR, _C_BG, _C_CG, _C_UC, _C_END = (
    0, 256, 768, 800, 1056, 1568, 1824, 2080, 2336, 2592)
_MAIN_W = 2560
_IN_W = _MAIN_W + LR_PAD
_OUT_GROUPS_FULL = (("k", 0, 256), ("v", 256, 768), ("q", 768, 1024), ("r", 1024, 1536),
                    ("uf", 1536, 1792), ("bg", 1792, 2048), ("cg", 2048, 2304),
                    ("uc", 2304, 2560), ("lr", 2560, 2688))
_OUT_GROUPS_STATE = (("k", 0, 256), ("v", 256, 768), ("lr", 768, 896))

V7X_VMEM_BYTES = 64 * 1024 * 1024


def _params(semantics, vmem_mb):
    assert vmem_mb * 1024 * 1024 < V7X_VMEM_BYTES
    return pltpu.CompilerParams(dimension_semantics=semantics,
                                vmem_limit_bytes=vmem_mb * 1024 * 1024)


def _resident(shape):
    zeros = (0,) * len(shape)
    return pl.BlockSpec(shape, lambda *_: zeros, pipeline_mode=pl.Buffered(1))


def _silu(a):
    return a / (1.0 + jnp.exp(-a))


def _rms(x, g):
    return x * lax.rsqrt(jnp.mean(x * x, axis=-1, keepdims=True) + EPS) * g


def _dot(a, b):
    return jnp.dot(a, b, preferred_element_type=F32)


def _dot_nt(a, b):
    return lax.dot_general(a, b, (((1,), (1,)), ((), ())), preferred_element_type=F32)


def _dot_tn(a, b):
    return lax.dot_general(a, b, (((0,), (0,)), ((), ())), preferred_element_type=F32)


def _pos_embed(n_tokens, dim):
    rows = n_tokens // GRID_W
    row = np.repeat(np.arange(rows, dtype=np.float32), GRID_W)
    col = np.tile(np.arange(GRID_W, dtype=np.float32), rows)
    quarter = dim // 4
    freqs = (1.0 / (10000.0 ** (np.arange(quarter, dtype=np.float32) / quarter))).astype(np.float32)

    def enc(p):
        ang = (p[:, None] * freqs[None, :]).astype(np.float32)
        return np.concatenate([np.sin(ang), np.cos(ang)], axis=-1)

    return np.concatenate([enc(row), enc(col)], axis=-1).astype(np.float32)


def _dft_cos_sin(n):
    idx = np.arange(n, dtype=np.int64)
    ang = 2.0 * np.pi * ((idx[:, None] * idx[None, :]) % n).astype(np.float64) / n
    return np.cos(ang) / np.sqrt(n), np.sin(ang) / np.sqrt(n)


def _dft_tables(seq):
    cl, sl = _dft_cos_sin(seq)
    pos_tab = np.concatenate([cl, -sl], axis=1)
    gw = FOUR_W // FOUR_G
    cc, sc = _dft_cos_sin(gw)
    eye = np.eye(FOUR_G)
    chan_tab = np.concatenate([np.kron(eye, cc), np.kron(eye, sc)], axis=1)
    return (jnp.asarray(pos_tab, dtype=F32).astype(BF16), jnp.asarray(chan_tab, dtype=F32).astype(BF16))


def _mod_kernel(c_ref, w_ref, b_ref, o_ref):
    s = _silu(c_ref[...]).astype(BF16)
    o_ref[0] = _dot(s, w_ref[0].astype(BF16)) + b_ref[0]


def _modulation(cvec, w_mod, b_mod):
    tn = 1536
    n = N_MOD * D_MODEL
    return pl.pallas_call(
        _mod_kernel,
        grid=(DEPTH, n // tn),
        in_specs=[pl.BlockSpec((MOD_ROWS, D_MODEL), lambda i, j: (0, 0)),
                  pl.BlockSpec((1, D_MODEL, tn), lambda i, j: (i, 0, j)),
                  pl.BlockSpec((1, 1, tn), lambda i, j: (i, 0, j))],
        out_specs=pl.BlockSpec((1, MOD_ROWS, tn), lambda i, j: (i, 0, j)),
        out_shape=jax.ShapeDtypeStruct((DEPTH, MOD_ROWS, n), F32),
        compiler_params=_params(("arbitrary", "arbitrary"), 40),
        name="modulation",
    )(cvec, w_mod, b_mod.reshape(DEPTH, 1, n))


def _inproj_kernel(*refs, add_pos, groups):
    it = iter(refs)
    x_ref = next(it)
    pos_ref = next(it) if add_pos else None
    g_ref, shift_ref, scale_ref, w_ref = next(it), next(it), next(it), next(it)
    out_refs = list(it)
    tm = x_ref.shape[1]
    n_sub = 2 if tm >= 512 else 1
    subs = [slice(i * tm // n_sub, (i + 1) * tm // n_sub) for i in range(n_sub)]
    hs = []
    for r in subs:
        x = x_ref[0, r, :]
        if add_pos:
            x = x + pos_ref[r, :]
        hs.append((_rms(x, g_ref[...]) * (1.0 + scale_ref[0]) + shift_ref[0]).astype(BF16))
    for r, h in zip(subs, hs):
        for o_ref, (_, lo, hi) in zip(out_refs, groups):
            o_ref[0, r, :] = _dot(h, w_ref[:, lo:hi]).astype(BF16)


def _inproj(x, pos, g, shift, scale, w, groups, tm):
    b, l, d = x.shape
    add_pos = pos is not None
    tok = lambda bi, ti: (bi, ti, 0)
    vec = lambda bi, ti: (bi, 0, 0)
    in_specs = [pl.BlockSpec((1, tm, d), tok)]
    args = [x]
    if add_pos:
        in_specs.append(pl.BlockSpec((tm, d), lambda bi, ti: (ti, 0)))
        args.append(pos)
    in_specs += [_resident((1, d)), pl.BlockSpec((1, 1, d), vec), pl.BlockSpec((1, 1, d), vec),
                 _resident(w.shape)]
    args += [g, shift, scale, w]
    outs = pl.pallas_call(
        functools.partial(_inproj_kernel, add_pos=add_pos, groups=groups),
        grid=(b, l // tm),
        in_specs=in_specs,
        out_specs=[pl.BlockSpec((1, tm, hi - lo), tok) for _, lo, hi in groups],
        out_shape=[jax.ShapeDtypeStruct((b, l, hi - lo), BF16) for _, lo, hi in groups],
        compiler_params=_params(("parallel", "parallel"), 48),
        name="inproj",
    )(*args)
    return {name: o for (name, _, _), o in zip(groups, outs)}


def _gla_kernel(*refs, seq, with_output, with_state):
    it = iter(refs)
    k_ref, v_ref, lr_ref, wd_ref, bd_ref, s0_ref = (next(it) for _ in range(6))
    if with_output:
        q_ref, r_ref, nw_ref = next(it), next(it), next(it)
        y_ref = next(it)
    if with_state:
        s_out_ref = next(it)
    st_ref = next(it)
    lg_hi, lg_lo = next(it), next(it)
    ds_scr = next(it)
    dec_scr = next(it)
    if with_output:
        o_scr = next(it)
        qst_scr = next(it)
    n_chunks = seq // CHUNK

    st_ref[...] = s0_ref[0]

    row = lax.broadcasted_iota(jnp.int32, (CHUNK, CHUNK), 0)
    col = lax.broadcasted_iota(jnp.int32, (CHUNK, CHUNK), 1)
    cum = (jnp.where(row >= col, 1.0, 0.0).astype(BF16), jnp.where(row <= col, 1.0, 0.0).astype(BF16))
    lane_head = lax.broadcasted_iota(jnp.int32, (1, K_W), 1) // DK
    head_mask = [lane_head == h for h in range(HEADS)]
    srow = lax.broadcasted_iota(jnp.int32, (HEADS * CHUNK, CHUNK), 0) % CHUNK
    scol = lax.broadcasted_iota(jnp.int32, (HEADS * CHUNK, CHUNK), 1)
    score_mask = (srow >= scol, srow <= scol)
    last_row = (CHUNK - 1, 0)
    mid_row = (CHUNK // 2 - 1, CHUNK // 2)

    def stack_heads(a):
        zero = jnp.zeros_like(a)
        return jnp.concatenate([jnp.where(head_mask[h], a, zero) for h in range(HEADS)], axis=0)

    def chunk_rows(n):
        return pl.ds(pl.multiple_of(n * CHUNK, CHUNK), CHUNK)

    gb = min(seq, 512)

    def gates(i, carry):
        rows = pl.ds(pl.multiple_of(i * gb, gb), gb)
        z = _dot(lr_ref[0, rows, :], wd_ref[...]) + bd_ref[...]
        logg = (jnp.minimum(z, 0.0) - jnp.log(1.0 + jnp.exp(-jnp.abs(z)))) * (1.0 / TAU)
        hi = logg.astype(BF16)
        lg_hi[rows, :] = hi
        lg_lo[rows, :] = (logg - hi.astype(F32)).astype(BF16)
        return carry

    lax.fori_loop(0, seq // gb, gates, 0)

    per_iter = 4

    def chunk_local(i, carry):
        loaded = []
        for u in range(per_iter):
            n = i * per_iter + u
            rows = chunk_rows(n)
            k = k_ref[0, rows, :].astype(F32)
            v = v_ref[0, rows, :]
            qs = q_ref[0, rows, :].astype(F32) * (DK ** -0.5) if with_output else None
            lg = [(lg_hi[rows, d * K_W:(d + 1) * K_W], lg_lo[rows, d * K_W:(d + 1) * K_W]) for d in range(2)]
            loaded.append((n, rows, k, v, qs, lg))
        chains = [(n, rows, k, v, qs, lg[dirn], dirn) for n, rows, k, v, qs, lg in loaded for dirn in range(2)]
        stores = []
        gs = [_dot(cum[dirn], hi) + _dot(cum[dirn], lo) for *_, (hi, lo), dirn in chains]
        outers, score_list = [], []
        for (n, rows, k, v, qs, _, dirn), g in zip(chains, gs):
            g_last = g[last_row[dirn]:last_row[dirn] + 1]
            stores.append((dec_scr, (dirn, pl.ds(pl.multiple_of(n * 8, 8), 8), slice(None)),
                           jnp.broadcast_to(jnp.exp(g_last), (8, K_W))))
            if with_output:
                g_mid = g[mid_row[dirn]:mid_row[dirn] + 1]
                q_in = stack_heads((qs * jnp.exp(g - g_mid)).astype(BF16))
                k_in = (k * jnp.exp(g_mid - g)).astype(BF16)
                score_list.append(_dot_nt(q_in, k_in))
                stores.append((qst_scr, (dirn, rows, slice(None)), (qs * jnp.exp(g)).astype(BF16)))
            outers.append(_dot_tn(v, (k * jnp.exp(g_last - g)).astype(BF16)))
        for (n, rows, k, v, qs, _, dirn), outer in zip(chains, outers):
            upd = jnp.zeros((DV, K_W), F32)
            for h in range(HEADS):
                upd = jnp.where(head_mask[h], outer[h * DV:(h + 1) * DV], upd)
            stores.append((ds_scr, (dirn, pl.ds(pl.multiple_of(n * DV, DV), DV), slice(None)), upd))
        if with_output:
            for (n, rows, k, v, qs, _, dirn), sc in zip(chains, score_list):
                scores = jnp.where(score_mask[dirn], sc, 0.0).astype(BF16)
                for h in range(HEADS):
                    hs = slice(h * CHUNK, (h + 1) * CHUNK)
                    vs = slice(h * DV, (h + 1) * DV)
                    stores.append((o_scr, (dirn, rows, vs), _dot(scores[hs], v[:, vs])))
        for ref, idx, val in stores:
            ref[idx] = val
        return carry

    lax.fori_loop(0, n_chunks // per_iter, chunk_local, 0)

    def scan(i, carry):
        steps = ((0, i), (1, n_chunks - 1 - i))
        new_st, inters = [], []
        for dirn, n in steps:
            st = st_ref[dirn]
            if with_output:
                q_st = stack_heads(qst_scr[dirn, chunk_rows(n), :])
                inters.append(_dot_nt(q_st, st.astype(BF16)))
            dec = dec_scr[dirn, pl.ds(pl.multiple_of(n * 8, 8), 8), :]
            new_st.append(dec[0:1] * st + ds_scr[dirn, pl.ds(pl.multiple_of(n * DV, DV), DV), :])
        for (dirn, n), st in zip(steps, new_st):
            st_ref[dirn] = st
        if with_output:
            for (dirn, n), inter in zip(steps, inters):
                for h in range(HEADS):
                    vs = slice(h * DV, (h + 1) * DV)
                    o_scr[dirn, chunk_rows(n), vs] += inter[h * CHUNK:(h + 1) * CHUNK]
        return carry

    lax.fori_loop(0, n_chunks, scan, 0, unroll=min(8, n_chunks))

    if with_state:
        s_out_ref[0] = st_ref[...]

    if with_output:
        eb = min(seq, 256)

        def finish(i, carry):
            rows = pl.ds(pl.multiple_of(i * eb, eb), eb)
            o = o_scr[0, rows, :] + o_scr[1, rows, :]
            gate = _silu(r_ref[0, rows, :].astype(F32))
            for h in range(HEADS):
                vs = slice(h * DV, (h + 1) * DV)
                y_ref[0, rows, vs] = (_rms(o[:, vs], nw_ref[:, vs]) * gate[:, vs]).astype(BF16)
            return carry

        lax.fori_loop(0, seq // eb, finish, 0)


def _gla(p, wd, bd, s0, nw, with_output, with_state):
    b, l, _ = p["k"].shape
    seq3 = lambda bi: (bi, 0, 0)
    in_specs = [pl.BlockSpec((1, l, K_W), seq3), pl.BlockSpec((1, l, V_W), seq3),
                pl.BlockSpec((1, l, LR_PAD), seq3), _resident(wd.shape), _resident(bd.shape),
                pl.BlockSpec((1, 2, DV, K_W), lambda bi: (bi, 0, 0, 0))]
    args = [p["k"], p["v"], p["lr"], wd, bd, s0]
    out_specs, out_shape = [], []
    scratch = [pltpu.VMEM((2, DV, K_W), F32), pltpu.VMEM((l, 2 * K_W), BF16), pltpu.VMEM((l, 2 * K_W), BF16),
               pltpu.VMEM((2, l // CHUNK * DV, K_W), F32), pltpu.VMEM((2, l // CHUNK * 8, K_W), F32)]
    if with_output:
        in_specs += [pl.BlockSpec((1, l, K_W), seq3), pl.BlockSpec((1, l, V_W), seq3),
                     _resident((1, V_W))]
        args += [p["q"], p["r"], nw]
        out_specs.append(pl.BlockSpec((1, l, V_W), seq3))
        out_shape.append(jax.ShapeDtypeStruct((b, l, V_W), BF16))
        scratch += [pltpu.VMEM((2, l, V_W), F32), pltpu.VMEM((2, l, K_W), BF16)]
    if with_state:
        out_specs.append(pl.BlockSpec((1, 2, DV, K_W), lambda bi: (bi, 0, 0, 0)))
        out_shape.append(jax.ShapeDtypeStruct((b, 2, DV, K_W), F32))
    outs = pl.pallas_call(
        functools.partial(_gla_kernel, seq=l, with_output=with_output, with_state=with_state),
        grid=(b,),
        in_specs=in_specs,
        out_specs=out_specs,
        out_shape=out_shape,
        scratch_shapes=scratch,
        compiler_params=_params(("parallel",), 48),
        name="gla",
    )(*args)
    outs = list(outs)
    y = outs.pop(0) if with_output else None
    s = outs.pop(0) if with_state else None
    return y, s


def _four_conv_kernel(uf_ref, bg_ref, cg_ref, uc_ref, cw_ref, cb_ref, chan_ref, pos_ref,
                      yf_ref, yc_ref, ab_scr, *, seq):
    ab = _dot(uf_ref[0], chan_ref[...])
    ab_scr[0:seq, :] = ab[:, :FOUR_W].astype(BF16)
    ab_scr[seq:2 * seq, :] = ab[:, FOUR_W:].astype(BF16)
    rb = min(seq, 512)
    for i in range(seq // rb):
        yf_ref[0, i * rb:(i + 1) * rb, :] = _dot(pos_ref[i * rb:(i + 1) * rb, :], ab_scr[...]).astype(BF16)

    z = cg_ref[0].astype(F32) * uc_ref[0].astype(F32)
    t = lax.broadcasted_iota(jnp.int32, (seq, CONV_W), 0)
    z_prev = jnp.where(t == 0, 0.0, pltpu.roll(z, 1, 0))
    z_next = jnp.where(t == seq - 1, 0.0, pltpu.roll(z, seq - 1, 0))
    y = cw_ref[0:1] * z_prev + cw_ref[1:2] * z + cw_ref[2:3] * z_next + cb_ref[...]
    yc_ref[0] = (bg_ref[0].astype(F32) * y).astype(BF16)


def _four_conv(p, conv_w, conv_b, chan_tab, pos_tab):
    b, l, _ = p["uf"].shape
    seq3 = lambda bi: (bi, 0, 0)
    tok = pl.BlockSpec((1, l, FOUR_W), seq3)
    return pl.pallas_call(
        functools.partial(_four_conv_kernel, seq=l),
        grid=(b,),
        in_specs=[tok, tok, tok, tok, _resident(conv_w.shape), _resident(conv_b.shape),
                  _resident(chan_tab.shape), _resident(pos_tab.shape)],
        out_specs=[tok, tok],
        out_shape=[jax.ShapeDtypeStruct((b, l, FOUR_W), BF16)] * 2,
        scratch_shapes=[pltpu.VMEM((2 * l, FOUR_W), BF16)],
        compiler_params=_params(("parallel",), 48),
        name="four_conv",
    )(p["uf"], p["bg"], p["cg"], p["uc"], conv_w, conv_b, chan_tab, pos_tab)


_FFN_TILES = ((0, 1536), (1536, HIDDEN))


def _out_ffn_kernel(*refs, add_pos, n_sub):
    it = iter(refs)
    yg_ref, yf_ref, yc_ref, wo_ref, x_ref = (next(it) for _ in range(5))
    pos_ref = next(it) if add_pos else None
    gmix_ref, gatemix_ref, gpre_ref, shift_ref, scale_ref, gate_ref, gpost_ref = (next(it) for _ in range(7))
    win_ref, wout_ref, o_ref = next(it), next(it), next(it)
    tm = x_ref.shape[1]
    subs = [slice(i * tm // n_sub, (i + 1) * tm // n_sub) for i in range(n_sub)]
    ys = [(_dot(yg_ref[0, r, :], wo_ref[0:V_W]) + _dot(yf_ref[0, r, :], wo_ref[V_W:V_W + FOUR_W])
           + _dot(yc_ref[0, r, :], wo_ref[V_W + FOUR_W:])) for r in subs]
    xs, hs = [], []
    for r, y in zip(subs, ys):
        x = x_ref[0, r, :]
        if add_pos:
            x = x + pos_ref[r, :]
        x = x + gatemix_ref[0] * _rms(y, gmix_ref[...])
        xs.append(x)
        hs.append((_rms(x, gpre_ref[...]) * (1.0 + scale_ref[0]) + shift_ref[0]).astype(BF16))
    ys = []
    for h in hs:
        y = jnp.zeros((h.shape[0], D_MODEL), F32)
        for lo, hi in _FFN_TILES:
            a = _dot(h, win_ref[:, lo:hi])
            u = _dot(h, win_ref[:, HIDDEN + lo:HIDDEN + hi])
            y = y + _dot((_silu(a) * u).astype(BF16), wout_ref[lo:hi, :])
        ys.append(y)
    for r, x, y in zip(subs, xs, ys):
        o_ref[0, r, :] = x + gate_ref[0] * _rms(y, gpost_ref[...])


def _out_ffn(yg, yf, yc, wo, x, pos, gmix, gatemix, gpre, shift, scale, gate, gpost, w_in, w_out, tm):
    b, l, d = x.shape
    add_pos = pos is not None
    tok = lambda bi, ti: (bi, ti, 0)
    vec = pl.BlockSpec((1, 1, d), lambda bi, ti: (bi, 0, 0))
    in_specs = [pl.BlockSpec((1, tm, V_W), tok), pl.BlockSpec((1, tm, FOUR_W), tok),
                pl.BlockSpec((1, tm, CONV_W), tok), _resident(wo.shape), pl.BlockSpec((1, tm, d), tok)]
    args = [yg, yf, yc, wo, x]
    if add_pos:
        in_specs.append(pl.BlockSpec((tm, d), lambda bi, ti: (ti, 0)))
        args.append(pos)
    in_specs += [_resident((1, d)), vec, _resident((1, d)), vec, vec, vec, _resident((1, d)),
                 _resident(w_in.shape), _resident(w_out.shape)]
    args += [gmix, gatemix, gpre, shift, scale, gate, gpost, w_in, w_out]
    return pl.pallas_call(
        functools.partial(_out_ffn_kernel, add_pos=add_pos, n_sub=2 if tm >= 512 else 1),
        grid=(b, l // tm),
        in_specs=in_specs,
        out_specs=pl.BlockSpec((1, tm, d), tok),
        out_shape=jax.ShapeDtypeStruct((b, l, d), F32),
        compiler_params=_params(("parallel", "parallel"), 56),
        name="out_ffn",
    )(*args)


def _arrange_w_in(w):
    main = jnp.concatenate([w[:, _C_K:_C_LR], w[:, _C_Q:_C_END]], axis=1)
    lr = jnp.pad(w[:, _C_LR:_C_Q], ((0, 0), (0, LR_PAD - 2 * RANK)))
    return jnp.concatenate([main, lr], axis=1).astype(BF16)


def _arrange_w_state(w_all):
    return jnp.concatenate([w_all[:, :K_W + V_W], w_all[:, _MAIN_W:]], axis=1)


def _arrange_w_dec(w_f, w_b):
    wd = jnp.zeros((LR_PAD, 2 * K_W), F32)
    wd = wd.at[0:RANK, :K_W].set(w_f).at[RANK:2 * RANK, K_W:].set(w_b)
    return wd.astype(BF16)


def kernel(x, c, ctx, c_ctx, w_mod, b_mod, g_mix_pre, g_mix_post, g_ffn_pre, g_ffn_post,
           w_in, w_dec_f, b_dec_f, w_dec_b, b_dec_b, gla_norm_w, conv_w, conv_b, w_out,
           w_ffn_in, w_ffn_out):
    b, n_lat, d = x.shape
    n_ctx = ctx.shape[1]
    assert d == D_MODEL and w_in.shape == (DEPTH, D_MODEL, _C_END)
    assert n_lat % 512 == 0 and n_ctx % CHUNK == 0 and b + 1 <= MOD_ROWS

    pos = jnp.asarray(_pos_embed(n_lat, d))
    tabs = {n_lat: _dft_tables(n_lat), n_ctx: _dft_tables(n_ctx)}

    cvec = jnp.zeros((MOD_ROWS, d), F32).at[:b].set(c).at[b].set(c_ctx)
    mod = _modulation(cvec, w_mod, b_mod)

    zero_state = jnp.zeros((b, 2, DV, K_W), F32)
    xc = ctx
    for i in range(DEPTH):
        last = i == DEPTH - 1
        m_lat = [mod[i, :b, j * d:(j + 1) * d].reshape(b, 1, d) for j in range(N_MOD)]
        m_ctx = [jnp.broadcast_to(mod[i, b, j * d:(j + 1) * d].reshape(1, 1, d), (b, 1, d))
                 for j in range(N_MOD)]
        w_all = _arrange_w_in(w_in[i])
        wd = _arrange_w_dec(w_dec_f[i], w_dec_b[i])
        bd = jnp.concatenate([b_dec_f[i], b_dec_b[i]]).reshape(1, 2 * K_W)
        nw = gla_norm_w[i].reshape(1, V_W)
        cb = conv_b[i].reshape(1, CONV_W)
        wo = w_out[i].astype(BF16)
        wfi = w_ffn_in[i].astype(BF16)
        wfo = w_ffn_out[i].astype(BF16)
        g_pre, g_post = g_mix_pre[i].reshape(1, d), g_mix_post[i].reshape(1, d)
        gf_pre, gf_post = g_ffn_pre[i].reshape(1, d), g_ffn_post[i].reshape(1, d)

        def mixer_and_ffn(xs, pos_s, m, s0, want_state, tm):
            p = _inproj(xs, pos_s, g_pre, m[0], m[1], w_all, _OUT_GROUPS_FULL, tm)
            yg, s = _gla(p, wd, bd, s0, nw, True, want_state)
            yf, yc = _four_conv(p, conv_w[i], cb, *reversed(tabs[xs.shape[1]]))
            xs = _out_ffn(yg, yf, yc, wo, xs, pos_s, g_post, m[2], gf_pre, m[3], m[4], m[5], gf_post,
                          wfi, wfo, tm)
            return xs, s

        if last:
            p = _inproj(xc, None, g_pre, m_ctx[0], m_ctx[1], _arrange_w_state(w_all),
                        _OUT_GROUPS_STATE, n_ctx)
            _, s = _gla(p, wd, bd, zero_state, None, False, True)
        else:
            xc, s = mixer_and_ffn(xc, None, m_ctx, zero_state, True, n_ctx)
        x, _ = mixer_and_ffn(x, pos if i == 0 else None, m_lat, s, False, 512)
    return x
```

```python
import functools

import numpy as np
import jax
import jax.numpy as jnp
from jax import lax
from jax.experimental import pallas as pl
from jax.experimental.pallas import tpu as pltpu

F32 = jnp.float32
BF16 = jnp.bfloat16

D_MODEL = 1024
DEPTH = 2
GRID_W = 64
N_MOD = 6
HEADS = 4
DK = 64
DV = 128
K_W = HEADS * DK
V_W = HEADS * DV
RANK = 16
TAU = 16.0
CHUNK = 64
FOUR_W = 256
FOUR_G = 4
CONV_W = 256
HIDDEN = 2816
EPS = 1e-6
LR_PAD = 128
MOD_ROWS = 16

_C_K, _C_V, _C_LR, _C_Q, _C_R, _C_FOUR, _C_BG, _C_CG, _C_UC, _C_END = (
    0, 256, 768, 800, 1056, 1568, 1824, 2080, 2336, 2592)
_MAIN_W = 2560
_IN_W = _MAIN_W + LR_PAD
_OUT_GROUPS_FULL = (("k", 0, 256), ("v", 256, 768), ("q", 768, 1024), ("r", 1024, 1536),
                    ("uf", 1536, 1792), ("bg", 1792, 2048), ("cg", 2048, 2304),
                    ("uc", 2304, 2560), ("lr", 2560, 2688))
_OUT_GROUPS_STATE = (("k", 0, 256), ("v", 256, 768), ("lr", 768, 896))

V7X_VMEM_BYTES = 64 * 1024 * 1024


def _params(semantics, vmem_mb):
    assert vmem_mb * 1024 * 1024 < V7X_VMEM_BYTES
    return pltpu.CompilerParams(dimension_semantics=semantics,
                                vmem_limit_bytes=vmem_mb * 1024 * 1024)


def _resident(shape):
    zeros = (0,) * len(shape)
    return pl.BlockSpec(shape, lambda *_: zeros, pipeline_mode=pl.Buffered(1))


def _silu(a):
    return a / (1.0 + jnp.exp(-a))


def _rms(x, g):
    return x * lax.rsqrt(jnp.mean(x * x, axis=-1, keepdims=True) + EPS) * g


def _dot(a, b):
    return jnp.dot(a, b, preferred_element_type=F32)


def _dot_nt(a, b):
    return lax.dot_general(a, b, (((1,), (1,)), ((), ())), preferred_element_type=F32)


def _dot_tn(a, b):
    return lax.dot_general(a, b, (((0,), (0,)), ((), ())), preferred_element_type=F32)


def _pos_embed(n_tokens, dim):
    rows = n_tokens // GRID_W
    row = np.repeat(np.arange(rows, dtype=np.float32), GRID_W)
    col = np.tile(np.arange(GRID_W, dtype=np.float32), rows)
    quarter = dim // 4
    freqs = (1.0 / (10000.0 ** (np.arange(quarter, dtype=np.float32) / quarter))).astype(np.float32)

    def enc(p):
        ang = (p[:, None] * freqs[None, :]).astype(np.float32)
        return np.concatenate([np.sin(ang), np.cos(ang)], axis=-1)

    return np.concatenate([enc(row), enc(col)], axis=-1).astype(np.float32)


def _dft_cos_sin(n):
    idx = np.arange(n, dtype=np.int64)
    ang = 2.0 * np.pi * ((idx[:, None] * idx[None, :]) % n).astype(np.float64) / n
    return np.cos(ang) / np.sqrt(n), np.sin(ang) / np.sqrt(n)


def _dft_tables(seq, radix):
    gw = FOUR_W // FOUR_G
    cc, sc = _dft_cos_sin(gw)
    eye = np.eye(FOUR_G)
    chan_tab = jnp.asarray(np.concatenate([np.kron(eye, cc), np.kron(eye, sc)], axis=1), dtype=F32).astype(BF16)
    if radix == 1:
        cl, sl = _dft_cos_sin(seq)
        return chan_tab, jnp.asarray(np.concatenate([cl, -sl], axis=1), dtype=F32).astype(BF16)
    n = seq // radix
    cn, sn = _dft_cos_sin(n)
    sub_tab = np.concatenate([cn, sn], axis=1) * np.sqrt(n / seq)
    ang = 2.0 * np.pi * (np.arange(1, radix)[:, None] * np.arange(n)[None, :]).astype(np.float64) / seq
    twc = np.broadcast_to(np.cos(ang)[:, :, None], (radix - 1, n, FOUR_W))
    tws = np.broadcast_to(np.sin(ang)[:, :, None], (radix - 1, n, FOUR_W))
    return (chan_tab, jnp.asarray(sub_tab, dtype=F32).astype(BF16),
            jnp.asarray(twc, dtype=F32), jnp.asarray(tws, dtype=F32))


def _mod_kernel(c_ref, w_ref, b_ref, o_ref):
    s = _silu(c_ref[...]).astype(BF16)
    o_ref[0] = _dot(s, w_ref[0].astype(BF16)) + b_ref[0]


def _modulation(cvec, w_mod, b_mod):
    tn = 1536
    n = N_MOD * D_MODEL
    return pl.pallas_call(
        _mod_kernel,
        grid=(DEPTH, n // tn),
        in_specs=[pl.BlockSpec((MOD_ROWS, D_MODEL), lambda i, j: (0, 0)),
                  pl.BlockSpec((1, D_MODEL, tn), lambda i, j: (i, 0, j)),
                  pl.BlockSpec((1, 1, tn), lambda i, j: (i, 0, j))],
        out_specs=pl.BlockSpec((1, MOD_ROWS, tn), lambda i, j: (i, 0, j)),
        out_shape=jax.ShapeDtypeStruct((DEPTH, MOD_ROWS, n), F32),
        compiler_params=_params(("arbitrary", "arbitrary"), 40),
        name="modulation",
    )(cvec, w_mod, b_mod.reshape(DEPTH, 1, n))


def _inproj_kernel(*refs, add_pos, groups):
    it = iter(refs)
    x_ref = next(it)
    pos_ref = next(it) if add_pos else None
    g_ref, shift_ref, scale_ref, w_ref = next(it), next(it), next(it), next(it)
    out_refs = list(it)
    tm = x_ref.shape[1]
    n_sub = 2 if tm >= 512 else 1
    subs = [slice(i * tm // n_sub, (i + 1) * tm // n_sub) for i in range(n_sub)]
    hs = []
    for r in subs:
        x = x_ref[0, r, :]
        if add_pos:
            x = x + pos_ref[r, :]
        hs.append((_rms(x, g_ref[...]) * (1.0 + scale_ref[0]) + shift_ref[0]).astype(BF16))
    for r, h in zip(subs, hs):
        for o_ref, (_, lo, hi) in zip(out_refs, groups):
            o_ref[0, r, :] = _dot(h, w_ref[:, lo:hi]).astype(BF16)


def _inproj(x, pos, g, shift, scale, w, groups, tm):
    b, l, d = x.shape
    add_pos = pos is not None
    tok = lambda bi, ti: (bi, ti, 0)
    vec = lambda bi, ti: (bi, 0, 0)
    in_specs = [pl.BlockSpec((1, tm, d), tok)]
    args = [x]
    if add_pos:
        in_specs.append(pl.BlockSpec((tm, d), lambda bi, ti: (ti, 0)))
        args.append(pos)
    in_specs += [_resident((1, d)), pl.BlockSpec((1, 1, d), vec), pl.BlockSpec((1, 1, d), vec),
                 _resident(w.shape)]
    args += [g, shift, scale, w]
    outs = pl.pallas_call(
        functools.partial(_inproj_kernel, add_pos=add_pos, groups=groups),
        grid=(b, l // tm),
        in_specs=in_specs,
        out_specs=[pl.BlockSpec((1, tm, hi - lo), tok) for _, lo, hi in groups],
        out_shape=[jax.ShapeDtypeStruct((b, l, hi - lo), BF16) for _, lo, hi in groups],
        compiler_params=_params(("parallel", "parallel"), 48),
        name="inproj",
    )(*args)
    return {name: o for (name, _, _), o in zip(groups, outs)}


def _gla_kernel(*refs, seq, with_output, with_state):
    it = iter(refs)
    k_ref, v_ref, lr_ref, wd_ref, bd_ref, s0_ref = (next(it) for _ in range(6))
    if with_output:
        q_ref, r_ref, nw_ref = next(it), next(it), next(it)
        y_ref = next(it)
    if with_state:
        s_out_ref = next(it)
    st_ref = next(it)
    lg_hi, lg_lo = next(it), next(it)
    ds_scr = next(it)
    dec_scr = next(it)
    if with_output:
        o_scr = next(it)
        qst_scr = next(it)
    n_chunks = seq // CHUNK

    st_ref[...] = s0_ref[0]

    row = lax.broadcasted_iota(jnp.int32, (CHUNK, CHUNK), 0)
    col = lax.broadcasted_iota(jnp.int32, (CHUNK, CHUNK), 1)
    cum = (jnp.where(row >= col, 1.0, 0.0).astype(BF16), jnp.where(row <= col, 1.0, 0.0).astype(BF16))
    lane_head = lax.broadcasted_iota(jnp.int32, (1, K_W), 1) // DK
    head_mask = [lane_head == h for h in range(HEADS)]
    srow = lax.broadcasted_iota(jnp.int32, (HEADS * CHUNK, CHUNK), 0) % CHUNK
    scol = lax.broadcasted_iota(jnp.int32, (HEADS * CHUNK, CHUNK), 1)
    score_mask = (srow >= scol, srow <= scol)
    last_row = (CHUNK - 1, 0)
    mid_row = (CHUNK // 2 - 1, CHUNK // 2)

    def stack_heads(a):
        zero = jnp.zeros_like(a)
        return jnp.concatenate([jnp.where(head_mask[h], a, zero) for h in range(HEADS)], axis=0)

    def chunk_rows(n):
        return pl.ds(pl.multiple_of(n * CHUNK, CHUNK), CHUNK)

    gb = min(seq, 512)

    def gates(i, carry):
        rows = pl.ds(pl.multiple_of(i * gb, gb), gb)
        z = _dot(lr_ref[0, rows, :], wd_ref[...]) + bd_ref[...]
        logg = (jnp.minimum(z, 0.0) - jnp.log(1.0 + jnp.exp(-jnp.abs(z)))) * (1.0 / TAU)
        hi = logg.astype(BF16)
        lg_hi[rows, :] = hi
        lg_lo[rows, :] = (logg - hi.astype(F32)).astype(BF16)
        return carry

    lax.fori_loop(0, seq // gb, gates, 0)

    per_iter = 4

    def chunk_local(i, carry):
        loaded = []
        for u in range(per_iter):
            n = i * per_iter + u
            rows = chunk_rows(n)
            k = k_ref[0, rows, :].astype(F32)
            v = v_ref[0, rows, :]
            qs = q_ref[0, rows, :].astype(F32) * (DK ** -0.5) if with_output else None
            lg = [(lg_hi[rows, d * K_W:(d + 1) * K_W], lg_lo[rows, d * K_W:(d + 1) * K_W]) for d in range(2)]
            loaded.append((n, rows, k, v, qs, lg))
        chains = [(n, rows, k, v, qs, lg[dirn], dirn) for n, rows, k, v, qs, lg in loaded for dirn in range(2)]
        stores = []
        gs = [_dot(cum[dirn], hi) + _dot(cum[dirn], lo) for *_, (hi, lo), dirn in chains]
        outers, score_list = [], []
        for (n, rows, k, v, qs, _, dirn), g in zip(chains, gs):
            g_last = g[last_row[dirn]:last_row[dirn] + 1]
            stores.append((dec_scr, (dirn, pl.ds(pl.multiple_of(n * 8, 8), 8), slice(None)),
                           jnp.broadcast_to(jnp.exp(g_last), (8, K_W))))
            if with_output:
                g_mid = g[mid_row[dirn]:mid_row[dirn] + 1]
                q_in = stack_heads((qs * jnp.exp(g - g_mid)).astype(BF16))
                k_in = (k * jnp.exp(g_mid - g)).astype(BF16)
                score_list.append(_dot_nt(q_in, k_in))
                stores.append((qst_scr, (dirn, rows, slice(None)), (qs * jnp.exp(g)).astype(BF16)))
            outers.append(_dot_tn(v, (k * jnp.exp(g_last - g)).astype(BF16)))
        for (n, rows, k, v, qs, _, dirn), outer in zip(chains, outers):
            upd = jnp.zeros((DV, K_W), F32)
            for h in range(HEADS):
                upd = jnp.where(head_mask[h], outer[h * DV:(h + 1) * DV], upd)
            stores.append((ds_scr, (dirn, pl.ds(pl.multiple_of(n * DV, DV), DV), slice(None)), upd))
        if with_output:
            for (n, rows, k, v, qs, _, dirn), sc in zip(chains, score_list):
                scores = jnp.where(score_mask[dirn], sc, 0.0).astype(BF16)
                for h in range(HEADS):
                    hs = slice(h * CHUNK, (h + 1) * CHUNK)
                    vs = slice(h * DV, (h + 1) * DV)
                    stores.append((o_scr, (dirn, rows, vs), _dot(scores[hs], v[:, vs])))
        for ref, idx, val in stores:
            ref[idx] = val
        return carry

    lax.fori_loop(0, n_chunks // per_iter, chunk_local, 0)

    def scan(i, carry):
        steps = ((0, i), (1, n_chunks - 1 - i))
        new_st, inters = [], []
        for dirn, n in steps:
            st = st_ref[dirn]
            if with_output:
                q_st = stack_heads(qst_scr[dirn, chunk_rows(n), :])
                inters.append(_dot_nt(q_st, st.astype(BF16)))
            dec = dec_scr[dirn, pl.ds(pl.multiple_of(n * 8, 8), 8), :]
            new_st.append(dec[0:1] * st + ds_scr[dirn, pl.ds(pl.multiple_of(n * DV, DV), DV), :])
        for (dirn, n), st in zip(steps, new_st):
            st_ref[dirn] = st
        if with_output:
            for (dirn, n), inter in zip(steps, inters):
                for h in range(HEADS):
                    vs = slice(h * DV, (h + 1) * DV)
                    o_scr[dirn, chunk_rows(n), vs] += inter[h * CHUNK:(h + 1) * CHUNK]
        return carry

    lax.fori_loop(0, n_chunks, scan, 0, unroll=min(8, n_chunks))

    if with_state:
        s_out_ref[0] = st_ref[...]

    if with_output:
        eb = min(seq, 256)

        def finish(i, carry):
            rows = pl.ds(pl.multiple_of(i * eb, eb), eb)
            o = o_scr[0, rows, :] + o_scr[1, rows, :]
            gate = _silu(r_ref[0, rows, :].astype(F32))
            for h in range(HEADS):
                vs = slice(h * DV, (h + 1) * DV)
                y_ref[0, rows, vs] = (_rms(o[:, vs], nw_ref[:, vs]) * gate[:, vs]).astype(BF16)
            return carry

        lax.fori_loop(0, seq // eb, finish, 0)


def _gla(p, wd, bd, s0, nw, with_output, with_state):
    b, l, _ = p["k"].shape
    seq3 = lambda bi: (bi, 0, 0)
    in_specs = [pl.BlockSpec((1, l, K_W), seq3), pl.BlockSpec((1, l, V_W), seq3),
                pl.BlockSpec((1, l, LR_PAD), seq3), _resident(wd.shape), _resident(bd.shape),
                pl.BlockSpec((1, 2, DV, K_W), lambda bi: (bi, 0, 0, 0))]
    args = [p["k"], p["v"], p["lr"], wd, bd, s0]
    out_specs, out_shape = [], []
    scratch = [pltpu.VMEM((2, DV, K_W), F32), pltpu.VMEM((l, 2 * K_W), BF16), pltpu.VMEM((l, 2 * K_W), BF16),
               pltpu.VMEM((2, l // CHUNK * DV, K_W), F32), pltpu.VMEM((2, l // CHUNK * 8, K_W), F32)]
    if with_output:
        in_specs += [pl.BlockSpec((1, l, K_W), seq3), pl.BlockSpec((1, l, V_W), seq3),
                     _resident((1, V_W))]
        args += [p["q"], p["r"], nw]
        out_specs.append(pl.BlockSpec((1, l, V_W), seq3))
        out_shape.append(jax.ShapeDtypeStruct((b, l, V_W), BF16))
        scratch += [pltpu.VMEM((2, l, V_W), F32), pltpu.VMEM((2, l, K_W), BF16)]
    if with_state:
        out_specs.append(pl.BlockSpec((1, 2, DV, K_W), lambda bi: (bi, 0, 0, 0)))
        out_shape.append(jax.ShapeDtypeStruct((b, 2, DV, K_W), F32))
    outs = pl.pallas_call(
        functools.partial(_gla_kernel, seq=l, with_output=with_output, with_state=with_state),
        grid=(b,),
        in_specs=in_specs,
        out_specs=out_specs,
        out_shape=out_shape,
        scratch_shapes=scratch,
        compiler_params=_params(("parallel",), 48),
        name="gla",
    )(*args)
    outs = list(outs)
    y = outs.pop(0) if with_output else None
    s = outs.pop(0) if with_state else None
    return y, s


def _four_conv_kernel(uf_ref, bg_ref, cg_ref, uc_ref, cw_ref, cb_ref, chan_ref, pos_ref,
                      yf_ref, yc_ref, ab_scr, z_scr, *, seq):
    ab = _dot(uf_ref[0], chan_ref[...])
    ab_scr[0:seq, :] = ab[:, :FOUR_W].astype(BF16)
    ab_scr[seq:2 * seq, :] = ab[:, FOUR_W:].astype(BF16)
    rb = min(seq, 512)
    for i in range(seq // rb):
        yf_ref[0, i * rb:(i + 1) * rb, :] = _dot(pos_ref[i * rb:(i + 1) * rb, :], ab_scr[...]).astype(BF16)
    _short_conv(bg_ref, cg_ref, uc_ref, cw_ref, cb_ref, yc_ref, z_scr, seq)


CONV_PAD = 8


def _short_conv(bg_ref, cg_ref, uc_ref, cw_ref, cb_ref, yc_ref, z_scr, seq):
    z_scr[0:CONV_PAD, :] = jnp.zeros((CONV_PAD, CONV_W), F32)
    z_scr[CONV_PAD + seq:, :] = jnp.zeros((CONV_PAD, CONV_W), F32)
    rb = min(seq, 256)
    for i in range(seq // rb):
        r = slice(i * rb, (i + 1) * rb)
        z_scr[CONV_PAD + i * rb:CONV_PAD + (i + 1) * rb, :] = cg_ref[0, r, :].astype(F32) * uc_ref[0, r, :].astype(F32)
    for i in range(seq // rb):
        r = slice(i * rb, (i + 1) * rb)
        s = CONV_PAD + i * rb
        y = (cw_ref[0:1] * z_scr[s - 1:s - 1 + rb, :] + cw_ref[1:2] * z_scr[s:s + rb, :]
             + cw_ref[2:3] * z_scr[s + 1:s + 1 + rb, :] + cb_ref[...])
        yc_ref[0, r, :] = (bg_ref[0, r, :].astype(F32) * y).astype(BF16)


FFT_RADIX = 8


def _four_conv_fft_kernel(uf_ref, bg_ref, cg_ref, uc_ref, cw_ref, cb_ref, chan_ref, sub_ref, twc_ref, tws_ref,
                          yf_ref, yc_ref, uf_scr, z_scr, *, seq):
    n = seq // FFT_RADIX
    lanes = uf_scr.shape[2]
    n_lane_tiles = FOUR_W // lanes
    for j in range(n_lane_tiles):
        uf_scr[j] = uf_ref[0, :, j * lanes:(j + 1) * lanes].astype(F32)

    def tokens(r):
        parts = [uf_scr[j, pl.ds(r, n, stride=FFT_RADIX), :] for j in range(n_lane_tiles)]
        return jnp.concatenate(parts, axis=1).astype(BF16)

    ab = [_dot(tokens(r), chan_ref[...]) for r in range(FFT_RADIX)]
    pq = []
    for ab_r in ab:
        a, b = ab_r[:, :FOUR_W].astype(BF16), ab_r[:, FOUR_W:].astype(BF16)
        w = jnp.concatenate([jnp.concatenate([a, b], axis=1), jnp.concatenate([-b, a], axis=1)], axis=0)
        pq.append(_dot(sub_ref[...], w))
    re, im = [pq[0][:, :FOUR_W]], [pq[0][:, FOUR_W:]]
    for r in range(1, FFT_RADIX):
        p, q = pq[r][:, :FOUR_W], pq[r][:, FOUR_W:]
        c, s = twc_ref[r - 1], tws_ref[r - 1]
        re.append(c * p - s * q)
        im.append(s * p + c * q)

    def quarter(z0, z1, z2, z3):
        s02 = (z0[0] + z2[0], z0[1] + z2[1])
        d02 = (z0[0] - z2[0], z0[1] - z2[1])
        s13 = (z1[0] + z3[0], z1[1] + z3[1])
        d13 = (z1[0] - z3[0], z1[1] - z3[1])
        return ((s02[0] + s13[0], s02[1] + s13[1]), (d02[0] - d13[1], d02[1] + d13[0]),
                (s02[0] - s13[0], s02[1] - s13[1]), (d02[0] + d13[1], d02[1] - d13[0]))

    zs = list(zip(re, im))
    even = quarter(zs[0], zs[2], zs[4], zs[6])
    odd = quarter(zs[1], zs[3], zs[5], zs[7])
    half = 0.5 ** 0.5
    odd_re = (odd[0][0], (odd[1][0] - odd[1][1]) * half, -odd[2][1], -(odd[3][0] + odd[3][1]) * half)
    for q in range(4):
        yf_ref[0, q * n:(q + 1) * n, :] = (even[q][0] + odd_re[q]).astype(BF16)
        yf_ref[0, (q + 4) * n:(q + 5) * n, :] = (even[q][0] - odd_re[q]).astype(BF16)
    _short_conv(bg_ref, cg_ref, uc_ref, cw_ref, cb_ref, yc_ref, z_scr, seq)


def _four_conv(p, conv_w, conv_b, tabs):
    b, l, _ = p["uf"].shape
    seq3 = lambda bi: (bi, 0, 0)
    tok = pl.BlockSpec((1, l, FOUR_W), seq3)
    if len(tabs) == 4:
        return pl.pallas_call(
            functools.partial(_four_conv_fft_kernel, seq=l),
            grid=(b,),
            in_specs=[tok, tok, tok, tok, _resident(conv_w.shape), _resident(conv_b.shape)]
                     + [_resident(t.shape) for t in tabs],
            out_specs=[tok, tok],
            out_shape=[jax.ShapeDtypeStruct((b, l, FOUR_W), BF16)] * 2,
            scratch_shapes=[pltpu.VMEM((FOUR_W // 128, l, 128), F32),
                            pltpu.VMEM((l + 2 * CONV_PAD, CONV_W), F32)],
            compiler_params=_params(("parallel",), 48),
            name="four_conv_fft",
        )(p["uf"], p["bg"], p["cg"], p["uc"], conv_w, conv_b, *tabs)
    chan_tab, pos_tab = tabs
    return pl.pallas_call(
        functools.partial(_four_conv_kernel, seq=l),
        grid=(b,),
        in_specs=[tok, tok, tok, tok, _resident(conv_w.shape), _resident(conv_b.shape),
                  _resident(chan_tab.shape), _resident(pos_tab.shape)],
        out_specs=[tok, tok],
        out_shape=[jax.ShapeDtypeStruct((b, l, FOUR_W), BF16)] * 2,
        scratch_shapes=[pltpu.VMEM((2 * l, FOUR_W), BF16), pltpu.VMEM((l + 2 * CONV_PAD, CONV_W), F32)],
        compiler_params=_params(("parallel",), 48),
        name="four_conv",
    )(p["uf"], p["bg"], p["cg"], p["uc"], conv_w, conv_b, chan_tab, pos_tab)


_FFN_TILES = ((0, 1536), (1536, HIDDEN))


def _out_ffn_kernel(*refs, add_pos, n_sub):
    it = iter(refs)
    yg_ref, yf_ref, yc_ref, wo_ref, x_ref = (next(it) for _ in range(5))
    pos_ref = next(it) if add_pos else None
    gmix_ref, gatemix_ref, gpre_ref, shift_ref, scale_ref, gate_ref, gpost_ref = (next(it) for _ in range(7))
    win_ref, wout_ref, o_ref = next(it), next(it), next(it)
    tm = x_ref.shape[1]
    subs = [slice(i * tm // n_sub, (i + 1) * tm // n_sub) for i in range(n_sub)]
    ys = [(_dot(yg_ref[0, r, :], wo_ref[0:V_W]) + _dot(yf_ref[0, r, :], wo_ref[V_W:V_W + FOUR_W])
           + _dot(yc_ref[0, r, :], wo_ref[V_W + FOUR_W:])) for r in subs]
    xs, hs = [], []
    for r, y in zip(subs, ys):
        x = x_ref[0, r, :]
        if add_pos:
            x = x + pos_ref[r, :]
        x = x + gatemix_ref[0] * _rms(y, gmix_ref[...])
        xs.append(x)
        hs.append((_rms(x, gpre_ref[...]) * (1.0 + scale_ref[0]) + shift_ref[0]).astype(BF16))
    ys = []
    for h in hs:
        y = jnp.zeros((h.shape[0], D_MODEL), F32)
        for lo, hi in _FFN_TILES:
            a = _dot(h, win_ref[:, lo:hi])
            u = _dot(h, win_ref[:, HIDDEN + lo:HIDDEN + hi])
            y = y + _dot((_silu(a) * u).astype(BF16), wout_ref[lo:hi, :])
        ys.append(y)
    for r, x, y in zip(subs, xs, ys):
        o_ref[0, r, :] = x + gate_ref[0] * _rms(y, gpost_ref[...])


def _out_ffn(yg, yf, yc, wo, x, pos, gmix, gatemix, gpre, shift, scale, gate, gpost, w_in, w_out, tm):
    b, l, d = x.shape
    add_pos = pos is not None
    tok = lambda bi, ti: (bi, ti, 0)
    vec = pl.BlockSpec((1, 1, d), lambda bi, ti: (bi, 0, 0))
    in_specs = [pl.BlockSpec((1, tm, V_W), tok), pl.BlockSpec((1, tm, FOUR_W), tok),
                pl.BlockSpec((1, tm, CONV_W), tok), _resident(wo.shape), pl.BlockSpec((1, tm, d), tok)]
    args = [yg, yf, yc, wo, x]
    if add_pos:
        in_specs.append(pl.BlockSpec((tm, d), lambda bi, ti: (ti, 0)))
        args.append(pos)
    in_specs += [_resident((1, d)), vec, _resident((1, d)), vec, vec, vec, _resident((1, d)),
                 _resident(w_in.shape), _resident(w_out.shape)]
    args += [gmix, gatemix, gpre, shift, scale, gate, gpost, w_in, w_out]
    return pl.pallas_call(
        functools.partial(_out_ffn_kernel, add_pos=add_pos, n_sub=2 if tm >= 512 else 1),
        grid=(b, l // tm),
        in_specs=in_specs,
        out_specs=pl.BlockSpec((1, tm, d), tok),
        out_shape=jax.ShapeDtypeStruct((b, l, d), F32),
        compiler_params=_params(("parallel", "parallel"), 56),
        name="out_ffn",
    )(*args)


def _arrange_w_in(w):
    main = jnp.concatenate([w[:, _C_K:_C_LR], w[:, _C_Q:_C_END]], axis=1)
    lr = jnp.pad(w[:, _C_LR:_C_Q], ((0, 0), (0, LR_PAD - 2 * RANK)))
    return jnp.concatenate([main, lr], axis=1).astype(BF16)


def _arrange_w_state(w_all):
    return jnp.concatenate([w_all[:, :K_W + V_W], w_all[:, _MAIN_W:]], axis=1)


def _arrange_w_dec(w_f, w_b):
    wd = jnp.zeros((LR_PAD, 2 * K_W), F32)
    wd = wd.at[0:RANK, :K_W].set(w_f).at[RANK:2 * RANK, K_W:].set(w_b)
    return wd.astype(BF16)


def kernel(x, c, ctx, c_ctx, w_mod, b_mod, g_mix_pre, g_mix_post, g_ffn_pre, g_ffn_post,
           w_in, w_dec_f, b_dec_f, w_dec_b, b_dec_b, gla_norm_w, conv_w, conv_b, w_out,
           w_ffn_in, w_ffn_out):
    b, n_lat, d = x.shape
    n_ctx = ctx.shape[1]
    assert d == D_MODEL and w_in.shape == (DEPTH, D_MODEL, _C_END)
    assert n_lat % 512 == 0 and n_ctx % CHUNK == 0 and b + 1 <= MOD_ROWS

    pos = jnp.asarray(_pos_embed(n_lat, d))
    tabs = {n_lat: _dft_tables(n_lat, FFT_RADIX), n_ctx: _dft_tables(n_ctx, 1)}

    cvec = jnp.zeros((MOD_ROWS, d), F32).at[:b].set(c).at[b].set(c_ctx)
    mod = _modulation(cvec, w_mod, b_mod)

    zero_state = jnp.zeros((b, 2, DV, K_W), F32)
    xc = ctx
    for i in range(DEPTH):
        last = i == DEPTH - 1
        m_lat = [mod[i, :b, j * d:(j + 1) * d].reshape(b, 1, d) for j in range(N_MOD)]
        m_ctx = [jnp.broadcast_to(mod[i, b, j * d:(j + 1) * d].reshape(1, 1, d), (b, 1, d))
                 for j in range(N_MOD)]
        w_all = _arrange_w_in(w_in[i])
        wd = _arrange_w_dec(w_dec_f[i], w_dec_b[i])
        bd = jnp.concatenate([b_dec_f[i], b_dec_b[i]]).reshape(1, 2 * K_W)
        nw = gla_norm_w[i].reshape(1, V_W)
        cb = conv_b[i].reshape(1, CONV_W)
        wo = w_out[i].astype(BF16)
        wfi = w_ffn_in[i].astype(BF16)
        wfo = w_ffn_out[i].astype(BF16)
        g_pre, g_post = g_mix_pre[i].reshape(1, d), g_mix_post[i].reshape(1, d)
        gf_pre, gf_post = g_ffn_pre[i].reshape(1, d), g_ffn_post[i].reshape(1, d)

        def mixer_and_ffn(xs, pos_s, m, s0, want_state, tm):
            p = _inproj(xs, pos_s, g_pre, m[0], m[1], w_all, _OUT_GROUPS_FULL, tm)
            yg, s = _gla(p, wd, bd, s0, nw, True, want_state)
            yf, yc = _four_conv(p, conv_w[i], cb, tabs[xs.shape[1]])
            xs = _out_ffn(yg, yf, yc, wo, xs, pos_s, g_post, m[2], gf_pre, m[3], m[4], m[5], gf_post,
                          wfi, wfo, tm)
            return xs, s

        if last:
            p = _inproj(xc, None, g_pre, m_ctx[0], m_ctx[1], _arrange_w_state(w_all),
                        _OUT_GROUPS_STATE, n_ctx)
            _, s = _gla(p, wd, bd, zero_state, None, False, True)
        else:
            xc, s = mixer_and_ffn(xc, None, m_ctx, zero_state, True, n_ctx)
        x, _ = mixer_and_ffn(x, pos if i == 0 else None, m_lat, s, False, 512)
    return x
```

```python
import functools

import numpy as np
import jax
import jax.numpy as jnp
from jax import lax
from jax.experimental import pallas as pl
from jax.experimental.pallas import tpu as pltpu

F32 = jnp.float32
BF16 = jnp.bfloat16

D_MODEL = 1024
DEPTH = 2
GRID_W = 64
N_MOD = 6
HEADS = 4
DK = 64
DV = 128
K_W = HEADS * DK
V_W = HEADS * DV
RANK = 16
TAU = 16.0
CHUNK = 64
FOUR_W = 256
FOUR_G = 4
CONV_W = 256
HIDDEN = 2816
EPS = 1e-6
LR_PAD = 128
MOD_ROWS = 16

_C_K, _C_V, _C_LR, _C_Q, _C_R, _C_FOUR, _C_BG, _C_CG, _C_UC, _C_END = (
    0, 256, 768, 800, 1056, 1568, 1824, 2080, 2336, 2592)
_MAIN_W = 2560
_IN_W = _MAIN_W + LR_PAD
_OUT_GROUPS_FULL = (("k", 0, 256), ("v", 256, 768), ("q", 768, 1024), ("r", 1024, 1536),
                    ("uf", 1536, 1792), ("bg", 1792, 2048), ("cg", 2048, 2304),
                    ("uc", 2304, 2560), ("lr", 2560, 2688))
_OUT_GROUPS_STATE = (("k", 0, 256), ("v", 256, 768), ("lr", 2560, 2688))

V7X_VMEM_BYTES = 64 * 1024 * 1024


def _params(semantics, vmem_mb):
    assert vmem_mb * 1024 * 1024 < V7X_VMEM_BYTES
    return pltpu.CompilerParams(dimension_semantics=semantics,
                                vmem_limit_bytes=vmem_mb * 1024 * 1024)


def _resident(shape):
    zeros = (0,) * len(shape)
    return pl.BlockSpec(shape, lambda *_: zeros, pipeline_mode=pl.Buffered(1))


def _layer(arr, i):
    shape = tuple(arr.shape[1:])
    zeros = (0,) * len(shape)
    return pl.BlockSpec((None,) + shape, lambda *_: (i,) + zeros, pipeline_mode=pl.Buffered(1))


def _mod_spec(i, j, row):
    if row is None:
        return pl.BlockSpec((None, None, 1, D_MODEL), lambda bi, *_: (i, bi, 0, j))
    return pl.BlockSpec((None, None, 1, D_MODEL), lambda *_: (i, row, 0, j))


def _silu(a):
    return a / (1.0 + jnp.exp(-a))


def _rms(x, g):
    return x * lax.rsqrt(jnp.mean(x * x, axis=-1, keepdims=True) + EPS) * g


def _dot(a, b):
    return jnp.dot(a, b, preferred_element_type=F32)


def _dot_nt(a, b):
    return lax.dot_general(a, b, (((1,), (1,)), ((), ())), preferred_element_type=F32)


def _dot_tn(a, b):
    return lax.dot_general(a, b, (((0,), (0,)), ((), ())), preferred_element_type=F32)


def _pos_embed(n_tokens, dim):
    rows = n_tokens // GRID_W
    row = np.repeat(np.arange(rows, dtype=np.float32), GRID_W)
    col = np.tile(np.arange(GRID_W, dtype=np.float32), rows)
    quarter = dim // 4
    freqs = (1.0 / (10000.0 ** (np.arange(quarter, dtype=np.float32) / quarter))).astype(np.float32)

    def enc(p):
        ang = (p[:, None] * freqs[None, :]).astype(np.float32)
        return np.concatenate([np.sin(ang), np.cos(ang)], axis=-1)

    return np.concatenate([enc(row), enc(col)], axis=-1).astype(np.float32)


def _dft_cos_sin(n):
    idx = np.arange(n, dtype=np.int64)
    ang = 2.0 * np.pi * ((idx[:, None] * idx[None, :]) % n).astype(np.float64) / n
    return np.cos(ang) / np.sqrt(n), np.sin(ang) / np.sqrt(n)


def _dft_tables(seq, radix):
    gw = FOUR_W // FOUR_G
    cc, sc = _dft_cos_sin(gw)
    eye = np.eye(FOUR_G)
    chan_tab = jnp.asarray(np.concatenate([np.kron(eye, cc), np.kron(eye, sc)], axis=1), dtype=F32).astype(BF16)
    if radix == 1:
        cl, sl = _dft_cos_sin(seq)
        return chan_tab, jnp.asarray(np.concatenate([cl, -sl], axis=1), dtype=F32).astype(BF16)
    n = seq // radix
    cn, sn = _dft_cos_sin(n)
    sub_tab = np.concatenate([cn, sn], axis=1) * np.sqrt(n / seq)
    ang = 2.0 * np.pi * (np.arange(1, radix)[:, None] * np.arange(n)[None, :]).astype(np.float64) / seq
    twc = np.broadcast_to(np.cos(ang)[:, :, None], (radix - 1, n, FOUR_W))
    tws = np.broadcast_to(np.sin(ang)[:, :, None], (radix - 1, n, FOUR_W))
    return (chan_tab, jnp.asarray(sub_tab, dtype=F32).astype(BF16),
            jnp.asarray(twc, dtype=F32), jnp.asarray(tws, dtype=F32))


def _mod_kernel(c_ref, w_ref, b_ref, o_ref):
    s = _silu(c_ref[...]).astype(BF16)
    o_ref[0] = _dot(s, w_ref[0].astype(BF16)) + b_ref[0]


def _modulation(cvec, w_mod, b_mod):
    tn = 1536
    n = N_MOD * D_MODEL
    return pl.pallas_call(
        _mod_kernel,
        grid=(DEPTH, n // tn),
        in_specs=[pl.BlockSpec((MOD_ROWS, D_MODEL), lambda i, j: (0, 0)),
                  pl.BlockSpec((1, D_MODEL, tn), lambda i, j: (i, 0, j)),
                  pl.BlockSpec((1, 1, tn), lambda i, j: (i, 0, j))],
        out_specs=pl.BlockSpec((1, MOD_ROWS, tn), lambda i, j: (i, 0, j)),
        out_shape=jax.ShapeDtypeStruct((DEPTH, MOD_ROWS, n), F32),
        compiler_params=_params(("arbitrary", "arbitrary"), 40),
        name="modulation",
    )(cvec, w_mod, b_mod.reshape(DEPTH, 1, n))


def _inproj_kernel(*refs, add_pos, groups):
    it = iter(refs)
    x_ref = next(it)
    pos_ref = next(it) if add_pos else None
    g_ref, shift_ref, scale_ref, w_ref = next(it), next(it), next(it), next(it)
    out_refs = list(it)
    tm = x_ref.shape[1]
    n_sub = 2 if tm >= 512 else 1
    subs = [slice(i * tm // n_sub, (i + 1) * tm // n_sub) for i in range(n_sub)]
    hs = []
    for r in subs:
        x = x_ref[0, r, :]
        if add_pos:
            x = x + pos_ref[r, :]
        hs.append((_rms(x, g_ref[...]) * (1.0 + scale_ref[...]) + shift_ref[...]).astype(BF16))
    for r, h in zip(subs, hs):
        for o_ref, (_, lo, hi) in zip(out_refs, groups):
            o_ref[0, r, :] = _dot(h, w_ref[:, lo:hi]).astype(BF16)


def _inproj(x, pos, lay, i, mod_row, groups, tm):
    b, l, d = x.shape
    add_pos = pos is not None
    tok = lambda bi, ti: (bi, ti, 0)
    in_specs = [pl.BlockSpec((1, tm, d), tok)]
    args = [x]
    if add_pos:
        in_specs.append(pl.BlockSpec((tm, d), lambda bi, ti: (ti, 0)))
        args.append(pos)
    in_specs += [_layer(lay["g_mix_pre"], i), _mod_spec(i, 0, mod_row), _mod_spec(i, 1, mod_row),
                 _layer(lay["w_in"], i)]
    args += [lay["g_mix_pre"], lay["mod"], lay["mod"], lay["w_in"]]
    outs = pl.pallas_call(
        functools.partial(_inproj_kernel, add_pos=add_pos, groups=groups),
        grid=(b, l // tm),
        in_specs=in_specs,
        out_specs=[pl.BlockSpec((1, tm, hi - lo), tok) for _, lo, hi in groups],
        out_shape=[jax.ShapeDtypeStruct((b, l, hi - lo), BF16) for _, lo, hi in groups],
        compiler_params=_params(("parallel", "parallel"), 48),
        name="inproj",
    )(*args)
    return {name: o for (name, _, _), o in zip(groups, outs)}


def _gla_kernel(*refs, seq, with_output, with_state):
    it = iter(refs)
    k_ref, v_ref, lr_ref, wd_ref, bd_ref, s0_ref = (next(it) for _ in range(6))
    if with_output:
        q_ref, r_ref, nw_ref = next(it), next(it), next(it)
        y_ref = next(it)
    if with_state:
        s_out_ref = next(it)
    st_ref = next(it)
    lg_hi, lg_lo = next(it), next(it)
    ds_scr = next(it)
    dec_scr = next(it)
    if with_output:
        o_scr = next(it)
        qst_scr = next(it)
    n_chunks = seq // CHUNK

    st_ref[...] = s0_ref[0]

    row = lax.broadcasted_iota(jnp.int32, (CHUNK, CHUNK), 0)
    col = lax.broadcasted_iota(jnp.int32, (CHUNK, CHUNK), 1)
    cum = (jnp.where(row >= col, 1.0, 0.0).astype(BF16), jnp.where(row <= col, 1.0, 0.0).astype(BF16))
    lane_head = lax.broadcasted_iota(jnp.int32, (1, K_W), 1) // DK
    head_mask = [lane_head == h for h in range(HEADS)]
    srow = lax.broadcasted_iota(jnp.int32, (HEADS * CHUNK, CHUNK), 0) % CHUNK
    scol = lax.broadcasted_iota(jnp.int32, (HEADS * CHUNK, CHUNK), 1)
    score_mask = (srow >= scol, srow <= scol)
    last_row = (CHUNK - 1, 0)
    mid_row = (CHUNK // 2 - 1, CHUNK // 2)

    def stack_heads(a):
        zero = jnp.zeros_like(a)
        return jnp.concatenate([jnp.where(head_mask[h], a, zero) for h in range(HEADS)], axis=0)

    def chunk_rows(n):
        return pl.ds(pl.multiple_of(n * CHUNK, CHUNK), CHUNK)

    gb = min(seq, 512)

    def gates(i, carry):
        rows = pl.ds(pl.multiple_of(i * gb, gb), gb)
        z = _dot(lr_ref[0, rows, :], wd_ref[...]) + bd_ref[...]
        logg = (jnp.minimum(z, 0.0) - jnp.log(1.0 + jnp.exp(-jnp.abs(z)))) * (1.0 / TAU)
        hi = logg.astype(BF16)
        lg_hi[rows, :] = hi
        lg_lo[rows, :] = (logg - hi.astype(F32)).astype(BF16)
        return carry

    lax.fori_loop(0, seq // gb, gates, 0)

    per_iter = 4

    def chunk_local(i, carry):
        loaded = []
        for u in range(per_iter):
            n = i * per_iter + u
            rows = chunk_rows(n)
            k = k_ref[0, rows, :].astype(F32)
            v = v_ref[0, rows, :]
            qs = q_ref[0, rows, :].astype(F32) * (DK ** -0.5) if with_output else None
            lg = [(lg_hi[rows, d * K_W:(d + 1) * K_W], lg_lo[rows, d * K_W:(d + 1) * K_W]) for d in range(2)]
            loaded.append((n, rows, k, v, qs, lg))
        chains = [(n, rows, k, v, qs, lg[dirn], dirn) for n, rows, k, v, qs, lg in loaded for dirn in range(2)]
        stores = []
        gs = [_dot(cum[dirn], hi) + _dot(cum[dirn], lo) for *_, (hi, lo), dirn in chains]
        outers, score_list = [], []
        for (n, rows, k, v, qs, _, dirn), g in zip(chains, gs):
            g_last = g[last_row[dirn]:last_row[dirn] + 1]
            stores.append((dec_scr, (dirn, pl.ds(pl.multiple_of(n * 8, 8), 8), slice(None)),
                           jnp.broadcast_to(jnp.exp(g_last), (8, K_W))))
            if with_output:
                g_mid = g[mid_row[dirn]:mid_row[dirn] + 1]
                q_in = stack_heads((qs * jnp.exp(g - g_mid)).astype(BF16))
                k_in = (k * jnp.exp(g_mid - g)).astype(BF16)
                score_list.append(_dot_nt(q_in, k_in))
                stores.append((qst_scr, (dirn, rows, slice(None)), (qs * jnp.exp(g)).astype(BF16)))
            outers.append(_dot_tn(v, (k * jnp.exp(g_last - g)).astype(BF16)))
        for (n, rows, k, v, qs, _, dirn), outer in zip(chains, outers):
            upd = jnp.zeros((DV, K_W), F32)
            for h in range(HEADS):
                upd = jnp.where(head_mask[h], outer[h * DV:(h + 1) * DV], upd)
            stores.append((ds_scr, (dirn, pl.ds(pl.multiple_of(n * DV, DV), DV), slice(None)), upd))
        if with_output:
            for (n, rows, k, v, qs, _, dirn), sc in zip(chains, score_list):
                scores = jnp.where(score_mask[dirn], sc, 0.0).astype(BF16)
                for h in range(HEADS):
                    hs = slice(h * CHUNK, (h + 1) * CHUNK)
                    vs = slice(h * DV, (h + 1) * DV)
                    stores.append((o_scr, (dirn, rows, vs), _dot(scores[hs], v[:, vs])))
        for ref, idx, val in stores:
            ref[idx] = val
        return carry

    lax.fori_loop(0, n_chunks // per_iter, chunk_local, 0)

    def scan(i, carry):
        steps = ((0, i), (1, n_chunks - 1 - i))
        new_st, inters = [], []
        for dirn, n in steps:
            st = st_ref[dirn]
            if with_output:
                q_st = stack_heads(qst_scr[dirn, chunk_rows(n), :])
                inters.append(_dot_nt(q_st, st.astype(BF16)))
            dec = dec_scr[dirn, pl.ds(pl.multiple_of(n * 8, 8), 8), :]
            new_st.append(dec[0:1] * st + ds_scr[dirn, pl.ds(pl.multiple_of(n * DV, DV), DV), :])
        for (dirn, n), st in zip(steps, new_st):
            st_ref[dirn] = st
        if with_output:
            for (dirn, n), inter in zip(steps, inters):
                for h in range(HEADS):
                    vs = slice(h * DV, (h + 1) * DV)
                    o_scr[dirn, chunk_rows(n), vs] += inter[h * CHUNK:(h + 1) * CHUNK]
        return carry

    lax.fori_loop(0, n_chunks, scan, 0, unroll=min(8, n_chunks))

    if with_state:
        s_out_ref[0] = st_ref[...]

    if with_output:
        eb = min(seq, 256)

        def finish(i, carry):
            rows = pl.ds(pl.multiple_of(i * eb, eb), eb)
            o = o_scr[0, rows, :] + o_scr[1, rows, :]
            gate = _silu(r_ref[0, rows, :].astype(F32))
            for h in range(HEADS):
                vs = slice(h * DV, (h + 1) * DV)
                y_ref[0, rows, vs] = (_rms(o[:, vs], nw_ref[:, vs]) * gate[:, vs]).astype(BF16)
            return carry

        lax.fori_loop(0, seq // eb, finish, 0)


def _gla(p, lay, i, s0, with_output, with_state):
    b, l, _ = p["k"].shape
    seq3 = lambda bi: (bi, 0, 0)
    in_specs = [pl.BlockSpec((1, l, K_W), seq3), pl.BlockSpec((1, l, V_W), seq3),
                pl.BlockSpec((1, l, LR_PAD), seq3), _layer(lay["w_dec"], i), _layer(lay["b_dec"], i),
                pl.BlockSpec((1, 2, DV, K_W), lambda bi: (bi, 0, 0, 0))]
    args = [p["k"], p["v"], p["lr"], lay["w_dec"], lay["b_dec"], s0]
    out_specs, out_shape = [], []
    scratch = [pltpu.VMEM((2, DV, K_W), F32), pltpu.VMEM((l, 2 * K_W), BF16), pltpu.VMEM((l, 2 * K_W), BF16),
               pltpu.VMEM((2, l // CHUNK * DV, K_W), F32), pltpu.VMEM((2, l // CHUNK * 8, K_W), F32)]
    if with_output:
        in_specs += [pl.BlockSpec((1, l, K_W), seq3), pl.BlockSpec((1, l, V_W), seq3),
                     _layer(lay["gla_norm_w"], i)]
        args += [p["q"], p["r"], lay["gla_norm_w"]]
        out_specs.append(pl.BlockSpec((1, l, V_W), seq3))
        out_shape.append(jax.ShapeDtypeStruct((b, l, V_W), BF16))
        scratch += [pltpu.VMEM((2, l, V_W), F32), pltpu.VMEM((2, l, K_W), BF16)]
    if with_state:
        out_specs.append(pl.BlockSpec((1, 2, DV, K_W), lambda bi: (bi, 0, 0, 0)))
        out_shape.append(jax.ShapeDtypeStruct((b, 2, DV, K_W), F32))
    outs = pl.pallas_call(
        functools.partial(_gla_kernel, seq=l, with_output=with_output, with_state=with_state),
        grid=(b,),
        in_specs=in_specs,
        out_specs=out_specs,
        out_shape=out_shape,
        scratch_shapes=scratch,
        compiler_params=_params(("parallel",), 48),
        name="gla",
    )(*args)
    outs = list(outs)
    y = outs.pop(0) if with_output else None
    s = outs.pop(0) if with_state else None
    return y, s


def _four_conv_kernel(uf_ref, bg_ref, cg_ref, uc_ref, cw_ref, cb_ref, chan_ref, pos_ref,
                      yf_ref, yc_ref, ab_scr, z_scr, *, seq):
    ab = _dot(uf_ref[0], chan_ref[...])
    ab_scr[0:seq, :] = ab[:, :FOUR_W].astype(BF16)
    ab_scr[seq:2 * seq, :] = ab[:, FOUR_W:].astype(BF16)
    rb = min(seq, 512)
    for i in range(seq // rb):
        yf_ref[0, i * rb:(i + 1) * rb, :] = _dot(pos_ref[i * rb:(i + 1) * rb, :], ab_scr[...]).astype(BF16)
    _short_conv(bg_ref, cg_ref, uc_ref, cw_ref, cb_ref, yc_ref, z_scr, seq)


CONV_PAD = 8


def _short_conv(bg_ref, cg_ref, uc_ref, cw_ref, cb_ref, yc_ref, z_scr, seq):
    z_scr[0:CONV_PAD, :] = jnp.zeros((CONV_PAD, CONV_W), F32)
    z_scr[CONV_PAD + seq:, :] = jnp.zeros((CONV_PAD, CONV_W), F32)
    rb = min(seq, 256)
    for i in range(seq // rb):
        r = slice(i * rb, (i + 1) * rb)
        z_scr[CONV_PAD + i * rb:CONV_PAD + (i + 1) * rb, :] = cg_ref[0, r, :].astype(F32) * uc_ref[0, r, :].astype(F32)
    for i in range(seq // rb):
        r = slice(i * rb, (i + 1) * rb)
        s = CONV_PAD + i * rb
        y = (cw_ref[0:1] * z_scr[s - 1:s - 1 + rb, :] + cw_ref[1:2] * z_scr[s:s + rb, :]
             + cw_ref[2:3] * z_scr[s + 1:s + 1 + rb, :] + cb_ref[...])
        yc_ref[0, r, :] = (bg_ref[0, r, :].astype(F32) * y).astype(BF16)


FFT_RADIX = 8


def _four_conv_fft_kernel(uf_ref, bg_ref, cg_ref, uc_ref, cw_ref, cb_ref, chan_ref, sub_ref, twc_ref, tws_ref,
                          yf_ref, yc_ref, uf_scr, z_scr, *, seq):
    n = seq // FFT_RADIX
    lanes = uf_scr.shape[2]
    n_lane_tiles = FOUR_W // lanes
    for j in range(n_lane_tiles):
        uf_scr[j] = uf_ref[0, :, j * lanes:(j + 1) * lanes].astype(F32)

    def tokens(r):
        parts = [uf_scr[j, pl.ds(r, n, stride=FFT_RADIX), :] for j in range(n_lane_tiles)]
        return jnp.concatenate(parts, axis=1).astype(BF16)

    ab = [_dot(tokens(r), chan_ref[...]) for r in range(FFT_RADIX)]
    pq = []
    for ab_r in ab:
        a, b = ab_r[:, :FOUR_W].astype(BF16), ab_r[:, FOUR_W:].astype(BF16)
        w = jnp.concatenate([jnp.concatenate([a, b], axis=1), jnp.concatenate([-b, a], axis=1)], axis=0)
        pq.append(_dot(sub_ref[...], w))
    re, im = [pq[0][:, :FOUR_W]], [pq[0][:, FOUR_W:]]
    for r in range(1, FFT_RADIX):
        p, q = pq[r][:, :FOUR_W], pq[r][:, FOUR_W:]
        c, s = twc_ref[r - 1], tws_ref[r - 1]
        re.append(c * p - s * q)
        im.append(s * p + c * q)

    def quarter(z0, z1, z2, z3):
        s02 = (z0[0] + z2[0], z0[1] + z2[1])
        d02 = (z0[0] - z2[0], z0[1] - z2[1])
        s13 = (z1[0] + z3[0], z1[1] + z3[1])
        d13 = (z1[0] - z3[0], z1[1] - z3[1])
        return ((s02[0] + s13[0], s02[1] + s13[1]), (d02[0] - d13[1], d02[1] + d13[0]),
                (s02[0] - s13[0], s02[1] - s13[1]), (d02[0] + d13[1], d02[1] - d13[0]))

    zs = list(zip(re, im))
    even = quarter(zs[0], zs[2], zs[4], zs[6])
    odd = quarter(zs[1], zs[3], zs[5], zs[7])
    half = 0.5 ** 0.5
    odd_re = (odd[0][0], (odd[1][0] - odd[1][1]) * half, -odd[2][1], -(odd[3][0] + odd[3][1]) * half)
    for q in range(4):
        yf_ref[0, q * n:(q + 1) * n, :] = (even[q][0] + odd_re[q]).astype(BF16)
        yf_ref[0, (q + 4) * n:(q + 5) * n, :] = (even[q][0] - odd_re[q]).astype(BF16)
    _short_conv(bg_ref, cg_ref, uc_ref, cw_ref, cb_ref, yc_ref, z_scr, seq)


def _four_conv(p, lay, i, tabs):
    b, l, _ = p["uf"].shape
    seq3 = lambda bi: (bi, 0, 0)
    tok = pl.BlockSpec((1, l, FOUR_W), seq3)
    factored = len(tabs) == 4
    if factored:
        body, name = _four_conv_fft_kernel, "four_conv_fft"
        dft_scratch = pltpu.VMEM((FOUR_W // 128, l, 128), F32)
    else:
        body, name = _four_conv_kernel, "four_conv"
        dft_scratch = pltpu.VMEM((2 * l, FOUR_W), BF16)
    return pl.pallas_call(
        functools.partial(body, seq=l),
        grid=(b,),
        in_specs=[tok, tok, tok, tok, _layer(lay["conv_w"], i), _layer(lay["conv_b"], i)]
                 + [_resident(t.shape) for t in tabs],
        out_specs=[tok, tok],
        out_shape=[jax.ShapeDtypeStruct((b, l, FOUR_W), BF16)] * 2,
        scratch_shapes=[dft_scratch, pltpu.VMEM((l + 2 * CONV_PAD, CONV_W), F32)],
        compiler_params=_params(("parallel",), 48),
        name=name,
    )(p["uf"], p["bg"], p["cg"], p["uc"], lay["conv_w"], lay["conv_b"], *tabs)


_FFN_TILES = ((0, 1536), (1536, HIDDEN))


def _out_ffn_kernel(*refs, add_pos, n_sub):
    it = iter(refs)
    yg_ref, yf_ref, yc_ref, wo_ref, x_ref = (next(it) for _ in range(5))
    pos_ref = next(it) if add_pos else None
    gmix_ref, gatemix_ref, gpre_ref, shift_ref, scale_ref, gate_ref, gpost_ref = (next(it) for _ in range(7))
    win_ref, wout_ref, o_ref = next(it), next(it), next(it)
    tm = x_ref.shape[1]
    subs = [slice(i * tm // n_sub, (i + 1) * tm // n_sub) for i in range(n_sub)]
    ys = [(_dot(yg_ref[0, r, :], wo_ref[0:V_W]) + _dot(yf_ref[0, r, :], wo_ref[V_W:V_W + FOUR_W])
           + _dot(yc_ref[0, r, :], wo_ref[V_W + FOUR_W:])) for r in subs]
    xs, hs = [], []
    for r, y in zip(subs, ys):
        x = x_ref[0, r, :]
        if add_pos:
            x = x + pos_ref[r, :]
        x = x + gatemix_ref[...] * _rms(y, gmix_ref[...])
        xs.append(x)
        hs.append((_rms(x, gpre_ref[...]) * (1.0 + scale_ref[...]) + shift_ref[...]).astype(BF16))
    ys = []
    for h in hs:
        y = jnp.zeros((h.shape[0], D_MODEL), F32)
        for lo, hi in _FFN_TILES:
            a = _dot(h, win_ref[:, lo:hi])
            u = _dot(h, win_ref[:, HIDDEN + lo:HIDDEN + hi])
            y = y + _dot((_silu(a) * u).astype(BF16), wout_ref[lo:hi, :])
        ys.append(y)
    for r, x, y in zip(subs, xs, ys):
        o_ref[0, r, :] = x + gate_ref[...] * _rms(y, gpost_ref[...])


def _out_ffn(yg, yf, yc, x, pos, lay, i, mod_row, tm):
    b, l, d = x.shape
    add_pos = pos is not None
    tok = lambda bi, ti: (bi, ti, 0)
    in_specs = [pl.BlockSpec((1, tm, V_W), tok), pl.BlockSpec((1, tm, FOUR_W), tok),
                pl.BlockSpec((1, tm, CONV_W), tok), _layer(lay["w_out"], i), pl.BlockSpec((1, tm, d), tok)]
    args = [yg, yf, yc, lay["w_out"], x]
    if add_pos:
        in_specs.append(pl.BlockSpec((tm, d), lambda bi, ti: (ti, 0)))
        args.append(pos)
    in_specs += [_layer(lay["g_mix_post"], i), _mod_spec(i, 2, mod_row), _layer(lay["g_ffn_pre"], i),
                 _mod_spec(i, 3, mod_row), _mod_spec(i, 4, mod_row), _mod_spec(i, 5, mod_row),
                 _layer(lay["g_ffn_post"], i), _layer(lay["w_ffn_in"], i), _layer(lay["w_ffn_out"], i)]
    args += [lay["g_mix_post"], lay["mod"], lay["g_ffn_pre"], lay["mod"], lay["mod"], lay["mod"],
             lay["g_ffn_post"], lay["w_ffn_in"], lay["w_ffn_out"]]
    return pl.pallas_call(
        functools.partial(_out_ffn_kernel, add_pos=add_pos, n_sub=2 if tm >= 512 else 1),
        grid=(b, l // tm),
        in_specs=in_specs,
        out_specs=pl.BlockSpec((1, tm, d), tok),
        out_shape=jax.ShapeDtypeStruct((b, l, d), F32),
        compiler_params=_params(("parallel", "parallel"), 56),
        name="out_ffn",
    )(*args)


def _arrange_w_in(w):
    main = jnp.concatenate([w[..., _C_K:_C_LR], w[..., _C_Q:_C_END]], axis=-1)
    lr = jnp.pad(w[..., _C_LR:_C_Q], ((0, 0), (0, 0), (0, LR_PAD - 2 * RANK)))
    return jnp.concatenate([main, lr], axis=-1).astype(BF16)


def _arrange_w_dec(w_f, w_b):
    wd = jnp.zeros((DEPTH, LR_PAD, 2 * K_W), F32)
    wd = wd.at[:, 0:RANK, :K_W].set(w_f).at[:, RANK:2 * RANK, K_W:].set(w_b)
    return wd.astype(BF16)


def kernel(x, c, ctx, c_ctx, w_mod, b_mod, g_mix_pre, g_mix_post, g_ffn_pre, g_ffn_post,
           w_in, w_dec_f, b_dec_f, w_dec_b, b_dec_b, gla_norm_w, conv_w, conv_b, w_out,
           w_ffn_in, w_ffn_out):
    b, n_lat, d = x.shape
    n_ctx = ctx.shape[1]
    assert d == D_MODEL and w_in.shape == (DEPTH, D_MODEL, _C_END)
    assert n_lat % 512 == 0 and n_ctx % CHUNK == 0 and b + 1 <= MOD_ROWS

    pos = jnp.asarray(_pos_embed(n_lat, d))
    tabs = {n_lat: _dft_tables(n_lat, FFT_RADIX), n_ctx: _dft_tables(n_ctx, 1)}

    cvec = jnp.zeros((MOD_ROWS, d), F32).at[:b].set(c).at[b].set(c_ctx)
    ctx_row = b
    lay = {
        "mod": _modulation(cvec, w_mod, b_mod).reshape(DEPTH, MOD_ROWS, 1, N_MOD * d),
        "w_in": _arrange_w_in(w_in),
        "w_dec": _arrange_w_dec(w_dec_f, w_dec_b),
        "b_dec": jnp.concatenate([b_dec_f, b_dec_b], axis=-1).reshape(DEPTH, 1, 2 * K_W),
        "gla_norm_w": gla_norm_w.reshape(DEPTH, 1, V_W),
        "conv_w": conv_w,
        "conv_b": conv_b.reshape(DEPTH, 1, CONV_W),
        "w_out": w_out.astype(BF16),
        "w_ffn_in": w_ffn_in.astype(BF16),
        "w_ffn_out": w_ffn_out.astype(BF16),
        "g_mix_pre": g_mix_pre.reshape(DEPTH, 1, d),
        "g_mix_post": g_mix_post.reshape(DEPTH, 1, d),
        "g_ffn_pre": g_ffn_pre.reshape(DEPTH, 1, d),
        "g_ffn_post": g_ffn_post.reshape(DEPTH, 1, d),
    }

    def mixer_and_ffn(i, xs, pos_s, mod_row, s0, want_state, tm):
        p = _inproj(xs, pos_s, lay, i, mod_row, _OUT_GROUPS_FULL, tm)
        yg, s = _gla(p, lay, i, s0, True, want_state)
        yf, yc = _four_conv(p, lay, i, tabs[xs.shape[1]])
        return _out_ffn(yg, yf, yc, xs, pos_s, lay, i, mod_row, tm), s

    zero_state = jnp.zeros((b, 2, DV, K_W), F32)
    xc = ctx
    for i in range(DEPTH):
        if i == DEPTH - 1:
            p = _inproj(xc, None, lay, i, ctx_row, _OUT_GROUPS_STATE, n_ctx)
            _, s = _gla(p, lay, i, zero_state, False, True)
        else:
            xc, s = mixer_and_ffn(i, xc, None, ctx_row, zero_state, True, n_ctx)
        x, _ = mixer_and_ffn(i, x, pos if i == 0 else None, None, s, False, 512)
    return x
```

```python
import functools

import numpy as np
import jax
import jax.numpy as jnp
from jax import lax
from jax.experimental import pallas as pl
from jax.experimental.pallas import tpu as pltpu

F32 = jnp.float32
BF16 = jnp.bfloat16

D_MODEL = 1024
DEPTH = 2
GRID_W = 64
N_MOD = 6
HEADS = 4
DK = 64
DV = 128
K_W = HEADS * DK
V_W = HEADS * DV
RANK = 16
TAU = 16.0
CHUNK = 64
FOUR_W = 256
FOUR_G = 4
CONV_W = 256
HIDDEN = 2816
EPS = 1e-6
LR_PAD = 128
MOD_ROWS = 16

_C_K, _C_V, _C_LR, _C_Q, _C_R, _C_FOUR, _C_BG, _C_CG, _C_UC, _C_END = (
    0, 256, 768, 800, 1056, 1568, 1824, 2080, 2336, 2592)
_MAIN_W = 2560
_IN_W = _MAIN_W + LR_PAD
_OUT_GROUPS_FULL = (("k", 0, 256), ("v", 256, 768), ("q", 768, 1024), ("r", 1024, 1536),
                    ("uf", 1536, 1792), ("bg", 1792, 2048), ("cg", 2048, 2304),
                    ("uc", 2304, 2560), ("lr", 2560, 2688))
_OUT_GROUPS_STATE = (("k", 0, 256), ("v", 256, 768), ("lr", 2560, 2688))

V7X_VMEM_BYTES = 64 * 1024 * 1024


def _params(semantics, vmem_mb):
    assert vmem_mb * 1024 * 1024 < V7X_VMEM_BYTES
    return pltpu.CompilerParams(dimension_semantics=semantics,
                                vmem_limit_bytes=vmem_mb * 1024 * 1024)


def _resident(shape):
    zeros = (0,) * len(shape)
    return pl.BlockSpec(shape, lambda *_: zeros, pipeline_mode=pl.Buffered(1))


def _layer(arr, i):
    shape = tuple(arr.shape[1:])
    zeros = (0,) * len(shape)
    return pl.BlockSpec((None,) + shape, lambda *_: (i,) + zeros, pipeline_mode=pl.Buffered(1))


def _mod_spec(i, j, row):
    if row is None:
        return pl.BlockSpec((None, None, 1, D_MODEL), lambda bi, *_: (i, bi, 0, j))
    return pl.BlockSpec((None, None, 1, D_MODEL), lambda *_: (i, row, 0, j))


def _silu(a):
    return a / (1.0 + jnp.exp(-a))


def _rms(x, g):
    return x * lax.rsqrt(jnp.mean(x * x, axis=-1, keepdims=True) + EPS) * g


def _dot(a, b):
    return jnp.dot(a, b, preferred_element_type=F32)


def _dot_nt(a, b):
    return lax.dot_general(a, b, (((1,), (1,)), ((), ())), preferred_element_type=F32)


def _dot_tn(a, b):
    return lax.dot_general(a, b, (((0,), (0,)), ((), ())), preferred_element_type=F32)


def _pos_embed(n_tokens, dim):
    rows = n_tokens // GRID_W
    row = np.repeat(np.arange(rows, dtype=np.float32), GRID_W)
    col = np.tile(np.arange(GRID_W, dtype=np.float32), rows)
    quarter = dim // 4
    freqs = (1.0 / (10000.0 ** (np.arange(quarter, dtype=np.float32) / quarter))).astype(np.float32)

    def enc(p):
        ang = (p[:, None] * freqs[None, :]).astype(np.float32)
        return np.concatenate([np.sin(ang), np.cos(ang)], axis=-1)

    return np.concatenate([enc(row), enc(col)], axis=-1).astype(np.float32)


def _dft_cos_sin(n):
    idx = np.arange(n, dtype=np.int64)
    ang = 2.0 * np.pi * ((idx[:, None] * idx[None, :]) % n).astype(np.float64) / n
    return np.cos(ang) / np.sqrt(n), np.sin(ang) / np.sqrt(n)


def _dft_tables(seq, radix):
    gw = FOUR_W // FOUR_G
    cc, sc = _dft_cos_sin(gw)
    eye = np.eye(FOUR_G)
    chan_tab = jnp.asarray(np.concatenate([np.kron(eye, cc), np.kron(eye, sc)], axis=1), dtype=F32).astype(BF16)
    if radix == 1:
        cl, sl = _dft_cos_sin(seq)
        return chan_tab, jnp.asarray(np.concatenate([cl, -sl], axis=1), dtype=F32).astype(BF16)
    n = seq // radix
    cn, sn = _dft_cos_sin(n)
    sub_tab = np.concatenate([cn, sn], axis=1) * np.sqrt(n / seq)
    ang = 2.0 * np.pi * (np.arange(1, radix)[:, None] * np.arange(n)[None, :]).astype(np.float64) / seq
    twc = np.broadcast_to(np.cos(ang)[:, :, None], (radix - 1, n, FOUR_W))
    tws = np.broadcast_to(np.sin(ang)[:, :, None], (radix - 1, n, FOUR_W))
    return (chan_tab, jnp.asarray(sub_tab, dtype=F32).astype(BF16),
            jnp.asarray(twc, dtype=F32), jnp.asarray(tws, dtype=F32))


def _mod_kernel(c_ref, w_ref, b_ref, o_ref):
    s = _silu(c_ref[...]).astype(BF16)
    o_ref[0] = _dot(s, w_ref[0].astype(BF16)) + b_ref[0]


def _modulation(cvec, w_mod, b_mod):
    tn = 1536
    n = N_MOD * D_MODEL
    return pl.pallas_call(
        _mod_kernel,
        grid=(DEPTH, n // tn),
        in_specs=[pl.BlockSpec((MOD_ROWS, D_MODEL), lambda i, j: (0, 0)),
                  pl.BlockSpec((1, D_MODEL, tn), lambda i, j: (i, 0, j)),
                  pl.BlockSpec((1, 1, tn), lambda i, j: (i, 0, j))],
        out_specs=pl.BlockSpec((1, MOD_ROWS, tn), lambda i, j: (i, 0, j)),
        out_shape=jax.ShapeDtypeStruct((DEPTH, MOD_ROWS, n), F32),
        compiler_params=_params(("arbitrary", "arbitrary"), 40),
        name="modulation",
    )(cvec, w_mod, b_mod.reshape(DEPTH, 1, n))


def _inproj_kernel(*refs, add_pos, groups):
    it = iter(refs)
    x_ref = next(it)
    pos_ref = next(it) if add_pos else None
    g_ref, shift_ref, scale_ref, w_ref = next(it), next(it), next(it), next(it)
    out_refs = list(it)
    tm = x_ref.shape[1]
    n_sub = 2 if tm >= 512 else 1
    subs = [slice(i * tm // n_sub, (i + 1) * tm // n_sub) for i in range(n_sub)]
    hs = []
    for r in subs:
        x = x_ref[0, r, :]
        if add_pos:
            x = x + pos_ref[r, :]
        hs.append((_rms(x, g_ref[...]) * (1.0 + scale_ref[...]) + shift_ref[...]).astype(BF16))
    for r, h in zip(subs, hs):
        for o_ref, (_, lo, hi) in zip(out_refs, groups):
            o_ref[0, r, :] = _dot(h, w_ref[:, lo:hi]).astype(BF16)


def _inproj(x, pos, lay, i, mod_row, groups, tm):
    b, l, d = x.shape
    add_pos = pos is not None
    tok = lambda bi, ti: (bi, ti, 0)
    in_specs = [pl.BlockSpec((1, tm, d), tok)]
    args = [x]
    if add_pos:
        in_specs.append(pl.BlockSpec((tm, d), lambda bi, ti: (ti, 0)))
        args.append(pos)
    in_specs += [_layer(lay["g_mix_pre"], i), _mod_spec(i, 0, mod_row), _mod_spec(i, 1, mod_row),
                 _layer(lay["w_in"], i)]
    args += [lay["g_mix_pre"], lay["mod"], lay["mod"], lay["w_in"]]
    outs = pl.pallas_call(
        functools.partial(_inproj_kernel, add_pos=add_pos, groups=groups),
        grid=(b, l // tm),
        in_specs=in_specs,
        out_specs=[pl.BlockSpec((1, tm, hi - lo), tok) for _, lo, hi in groups],
        out_shape=[jax.ShapeDtypeStruct((b, l, hi - lo), BF16) for _, lo, hi in groups],
        compiler_params=_params(("parallel", "parallel"), 48),
        name="inproj",
    )(*args)
    return {name: o for (name, _, _), o in zip(groups, outs)}


def _gla_kernel(*refs, seq, with_output, with_state):
    it = iter(refs)
    k_ref, v_ref, lr_ref, wd_ref, bd_ref, s0_ref = (next(it) for _ in range(6))
    if with_output:
        q_ref, r_ref, nw_ref = next(it), next(it), next(it)
        y_ref = next(it)
    if with_state:
        s_out_ref = next(it)
    st_ref = next(it)
    lg_hi, lg_lo = next(it), next(it)
    ds_scr = next(it)
    dec_scr = next(it)
    if with_output:
        o_scr = next(it)
        qst_scr = next(it)
    n_chunks = seq // CHUNK

    st_ref[...] = s0_ref[0]

    row = lax.broadcasted_iota(jnp.int32, (CHUNK, CHUNK), 0)
    col = lax.broadcasted_iota(jnp.int32, (CHUNK, CHUNK), 1)
    cum = (jnp.where(row >= col, 1.0, 0.0).astype(BF16), jnp.where(row <= col, 1.0, 0.0).astype(BF16))
    lane_head = lax.broadcasted_iota(jnp.int32, (1, K_W), 1) // DK
    head_mask = [lane_head == h for h in range(HEADS)]
    srow = lax.broadcasted_iota(jnp.int32, (HEADS * CHUNK, CHUNK), 0) % CHUNK
    scol = lax.broadcasted_iota(jnp.int32, (HEADS * CHUNK, CHUNK), 1)
    score_mask = (srow >= scol, srow <= scol)
    last_row = (CHUNK - 1, 0)
    mid_row = (CHUNK // 2 - 1, CHUNK // 2)

    def stack_heads(a):
        zero = jnp.zeros_like(a)
        return jnp.concatenate([jnp.where(head_mask[h], a, zero) for h in range(HEADS)], axis=0)

    def chunk_rows(n):
        return pl.ds(pl.multiple_of(n * CHUNK, CHUNK), CHUNK)

    per_iter = 4
    n_groups = n_chunks // per_iter
    gb = per_iter * CHUNK

    def gate_pieces(grp):
        rows = pl.ds(pl.multiple_of(grp * gb, gb), gb)
        z = _dot(lr_ref[0, rows, :], wd_ref[...]) + bd_ref[...]
        logg = (jnp.minimum(z, 0.0) - jnp.log(1.0 + jnp.exp(-jnp.abs(z)))) * (1.0 / TAU)
        hi = logg.astype(BF16)
        return [(lg_hi, (rows, slice(None)), hi),
                (lg_lo, (rows, slice(None)), (logg - hi.astype(F32)).astype(BF16))]

    for ref, idx, val in gate_pieces(0):
        ref[idx] = val

    def chunk_local(i, carry):
        next_gates = gate_pieces(jnp.minimum(i + 1, n_groups - 1))
        loaded = []
        for u in range(per_iter):
            n = i * per_iter + u
            rows = chunk_rows(n)
            k = k_ref[0, rows, :].astype(F32)
            v = v_ref[0, rows, :]
            qs = q_ref[0, rows, :].astype(F32) * (DK ** -0.5) if with_output else None
            lg = [(lg_hi[rows, d * K_W:(d + 1) * K_W], lg_lo[rows, d * K_W:(d + 1) * K_W]) for d in range(2)]
            loaded.append((n, rows, k, v, qs, lg))
        chains = [(n, rows, k, v, qs, lg[dirn], dirn) for n, rows, k, v, qs, lg in loaded for dirn in range(2)]
        stores = []
        gs = [_dot(cum[dirn], hi) + _dot(cum[dirn], lo) for *_, (hi, lo), dirn in chains]
        outers, score_list = [], []
        for (n, rows, k, v, qs, _, dirn), g in zip(chains, gs):
            g_last = g[last_row[dirn]:last_row[dirn] + 1]
            stores.append((dec_scr, (dirn, pl.ds(pl.multiple_of(n * 8, 8), 8), slice(None)),
                           jnp.broadcast_to(jnp.exp(g_last), (8, K_W))))
            if with_output:
                g_mid = g[mid_row[dirn]:mid_row[dirn] + 1]
                q_in = stack_heads((qs * jnp.exp(g - g_mid)).astype(BF16))
                k_in = (k * jnp.exp(g_mid - g)).astype(BF16)
                score_list.append(_dot_nt(q_in, k_in))
                stores.append((qst_scr, (dirn, rows, slice(None)), (qs * jnp.exp(g)).astype(BF16)))
            outers.append(_dot_tn(v, (k * jnp.exp(g_last - g)).astype(BF16)))
        for (n, rows, k, v, qs, _, dirn), outer in zip(chains, outers):
            upd = jnp.zeros((DV, K_W), F32)
            for h in range(HEADS):
                upd = jnp.where(head_mask[h], outer[h * DV:(h + 1) * DV], upd)
            stores.append((ds_scr, (dirn, pl.ds(pl.multiple_of(n * DV, DV), DV), slice(None)), upd))
        if with_output:
            for (n, rows, k, v, qs, _, dirn), sc in zip(chains, score_list):
                scores = jnp.where(score_mask[dirn], sc, 0.0).astype(BF16)
                for h in range(HEADS):
                    hs = slice(h * CHUNK, (h + 1) * CHUNK)
                    vs = slice(h * DV, (h + 1) * DV)
                    stores.append((o_scr, (dirn, rows, vs), _dot(scores[hs], v[:, vs])))
        for ref, idx, val in stores + next_gates:
            ref[idx] = val
        return carry

    lax.fori_loop(0, n_groups, chunk_local, 0)

    def scan_step(i, finish):
        steps = ((0, i), (1, n_chunks - 1 - i))
        new_st, inters = [], []
        for dirn, n in steps:
            st = st_ref[dirn]
            if with_output:
                q_st = stack_heads(qst_scr[dirn, chunk_rows(n), :])
                inters.append(_dot_nt(q_st, st.astype(BF16)))
            dec = dec_scr[dirn, pl.ds(pl.multiple_of(n * 8, 8), 8), :]
            new_st.append(dec[0:1] * st + ds_scr[dirn, pl.ds(pl.multiple_of(n * DV, DV), DV), :])
        for (dirn, n), st in zip(steps, new_st):
            st_ref[dirn] = st
        if with_output:
            for (dirn, n), inter in zip(steps, inters):
                rows = chunk_rows(n)
                gate = _silu(r_ref[0, rows, :].astype(F32)) if finish else None
                for h in range(HEADS):
                    vs = slice(h * DV, (h + 1) * DV)
                    o = o_scr[dirn, rows, vs] + inter[h * CHUNK:(h + 1) * CHUNK]
                    if finish:
                        o = o + o_scr[1 - dirn, rows, vs]
                        y_ref[0, rows, vs] = (_rms(o, nw_ref[:, vs]) * gate[:, vs]).astype(BF16)
                    else:
                        o_scr[dirn, rows, vs] = o

    half = n_chunks // 2
    unroll = min(8, half)

    def scan_first(i, carry):
        scan_step(i, False)
        return carry

    def scan_second(i, carry):
        scan_step(i, True)
        return carry

    lax.fori_loop(0, half, scan_first, 0, unroll=unroll)
    lax.fori_loop(half, n_chunks, scan_second, 0, unroll=unroll)

    if with_state:
        s_out_ref[0] = st_ref[...]


def _gla(p, lay, i, s0, with_output, with_state):
    b, l, _ = p["k"].shape
    seq3 = lambda bi: (bi, 0, 0)
    in_specs = [pl.BlockSpec((1, l, K_W), seq3), pl.BlockSpec((1, l, V_W), seq3),
                pl.BlockSpec((1, l, LR_PAD), seq3), _layer(lay["w_dec"], i), _layer(lay["b_dec"], i),
                pl.BlockSpec((1, 2, DV, K_W), lambda bi: (bi, 0, 0, 0))]
    args = [p["k"], p["v"], p["lr"], lay["w_dec"], lay["b_dec"], s0]
    out_specs, out_shape = [], []
    scratch = [pltpu.VMEM((2, DV, K_W), F32), pltpu.VMEM((l, 2 * K_W), BF16), pltpu.VMEM((l, 2 * K_W), BF16),
               pltpu.VMEM((2, l // CHUNK * DV, K_W), F32), pltpu.VMEM((2, l // CHUNK * 8, K_W), F32)]
    if with_output:
        in_specs += [pl.BlockSpec((1, l, K_W), seq3), pl.BlockSpec((1, l, V_W), seq3),
                     _layer(lay["gla_norm_w"], i)]
        args += [p["q"], p["r"], lay["gla_norm_w"]]
        out_specs.append(pl.BlockSpec((1, l, V_W), seq3))
        out_shape.append(jax.ShapeDtypeStruct((b, l, V_W), BF16))
        scratch += [pltpu.VMEM((2, l, V_W), F32), pltpu.VMEM((2, l, K_W), BF16)]
    if with_state:
        out_specs.append(pl.BlockSpec((1, 2, DV, K_W), lambda bi: (bi, 0, 0, 0)))
        out_shape.append(jax.ShapeDtypeStruct((b, 2, DV, K_W), F32))
    outs = pl.pallas_call(
        functools.partial(_gla_kernel, seq=l, with_output=with_output, with_state=with_state),
        grid=(b,),
        in_specs=in_specs,
        out_specs=out_specs,
        out_shape=out_shape,
        scratch_shapes=scratch,
        compiler_params=_params(("parallel",), 48),
        name="gla",
    )(*args)
    outs = list(outs)
    y = outs.pop(0) if with_output else None
    s = outs.pop(0) if with_state else None
    return y, s


def _four_conv_kernel(uf_ref, bg_ref, cg_ref, uc_ref, cw_ref, cb_ref, chan_ref, pos_ref,
                      yf_ref, yc_ref, ab_scr, z_scr, *, seq):
    ab = _dot(uf_ref[0], chan_ref[...])
    ab_scr[0:seq, :] = ab[:, :FOUR_W].astype(BF16)
    ab_scr[seq:2 * seq, :] = ab[:, FOUR_W:].astype(BF16)
    rb = min(seq, 512)
    for i in range(seq // rb):
        yf_ref[0, i * rb:(i + 1) * rb, :] = _dot(pos_ref[i * rb:(i + 1) * rb, :], ab_scr[...]).astype(BF16)
    _short_conv(bg_ref, cg_ref, uc_ref, cw_ref, cb_ref, yc_ref, z_scr, seq)


CONV_PAD = 8


def _short_conv(bg_ref, cg_ref, uc_ref, cw_ref, cb_ref, yc_ref, z_scr, seq):
    z_scr[0:CONV_PAD, :] = jnp.zeros((CONV_PAD, CONV_W), F32)
    z_scr[CONV_PAD + seq:, :] = jnp.zeros((CONV_PAD, CONV_W), F32)
    rb = min(seq, 256)
    for i in range(seq // rb):
        r = slice(i * rb, (i + 1) * rb)
        z_scr[CONV_PAD + i * rb:CONV_PAD + (i + 1) * rb, :] = cg_ref[0, r, :].astype(F32) * uc_ref[0, r, :].astype(F32)
    for i in range(seq // rb):
        r = slice(i * rb, (i + 1) * rb)
        s = CONV_PAD + i * rb
        y = (cw_ref[0:1] * z_scr[s - 1:s - 1 + rb, :] + cw_ref[1:2] * z_scr[s:s + rb, :]
             + cw_ref[2:3] * z_scr[s + 1:s + 1 + rb, :] + cb_ref[...])
        yc_ref[0, r, :] = (bg_ref[0, r, :].astype(F32) * y).astype(BF16)


FFT_RADIX = 8


def _four_conv_fft_kernel(uf_ref, bg_ref, cg_ref, uc_ref, cw_ref, cb_ref, chan_ref, sub_ref, twc_ref, tws_ref,
                          yf_ref, yc_ref, uf_scr, z_scr, *, seq):
    n = seq // FFT_RADIX
    lanes = uf_scr.shape[2]
    n_lane_tiles = FOUR_W // lanes
    for j in range(n_lane_tiles):
        uf_scr[j] = uf_ref[0, :, j * lanes:(j + 1) * lanes].astype(F32)

    def tokens(r):
        parts = [uf_scr[j, pl.ds(r, n, stride=FFT_RADIX), :] for j in range(n_lane_tiles)]
        return jnp.concatenate(parts, axis=1).astype(BF16)

    ab = [_dot(tokens(r), chan_ref[...]) for r in range(FFT_RADIX)]
    pq = []
    for ab_r in ab:
        a, b = ab_r[:, :FOUR_W].astype(BF16), ab_r[:, FOUR_W:].astype(BF16)
        w = jnp.concatenate([jnp.concatenate([a, b], axis=1), jnp.concatenate([-b, a], axis=1)], axis=0)
        pq.append(_dot(sub_ref[...], w))
    re, im = [pq[0][:, :FOUR_W]], [pq[0][:, FOUR_W:]]
    for r in range(1, FFT_RADIX):
        p, q = pq[r][:, :FOUR_W], pq[r][:, FOUR_W:]
        c, s = twc_ref[r - 1], tws_ref[r - 1]
        re.append(c * p - s * q)
        im.append(s * p + c * q)

    def quarter(z0, z1, z2, z3):
        s02 = (z0[0] + z2[0], z0[1] + z2[1])
        d02 = (z0[0] - z2[0], z0[1] - z2[1])
        s13 = (z1[0] + z3[0], z1[1] + z3[1])
        d13 = (z1[0] - z3[0], z1[1] - z3[1])
        return ((s02[0] + s13[0], s02[1] + s13[1]), (d02[0] - d13[1], d02[1] + d13[0]),
                (s02[0] - s13[0], s02[1] - s13[1]), (d02[0] + d13[1], d02[1] - d13[0]))

    zs = list(zip(re, im))
    even = quarter(zs[0], zs[2], zs[4], zs[6])
    odd = quarter(zs[1], zs[3], zs[5], zs[7])
    half = 0.5 ** 0.5
    odd_re = (odd[0][0], (odd[1][0] - odd[1][1]) * half, -odd[2][1], -(odd[3][0] + odd[3][1]) * half)
    for q in range(4):
        yf_ref[0, q * n:(q + 1) * n, :] = (even[q][0] + odd_re[q]).astype(BF16)
        yf_ref[0, (q + 4) * n:(q + 5) * n, :] = (even[q][0] - odd_re[q]).astype(BF16)
    _short_conv(bg_ref, cg_ref, uc_ref, cw_ref, cb_ref, yc_ref, z_scr, seq)


def _four_conv(p, lay, i, tabs):
    b, l, _ = p["uf"].shape
    seq3 = lambda bi: (bi, 0, 0)
    tok = pl.BlockSpec((1, l, FOUR_W), seq3)
    factored = len(tabs) == 4
    if factored:
        body, name = _four_conv_fft_kernel, "four_conv_fft"
        dft_scratch = pltpu.VMEM((FOUR_W // 128, l, 128), F32)
    else:
        body, name = _four_conv_kernel, "four_conv"
        dft_scratch = pltpu.VMEM((2 * l, FOUR_W), BF16)
    return pl.pallas_call(
        functools.partial(body, seq=l),
        grid=(b,),
        in_specs=[tok, tok, tok, tok, _layer(lay["conv_w"], i), _layer(lay["conv_b"], i)]
                 + [_resident(t.shape) for t in tabs],
        out_specs=[tok, tok],
        out_shape=[jax.ShapeDtypeStruct((b, l, FOUR_W), BF16)] * 2,
        scratch_shapes=[dft_scratch, pltpu.VMEM((l + 2 * CONV_PAD, CONV_W), F32)],
        compiler_params=_params(("parallel",), 48),
        name=name,
    )(p["uf"], p["bg"], p["cg"], p["uc"], lay["conv_w"], lay["conv_b"], *tabs)


_FFN_TILES = ((0, 1536), (1536, HIDDEN))


def _out_ffn_kernel(*refs, add_pos, n_sub):
    it = iter(refs)
    yg_ref, yf_ref, yc_ref, wo_ref, x_ref = (next(it) for _ in range(5))
    pos_ref = next(it) if add_pos else None
    gmix_ref, gatemix_ref, gpre_ref, shift_ref, scale_ref, gate_ref, gpost_ref = (next(it) for _ in range(7))
    win_ref, wout_ref, o_ref = next(it), next(it), next(it)
    tm = x_ref.shape[1]
    subs = [slice(i * tm // n_sub, (i + 1) * tm // n_sub) for i in range(n_sub)]
    ys = [(_dot(yg_ref[0, r, :], wo_ref[0:V_W]) + _dot(yf_ref[0, r, :], wo_ref[V_W:V_W + FOUR_W])
           + _dot(yc_ref[0, r, :], wo_ref[V_W + FOUR_W:])) for r in subs]
    xs, hs = [], []
    for r, y in zip(subs, ys):
        x = x_ref[0, r, :]
        if add_pos:
            x = x + pos_ref[r, :]
        x = x + gatemix_ref[...] * _rms(y, gmix_ref[...])
        xs.append(x)
        hs.append((_rms(x, gpre_ref[...]) * (1.0 + scale_ref[...]) + shift_ref[...]).astype(BF16))
    ys = []
    for h in hs:
        y = jnp.zeros((h.shape[0], D_MODEL), F32)
        for lo, hi in _FFN_TILES:
            a = _dot(h, win_ref[:, lo:hi])
            u = _dot(h, win_ref[:, HIDDEN + lo:HIDDEN + hi])
            y = y + _dot((_silu(a) * u).astype(BF16), wout_ref[lo:hi, :])
        ys.append(y)
    for r, x, y in zip(subs, xs, ys):
        o_ref[0, r, :] = x + gate_ref[...] * _rms(y, gpost_ref[...])


def _out_ffn(yg, yf, yc, x, pos, lay, i, mod_row, tm):
    b, l, d = x.shape
    add_pos = pos is not None
    tok = lambda bi, ti: (bi, ti, 0)
    in_specs = [pl.BlockSpec((1, tm, V_W), tok), pl.BlockSpec((1, tm, FOUR_W), tok),
                pl.BlockSpec((1, tm, CONV_W), tok), _layer(lay["w_out"], i), pl.BlockSpec((1, tm, d), tok)]
    args = [yg, yf, yc, lay["w_out"], x]
    if add_pos:
        in_specs.append(pl.BlockSpec((tm, d), lambda bi, ti: (ti, 0)))
        args.append(pos)
    in_specs += [_layer(lay["g_mix_post"], i), _mod_spec(i, 2, mod_row), _layer(lay["g_ffn_pre"], i),
                 _mod_spec(i, 3, mod_row), _mod_spec(i, 4, mod_row), _mod_spec(i, 5, mod_row),
                 _layer(lay["g_ffn_post"], i), _layer(lay["w_ffn_in"], i), _layer(lay["w_ffn_out"], i)]
    args += [lay["g_mix_post"], lay["mod"], lay["g_ffn_pre"], lay["mod"], lay["mod"], lay["mod"],
             lay["g_ffn_post"], lay["w_ffn_in"], lay["w_ffn_out"]]
    return pl.pallas_call(
        functools.partial(_out_ffn_kernel, add_pos=add_pos, n_sub=2 if tm >= 512 else 1),
        grid=(b, l // tm),
        in_specs=in_specs,
        out_specs=pl.BlockSpec((1, tm, d), tok),
        out_shape=jax.ShapeDtypeStruct((b, l, d), F32),
        compiler_params=_params(("parallel", "parallel"), 56),
        name="out_ffn",
    )(*args)


def _arrange_w_in(w):
    main = jnp.concatenate([w[..., _C_K:_C_LR], w[..., _C_Q:_C_END]], axis=-1)
    lr = jnp.pad(w[..., _C_LR:_C_Q], ((0, 0), (0, 0), (0, LR_PAD - 2 * RANK)))
    return jnp.concatenate([main, lr], axis=-1).astype(BF16)


def _arrange_w_dec(w_f, w_b):
    wd = jnp.zeros((DEPTH, LR_PAD, 2 * K_W), F32)
    wd = wd.at[:, 0:RANK, :K_W].set(w_f).at[:, RANK:2 * RANK, K_W:].set(w_b)
    return wd.astype(BF16)


def kernel(x, c, ctx, c_ctx, w_mod, b_mod, g_mix_pre, g_mix_post, g_ffn_pre, g_ffn_post,
           w_in, w_dec_f, b_dec_f, w_dec_b, b_dec_b, gla_norm_w, conv_w, conv_b, w_out,
           w_ffn_in, w_ffn_out):
    b, n_lat, d = x.shape
    n_ctx = ctx.shape[1]
    assert d == D_MODEL and w_in.shape == (DEPTH, D_MODEL, _C_END)
    assert n_lat % 512 == 0 and n_ctx % CHUNK == 0 and b + 1 <= MOD_ROWS

    pos = jnp.asarray(_pos_embed(n_lat, d))
    tabs = {n_lat: _dft_tables(n_lat, FFT_RADIX), n_ctx: _dft_tables(n_ctx, 1)}

    cvec = jnp.zeros((MOD_ROWS, d), F32).at[:b].set(c).at[b].set(c_ctx)
    ctx_row = b
    lay = {
        "mod": _modulation(cvec, w_mod, b_mod).reshape(DEPTH, MOD_ROWS, 1, N_MOD * d),
        "w_in": _arrange_w_in(w_in),
        "w_dec": _arrange_w_dec(w_dec_f, w_dec_b),
        "b_dec": jnp.concatenate([b_dec_f, b_dec_b], axis=-1).reshape(DEPTH, 1, 2 * K_W),
        "gla_norm_w": gla_norm_w.reshape(DEPTH, 1, V_W),
        "conv_w": conv_w,
        "conv_b": conv_b.reshape(DEPTH, 1, CONV_W),
        "w_out": w_out.astype(BF16),
        "w_ffn_in": w_ffn_in.astype(BF16),
        "w_ffn_out": w_ffn_out.astype(BF16),
        "g_mix_pre": g_mix_pre.reshape(DEPTH, 1, d),
        "g_mix_post": g_mix_post.reshape(DEPTH, 1, d),
        "g_ffn_pre": g_ffn_pre.reshape(DEPTH, 1, d),
        "g_ffn_post": g_ffn_post.reshape(DEPTH, 1, d),
    }

    def mixer_and_ffn(i, xs, pos_s, mod_row, s0, want_state, tm):
        p = _inproj(xs, pos_s, lay, i, mod_row, _OUT_GROUPS_FULL, tm)
        yg, s = _gla(p, lay, i, s0, True, want_state)
        yf, yc = _four_conv(p, lay, i, tabs[xs.shape[1]])
        return _out_ffn(yg, yf, yc, xs, pos_s, lay, i, mod_row, tm), s

    zero_state = jnp.zeros((b, 2, DV, K_W), F32)
    xc = ctx
    for i in range(DEPTH):
        if i == DEPTH - 1:
            p = _inproj(xc, None, lay, i, ctx_row, _OUT_GROUPS_STATE, n_ctx)
            _, s = _gla(p, lay, i, zero_state, False, True)
        else:
            xc, s = mixer_and_ffn(i, xc, None, ctx_row, zero_state, True, n_ctx)
        x, _ = mixer_and_ffn(i, x, pos if i == 0 else None, None, s, False, 512)
    return x
```

```python
import functools

import numpy as np
import jax
import jax.numpy as jnp
from jax import lax
from jax.experimental import pallas as pl
from jax.experimental.pallas import tpu as pltpu

F32 = jnp.float32
BF16 = jnp.bfloat16

D_MODEL = 1024
DEPTH = 2
GRID_W = 64
N_MOD = 6
HEADS = 4
DK = 64
DV = 128
K_W = HEADS * DK
V_W = HEADS * DV
RANK = 16
TAU = 16.0
CHUNK = 64
FOUR_W = 256
FOUR_G = 4
CONV_W = 256
HIDDEN = 2816
EPS = 1e-6
LR_PAD = 128
MOD_ROWS = 16

_C_K, _C_V, _C_LR, _C_Q, _C_R, _C_FOUR, _C_BG, _C_CG, _C_UC, _C_END = (
    0, 256, 768, 800, 1056, 1568, 1824, 2080, 2336, 2592)
_MAIN_W = 2560
_IN_W = _MAIN_W + LR_PAD
_OUT_GROUPS_FULL = (("k", 0, 256), ("v", 256, 768), ("q", 768, 1024), ("r", 1024, 1536),
                    ("uf", 1536, 1792), ("bg", 1792, 2048), ("cg", 2048, 2304),
                    ("uc", 2304, 2560), ("lr", 2560, 2688))
_OUT_GROUPS_STATE = (("k", 0, 256), ("v", 256, 768), ("lr", 2560, 2688))

V7X_VMEM_BYTES = 64 * 1024 * 1024


def _params(semantics, vmem_mb):
    assert vmem_mb * 1024 * 1024 < V7X_VMEM_BYTES
    return pltpu.CompilerParams(dimension_semantics=semantics,
                                vmem_limit_bytes=vmem_mb * 1024 * 1024)


def _resident(shape):
    zeros = (0,) * len(shape)
    return pl.BlockSpec(shape, lambda *_: zeros, pipeline_mode=pl.Buffered(1))


def _layer(arr, i):
    shape = tuple(arr.shape[1:])
    zeros = (0,) * len(shape)
    return pl.BlockSpec((None,) + shape, lambda *_: (i,) + zeros, pipeline_mode=pl.Buffered(1))


def _mod_spec(i, j, row):
    if row is None:
        return pl.BlockSpec((None, None, 1, D_MODEL), lambda bi, *_: (i, bi, 0, j))
    return pl.BlockSpec((None, None, 1, D_MODEL), lambda *_: (i, row, 0, j))


def _silu(a):
    return a / (1.0 + jnp.exp(-a))


def _rms(x, g):
    return x * lax.rsqrt(jnp.mean(x * x, axis=-1, keepdims=True) + EPS) * g


def _dot(a, b):
    return jnp.dot(a, b, preferred_element_type=F32)


def _dot_nt(a, b):
    return lax.dot_general(a, b, (((1,), (1,)), ((), ())), preferred_element_type=F32)


def _dot_tn(a, b):
    return lax.dot_general(a, b, (((0,), (0,)), ((), ())), preferred_element_type=F32)


def _pos_embed(n_tokens, dim):
    rows = n_tokens // GRID_W
    row = np.repeat(np.arange(rows, dtype=np.float32), GRID_W)
    col = np.tile(np.arange(GRID_W, dtype=np.float32), rows)
    quarter = dim // 4
    freqs = (1.0 / (10000.0 ** (np.arange(quarter, dtype=np.float32) / quarter))).astype(np.float32)

    def enc(p):
        ang = (p[:, None] * freqs[None, :]).astype(np.float32)
        return np.concatenate([np.sin(ang), np.cos(ang)], axis=-1)

    return np.concatenate([enc(row), enc(col)], axis=-1).astype(np.float32)


def _dft_cos_sin(n):
    idx = np.arange(n, dtype=np.int64)
    ang = 2.0 * np.pi * ((idx[:, None] * idx[None, :]) % n).astype(np.float64) / n
    return np.cos(ang) / np.sqrt(n), np.sin(ang) / np.sqrt(n)


def _dft_tables(seq, radix):
    gw = FOUR_W // FOUR_G
    cc, sc = _dft_cos_sin(gw)
    eye = np.eye(FOUR_G)
    chan_tab = jnp.asarray(np.concatenate([np.kron(eye, cc), np.kron(eye, sc)], axis=1), dtype=F32).astype(BF16)
    if radix == 1:
        cl, sl = _dft_cos_sin(seq)
        return chan_tab, jnp.asarray(np.concatenate([cl, -sl], axis=1), dtype=F32).astype(BF16)
    n = seq // radix
    cn, sn = _dft_cos_sin(n)
    sub_tab = np.concatenate([cn, sn], axis=1) * np.sqrt(n / seq)
    ang = 2.0 * np.pi * (np.arange(1, radix)[:, None] * np.arange(n)[None, :]).astype(np.float64) / seq
    twc = np.broadcast_to(np.cos(ang)[:, :, None], (radix - 1, n, FOUR_W))
    tws = np.broadcast_to(np.sin(ang)[:, :, None], (radix - 1, n, FOUR_W))
    return (chan_tab, jnp.asarray(sub_tab, dtype=F32).astype(BF16),
            jnp.asarray(twc, dtype=F32), jnp.asarray(tws, dtype=F32))


def _mod_kernel(c_ref, w_ref, b_ref, o_ref):
    s = _silu(c_ref[...]).astype(BF16)
    o_ref[0] = _dot(s, w_ref[0].astype(BF16)) + b_ref[0]


def _modulation(cvec, w_mod, b_mod):
    tn = 1536
    n = N_MOD * D_MODEL
    return pl.pallas_call(
        _mod_kernel,
        grid=(DEPTH, n // tn),
        in_specs=[pl.BlockSpec((MOD_ROWS, D_MODEL), lambda i, j: (0, 0)),
                  pl.BlockSpec((1, D_MODEL, tn), lambda i, j: (i, 0, j)),
                  pl.BlockSpec((1, 1, tn), lambda i, j: (i, 0, j))],
        out_specs=pl.BlockSpec((1, MOD_ROWS, tn), lambda i, j: (i, 0, j)),
        out_shape=jax.ShapeDtypeStruct((DEPTH, MOD_ROWS, n), F32),
        compiler_params=_params(("arbitrary", "arbitrary"), 40),
        name="modulation",
    )(cvec, w_mod, b_mod.reshape(DEPTH, 1, n))


def _inproj_kernel(*refs, add_pos, groups):
    it = iter(refs)
    x_ref = next(it)
    pos_ref = next(it) if add_pos else None
    g_ref, shift_ref, scale_ref, w_ref = next(it), next(it), next(it), next(it)
    out_refs = list(it)
    tm = x_ref.shape[1]
    n_sub = 2 if tm >= 512 else 1
    subs = [slice(i * tm // n_sub, (i + 1) * tm // n_sub) for i in range(n_sub)]
    hs = []
    for r in subs:
        x = x_ref[0, r, :]
        if add_pos:
            x = x + pos_ref[r, :]
        hs.append((_rms(x, g_ref[...]) * (1.0 + scale_ref[...]) + shift_ref[...]).astype(BF16))
    for r, h in zip(subs, hs):
        for o_ref, (_, lo, hi) in zip(out_refs, groups):
            o_ref[0, r, :] = _dot(h, w_ref[:, lo:hi]).astype(BF16)


def _inproj(x, pos, lay, i, mod_row, groups, tm):
    b, l, d = x.shape
    add_pos = pos is not None
    tok = lambda bi, ti: (bi, ti, 0)
    in_specs = [pl.BlockSpec((1, tm, d), tok)]
    args = [x]
    if add_pos:
        in_specs.append(pl.BlockSpec((tm, d), lambda bi, ti: (ti, 0)))
        args.append(pos)
    in_specs += [_layer(lay["g_mix_pre"], i), _mod_spec(i, 0, mod_row), _mod_spec(i, 1, mod_row),
                 _layer(lay["w_in"], i)]
    args += [lay["g_mix_pre"], lay["mod"], lay["mod"], lay["w_in"]]
    outs = pl.pallas_call(
        functools.partial(_inproj_kernel, add_pos=add_pos, groups=groups),
        grid=(b, l // tm),
        in_specs=in_specs,
        out_specs=[pl.BlockSpec((1, tm, hi - lo), tok) for _, lo, hi in groups],
        out_shape=[jax.ShapeDtypeStruct((b, l, hi - lo), BF16) for _, lo, hi in groups],
        compiler_params=_params(("parallel", "parallel"), 48),
        name="inproj",
    )(*args)
    return {name: o for (name, _, _), o in zip(groups, outs)}


def _gla_kernel(*refs, seq, with_output, with_state):
    it = iter(refs)
    k_ref, v_ref, lr_ref, wd_ref, bd_ref, s0_ref = (next(it) for _ in range(6))
    if with_output:
        q_ref, r_ref, nw_ref = next(it), next(it), next(it)
        y_ref = next(it)
    if with_state:
        s_out_ref = next(it)
    st_ref = next(it)
    lg_hi, lg_lo = next(it), next(it)
    ds_scr = next(it)
    dec_scr = next(it)
    if with_output:
        o_scr = next(it)
        qst_scr = next(it)
    n_chunks = seq // CHUNK

    st_ref[...] = s0_ref[0]

    row = lax.broadcasted_iota(jnp.int32, (CHUNK, CHUNK), 0)
    col = lax.broadcasted_iota(jnp.int32, (CHUNK, CHUNK), 1)
    cum = (jnp.where(row >= col, 1.0, 0.0).astype(BF16), jnp.where(row <= col, 1.0, 0.0).astype(BF16))
    lane_head = lax.broadcasted_iota(jnp.int32, (1, K_W), 1) // DK
    head_mask = [lane_head == h for h in range(HEADS)]
    srow = lax.broadcasted_iota(jnp.int32, (CHUNK, HEADS * CHUNK), 0)
    scol = lax.broadcasted_iota(jnp.int32, (CHUNK, HEADS * CHUNK), 1) % CHUNK
    score_mask = (srow >= scol, srow <= scol)
    zero_v = jnp.zeros((CHUNK, DV), BF16)
    last_row = (CHUNK - 1, 0)
    mid_row = (CHUNK // 2 - 1, CHUNK // 2)

    def stack_heads(a):
        zero = jnp.zeros_like(a)
        return jnp.concatenate([jnp.where(head_mask[h], a, zero) for h in range(HEADS)], axis=0)

    def chunk_rows(n):
        return pl.ds(pl.multiple_of(n * CHUNK, CHUNK), CHUNK)

    per_iter = 4
    n_groups = n_chunks // per_iter
    gb = per_iter * CHUNK

    def gate_pieces(grp):
        rows = pl.ds(pl.multiple_of(grp * gb, gb), gb)
        z = _dot(lr_ref[0, rows, :], wd_ref[...]) + bd_ref[...]
        logg = (jnp.minimum(z, 0.0) - jnp.log(1.0 + jnp.exp(-jnp.abs(z)))) * (1.0 / TAU)
        hi = logg.astype(BF16)
        return [(lg_hi, (rows, slice(None)), hi),
                (lg_lo, (rows, slice(None)), (logg - hi.astype(F32)).astype(BF16))]

    for ref, idx, val in gate_pieces(0):
        ref[idx] = val

    def chunk_local(i, carry):
        next_gates = gate_pieces(jnp.minimum(i + 1, n_groups - 1))
        loaded = []
        for u in range(per_iter):
            n = i * per_iter + u
            rows = chunk_rows(n)
            k = k_ref[0, rows, :].astype(F32)
            v = v_ref[0, rows, :]
            v_heads = [v[:, h * DV:(h + 1) * DV] for h in range(HEADS)]
            v_rows = jnp.concatenate(v_heads, axis=0)
            v_diag = jnp.concatenate(
                [jnp.concatenate([v_heads[h] if hh == h else zero_v for hh in range(HEADS)], axis=1)
                 for h in range(HEADS)], axis=0) if with_output else None
            v = (v, v_rows, v_diag)
            qs = q_ref[0, rows, :].astype(F32) * (DK ** -0.5) if with_output else None
            lg = [(lg_hi[rows, d * K_W:(d + 1) * K_W], lg_lo[rows, d * K_W:(d + 1) * K_W]) for d in range(2)]
            loaded.append((n, rows, k, v, qs, lg))
        chains = [(n, rows, k, v, qs, lg[dirn], dirn) for n, rows, k, v, qs, lg in loaded for dirn in range(2)]
        stores = []
        gs = [_dot(cum[dirn], hi) + _dot(cum[dirn], lo) for *_, (hi, lo), dirn in chains]
        score_list = []
        for (n, rows, k, (v, v_rows, v_diag), qs, _, dirn), g in zip(chains, gs):
            g_last = g[last_row[dirn]:last_row[dirn] + 1]
            stores.append((dec_scr, (dirn, pl.ds(pl.multiple_of(n * 8, 8), 8), slice(None)),
                           jnp.broadcast_to(jnp.exp(g_last), (8, K_W))))
            if with_output:
                g_mid = g[mid_row[dirn]:mid_row[dirn] + 1]
                q_in = (qs * jnp.exp(g - g_mid)).astype(BF16)
                k_in = stack_heads((k * jnp.exp(g_mid - g)).astype(BF16))
                score_list.append(_dot_nt(q_in, k_in))
                stores.append((qst_scr, (dirn, rows, slice(None)), (qs * jnp.exp(g)).astype(BF16)))
            k_upd = stack_heads((k * jnp.exp(g_last - g)).astype(BF16))
            stores.append((ds_scr, (dirn, pl.ds(pl.multiple_of(n * DV, DV), DV), slice(None)),
                           _dot_tn(v_rows, k_upd)))
        if with_output:
            for (n, rows, k, (v, v_rows, v_diag), qs, _, dirn), sc in zip(chains, score_list):
                scores = jnp.where(score_mask[dirn], sc, 0.0).astype(BF16)
                stores.append((o_scr, (dirn, rows, slice(None)), _dot(scores, v_diag)))
        for ref, idx, val in stores + next_gates:
            ref[idx] = val
        return carry

    lax.fori_loop(0, n_groups, chunk_local, 0)

    def scan_step(i, finish):
        steps = ((0, i), (1, n_chunks - 1 - i))
        new_st, inters = [], []
        for dirn, n in steps:
            st = st_ref[dirn]
            if with_output:
                q_st = stack_heads(qst_scr[dirn, chunk_rows(n), :])
                inters.append(_dot_nt(q_st, st.astype(BF16)))
            dec = dec_scr[dirn, pl.ds(pl.multiple_of(n * 8, 8), 8), :]
            new_st.append(dec[0:1] * st + ds_scr[dirn, pl.ds(pl.multiple_of(n * DV, DV), DV), :])
        for (dirn, n), st in zip(steps, new_st):
            st_ref[dirn] = st
        if with_output:
            for (dirn, n), inter in zip(steps, inters):
                rows = chunk_rows(n)
                gate = _silu(r_ref[0, rows, :].astype(F32)) if finish else None
                for h in range(HEADS):
                    vs = slice(h * DV, (h + 1) * DV)
                    o = o_scr[dirn, rows, vs] + inter[h * CHUNK:(h + 1) * CHUNK]
                    if finish:
                        o = o + o_scr[1 - dirn, rows, vs]
                        y_ref[0, rows, vs] = (_rms(o, nw_ref[:, vs]) * gate[:, vs]).astype(BF16)
                    else:
                        o_scr[dirn, rows, vs] = o

    half = n_chunks // 2
    unroll = min(8, half)

    def scan_first(i, carry):
        scan_step(i, False)
        return carry

    def scan_second(i, carry):
        scan_step(i, True)
        return carry

    lax.fori_loop(0, half, scan_first, 0, unroll=unroll)
    lax.fori_loop(half, n_chunks, scan_second, 0, unroll=unroll)

    if with_state:
        s_out_ref[0] = st_ref[...]


def _gla(p, lay, i, s0, with_output, with_state):
    b, l, _ = p["k"].shape
    seq3 = lambda bi: (bi, 0, 0)
    in_specs = [pl.BlockSpec((1, l, K_W), seq3), pl.BlockSpec((1, l, V_W), seq3),
                pl.BlockSpec((1, l, LR_PAD), seq3), _layer(lay["w_dec"], i), _layer(lay["b_dec"], i),
                pl.BlockSpec((1, 2, DV, K_W), lambda bi: (bi, 0, 0, 0))]
    args = [p["k"], p["v"], p["lr"], lay["w_dec"], lay["b_dec"], s0]
    out_specs, out_shape = [], []
    scratch = [pltpu.VMEM((2, DV, K_W), F32), pltpu.VMEM((l, 2 * K_W), BF16), pltpu.VMEM((l, 2 * K_W), BF16),
               pltpu.VMEM((2, l // CHUNK * DV, K_W), F32), pltpu.VMEM((2, l // CHUNK * 8, K_W), F32)]
    if with_output:
        in_specs += [pl.BlockSpec((1, l, K_W), seq3), pl.BlockSpec((1, l, V_W), seq3),
                     _layer(lay["gla_norm_w"], i)]
        args += [p["q"], p["r"], lay["gla_norm_w"]]
        out_specs.append(pl.BlockSpec((1, l, V_W), seq3))
        out_shape.append(jax.ShapeDtypeStruct((b, l, V_W), BF16))
        scratch += [pltpu.VMEM((2, l, V_W), F32), pltpu.VMEM((2, l, K_W), BF16)]
    if with_state:
        out_specs.append(pl.BlockSpec((1, 2, DV, K_W), lambda bi: (bi, 0, 0, 0)))
        out_shape.append(jax.ShapeDtypeStruct((b, 2, DV, K_W), F32))
    outs = pl.pallas_call(
        functools.partial(_gla_kernel, seq=l, with_output=with_output, with_state=with_state),
        grid=(b,),
        in_specs=in_specs,
        out_specs=out_specs,
        out_shape=out_shape,
        scratch_shapes=scratch,
        compiler_params=_params(("parallel",), 48),
        name="gla",
    )(*args)
    outs = list(outs)
    y = outs.pop(0) if with_output else None
    s = outs.pop(0) if with_state else None
    return y, s


def _four_conv_kernel(uf_ref, bg_ref, cg_ref, uc_ref, cw_ref, cb_ref, chan_ref, pos_ref,
                      yf_ref, yc_ref, ab_scr, z_scr, *, seq):
    ab = _dot(uf_ref[0], chan_ref[...])
    ab_scr[0:seq, :] = ab[:, :FOUR_W].astype(BF16)
    ab_scr[seq:2 * seq, :] = ab[:, FOUR_W:].astype(BF16)
    rb = min(seq, 512)
    for i in range(seq // rb):
        yf_ref[0, i * rb:(i + 1) * rb, :] = _dot(pos_ref[i * rb:(i + 1) * rb, :], ab_scr[...]).astype(BF16)
    _short_conv(bg_ref, cg_ref, uc_ref, cw_ref, cb_ref, yc_ref, z_scr, seq)


CONV_PAD = 8


def _short_conv(bg_ref, cg_ref, uc_ref, cw_ref, cb_ref, yc_ref, z_scr, seq):
    z_scr[0:CONV_PAD, :] = jnp.zeros((CONV_PAD, CONV_W), F32)
    z_scr[CONV_PAD + seq:, :] = jnp.zeros((CONV_PAD, CONV_W), F32)
    rb = min(seq, 256)
    for i in range(seq // rb):
        r = slice(i * rb, (i + 1) * rb)
        z_scr[CONV_PAD + i * rb:CONV_PAD + (i + 1) * rb, :] = cg_ref[0, r, :].astype(F32) * uc_ref[0, r, :].astype(F32)
    for i in range(seq // rb):
        r = slice(i * rb, (i + 1) * rb)
        s = CONV_PAD + i * rb
        y = (cw_ref[0:1] * z_scr[s - 1:s - 1 + rb, :] + cw_ref[1:2] * z_scr[s:s + rb, :]
             + cw_ref[2:3] * z_scr[s + 1:s + 1 + rb, :] + cb_ref[...])
        yc_ref[0, r, :] = (bg_ref[0, r, :].astype(F32) * y).astype(BF16)


FFT_RADIX = 8


def _four_conv_fft_kernel(uf_ref, bg_ref, cg_ref, uc_ref, cw_ref, cb_ref, chan_ref, sub_ref, twc_ref, tws_ref,
                          yf_ref, yc_ref, uf_scr, z_scr, *, seq):
    n = seq // FFT_RADIX
    lanes = uf_scr.shape[2]
    n_lane_tiles = FOUR_W // lanes
    for j in range(n_lane_tiles):
        uf_scr[j] = uf_ref[0, :, j * lanes:(j + 1) * lanes].astype(F32)

    def tokens(r):
        parts = [uf_scr[j, pl.ds(r, n, stride=FFT_RADIX), :] for j in range(n_lane_tiles)]
        return jnp.concatenate(parts, axis=1).astype(BF16)

    ab = [_dot(tokens(r), chan_ref[...]) for r in range(FFT_RADIX)]
    pq = []
    for ab_r in ab:
        a, b = ab_r[:, :FOUR_W].astype(BF16), ab_r[:, FOUR_W:].astype(BF16)
        w = jnp.concatenate([jnp.concatenate([a, b], axis=1), jnp.concatenate([-b, a], axis=1)], axis=0)
        pq.append(_dot(sub_ref[...], w))
    re, im = [pq[0][:, :FOUR_W]], [pq[0][:, FOUR_W:]]
    for r in range(1, FFT_RADIX):
        p, q = pq[r][:, :FOUR_W], pq[r][:, FOUR_W:]
        c, s = twc_ref[r - 1], tws_ref[r - 1]
        re.append(c * p - s * q)
        im.append(s * p + c * q)

    def quarter(z0, z1, z2, z3):
        s02 = (z0[0] + z2[0], z0[1] + z2[1])
        d02 = (z0[0] - z2[0], z0[1] - z2[1])
        s13 = (z1[0] + z3[0], z1[1] + z3[1])
        d13 = (z1[0] - z3[0], z1[1] - z3[1])
        return ((s02[0] + s13[0], s02[1] + s13[1]), (d02[0] - d13[1], d02[1] + d13[0]),
                (s02[0] - s13[0], s02[1] - s13[1]), (d02[0] + d13[1], d02[1] - d13[0]))

    zs = list(zip(re, im))
    even = quarter(zs[0], zs[2], zs[4], zs[6])
    odd = quarter(zs[1], zs[3], zs[5], zs[7])
    half = 0.5 ** 0.5
    odd_re = (odd[0][0], (odd[1][0] - odd[1][1]) * half, -odd[2][1], -(odd[3][0] + odd[3][1]) * half)
    for q in range(4):
        yf_ref[0, q * n:(q + 1) * n, :] = (even[q][0] + odd_re[q]).astype(BF16)
        yf_ref[0, (q + 4) * n:(q + 5) * n, :] = (even[q][0] - odd_re[q]).astype(BF16)
    _short_conv(bg_ref, cg_ref, uc_ref, cw_ref, cb_ref, yc_ref, z_scr, seq)


def _four_conv(p, lay, i, tabs):
    b, l, _ = p["uf"].shape
    seq3 = lambda bi: (bi, 0, 0)
    tok = pl.BlockSpec((1, l, FOUR_W), seq3)
    factored = len(tabs) == 4
    if factored:
        body, name = _four_conv_fft_kernel, "four_conv_fft"
        dft_scratch = pltpu.VMEM((FOUR_W // 128, l, 128), F32)
    else:
        body, name = _four_conv_kernel, "four_conv"
        dft_scratch = pltpu.VMEM((2 * l, FOUR_W), BF16)
    return pl.pallas_call(
        functools.partial(body, seq=l),
        grid=(b,),
        in_specs=[tok, tok, tok, tok, _layer(lay["conv_w"], i), _layer(lay["conv_b"], i)]
                 + [_resident(t.shape) for t in tabs],
        out_specs=[tok, tok],
        out_shape=[jax.ShapeDtypeStruct((b, l, FOUR_W), BF16)] * 2,
        scratch_shapes=[dft_scratch, pltpu.VMEM((l + 2 * CONV_PAD, CONV_W), F32)],
        compiler_params=_params(("parallel",), 48),
        name=name,
    )(p["uf"], p["bg"], p["cg"], p["uc"], lay["conv_w"], lay["conv_b"], *tabs)


_FFN_TILES = ((0, 1536), (1536, HIDDEN))


def _out_ffn_kernel(*refs, add_pos, n_sub):
    it = iter(refs)
    yg_ref, yf_ref, yc_ref, wo_ref, x_ref = (next(it) for _ in range(5))
    pos_ref = next(it) if add_pos else None
    gmix_ref, gatemix_ref, gpre_ref, shift_ref, scale_ref, gate_ref, gpost_ref = (next(it) for _ in range(7))
    win_ref, wout_ref, o_ref = next(it), next(it), next(it)
    tm = x_ref.shape[1]
    subs = [slice(i * tm // n_sub, (i + 1) * tm // n_sub) for i in range(n_sub)]
    ys = [(_dot(yg_ref[0, r, :], wo_ref[0:V_W]) + _dot(yf_ref[0, r, :], wo_ref[V_W:V_W + FOUR_W])
           + _dot(yc_ref[0, r, :], wo_ref[V_W + FOUR_W:])) for r in subs]
    xs, hs = [], []
    for r, y in zip(subs, ys):
        x = x_ref[0, r, :]
        if add_pos:
            x = x + pos_ref[r, :]
        x = x + gatemix_ref[...] * _rms(y, gmix_ref[...])
        xs.append(x)
        hs.append((_rms(x, gpre_ref[...]) * (1.0 + scale_ref[...]) + shift_ref[...]).astype(BF16))
    ys = []
    for h in hs:
        y = jnp.zeros((h.shape[0], D_MODEL), F32)
        for lo, hi in _FFN_TILES:
            a = _dot(h, win_ref[:, lo:hi])
            u = _dot(h, win_ref[:, HIDDEN + lo:HIDDEN + hi])
            y = y + _dot((_silu(a) * u).astype(BF16), wout_ref[lo:hi, :])
        ys.append(y)
    for r, x, y in zip(subs, xs, ys):
        o_ref[0, r, :] = x + gate_ref[...] * _rms(y, gpost_ref[...])


def _out_ffn(yg, yf, yc, x, pos, lay, i, mod_row, tm):
    b, l, d = x.shape
    add_pos = pos is not None
    tok = lambda bi, ti: (bi, ti, 0)
    in_specs = [pl.BlockSpec((1, tm, V_W), tok), pl.BlockSpec((1, tm, FOUR_W), tok),
                pl.BlockSpec((1, tm, CONV_W), tok), _layer(lay["w_out"], i), pl.BlockSpec((1, tm, d), tok)]
    args = [yg, yf, yc, lay["w_out"], x]
    if add_pos:
        in_specs.append(pl.BlockSpec((tm, d), lambda bi, ti: (ti, 0)))
        args.append(pos)
    in_specs += [_layer(lay["g_mix_post"], i), _mod_spec(i, 2, mod_row), _layer(lay["g_ffn_pre"], i),
                 _mod_spec(i, 3, mod_row), _mod_spec(i, 4, mod_row), _mod_spec(i, 5, mod_row),
                 _layer(lay["g_ffn_post"], i), _layer(lay["w_ffn_in"], i), _layer(lay["w_ffn_out"], i)]
    args += [lay["g_mix_post"], lay["mod"], lay["g_ffn_pre"], lay["mod"], lay["mod"], lay["mod"],
             lay["g_ffn_post"], lay["w_ffn_in"], lay["w_ffn_out"]]
    return pl.pallas_call(
        functools.partial(_out_ffn_kernel, add_pos=add_pos, n_sub=2 if tm >= 512 else 1),
        grid=(b, l // tm),
        in_specs=in_specs,
        out_specs=pl.BlockSpec((1, tm, d), tok),
        out_shape=jax.ShapeDtypeStruct((b, l, d), F32),
        compiler_params=_params(("parallel", "parallel"), 56),
        name="out_ffn",
    )(*args)


def _arrange_w_in(w):
    main = jnp.concatenate([w[..., _C_K:_C_LR], w[..., _C_Q:_C_END]], axis=-1)
    lr = jnp.pad(w[..., _C_LR:_C_Q], ((0, 0), (0, 0), (0, LR_PAD - 2 * RANK)))
    return jnp.concatenate([main, lr], axis=-1).astype(BF16)


def _arrange_w_dec(w_f, w_b):
    wd = jnp.zeros((DEPTH, LR_PAD, 2 * K_W), F32)
    wd = wd.at[:, 0:RANK, :K_W].set(w_f).at[:, RANK:2 * RANK, K_W:].set(w_b)
    return wd.astype(BF16)


def kernel(x, c, ctx, c_ctx, w_mod, b_mod, g_mix_pre, g_mix_post, g_ffn_pre, g_ffn_post,
           w_in, w_dec_f, b_dec_f, w_dec_b, b_dec_b, gla_norm_w, conv_w, conv_b, w_out,
           w_ffn_in, w_ffn_out):
    b, n_lat, d = x.shape
    n_ctx = ctx.shape[1]
    assert d == D_MODEL and w_in.shape == (DEPTH, D_MODEL, _C_END)
    assert n_lat % 512 == 0 and n_ctx % CHUNK == 0 and b + 1 <= MOD_ROWS

    pos = jnp.asarray(_pos_embed(n_lat, d))
    tabs = {n_lat: _dft_tables(n_lat, FFT_RADIX), n_ctx: _dft_tables(n_ctx, 1)}

    cvec = jnp.zeros((MOD_ROWS, d), F32).at[:b].set(c).at[b].set(c_ctx)
    ctx_row = b
    lay = {
        "mod": _modulation(cvec, w_mod, b_mod).reshape(DEPTH, MOD_ROWS, 1, N_MOD * d),
        "w_in": _arrange_w_in(w_in),
        "w_dec": _arrange_w_dec(w_dec_f, w_dec_b),
        "b_dec": jnp.concatenate([b_dec_f, b_dec_b], axis=-1).reshape(DEPTH, 1, 2 * K_W),
        "gla_norm_w": gla_norm_w.reshape(DEPTH, 1, V_W),
        "conv_w": conv_w,
        "conv_b": conv_b.reshape(DEPTH, 1, CONV_W),
        "w_out": w_out.astype(BF16),
        "w_ffn_in": w_ffn_in.astype(BF16),
        "w_ffn_out": w_ffn_out.astype(BF16),
        "g_mix_pre": g_mix_pre.reshape(DEPTH, 1, d),
        "g_mix_post": g_mix_post.reshape(DEPTH, 1, d),
        "g_ffn_pre": g_ffn_pre.reshape(DEPTH, 1, d),
        "g_ffn_post": g_ffn_post.reshape(DEPTH, 1, d),
    }

    def mixer_and_ffn(i, xs, pos_s, mod_row, s0, want_state, tm):
        p = _inproj(xs, pos_s, lay, i, mod_row, _OUT_GROUPS_FULL, tm)
        yg, s = _gla(p, lay, i, s0, True, want_state)
        yf, yc = _four_conv(p, lay, i, tabs[xs.shape[1]])
        return _out_ffn(yg, yf, yc, xs, pos_s, lay, i, mod_row, tm), s

    zero_state = jnp.zeros((b, 2, DV, K_W), F32)
    xc = ctx
    for i in range(DEPTH):
        if i == DEPTH - 1:
            p = _inproj(xc, None, lay, i, ctx_row, _OUT_GROUPS_STATE, n_ctx)
            _, s = _gla(p, lay, i, zero_state, False, True)
        else:
            xc, s = mixer_and_ffn(i, xc, None, ctx_row, zero_state, True, n_ctx)
        x, _ = mixer_and_ffn(i, x, pos if i == 0 else None, None, s, False, 512)
    return x
```

```python
import functools

import numpy as np
import jax
import jax.numpy as jnp
from jax import lax
from jax.experimental import pallas as pl
from jax.experimental.pallas import tpu as pltpu

F32 = jnp.float32
BF16 = jnp.bfloat16

D_MODEL = 1024
DEPTH = 2
GRID_W = 64
N_MOD = 6
HEADS = 4
DK = 64
DV = 128
K_W = HEADS * DK
V_W = HEADS * DV
RANK = 16
TAU = 16.0
CHUNK = 64
FOUR_W = 256
FOUR_G = 4
CONV_W = 256
HIDDEN = 2816
EPS = 1e-6
LR_PAD = 128
MOD_ROWS = 16

_C_K, _C_V, _C_LR, _C_Q, _C_R, _C_FOUR, _C_BG, _C_CG, _C_UC, _C_END = (
    0, 256, 768, 800, 1056, 1568, 1824, 2080, 2336, 2592)
_MAIN_W = 2560
_IN_W = _MAIN_W + LR_PAD
_OUT_GROUPS_FULL = (("k", 0, 256), ("v", 256, 768), ("q", 768, 1024), ("r", 1024, 1536),
                    ("uf", 1536, 1792), ("bg", 1792, 2048), ("cg", 2048, 2304),
                    ("uc", 2304, 2560), ("lr", 2560, 2688))
_OUT_GROUPS_STATE = (("k", 0, 256), ("v", 256, 768), ("lr", 2560, 2688))

V7X_VMEM_BYTES = 64 * 1024 * 1024


def _params(semantics, vmem_mb):
    assert vmem_mb * 1024 * 1024 < V7X_VMEM_BYTES
    return pltpu.CompilerParams(dimension_semantics=semantics,
                                vmem_limit_bytes=vmem_mb * 1024 * 1024)


def _resident(shape):
    zeros = (0,) * len(shape)
    return pl.BlockSpec(shape, lambda *_: zeros, pipeline_mode=pl.Buffered(1))


def _layer(arr, i):
    shape = tuple(arr.shape[1:])
    zeros = (0,) * len(shape)
    return pl.BlockSpec((None,) + shape, lambda *_: (i,) + zeros, pipeline_mode=pl.Buffered(1))


def _mod_spec(i, j, row):
    if row is None:
        return pl.BlockSpec((None, None, 1, D_MODEL), lambda bi, *_: (i, bi, 0, j))
    return pl.BlockSpec((None, None, 1, D_MODEL), lambda *_: (i, row, 0, j))


def _silu(a):
    return a / (1.0 + jnp.exp(-a))


def _rms(x, g):
    return x * lax.rsqrt(jnp.mean(x * x, axis=-1, keepdims=True) + EPS) * g


def _dot(a, b):
    return jnp.dot(a, b, preferred_element_type=F32)


def _dot_nt(a, b):
    return lax.dot_general(a, b, (((1,), (1,)), ((), ())), preferred_element_type=F32)


def _dot_tn(a, b):
    return lax.dot_general(a, b, (((0,), (0,)), ((), ())), preferred_element_type=F32)


def _pos_embed(n_tokens, dim):
    rows = n_tokens // GRID_W
    row = np.repeat(np.arange(rows, dtype=np.float32), GRID_W)
    col = np.tile(np.arange(GRID_W, dtype=np.float32), rows)
    quarter = dim // 4
    freqs = (1.0 / (10000.0 ** (np.arange(quarter, dtype=np.float32) / quarter))).astype(np.float32)

    def enc(p):
        ang = (p[:, None] * freqs[None, :]).astype(np.float32)
        return np.concatenate([np.sin(ang), np.cos(ang)], axis=-1)

    return np.concatenate([enc(row), enc(col)], axis=-1).astype(np.float32)


def _dft_cos_sin(n):
    idx = np.arange(n, dtype=np.int64)
    ang = 2.0 * np.pi * ((idx[:, None] * idx[None, :]) % n).astype(np.float64) / n
    return np.cos(ang) / np.sqrt(n), np.sin(ang) / np.sqrt(n)


def _dft_tables(seq, radix):
    gw = FOUR_W // FOUR_G
    cc, sc = _dft_cos_sin(gw)
    eye = np.eye(FOUR_G)
    chan_tab = jnp.asarray(np.concatenate([np.kron(eye, cc), np.kron(eye, sc)], axis=1), dtype=F32).astype(BF16)
    if radix == 1:
        cl, sl = _dft_cos_sin(seq)
        return chan_tab, jnp.asarray(np.concatenate([cl, -sl], axis=1), dtype=F32).astype(BF16)
    n = seq // radix
    cn, sn = _dft_cos_sin(n)
    sub_tab = np.concatenate([cn, sn], axis=1) * np.sqrt(n / seq)
    ang = 2.0 * np.pi * (np.arange(1, radix)[:, None] * np.arange(n)[None, :]).astype(np.float64) / seq
    twc = np.broadcast_to(np.cos(ang)[:, :, None], (radix - 1, n, FOUR_W))
    tws = np.broadcast_to(np.sin(ang)[:, :, None], (radix - 1, n, FOUR_W))
    return (chan_tab, jnp.asarray(sub_tab, dtype=F32).astype(BF16),
            jnp.asarray(twc, dtype=F32), jnp.asarray(tws, dtype=F32))


def _mod_kernel(c_ref, w_ref, b_ref, o_ref):
    s = _silu(c_ref[...]).astype(BF16)
    o_ref[0] = _dot(s, w_ref[0].astype(BF16)) + b_ref[0]


def _modulation(cvec, w_mod, b_mod):
    tn = 1536
    n = N_MOD * D_MODEL
    return pl.pallas_call(
        _mod_kernel,
        grid=(DEPTH, n // tn),
        in_specs=[pl.BlockSpec((MOD_ROWS, D_MODEL), lambda i, j: (0, 0)),
                  pl.BlockSpec((1, D_MODEL, tn), lambda i, j: (i, 0, j)),
                  pl.BlockSpec((1, 1, tn), lambda i, j: (i, 0, j))],
        out_specs=pl.BlockSpec((1, MOD_ROWS, tn), lambda i, j: (i, 0, j)),
        out_shape=jax.ShapeDtypeStruct((DEPTH, MOD_ROWS, n), F32),
        compiler_params=_params(("arbitrary", "arbitrary"), 40),
        name="modulation",
    )(cvec, w_mod, b_mod.reshape(DEPTH, 1, n))


def _inproj_kernel(*refs, add_pos, groups):
    it = iter(refs)
    x_ref = next(it)
    pos_ref = next(it) if add_pos else None
    g_ref, shift_ref, scale_ref, w_ref = next(it), next(it), next(it), next(it)
    out_refs = list(it)
    tm = x_ref.shape[1]
    n_sub = 2 if tm >= 512 else 1
    subs = [slice(i * tm // n_sub, (i + 1) * tm // n_sub) for i in range(n_sub)]
    hs = []
    for r in subs:
        x = x_ref[0, r, :]
        if add_pos:
            x = x + pos_ref[r, :]
        hs.append((_rms(x, g_ref[...]) * (1.0 + scale_ref[...]) + shift_ref[...]).astype(BF16))
    for r, h in zip(subs, hs):
        for o_ref, (_, lo, hi) in zip(out_refs, groups):
            o_ref[0, r, :] = _dot(h, w_ref[:, lo:hi]).astype(BF16)


def _inproj(x, pos, lay, i, mod_row, groups, tm):
    b, l, d = x.shape
    add_pos = pos is not None
    tok = lambda bi, ti: (bi, ti, 0)
    in_specs = [pl.BlockSpec((1, tm, d), tok)]
    args = [x]
    if add_pos:
        in_specs.append(pl.BlockSpec((tm, d), lambda bi, ti: (ti, 0)))
        args.append(pos)
    in_specs += [_layer(lay["g_mix_pre"], i), _mod_spec(i, 0, mod_row), _mod_spec(i, 1, mod_row),
                 _layer(lay["w_in"], i)]
    args += [lay["g_mix_pre"], lay["mod"], lay["mod"], lay["w_in"]]
    outs = pl.pallas_call(
        functools.partial(_inproj_kernel, add_pos=add_pos, groups=groups),
        grid=(b, l // tm),
        in_specs=in_specs,
        out_specs=[pl.BlockSpec((1, tm, hi - lo), tok) for _, lo, hi in groups],
        out_shape=[jax.ShapeDtypeStruct((b, l, hi - lo), BF16) for _, lo, hi in groups],
        compiler_params=_params(("parallel", "parallel"), 48),
        name="inproj",
    )(*args)
    return {name: o for (name, _, _), o in zip(groups, outs)}


def _gla_kernel(*refs, seq, with_output, with_state):
    it = iter(refs)
    k_ref, v_ref, lr_ref, wd_ref, bd_ref, s0_ref = (next(it) for _ in range(6))
    if with_output:
        q_ref, r_ref, nw_ref = next(it), next(it), next(it)
        y_ref = next(it)
    if with_state:
        s_out_ref = next(it)
    st_ref = next(it)
    lg_hi, lg_lo = next(it), next(it)
    ds_scr = next(it)
    dec_scr = next(it)
    if with_output:
        o_scr = next(it)
        qst_scr = next(it)
    n_chunks = seq // CHUNK

    st_ref[...] = s0_ref[0]

    row = lax.broadcasted_iota(jnp.int32, (CHUNK, CHUNK), 0)
    col = lax.broadcasted_iota(jnp.int32, (CHUNK, CHUNK), 1)
    cum = (jnp.where(row >= col, 1.0, 0.0).astype(BF16), jnp.where(row <= col, 1.0, 0.0).astype(BF16))
    lane_head = lax.broadcasted_iota(jnp.int32, (1, K_W), 1) // DK
    head_mask = [lane_head == h for h in range(HEADS)]
    srow = lax.broadcasted_iota(jnp.int32, (CHUNK, HEADS * CHUNK), 0)
    scol = lax.broadcasted_iota(jnp.int32, (CHUNK, HEADS * CHUNK), 1) % CHUNK
    score_mask = (srow >= scol, srow <= scol)
    zero_v = jnp.zeros((CHUNK, DV), BF16)
    last_row = (CHUNK - 1, 0)
    mid_row = (CHUNK // 2 - 1, CHUNK // 2)

    def stack_heads(a):
        zero = jnp.zeros_like(a)
        return jnp.concatenate([jnp.where(head_mask[h], a, zero) for h in range(HEADS)], axis=0)

    def chunk_rows(n):
        return pl.ds(pl.multiple_of(n * CHUNK, CHUNK), CHUNK)

    per_iter = 4
    n_groups = n_chunks // per_iter
    gb = per_iter * CHUNK

    def gate_pieces(grp):
        rows = pl.ds(pl.multiple_of(grp * gb, gb), gb)
        z = _dot(lr_ref[0, rows, :], wd_ref[...]) + bd_ref[...]
        logg = (jnp.minimum(z, 0.0) - jnp.log(1.0 + jnp.exp(-jnp.abs(z)))) * (1.0 / TAU)
        hi = logg.astype(BF16)
        return [(lg_hi, (rows, slice(None)), hi),
                (lg_lo, (rows, slice(None)), (logg - hi.astype(F32)).astype(BF16))]

    for ref, idx, val in gate_pieces(0):
        ref[idx] = val

    def chunk_local(i, carry):
        next_gates = gate_pieces(jnp.minimum(i + 1, n_groups - 1))
        loaded = []
        for u in range(per_iter):
            n = i * per_iter + u
            rows = chunk_rows(n)
            k = k_ref[0, rows, :].astype(F32)
            v = v_ref[0, rows, :]
            v_heads = [v[:, h * DV:(h + 1) * DV] for h in range(HEADS)]
            v_rows = jnp.concatenate(v_heads, axis=0)
            v_diag = jnp.concatenate(
                [jnp.concatenate([v_heads[h] if hh == h else zero_v for hh in range(HEADS)], axis=1)
                 for h in range(HEADS)], axis=0) if with_output else None
            v = (v, v_rows, v_diag)
            qs = q_ref[0, rows, :].astype(F32) * (DK ** -0.5) if with_output else None
            lg = [(lg_hi[rows, d * K_W:(d + 1) * K_W], lg_lo[rows, d * K_W:(d + 1) * K_W]) for d in range(2)]
            loaded.append((n, rows, k, v, qs, lg))
        chains = [(n, rows, k, v, qs, lg[dirn], dirn) for n, rows, k, v, qs, lg in loaded for dirn in range(2)]
        stores = []
        gs = [_dot(cum[dirn], hi) + _dot(cum[dirn], lo) for *_, (hi, lo), dirn in chains]
        score_list = []
        for (n, rows, k, (v, v_rows, v_diag), qs, _, dirn), g in zip(chains, gs):
            g_last = g[last_row[dirn]:last_row[dirn] + 1]
            stores.append((dec_scr, (dirn, pl.ds(pl.multiple_of(n * 8, 8), 8), slice(None)),
                           jnp.broadcast_to(jnp.exp(g_last), (8, K_W))))
            if with_output:
                g_mid = g[mid_row[dirn]:mid_row[dirn] + 1]
                q_in = (qs * jnp.exp(g - g_mid)).astype(BF16)
                k_in = stack_heads((k * jnp.exp(g_mid - g)).astype(BF16))
                score_list.append(_dot_nt(q_in, k_in))
                stores.append((qst_scr, (dirn, rows, slice(None)), (qs * jnp.exp(g)).astype(BF16)))
            k_upd = stack_heads((k * jnp.exp(g_last - g)).astype(BF16))
            stores.append((ds_scr, (dirn, pl.ds(pl.multiple_of(n * DV, DV), DV), slice(None)),
                           _dot_tn(v_rows, k_upd)))
        if with_output:
            for (n, rows, k, (v, v_rows, v_diag), qs, _, dirn), sc in zip(chains, score_list):
                scores = jnp.where(score_mask[dirn], sc, 0.0).astype(BF16)
                stores.append((o_scr, (dirn, rows, slice(None)), _dot(scores, v_diag)))
        for ref, idx, val in stores + next_gates:
            ref[idx] = val
        return carry

    lax.fori_loop(0, n_groups, chunk_local, 0)

    def scan_step(i, finish):
        steps = ((0, i), (1, n_chunks - 1 - i))
        new_st, inters = [], []
        for dirn, n in steps:
            st = st_ref[dirn]
            if with_output:
                q_st = stack_heads(qst_scr[dirn, chunk_rows(n), :])
                inters.append(_dot_nt(q_st, st.astype(BF16)))
            dec = dec_scr[dirn, pl.ds(pl.multiple_of(n * 8, 8), 8), :]
            new_st.append(dec[0:1] * st + ds_scr[dirn, pl.ds(pl.multiple_of(n * DV, DV), DV), :])
        for (dirn, n), st in zip(steps, new_st):
            st_ref[dirn] = st
        if with_output:
            for (dirn, n), inter in zip(steps, inters):
                rows = chunk_rows(n)
                gate = _silu(r_ref[0, rows, :].astype(F32)) if finish else None
                for h in range(HEADS):
                    vs = slice(h * DV, (h + 1) * DV)
                    o = o_scr[dirn, rows, vs] + inter[h * CHUNK:(h + 1) * CHUNK]
                    if finish:
                        o = o + o_scr[1 - dirn, rows, vs]
                        y_ref[0, rows, vs] = (_rms(o, nw_ref[:, vs]) * gate[:, vs]).astype(BF16)
                    else:
                        o_scr[dirn, rows, vs] = o

    half = n_chunks // 2
    unroll = min(8, half)

    def scan_first(i, carry):
        scan_step(i, False)
        return carry

    def scan_second(i, carry):
        scan_step(i, True)
        return carry

    lax.fori_loop(0, half, scan_first, 0, unroll=unroll)
    lax.fori_loop(half, n_chunks, scan_second, 0, unroll=unroll)

    if with_state:
        s_out_ref[0] = st_ref[...]


def _gla(p, lay, i, s0, with_output, with_state):
    b, l, _ = p["k"].shape
    seq3 = lambda bi: (bi, 0, 0)
    in_specs = [pl.BlockSpec((1, l, K_W), seq3), pl.BlockSpec((1, l, V_W), seq3),
                pl.BlockSpec((1, l, LR_PAD), seq3), _layer(lay["w_dec"], i), _layer(lay["b_dec"], i),
                pl.BlockSpec((1, 2, DV, K_W), lambda bi: (bi, 0, 0, 0))]
    args = [p["k"], p["v"], p["lr"], lay["w_dec"], lay["b_dec"], s0]
    out_specs, out_shape = [], []
    scratch = [pltpu.VMEM((2, DV, K_W), F32), pltpu.VMEM((l, 2 * K_W), BF16), pltpu.VMEM((l, 2 * K_W), BF16),
               pltpu.VMEM((2, l // CHUNK * DV, K_W), F32), pltpu.VMEM((2, l // CHUNK * 8, K_W), F32)]
    if with_output:
        in_specs += [pl.BlockSpec((1, l, K_W), seq3), pl.BlockSpec((1, l, V_W), seq3),
                     _layer(lay["gla_norm_w"], i)]
        args += [p["q"], p["r"], lay["gla_norm_w"]]
        out_specs.append(pl.BlockSpec((1, l, V_W), seq3))
        out_shape.append(jax.ShapeDtypeStruct((b, l, V_W), BF16))
        scratch += [pltpu.VMEM((2, l, V_W), F32), pltpu.VMEM((2, l, K_W), BF16)]
    if with_state:
        out_specs.append(pl.BlockSpec((1, 2, DV, K_W), lambda bi: (bi, 0, 0, 0)))
        out_shape.append(jax.ShapeDtypeStruct((b, 2, DV, K_W), F32))
    outs = pl.pallas_call(
        functools.partial(_gla_kernel, seq=l, with_output=with_output, with_state=with_state),
        grid=(b,),
        in_specs=in_specs,
        out_specs=out_specs,
        out_shape=out_shape,
        scratch_shapes=scratch,
        compiler_params=_params(("parallel",), 48),
        name="gla",
    )(*args)
    outs = list(outs)
    y = outs.pop(0) if with_output else None
    s = outs.pop(0) if with_state else None
    return y, s


def _fourier_kernel(uf_ref, chan_ref, pos_ref, yf_ref, ab_scr, *, seq):
    ab = _dot(uf_ref[0], chan_ref[...])
    ab_scr[0:seq, :] = ab[:, :FOUR_W].astype(BF16)
    ab_scr[seq:2 * seq, :] = ab[:, FOUR_W:].astype(BF16)
    rb = min(seq, 512)
    for i in range(seq // rb):
        yf_ref[0, i * rb:(i + 1) * rb, :] = _dot(pos_ref[i * rb:(i + 1) * rb, :], ab_scr[...]).astype(BF16)


FFT_RADIX = 8


def _fourier_fft_kernel(uf_ref, chan_ref, sub_ref, twc_ref, tws_ref, yf_ref, uf_scr, *, seq):
    n = seq // FFT_RADIX
    lanes = uf_scr.shape[2]
    n_lane_tiles = FOUR_W // lanes
    for j in range(n_lane_tiles):
        uf_scr[j] = uf_ref[0, :, j * lanes:(j + 1) * lanes].astype(F32)

    def tokens(r):
        parts = [uf_scr[j, pl.ds(r, n, stride=FFT_RADIX), :] for j in range(n_lane_tiles)]
        return jnp.concatenate(parts, axis=1).astype(BF16)

    ab = [_dot(tokens(r), chan_ref[...]) for r in range(FFT_RADIX)]
    pq = []
    for ab_r in ab:
        a, b = ab_r[:, :FOUR_W].astype(BF16), ab_r[:, FOUR_W:].astype(BF16)
        w = jnp.concatenate([jnp.concatenate([a, b], axis=1), jnp.concatenate([-b, a], axis=1)], axis=0)
        pq.append(_dot(sub_ref[...], w))
    re, im = [pq[0][:, :FOUR_W]], [pq[0][:, FOUR_W:]]
    for r in range(1, FFT_RADIX):
        p, q = pq[r][:, :FOUR_W], pq[r][:, FOUR_W:]
        c, s = twc_ref[r - 1], tws_ref[r - 1]
        re.append(c * p - s * q)
        im.append(s * p + c * q)

    def quarter(z0, z1, z2, z3):
        s02 = (z0[0] + z2[0], z0[1] + z2[1])
        d02 = (z0[0] - z2[0], z0[1] - z2[1])
        s13 = (z1[0] + z3[0], z1[1] + z3[1])
        d13 = (z1[0] - z3[0], z1[1] - z3[1])
        return ((s02[0] + s13[0], s02[1] + s13[1]), (d02[0] - d13[1], d02[1] + d13[0]),
                (s02[0] - s13[0], s02[1] - s13[1]), (d02[0] + d13[1], d02[1] - d13[0]))

    zs = list(zip(re, im))
    even = quarter(zs[0], zs[2], zs[4], zs[6])
    odd = quarter(zs[1], zs[3], zs[5], zs[7])
    half = 0.5 ** 0.5
    odd_re = (odd[0][0], (odd[1][0] - odd[1][1]) * half, -odd[2][1], -(odd[3][0] + odd[3][1]) * half)
    for q in range(4):
        yf_ref[0, q * n:(q + 1) * n, :] = (even[q][0] + odd_re[q]).astype(BF16)
        yf_ref[0, (q + 4) * n:(q + 5) * n, :] = (even[q][0] - odd_re[q]).astype(BF16)


def _fourier(uf, tabs):
    b, l, _ = uf.shape
    tok = pl.BlockSpec((1, l, FOUR_W), lambda bi: (bi, 0, 0))
    if len(tabs) == 4:
        body, name = _fourier_fft_kernel, "fourier_fft"
        scratch = pltpu.VMEM((FOUR_W // 128, l, 128), F32)
    else:
        body, name = _fourier_kernel, "fourier"
        scratch = pltpu.VMEM((2 * l, FOUR_W), BF16)
    return pl.pallas_call(
        functools.partial(body, seq=l),
        grid=(b,),
        in_specs=[tok] + [_resident(t.shape) for t in tabs],
        out_specs=tok,
        out_shape=jax.ShapeDtypeStruct((b, l, FOUR_W), BF16),
        scratch_shapes=[scratch],
        compiler_params=_params(("parallel",), 48),
        name=name,
    )(uf, *tabs)


_FFN_TILES = ((0, 1536), (1536, HIDDEN))


CONV_PAD = 8
HALO_ROWS = 16


def _out_ffn_kernel(*refs, add_pos, n_sub):
    it = iter(refs)
    yg_ref, yf_ref, bg_ref, cg_ref, uc_ref = (next(it) for _ in range(5))
    cg_prev_ref, uc_prev_ref, cg_next_ref, uc_next_ref, cw_ref, cb_ref = (next(it) for _ in range(6))
    wo_ref, x_ref = next(it), next(it)
    pos_ref = next(it) if add_pos else None
    gmix_ref, gatemix_ref, gpre_ref, shift_ref, scale_ref, gate_ref, gpost_ref = (next(it) for _ in range(7))
    win_ref, wout_ref, o_ref, z_scr = next(it), next(it), next(it), next(it)
    tm = x_ref.shape[1]
    subs = [slice(i * tm // n_sub, (i + 1) * tm // n_sub) for i in range(n_sub)]

    t, nt = pl.program_id(1), pl.num_programs(1)
    z_before = (cg_prev_ref[0].astype(F32) * uc_prev_ref[0].astype(F32))[HALO_ROWS - CONV_PAD:]
    z_after = (cg_next_ref[0].astype(F32) * uc_next_ref[0].astype(F32))[:CONV_PAD]
    z_scr[0:CONV_PAD, :] = jnp.where(t > 0, z_before, 0.0)
    z_scr[CONV_PAD + tm:, :] = jnp.where(t < nt - 1, z_after, 0.0)
    z_scr[CONV_PAD:CONV_PAD + tm, :] = cg_ref[0].astype(F32) * uc_ref[0].astype(F32)
    ycs = []
    for r in subs:
        s, n = CONV_PAD + r.start, r.stop - r.start
        y = (cw_ref[0:1] * z_scr[s - 1:s - 1 + n, :] + cw_ref[1:2] * z_scr[s:s + n, :]
             + cw_ref[2:3] * z_scr[s + 1:s + 1 + n, :] + cb_ref[...])
        ycs.append((bg_ref[0, r, :].astype(F32) * y).astype(BF16))

    ys = [(_dot(yg_ref[0, r, :], wo_ref[0:V_W]) + _dot(yf_ref[0, r, :], wo_ref[V_W:V_W + FOUR_W])
           + _dot(yc, wo_ref[V_W + FOUR_W:])) for r, yc in zip(subs, ycs)]
    xs, hs = [], []
    for r, y in zip(subs, ys):
        x = x_ref[0, r, :]
        if add_pos:
            x = x + pos_ref[r, :]
        x = x + gatemix_ref[...] * _rms(y, gmix_ref[...])
        xs.append(x)
        hs.append((_rms(x, gpre_ref[...]) * (1.0 + scale_ref[...]) + shift_ref[...]).astype(BF16))
    ys = []
    for h in hs:
        y = jnp.zeros((h.shape[0], D_MODEL), F32)
        for lo, hi in _FFN_TILES:
            a = _dot(h, win_ref[:, lo:hi])
            u = _dot(h, win_ref[:, HIDDEN + lo:HIDDEN + hi])
            y = y + _dot((_silu(a) * u).astype(BF16), wout_ref[lo:hi, :])
        ys.append(y)
    for r, x, y in zip(subs, xs, ys):
        o_ref[0, r, :] = x + gate_ref[...] * _rms(y, gpost_ref[...])


def _out_ffn(yg, yf, p, x, pos, lay, i, mod_row, tm):
    b, l, d = x.shape
    add_pos = pos is not None
    tok = lambda bi, ti: (bi, ti, 0)
    halo_per_tile, n_halo = tm // HALO_ROWS, l // HALO_ROWS
    before = lambda bi, ti: (bi, jnp.maximum(ti * halo_per_tile - 1, 0), 0)
    after = lambda bi, ti: (bi, jnp.minimum((ti + 1) * halo_per_tile, n_halo - 1), 0)
    conv_tok = pl.BlockSpec((1, tm, CONV_W), tok)
    in_specs = [pl.BlockSpec((1, tm, V_W), tok), pl.BlockSpec((1, tm, FOUR_W), tok), conv_tok, conv_tok, conv_tok,
                pl.BlockSpec((1, HALO_ROWS, CONV_W), before), pl.BlockSpec((1, HALO_ROWS, CONV_W), before),
                pl.BlockSpec((1, HALO_ROWS, CONV_W), after), pl.BlockSpec((1, HALO_ROWS, CONV_W), after),
                _layer(lay["conv_w"], i), _layer(lay["conv_b"], i),
                _layer(lay["w_out"], i), pl.BlockSpec((1, tm, d), tok)]
    args = [yg, yf, p["bg"], p["cg"], p["uc"], p["cg"], p["uc"], p["cg"], p["uc"],
            lay["conv_w"], lay["conv_b"], lay["w_out"], x]
    if add_pos:
        in_specs.append(pl.BlockSpec((tm, d), lambda bi, ti: (ti, 0)))
        args.append(pos)
    in_specs += [_layer(lay["g_mix_post"], i), _mod_spec(i, 2, mod_row), _layer(lay["g_ffn_pre"], i),
                 _mod_spec(i, 3, mod_row), _mod_spec(i, 4, mod_row), _mod_spec(i, 5, mod_row),
                 _layer(lay["g_ffn_post"], i), _layer(lay["w_ffn_in"], i), _layer(lay["w_ffn_out"], i)]
    args += [lay["g_mix_post"], lay["mod"], lay["g_ffn_pre"], lay["mod"], lay["mod"], lay["mod"],
             lay["g_ffn_post"], lay["w_ffn_in"], lay["w_ffn_out"]]
    return pl.pallas_call(
        functools.partial(_out_ffn_kernel, add_pos=add_pos, n_sub=2 if tm >= 512 else 1),
        grid=(b, l // tm),
        in_specs=in_specs,
        out_specs=pl.BlockSpec((1, tm, d), tok),
        out_shape=jax.ShapeDtypeStruct((b, l, d), F32),
        scratch_shapes=[pltpu.VMEM((tm + 2 * CONV_PAD, CONV_W), F32)],
        compiler_params=_params(("parallel", "parallel"), 56),
        name="out_ffn",
    )(*args)


def _arrange_w_in(w):
    main = jnp.concatenate([w[..., _C_K:_C_LR], w[..., _C_Q:_C_END]], axis=-1)
    lr = jnp.pad(w[..., _C_LR:_C_Q], ((0, 0), (0, 0), (0, LR_PAD - 2 * RANK)))
    return jnp.concatenate([main, lr], axis=-1).astype(BF16)


def _arrange_w_dec(w_f, w_b):
    wd = jnp.zeros((DEPTH, LR_PAD, 2 * K_W), F32)
    wd = wd.at[:, 0:RANK, :K_W].set(w_f).at[:, RANK:2 * RANK, K_W:].set(w_b)
    return wd.astype(BF16)


def kernel(x, c, ctx, c_ctx, w_mod, b_mod, g_mix_pre, g_mix_post, g_ffn_pre, g_ffn_post,
           w_in, w_dec_f, b_dec_f, w_dec_b, b_dec_b, gla_norm_w, conv_w, conv_b, w_out,
           w_ffn_in, w_ffn_out):
    b, n_lat, d = x.shape
    n_ctx = ctx.shape[1]
    assert d == D_MODEL and w_in.shape == (DEPTH, D_MODEL, _C_END)
    assert n_lat % 512 == 0 and n_ctx % CHUNK == 0 and b + 1 <= MOD_ROWS

    pos = jnp.asarray(_pos_embed(n_lat, d))
    tabs = {n_lat: _dft_tables(n_lat, FFT_RADIX), n_ctx: _dft_tables(n_ctx, 1)}

    cvec = jnp.zeros((MOD_ROWS, d), F32).at[:b].set(c).at[b].set(c_ctx)
    ctx_row = b
    lay = {
        "mod": _modulation(cvec, w_mod, b_mod).reshape(DEPTH, MOD_ROWS, 1, N_MOD * d),
        "w_in": _arrange_w_in(w_in),
        "w_dec": _arrange_w_dec(w_dec_f, w_dec_b),
        "b_dec": jnp.concatenate([b_dec_f, b_dec_b], axis=-1).reshape(DEPTH, 1, 2 * K_W),
        "gla_norm_w": gla_norm_w.reshape(DEPTH, 1, V_W),
        "conv_w": conv_w,
        "conv_b": conv_b.reshape(DEPTH, 1, CONV_W),
        "w_out": w_out.astype(BF16),
        "w_ffn_in": w_ffn_in.astype(BF16),
        "w_ffn_out": w_ffn_out.astype(BF16),
        "g_mix_pre": g_mix_pre.reshape(DEPTH, 1, d),
        "g_mix_post": g_mix_post.reshape(DEPTH, 1, d),
        "g_ffn_pre": g_ffn_pre.reshape(DEPTH, 1, d),
        "g_ffn_post": g_ffn_post.reshape(DEPTH, 1, d),
    }

    def mixer_and_ffn(i, xs, pos_s, mod_row, s0, want_state, tm):
        p = _inproj(xs, pos_s, lay, i, mod_row, _OUT_GROUPS_FULL, tm)
        yg, s = _gla(p, lay, i, s0, True, want_state)
        yf = _fourier(p["uf"], tabs[xs.shape[1]])
        return _out_ffn(yg, yf, p, xs, pos_s, lay, i, mod_row, tm), s

    zero_state = jnp.zeros((b, 2, DV, K_W), F32)
    xc = ctx
    for i in range(DEPTH):
        if i == DEPTH - 1:
            p = _inproj(xc, None, lay, i, ctx_row, _OUT_GROUPS_STATE, n_ctx)
            _, s = _gla(p, lay, i, zero_state, False, True)
        else:
            xc, s = mixer_and_ffn(i, xc, None, ctx_row, zero_state, True, n_ctx)
        x, _ = mixer_and_ffn(i, x, pos if i == 0 else None, None, s, False, 512)
    return x
```

```python
import functools

import numpy as np
import jax
import jax.numpy as jnp
from jax import lax
from jax.experimental import pallas as pl
from jax.experimental.pallas import tpu as pltpu

F32 = jnp.float32
BF16 = jnp.bfloat16

D_MODEL = 1024
DEPTH = 2
GRID_W = 64
N_MOD = 6
HEADS = 4
DK = 64
DV = 128
K_W = HEADS * DK
V_W = HEADS * DV
RANK = 16
TAU = 16.0
CHUNK = 64
FOUR_W = 256
FOUR_G = 4
CONV_W = 256
HIDDEN = 2816
EPS = 1e-6
LR_PAD = 128
MOD_ROWS = 16

_C_K, _C_V, _C_LR, _C_Q, _C_R, _C_FOUR, _C_BG, _C_CG, _C_UC, _C_END = (
    0, 256, 768, 800, 1056, 1568, 1824, 2080, 2336, 2592)
_MAIN_W = 2560
_IN_W = _MAIN_W + LR_PAD
_OUT_GROUPS_FULL = (("k", 0, 256), ("v", 256, 768), ("q", 768, 1024), ("r", 1024, 1536),
                    ("uf", 1536, 1792), ("bg", 1792, 2048), ("cg", 2048, 2304),
                    ("uc", 2304, 2560))
_OUT_GROUPS_STATE = (("k", 0, 256), ("v", 256, 768))

V7X_VMEM_BYTES = 64 * 1024 * 1024


def _params(semantics, vmem_mb):
    assert vmem_mb * 1024 * 1024 < V7X_VMEM_BYTES
    return pltpu.CompilerParams(dimension_semantics=semantics,
                                vmem_limit_bytes=vmem_mb * 1024 * 1024)


def _resident(shape):
    zeros = (0,) * len(shape)
    return pl.BlockSpec(shape, lambda *_: zeros, pipeline_mode=pl.Buffered(1))


def _layer(arr, i):
    shape = tuple(arr.shape[1:])
    zeros = (0,) * len(shape)
    return pl.BlockSpec((None,) + shape, lambda *_: (i,) + zeros, pipeline_mode=pl.Buffered(1))


def _mod_spec(i, j, row):
    if row is None:
        return pl.BlockSpec((None, None, 1, D_MODEL), lambda bi, *_: (i, bi, 0, j))
    return pl.BlockSpec((None, None, 1, D_MODEL), lambda *_: (i, row, 0, j))


def _silu(a):
    return a / (1.0 + jnp.exp(-a))


def _rms(x, g):
    return x * lax.rsqrt(jnp.mean(x * x, axis=-1, keepdims=True) + EPS) * g


def _dot(a, b):
    return jnp.dot(a, b, preferred_element_type=F32)


def _dot_nt(a, b):
    return lax.dot_general(a, b, (((1,), (1,)), ((), ())), preferred_element_type=F32)


def _dot_tn(a, b):
    return lax.dot_general(a, b, (((0,), (0,)), ((), ())), preferred_element_type=F32)


def _pos_embed(n_tokens, dim):
    rows = n_tokens // GRID_W
    row = np.repeat(np.arange(rows, dtype=np.float32), GRID_W)
    col = np.tile(np.arange(GRID_W, dtype=np.float32), rows)
    quarter = dim // 4
    freqs = (1.0 / (10000.0 ** (np.arange(quarter, dtype=np.float32) / quarter))).astype(np.float32)

    def enc(p):
        ang = (p[:, None] * freqs[None, :]).astype(np.float32)
        return np.concatenate([np.sin(ang), np.cos(ang)], axis=-1)

    return np.concatenate([enc(row), enc(col)], axis=-1).astype(np.float32)


def _dft_cos_sin(n):
    idx = np.arange(n, dtype=np.int64)
    ang = 2.0 * np.pi * ((idx[:, None] * idx[None, :]) % n).astype(np.float64) / n
    return np.cos(ang) / np.sqrt(n), np.sin(ang) / np.sqrt(n)


def _dft_tables(seq, radix):
    gw = FOUR_W // FOUR_G
    cc, sc = _dft_cos_sin(gw)
    eye = np.eye(FOUR_G)
    chan_tab = jnp.asarray(np.concatenate([np.kron(eye, cc), np.kron(eye, sc)], axis=1), dtype=F32).astype(BF16)
    if radix == 1:
        cl, sl = _dft_cos_sin(seq)
        return chan_tab, jnp.asarray(np.concatenate([cl, -sl], axis=1), dtype=F32).astype(BF16)
    n = seq // radix
    cn, sn = _dft_cos_sin(n)
    sub_tab = np.concatenate([cn, sn], axis=1) * np.sqrt(n / seq)
    ang = 2.0 * np.pi * (np.arange(1, radix)[:, None] * np.arange(n)[None, :]).astype(np.float64) / seq
    twc = np.broadcast_to(np.cos(ang)[:, :, None], (radix - 1, n, FOUR_W))
    tws = np.broadcast_to(np.sin(ang)[:, :, None], (radix - 1, n, FOUR_W))
    return (chan_tab, jnp.asarray(sub_tab, dtype=F32).astype(BF16),
            jnp.asarray(twc, dtype=F32), jnp.asarray(tws, dtype=F32))


def _mod_kernel(c_ref, w_ref, b_ref, o_ref):
    s = _silu(c_ref[...]).astype(BF16)
    o_ref[0] = _dot(s, w_ref[0].astype(BF16)) + b_ref[0]


def _modulation(cvec, w_mod, b_mod):
    tn = 1536
    n = N_MOD * D_MODEL
    return pl.pallas_call(
        _mod_kernel,
        grid=(DEPTH, n // tn),
        in_specs=[pl.BlockSpec((MOD_ROWS, D_MODEL), lambda i, j: (0, 0)),
                  pl.BlockSpec((1, D_MODEL, tn), lambda i, j: (i, 0, j)),
                  pl.BlockSpec((1, 1, tn), lambda i, j: (i, 0, j))],
        out_specs=pl.BlockSpec((1, MOD_ROWS, tn), lambda i, j: (i, 0, j)),
        out_shape=jax.ShapeDtypeStruct((DEPTH, MOD_ROWS, n), F32),
        compiler_params=_params(("arbitrary", "arbitrary"), 40),
        name="modulation",
    )(cvec, w_mod, b_mod.reshape(DEPTH, 1, n))


def _inproj_kernel(*refs, add_pos, groups):
    it = iter(refs)
    x_ref = next(it)
    pos_ref = next(it) if add_pos else None
    g_ref, shift_ref, scale_ref, w_ref, wd_ref, bd_ref = (next(it) for _ in range(6))
    out_refs = list(it)
    lg_hi_ref, lg_lo_ref = out_refs[-2:]
    tm = x_ref.shape[1]
    n_sub = 2 if tm >= 512 else 1
    subs = [slice(i * tm // n_sub, (i + 1) * tm // n_sub) for i in range(n_sub)]
    hs = []
    for r in subs:
        x = x_ref[0, r, :]
        if add_pos:
            x = x + pos_ref[r, :]
        hs.append((_rms(x, g_ref[...]) * (1.0 + scale_ref[...]) + shift_ref[...]).astype(BF16))
    def project(r, h):
        for o_ref, (_, lo, hi_col) in zip(out_refs, groups):
            o_ref[0, r, :] = _dot(h, w_ref[:, lo:hi_col]).astype(BF16)

    low_rank = [_dot(h, w_ref[:, _MAIN_W:_IN_W]).astype(BF16) for h in hs]
    project(subs[0], hs[0])
    zs = [_dot(lr, wd_ref[...]) + bd_ref[...] for lr in low_rank]
    for r, h in list(zip(subs, hs))[1:]:
        project(r, h)
    for r, z in zip(subs, zs):
        logg = (jnp.minimum(z, 0.0) - jnp.log(1.0 + jnp.exp(-jnp.abs(z)))) * (1.0 / TAU)
        hi = logg.astype(BF16)
        lg_hi_ref[0, r, :] = hi
        lg_lo_ref[0, r, :] = (logg - hi.astype(F32)).astype(BF16)


def _inproj(x, pos, lay, i, mod_row, groups, tm):
    b, l, d = x.shape
    add_pos = pos is not None
    tok = lambda bi, ti: (bi, ti, 0)
    in_specs = [pl.BlockSpec((1, tm, d), tok)]
    args = [x]
    if add_pos:
        in_specs.append(pl.BlockSpec((tm, d), lambda bi, ti: (ti, 0)))
        args.append(pos)
    in_specs += [_layer(lay["g_mix_pre"], i), _mod_spec(i, 0, mod_row), _mod_spec(i, 1, mod_row),
                 _layer(lay["w_in"], i), _layer(lay["w_dec"], i), _layer(lay["b_dec"], i)]
    args += [lay["g_mix_pre"], lay["mod"], lay["mod"], lay["w_in"], lay["w_dec"], lay["b_dec"]]
    widths = [hi - lo for _, lo, hi in groups] + [2 * K_W, 2 * K_W]
    names = [name for name, _, _ in groups] + ["lg_hi", "lg_lo"]
    outs = pl.pallas_call(
        functools.partial(_inproj_kernel, add_pos=add_pos, groups=groups),
        grid=(b, l // tm),
        in_specs=in_specs,
        out_specs=[pl.BlockSpec((1, tm, w), tok) for w in widths],
        out_shape=[jax.ShapeDtypeStruct((b, l, w), BF16) for w in widths],
        compiler_params=_params(("parallel", "parallel"), 48),
        name="inproj",
    )(*args)
    return dict(zip(names, outs))


def _gla_kernel(*refs, seq, with_output, with_state):
    it = iter(refs)
    k_ref, v_ref, lg_hi, lg_lo, s0_ref = (next(it) for _ in range(5))
    if with_output:
        q_ref, r_ref, nw_ref = next(it), next(it), next(it)
        y_ref = next(it)
    if with_state:
        s_out_ref = next(it)
    st_ref = next(it)
    ds_scr = next(it)
    dec_scr = next(it)
    if with_output:
        o_scr = next(it)
        qst_scr = next(it)
    n_chunks = seq // CHUNK

    st_ref[...] = s0_ref[0]

    row = lax.broadcasted_iota(jnp.int32, (CHUNK, CHUNK), 0)
    col = lax.broadcasted_iota(jnp.int32, (CHUNK, CHUNK), 1)
    cum = (jnp.where(row >= col, 1.0, 0.0).astype(BF16), jnp.where(row <= col, 1.0, 0.0).astype(BF16))
    lane_head = lax.broadcasted_iota(jnp.int32, (1, K_W), 1) // DK
    head_mask = [lane_head == h for h in range(HEADS)]
    srow = lax.broadcasted_iota(jnp.int32, (CHUNK, HEADS * CHUNK), 0)
    scol = lax.broadcasted_iota(jnp.int32, (CHUNK, HEADS * CHUNK), 1) % CHUNK
    score_mask = (srow >= scol, srow <= scol)
    zero_v = jnp.zeros((CHUNK, DV), BF16)
    last_row = (CHUNK - 1, 0)
    mid_row = (CHUNK // 2 - 1, CHUNK // 2)

    def stack_heads(a):
        zero = jnp.zeros_like(a)
        return jnp.concatenate([jnp.where(head_mask[h], a, zero) for h in range(HEADS)], axis=0)

    def chunk_rows(n):
        return pl.ds(pl.multiple_of(n * CHUNK, CHUNK), CHUNK)

    per_iter = 4

    def chunk_local(i, carry):
        loaded = []
        for u in range(per_iter):
            n = i * per_iter + u
            rows = chunk_rows(n)
            k = k_ref[0, rows, :].astype(F32)
            v = v_ref[0, rows, :]
            v_heads = [v[:, h * DV:(h + 1) * DV] for h in range(HEADS)]
            v_rows = jnp.concatenate(v_heads, axis=0)
            v_diag = jnp.concatenate(
                [jnp.concatenate([v_heads[h] if hh == h else zero_v for hh in range(HEADS)], axis=1)
                 for h in range(HEADS)], axis=0) if with_output else None
            v = (v, v_rows, v_diag)
            qs = q_ref[0, rows, :].astype(F32) * (DK ** -0.5) if with_output else None
            lg = [(lg_hi[0, rows, d * K_W:(d + 1) * K_W], lg_lo[0, rows, d * K_W:(d + 1) * K_W]) for d in range(2)]
            loaded.append((n, rows, k, v, qs, lg))
        chains = [(n, rows, k, v, qs, lg[dirn], dirn) for n, rows, k, v, qs, lg in loaded for dirn in range(2)]
        stores = []
        gs = [_dot(cum[dirn], hi) + _dot(cum[dirn], lo) for *_, (hi, lo), dirn in chains]
        score_list = []
        for (n, rows, k, (v, v_rows, v_diag), qs, _, dirn), g in zip(chains, gs):
            g_last = g[last_row[dirn]:last_row[dirn] + 1]
            stores.append((dec_scr, (dirn, pl.ds(pl.multiple_of(n * 8, 8), 8), slice(None)),
                           jnp.broadcast_to(jnp.exp(g_last), (8, K_W))))
            if with_output:
                g_mid = g[mid_row[dirn]:mid_row[dirn] + 1]
                q_in = qs * jnp.exp(g - g_mid)
                k_in = k * jnp.exp(g_mid - g)
                score_list.append(_dot_nt(q_in.astype(BF16), stack_heads(k_in.astype(BF16))))
                stores.append((qst_scr, (dirn, rows, slice(None)), (q_in * jnp.exp(g_mid)).astype(BF16)))
                k_upd = stack_heads((k_in * jnp.exp(g_last - g_mid)).astype(BF16))
            else:
                k_upd = stack_heads((k * jnp.exp(g_last - g)).astype(BF16))
            stores.append((ds_scr, (dirn, pl.ds(pl.multiple_of(n * DV, DV), DV), slice(None)),
                           _dot_tn(v_rows, k_upd)))
        if with_output:
            for (n, rows, k, (v, v_rows, v_diag), qs, _, dirn), sc in zip(chains, score_list):
                scores = jnp.where(score_mask[dirn], sc, 0.0).astype(BF16)
                stores.append((o_scr, (dirn, rows, slice(None)), _dot(scores, v_diag)))
        for ref, idx, val in stores:
            ref[idx] = val
        return carry

    lax.fori_loop(0, n_chunks // per_iter, chunk_local, 0)

    def scan_step(i, finish):
        steps = ((0, i), (1, n_chunks - 1 - i))
        new_st, inters = [], []
        for dirn, n in steps:
            st = st_ref[dirn]
            if with_output:
                q_st = stack_heads(qst_scr[dirn, chunk_rows(n), :])
                inters.append(_dot_nt(q_st, st.astype(BF16)))
            dec = dec_scr[dirn, pl.ds(pl.multiple_of(n * 8, 8), 8), :]
            new_st.append(dec[0:1] * st + ds_scr[dirn, pl.ds(pl.multiple_of(n * DV, DV), DV), :])
        for (dirn, n), st in zip(steps, new_st):
            st_ref[dirn] = st
        if with_output:
            for (dirn, n), inter in zip(steps, inters):
                rows = chunk_rows(n)
                gate = _silu(r_ref[0, rows, :].astype(F32)) if finish else None
                for h in range(HEADS):
                    vs = slice(h * DV, (h + 1) * DV)
                    o = o_scr[dirn, rows, vs] + inter[h * CHUNK:(h + 1) * CHUNK]
                    if finish:
                        o = o + o_scr[1 - dirn, rows, vs]
                        y_ref[0, rows, vs] = (_rms(o, nw_ref[:, vs]) * gate[:, vs]).astype(BF16)
                    else:
                        o_scr[dirn, rows, vs] = o

    half = n_chunks // 2
    unroll = min(8, half)

    def scan_first(i, carry):
        scan_step(i, False)
        return carry

    def scan_second(i, carry):
        scan_step(i, True)
        return carry

    lax.fori_loop(0, half, scan_first, 0, unroll=unroll)
    lax.fori_loop(half, n_chunks, scan_second, 0, unroll=unroll)

    if with_state:
        s_out_ref[0] = st_ref[...]


def _gla(p, lay, i, s0, with_output, with_state):
    b, l, _ = p["k"].shape
    seq3 = lambda bi: (bi, 0, 0)
    in_specs = [pl.BlockSpec((1, l, K_W), seq3), pl.BlockSpec((1, l, V_W), seq3),
                pl.BlockSpec((1, l, 2 * K_W), seq3), pl.BlockSpec((1, l, 2 * K_W), seq3),
                pl.BlockSpec((1, 2, DV, K_W), lambda bi: (bi, 0, 0, 0))]
    args = [p["k"], p["v"], p["lg_hi"], p["lg_lo"], s0]
    out_specs, out_shape = [], []
    scratch = [pltpu.VMEM((2, DV, K_W), F32),
               pltpu.VMEM((2, l // CHUNK * DV, K_W), F32), pltpu.VMEM((2, l // CHUNK * 8, K_W), F32)]
    if with_output:
        in_specs += [pl.BlockSpec((1, l, K_W), seq3), pl.BlockSpec((1, l, V_W), seq3),
                     _layer(lay["gla_norm_w"], i)]
        args += [p["q"], p["r"], lay["gla_norm_w"]]
        out_specs.append(pl.BlockSpec((1, l, V_W), seq3))
        out_shape.append(jax.ShapeDtypeStruct((b, l, V_W), BF16))
        scratch += [pltpu.VMEM((2, l, V_W), F32), pltpu.VMEM((2, l, K_W), BF16)]
    if with_state:
        out_specs.append(pl.BlockSpec((1, 2, DV, K_W), lambda bi: (bi, 0, 0, 0)))
        out_shape.append(jax.ShapeDtypeStruct((b, 2, DV, K_W), F32))
    outs = pl.pallas_call(
        functools.partial(_gla_kernel, seq=l, with_output=with_output, with_state=with_state),
        grid=(b,),
        in_specs=in_specs,
        out_specs=out_specs,
        out_shape=out_shape,
        scratch_shapes=scratch,
        compiler_params=_params(("parallel",), 48),
        name="gla",
    )(*args)
    outs = list(outs)
    y = outs.pop(0) if with_output else None
    s = outs.pop(0) if with_state else None
    return y, s


def _fourier_kernel(uf_ref, chan_ref, pos_ref, yf_ref, ab_scr, *, seq):
    ab = _dot(uf_ref[0], chan_ref[...])
    ab_scr[0:seq, :] = ab[:, :FOUR_W].astype(BF16)
    ab_scr[seq:2 * seq, :] = ab[:, FOUR_W:].astype(BF16)
    rb = min(seq, 512)
    for i in range(seq // rb):
        yf_ref[0, i * rb:(i + 1) * rb, :] = _dot(pos_ref[i * rb:(i + 1) * rb, :], ab_scr[...]).astype(BF16)


FFT_RADIX = 8


def _fourier_fft_kernel(uf_ref, chan_ref, sub_ref, twc_ref, tws_ref, yf_ref, uf_scr, *, seq):
    n = seq // FFT_RADIX
    lanes = uf_scr.shape[2]
    n_lane_tiles = FOUR_W // lanes
    for j in range(n_lane_tiles):
        uf_scr[j] = uf_ref[0, :, j * lanes:(j + 1) * lanes].astype(F32)

    def tokens(r):
        parts = [uf_scr[j, pl.ds(r, n, stride=FFT_RADIX), :] for j in range(n_lane_tiles)]
        return jnp.concatenate(parts, axis=1).astype(BF16)

    ab = [_dot(tokens(r), chan_ref[...]) for r in range(FFT_RADIX)]
    pq = []
    for ab_r in ab:
        a, b = ab_r[:, :FOUR_W].astype(BF16), ab_r[:, FOUR_W:].astype(BF16)
        w = jnp.concatenate([jnp.concatenate([a, b], axis=1), jnp.concatenate([-b, a], axis=1)], axis=0)
        pq.append(_dot(sub_ref[...], w))
    re, im = [pq[0][:, :FOUR_W]], [pq[0][:, FOUR_W:]]
    for r in range(1, FFT_RADIX):
        p, q = pq[r][:, :FOUR_W], pq[r][:, FOUR_W:]
        c, s = twc_ref[r - 1], tws_ref[r - 1]
        re.append(c * p - s * q)
        im.append(s * p + c * q)

    def quarter(z0, z1, z2, z3):
        s02 = (z0[0] + z2[0], z0[1] + z2[1])
        d02 = (z0[0] - z2[0], z0[1] - z2[1])
        s13 = (z1[0] + z3[0], z1[1] + z3[1])
        d13 = (z1[0] - z3[0], z1[1] - z3[1])
        return ((s02[0] + s13[0], s02[1] + s13[1]), (d02[0] - d13[1], d02[1] + d13[0]),
                (s02[0] - s13[0], s02[1] - s13[1]), (d02[0] + d13[1], d02[1] - d13[0]))

    zs = list(zip(re, im))
    even = quarter(zs[0], zs[2], zs[4], zs[6])
    odd = quarter(zs[1], zs[3], zs[5], zs[7])
    half = 0.5 ** 0.5
    odd_re = (odd[0][0], (odd[1][0] - odd[1][1]) * half, -odd[2][1], -(odd[3][0] + odd[3][1]) * half)
    for q in range(4):
        yf_ref[0, q * n:(q + 1) * n, :] = (even[q][0] + odd_re[q]).astype(BF16)
        yf_ref[0, (q + 4) * n:(q + 5) * n, :] = (even[q][0] - odd_re[q]).astype(BF16)


def _fourier(uf, tabs):
    b, l, _ = uf.shape
    tok = pl.BlockSpec((1, l, FOUR_W), lambda bi: (bi, 0, 0))
    if len(tabs) == 4:
        body, name = _fourier_fft_kernel, "fourier_fft"
        scratch = pltpu.VMEM((FOUR_W // 128, l, 128), F32)
    else:
        body, name = _fourier_kernel, "fourier"
        scratch = pltpu.VMEM((2 * l, FOUR_W), BF16)
    return pl.pallas_call(
        functools.partial(body, seq=l),
        grid=(b,),
        in_specs=[tok] + [_resident(t.shape) for t in tabs],
        out_specs=tok,
        out_shape=jax.ShapeDtypeStruct((b, l, FOUR_W), BF16),
        scratch_shapes=[scratch],
        compiler_params=_params(("parallel",), 48),
        name=name,
    )(uf, *tabs)


_FFN_TILES = ((0, 1536), (1536, HIDDEN))


CONV_PAD = 8
HALO_ROWS = 16


def _out_ffn_kernel(*refs, add_pos, n_sub):
    it = iter(refs)
    yg_ref, yf_ref, bg_ref, cg_ref, uc_ref = (next(it) for _ in range(5))
    cg_prev_ref, uc_prev_ref, cg_next_ref, uc_next_ref, cw_ref, cb_ref = (next(it) for _ in range(6))
    wo_ref, x_ref = next(it), next(it)
    pos_ref = next(it) if add_pos else None
    gmix_ref, gatemix_ref, gpre_ref, shift_ref, scale_ref, gate_ref, gpost_ref = (next(it) for _ in range(7))
    win_ref, wout_ref, o_ref, z_scr = next(it), next(it), next(it), next(it)
    tm = x_ref.shape[1]
    subs = [slice(i * tm // n_sub, (i + 1) * tm // n_sub) for i in range(n_sub)]

    t, nt = pl.program_id(1), pl.num_programs(1)
    z_before = (cg_prev_ref[0].astype(F32) * uc_prev_ref[0].astype(F32))[HALO_ROWS - CONV_PAD:]
    z_after = (cg_next_ref[0].astype(F32) * uc_next_ref[0].astype(F32))[:CONV_PAD]
    z_scr[0:CONV_PAD, :] = jnp.where(t > 0, z_before, 0.0)
    z_scr[CONV_PAD + tm:, :] = jnp.where(t < nt - 1, z_after, 0.0)
    z_scr[CONV_PAD:CONV_PAD + tm, :] = cg_ref[0].astype(F32) * uc_ref[0].astype(F32)
    ycs = []
    for r in subs:
        s, n = CONV_PAD + r.start, r.stop - r.start
        y = (cw_ref[0:1] * z_scr[s - 1:s - 1 + n, :] + cw_ref[1:2] * z_scr[s:s + n, :]
             + cw_ref[2:3] * z_scr[s + 1:s + 1 + n, :] + cb_ref[...])
        ycs.append((bg_ref[0, r, :].astype(F32) * y).astype(BF16))

    ys = [(_dot(yg_ref[0, r, :], wo_ref[0:V_W]) + _dot(yf_ref[0, r, :], wo_ref[V_W:V_W + FOUR_W])
           + _dot(yc, wo_ref[V_W + FOUR_W:])) for r, yc in zip(subs, ycs)]
    xs, hs = [], []
    for r, y in zip(subs, ys):
        x = x_ref[0, r, :]
        if add_pos:
            x = x + pos_ref[r, :]
        x = x + gatemix_ref[...] * _rms(y, gmix_ref[...])
        xs.append(x)
        hs.append((_rms(x, gpre_ref[...]) * (1.0 + scale_ref[...]) + shift_ref[...]).astype(BF16))
    ys = []
    for h in hs:
        y = jnp.zeros((h.shape[0], D_MODEL), F32)
        for lo, hi in _FFN_TILES:
            a = _dot(h, win_ref[:, lo:hi])
            u = _dot(h, win_ref[:, HIDDEN + lo:HIDDEN + hi])
            y = y + _dot((_silu(a) * u).astype(BF16), wout_ref[lo:hi, :])
        ys.append(y)
    for r, x, y in zip(subs, xs, ys):
        o_ref[0, r, :] = x + gate_ref[...] * _rms(y, gpost_ref[...])


def _out_ffn(yg, yf, p, x, pos, lay, i, mod_row, tm):
    b, l, d = x.shape
    add_pos = pos is not None
    tok = lambda bi, ti: (bi, ti, 0)
    halo_per_tile, n_halo = tm // HALO_ROWS, l // HALO_ROWS
    before = lambda bi, ti: (bi, jnp.maximum(ti * halo_per_tile - 1, 0), 0)
    after = lambda bi, ti: (bi, jnp.minimum((ti + 1) * halo_per_tile, n_halo - 1), 0)
    conv_tok = pl.BlockSpec((1, tm, CONV_W), tok)
    in_specs = [pl.BlockSpec((1, tm, V_W), tok), pl.BlockSpec((1, tm, FOUR_W), tok), conv_tok, conv_tok, conv_tok,
                pl.BlockSpec((1, HALO_ROWS, CONV_W), before), pl.BlockSpec((1, HALO_ROWS, CONV_W), before),
                pl.BlockSpec((1, HALO_ROWS, CONV_W), after), pl.BlockSpec((1, HALO_ROWS, CONV_W), after),
                _layer(lay["conv_w"], i), _layer(lay["conv_b"], i),
                _layer(lay["w_out"], i), pl.BlockSpec((1, tm, d), tok)]
    args = [yg, yf, p["bg"], p["cg"], p["uc"], p["cg"], p["uc"], p["cg"], p["uc"],
            lay["conv_w"], lay["conv_b"], lay["w_out"], x]
    if add_pos:
        in_specs.append(pl.BlockSpec((tm, d), lambda bi, ti: (ti, 0)))
        args.append(pos)
    in_specs += [_layer(lay["g_mix_post"], i), _mod_spec(i, 2, mod_row), _layer(lay["g_ffn_pre"], i),
                 _mod_spec(i, 3, mod_row), _mod_spec(i, 4, mod_row), _mod_spec(i, 5, mod_row),
                 _layer(lay["g_ffn_post"], i), _layer(lay["w_ffn_in"], i), _layer(lay["w_ffn_out"], i)]
    args += [lay["g_mix_post"], lay["mod"], lay["g_ffn_pre"], lay["mod"], lay["mod"], lay["mod"],
             lay["g_ffn_post"], lay["w_ffn_in"], lay["w_ffn_out"]]
    return pl.pallas_call(
        functools.partial(_out_ffn_kernel, add_pos=add_pos, n_sub=2 if tm >= 512 else 1),
        grid=(b, l // tm),
        in_specs=in_specs,
        out_specs=pl.BlockSpec((1, tm, d), tok),
        out_shape=jax.ShapeDtypeStruct((b, l, d), F32),
        scratch_shapes=[pltpu.VMEM((tm + 2 * CONV_PAD, CONV_W), F32)],
        compiler_params=_params(("parallel", "parallel"), 56),
        name="out_ffn",
    )(*args)


def _arrange_w_in(w):
    main = jnp.concatenate([w[..., _C_K:_C_LR], w[..., _C_Q:_C_END]], axis=-1)
    lr = jnp.pad(w[..., _C_LR:_C_Q], ((0, 0), (0, 0), (0, LR_PAD - 2 * RANK)))
    return jnp.concatenate([main, lr], axis=-1).astype(BF16)


def _arrange_w_dec(w_f, w_b):
    wd = jnp.zeros((DEPTH, LR_PAD, 2 * K_W), F32)
    wd = wd.at[:, 0:RANK, :K_W].set(w_f).at[:, RANK:2 * RANK, K_W:].set(w_b)
    return wd.astype(BF16)


def kernel(x, c, ctx, c_ctx, w_mod, b_mod, g_mix_pre, g_mix_post, g_ffn_pre, g_ffn_post,
           w_in, w_dec_f, b_dec_f, w_dec_b, b_dec_b, gla_norm_w, conv_w, conv_b, w_out,
           w_ffn_in, w_ffn_out):
    b, n_lat, d = x.shape
    n_ctx = ctx.shape[1]
    assert d == D_MODEL and w_in.shape == (DEPTH, D_MODEL, _C_END)
    assert n_lat % 512 == 0 and n_ctx % CHUNK == 0 and b + 1 <= MOD_ROWS

    pos = jnp.asarray(_pos_embed(n_lat, d))
    tabs = {n_lat: _dft_tables(n_lat, FFT_RADIX), n_ctx: _dft_tables(n_ctx, 1)}

    cvec = jnp.zeros((MOD_ROWS, d), F32).at[:b].set(c).at[b].set(c_ctx)
    ctx_row = b
    lay = {
        "mod": _modulation(cvec, w_mod, b_mod).reshape(DEPTH, MOD_ROWS, 1, N_MOD * d),
        "w_in": _arrange_w_in(w_in),
        "w_dec": _arrange_w_dec(w_dec_f, w_dec_b),
        "b_dec": jnp.concatenate([b_dec_f, b_dec_b], axis=-1).reshape(DEPTH, 1, 2 * K_W),
        "gla_norm_w": gla_norm_w.reshape(DEPTH, 1, V_W),
        "conv_w": conv_w,
        "conv_b": conv_b.reshape(DEPTH, 1, CONV_W),
        "w_out": w_out.astype(BF16),
        "w_ffn_in": w_ffn_in.astype(BF16),
        "w_ffn_out": w_ffn_out.astype(BF16),
        "g_mix_pre": g_mix_pre.reshape(DEPTH, 1, d),
        "g_mix_post": g_mix_post.reshape(DEPTH, 1, d),
        "g_ffn_pre": g_ffn_pre.reshape(DEPTH, 1, d),
        "g_ffn_post": g_ffn_post.reshape(DEPTH, 1, d),
    }

    def mixer_and_ffn(i, xs, pos_s, mod_row, s0, want_state, tm):
        p = _inproj(xs, pos_s, lay, i, mod_row, _OUT_GROUPS_FULL, tm)
        yg, s = _gla(p, lay, i, s0, True, want_state)
        yf = _fourier(p["uf"], tabs[xs.shape[1]])
        return _out_ffn(yg, yf, p, xs, pos_s, lay, i, mod_row, tm), s

    zero_state = jnp.zeros((b, 2, DV, K_W), F32)
    xc = ctx
    for i in range(DEPTH):
        if i == DEPTH - 1:
            p = _inproj(xc, None, lay, i, ctx_row, _OUT_GROUPS_STATE, n_ctx)
            _, s = _gla(p, lay, i, zero_state, False, True)
        else:
            xc, s = mixer_and_ffn(i, xc, None, ctx_row, zero_state, True, n_ctx)
        x, _ = mixer_and_ffn(i, x, pos if i == 0 else None, None, s, False, 512)
    return x
```

```python
import functools

import numpy as np
import jax
import jax.numpy as jnp
from jax import lax
from jax.experimental import pallas as pl
from jax.experimental.pallas import tpu as pltpu

F32 = jnp.float32
BF16 = jnp.bfloat16

D_MODEL = 1024
DEPTH = 2
GRID_W = 64
N_MOD = 6
HEADS = 4
DK = 64
DV = 128
K_W = HEADS * DK
V_W = HEADS * DV
RANK = 16
TAU = 16.0
CHUNK = 64
FOUR_W = 256
FOUR_G = 4
CONV_W = 256
HIDDEN = 2816
EPS = 1e-6
LR_PAD = 128
MOD_ROWS = 16

_C_K, _C_V, _C_LR, _C_Q, _C_R, _C_FOUR, _C_BG, _C_CG, _C_UC, _C_END = (
    0, 256, 768, 800, 1056, 1568, 1824, 2080, 2336, 2592)
_MAIN_W = 2560
_IN_W = _MAIN_W + LR_PAD
_OUT_GROUPS_FULL = (("k", 0, 256), ("v", 256, 768), ("q", 768, 1024), ("r", 1024, 1536),
                    ("uf", 1536, 1792), ("bg", 1792, 2048), ("cg", 2048, 2304),
                    ("uc", 2304, 2560))
_OUT_GROUPS_STATE = (("k", 0, 256), ("v", 256, 768))

V7X_VMEM_BYTES = 64 * 1024 * 1024


def _params(semantics, vmem_mb):
    assert vmem_mb * 1024 * 1024 < V7X_VMEM_BYTES
    return pltpu.CompilerParams(dimension_semantics=semantics,
                                vmem_limit_bytes=vmem_mb * 1024 * 1024)


def _resident(shape):
    zeros = (0,) * len(shape)
    return pl.BlockSpec(shape, lambda *_: zeros, pipeline_mode=pl.Buffered(1))


def _layer(arr, i):
    shape = tuple(arr.shape[1:])
    zeros = (0,) * len(shape)
    return pl.BlockSpec((None,) + shape, lambda *_: (i,) + zeros, pipeline_mode=pl.Buffered(1))


def _mod_spec(i, j, row):
    if row is None:
        return pl.BlockSpec((None, None, 1, D_MODEL), lambda bi, *_: (i, bi, 0, j))
    return pl.BlockSpec((None, None, 1, D_MODEL), lambda *_: (i, row, 0, j))


def _silu(a):
    return a / (1.0 + jnp.exp(-a))


def _rms(x, g):
    return x * lax.rsqrt(jnp.mean(x * x, axis=-1, keepdims=True) + EPS) * g


def _dot(a, b):
    return jnp.dot(a, b, preferred_element_type=F32)


def _dot_nt(a, b):
    return lax.dot_general(a, b, (((1,), (1,)), ((), ())), preferred_element_type=F32)


def _dot_tn(a, b):
    return lax.dot_general(a, b, (((0,), (0,)), ((), ())), preferred_element_type=F32)


def _pos_embed(n_tokens, dim):
    rows = n_tokens // GRID_W
    row = np.repeat(np.arange(rows, dtype=np.float32), GRID_W)
    col = np.tile(np.arange(GRID_W, dtype=np.float32), rows)
    quarter = dim // 4
    freqs = (1.0 / (10000.0 ** (np.arange(quarter, dtype=np.float32) / quarter))).astype(np.float32)

    def enc(p):
        ang = (p[:, None] * freqs[None, :]).astype(np.float32)
        return np.concatenate([np.sin(ang), np.cos(ang)], axis=-1)

    return np.concatenate([enc(row), enc(col)], axis=-1).astype(np.float32)


def _dft_cos_sin(n):
    idx = np.arange(n, dtype=np.int64)
    ang = 2.0 * np.pi * ((idx[:, None] * idx[None, :]) % n).astype(np.float64) / n
    return np.cos(ang) / np.sqrt(n), np.sin(ang) / np.sqrt(n)


def _dft_tables(seq, radix):
    gw = FOUR_W // FOUR_G
    cc, sc = _dft_cos_sin(gw)
    eye = np.eye(FOUR_G)
    chan_tab = jnp.asarray(np.concatenate([np.kron(eye, cc), np.kron(eye, sc)], axis=1), dtype=F32).astype(BF16)
    if radix == 1:
        cl, sl = _dft_cos_sin(seq)
        return chan_tab, jnp.asarray(np.concatenate([cl, -sl], axis=1), dtype=F32).astype(BF16)
    n = seq // radix
    cn, sn = _dft_cos_sin(n)
    sub_tab = np.concatenate([cn, sn], axis=1) * np.sqrt(n / seq)
    ang = 2.0 * np.pi * (np.arange(1, radix)[:, None] * np.arange(n)[None, :]).astype(np.float64) / seq
    twc = np.broadcast_to(np.cos(ang)[:, :, None], (radix - 1, n, FOUR_W))
    tws = np.broadcast_to(np.sin(ang)[:, :, None], (radix - 1, n, FOUR_W))
    return (chan_tab, jnp.asarray(sub_tab, dtype=F32).astype(BF16),
            jnp.asarray(twc, dtype=F32), jnp.asarray(tws, dtype=F32))


def _mod_kernel(c_ref, w_ref, b_ref, o_ref):
    s = _silu(c_ref[...]).astype(BF16)
    o_ref[0] = _dot(s, w_ref[0].astype(BF16)) + b_ref[0]


def _modulation(cvec, w_mod, b_mod):
    tn = 1536
    n = N_MOD * D_MODEL
    return pl.pallas_call(
        _mod_kernel,
        grid=(DEPTH, n // tn),
        in_specs=[pl.BlockSpec((MOD_ROWS, D_MODEL), lambda i, j: (0, 0)),
                  pl.BlockSpec((1, D_MODEL, tn), lambda i, j: (i, 0, j)),
                  pl.BlockSpec((1, 1, tn), lambda i, j: (i, 0, j))],
        out_specs=pl.BlockSpec((1, MOD_ROWS, tn), lambda i, j: (i, 0, j)),
        out_shape=jax.ShapeDtypeStruct((DEPTH, MOD_ROWS, n), F32),
        compiler_params=_params(("arbitrary", "arbitrary"), 40),
        name="modulation",
    )(cvec, w_mod, b_mod.reshape(DEPTH, 1, n))


def _inproj_kernel(*refs, add_pos, groups):
    it = iter(refs)
    x_ref = next(it)
    pos_ref = next(it) if add_pos else None
    g_ref, shift_ref, scale_ref, w_ref, wd_ref, bd_ref = (next(it) for _ in range(6))
    out_refs = list(it)
    lg_hi_ref, lg_lo_ref = out_refs[-2:]
    tm = x_ref.shape[1]
    n_sub = 2 if tm >= 512 else 1
    subs = [slice(i * tm // n_sub, (i + 1) * tm // n_sub) for i in range(n_sub)]
    hs = []
    for r in subs:
        x = x_ref[0, r, :]
        if add_pos:
            x = x + pos_ref[r, :]
        hs.append((_rms(x, g_ref[...]) * (1.0 + scale_ref[...]) + shift_ref[...]).astype(BF16))
    def project(r, h):
        for o_ref, (_, lo, hi_col) in zip(out_refs, groups):
            o_ref[0, r, :] = _dot(h, w_ref[:, lo:hi_col]).astype(BF16)

    low_rank = [_dot(h, w_ref[:, _MAIN_W:_IN_W]).astype(BF16) for h in hs]
    project(subs[0], hs[0])
    zs = [_dot(lr, wd_ref[...]) + bd_ref[...] for lr in low_rank]
    for r, h in list(zip(subs, hs))[1:]:
        project(r, h)
    for r, z in zip(subs, zs):
        logg = (jnp.minimum(z, 0.0) - jnp.log(1.0 + jnp.exp(-jnp.abs(z)))) * (1.0 / TAU)
        hi = logg.astype(BF16)
        lg_hi_ref[0, r, :] = hi
        lg_lo_ref[0, r, :] = (logg - hi.astype(F32)).astype(BF16)


def _inproj(x, pos, lay, i, mod_row, groups, tm):
    b, l, d = x.shape
    add_pos = pos is not None
    tok = lambda bi, ti: (bi, ti, 0)
    in_specs = [pl.BlockSpec((1, tm, d), tok)]
    args = [x]
    if add_pos:
        in_specs.append(pl.BlockSpec((tm, d), lambda bi, ti: (ti, 0)))
        args.append(pos)
    in_specs += [_layer(lay["g_mix_pre"], i), _mod_spec(i, 0, mod_row), _mod_spec(i, 1, mod_row),
                 _layer(lay["w_in"], i), _layer(lay["w_dec"], i), _layer(lay["b_dec"], i)]
    args += [lay["g_mix_pre"], lay["mod"], lay["mod"], lay["w_in"], lay["w_dec"], lay["b_dec"]]
    widths = [hi - lo for _, lo, hi in groups] + [2 * K_W, 2 * K_W]
    names = [name for name, _, _ in groups] + ["lg_hi", "lg_lo"]
    outs = pl.pallas_call(
        functools.partial(_inproj_kernel, add_pos=add_pos, groups=groups),
        grid=(b, l // tm),
        in_specs=in_specs,
        out_specs=[pl.BlockSpec((1, tm, w), tok) for w in widths],
        out_shape=[jax.ShapeDtypeStruct((b, l, w), BF16) for w in widths],
        compiler_params=_params(("parallel", "parallel"), 48),
        name="inproj",
    )(*args)
    return dict(zip(names, outs))


def _gla_kernel(*refs, seq, with_output, with_state):
    it = iter(refs)
    k_ref, v_ref, lg_hi, lg_lo, s0_ref = (next(it) for _ in range(5))
    if with_output:
        q_ref, r_ref, nw_ref = next(it), next(it), next(it)
        y_ref = next(it)
    if with_state:
        s_out_ref = next(it)
    st_ref = next(it)
    ds_scr = next(it)
    dec_scr = next(it)
    if with_output:
        o_scr = next(it)
        qst_scr = next(it)
        gain_scr = next(it)
    n_chunks = seq // CHUNK

    st_ref[...] = s0_ref[0]

    row = lax.broadcasted_iota(jnp.int32, (CHUNK, CHUNK), 0)
    col = lax.broadcasted_iota(jnp.int32, (CHUNK, CHUNK), 1)
    cum = (jnp.where(row >= col, 1.0, 0.0).astype(BF16), jnp.where(row <= col, 1.0, 0.0).astype(BF16))
    lane_head = lax.broadcasted_iota(jnp.int32, (1, K_W), 1) // DK
    head_mask = [lane_head == h for h in range(HEADS)]
    srow = lax.broadcasted_iota(jnp.int32, (CHUNK, HEADS * CHUNK), 0)
    scol = lax.broadcasted_iota(jnp.int32, (CHUNK, HEADS * CHUNK), 1) % CHUNK
    score_mask = (srow >= scol, srow <= scol)
    zero_v = jnp.zeros((CHUNK, DV), BF16)
    last_row = (CHUNK - 1, 0)
    mid_row = (CHUNK // 2 - 1, CHUNK // 2)

    def stack_heads(a):
        zero = jnp.zeros_like(a)
        return jnp.concatenate([jnp.where(head_mask[h], a, zero) for h in range(HEADS)], axis=0)

    def chunk_rows(n):
        return pl.ds(pl.multiple_of(n * CHUNK, CHUNK), CHUNK)

    per_iter = min(8, n_chunks)

    def chunk_local(i, carry):
        loaded = []
        for u in range(per_iter):
            n = i * per_iter + u
            rows = chunk_rows(n)
            k = k_ref[0, rows, :].astype(F32)
            v = v_ref[0, rows, :]
            v_heads = [v[:, h * DV:(h + 1) * DV] for h in range(HEADS)]
            v_rows = jnp.concatenate(v_heads, axis=0)
            v_diag = jnp.concatenate(
                [jnp.concatenate([v_heads[h] if hh == h else zero_v for hh in range(HEADS)], axis=1)
                 for h in range(HEADS)], axis=0) if with_output else None
            v = (v, v_rows, v_diag)
            qs = q_ref[0, rows, :].astype(F32) * (DK ** -0.5) if with_output else None
            lg = [(lg_hi[0, rows, d * K_W:(d + 1) * K_W], lg_lo[0, rows, d * K_W:(d + 1) * K_W]) for d in range(2)]
            loaded.append((n, rows, k, v, qs, lg))
        chains = [(n, rows, k, v, qs, lg[dirn], dirn) for n, rows, k, v, qs, lg in loaded for dirn in range(2)]
        stores = []
        if with_output:
            for n, rows, *_ in loaded:
                stores.append((gain_scr, (rows, slice(None)),
                               _silu(r_ref[0, rows, :].astype(F32)) * nw_ref[...]))
        gs = [_dot(cum[dirn], hi) + _dot(cum[dirn], lo) for *_, (hi, lo), dirn in chains]
        score_list = []
        for (n, rows, k, (v, v_rows, v_diag), qs, _, dirn), g in zip(chains, gs):
            g_last = g[last_row[dirn]:last_row[dirn] + 1]
            stores.append((dec_scr, (dirn, pl.ds(pl.multiple_of(n * 8, 8), 8), slice(None)),
                           jnp.broadcast_to(jnp.exp(g_last), (8, K_W))))
            if with_output:
                g_mid = g[mid_row[dirn]:mid_row[dirn] + 1]
                q_in = qs * jnp.exp(g - g_mid)
                k_in = k * jnp.exp(g_mid - g)
                score_list.append(_dot_nt(q_in.astype(BF16), stack_heads(k_in.astype(BF16))))
                stores.append((qst_scr, (dirn, rows, slice(None)), (q_in * jnp.exp(g_mid)).astype(BF16)))
                k_upd = stack_heads((k_in * jnp.exp(g_last - g_mid)).astype(BF16))
            else:
                k_upd = stack_heads((k * jnp.exp(g_last - g)).astype(BF16))
            stores.append((ds_scr, (dirn, pl.ds(pl.multiple_of(n * DV, DV), DV), slice(None)),
                           _dot_tn(v_rows, k_upd)))
        if with_output:
            for (n, rows, k, (v, v_rows, v_diag), qs, _, dirn), sc in zip(chains, score_list):
                scores = jnp.where(score_mask[dirn], sc, 0.0).astype(BF16)
                stores.append((o_scr, (dirn, rows, slice(None)), _dot(scores, v_diag)))
        for ref, idx, val in stores:
            ref[idx] = val
        return carry

    lax.fori_loop(0, n_chunks // per_iter, chunk_local, 0)

    def scan_step(i, finish):
        steps = ((0, i), (1, n_chunks - 1 - i))
        new_st, inters = [], []
        for dirn, n in steps:
            st = st_ref[dirn]
            if with_output:
                q_st = stack_heads(qst_scr[dirn, chunk_rows(n), :])
                inters.append(_dot_nt(q_st, st.astype(BF16)))
            dec = dec_scr[dirn, pl.ds(pl.multiple_of(n * 8, 8), 8), :]
            new_st.append(dec[0:1] * st + ds_scr[dirn, pl.ds(pl.multiple_of(n * DV, DV), DV), :])
        for (dirn, n), st in zip(steps, new_st):
            st_ref[dirn] = st
        if with_output:
            for (dirn, n), inter in zip(steps, inters):
                rows = chunk_rows(n)
                for h in range(HEADS):
                    vs = slice(h * DV, (h + 1) * DV)
                    o = o_scr[dirn, rows, vs] + inter[h * CHUNK:(h + 1) * CHUNK]
                    if finish:
                        o = o + o_scr[1 - dirn, rows, vs]
                        y_ref[0, rows, vs] = (_rms(o, gain_scr[rows, vs])).astype(BF16)
                    else:
                        o_scr[dirn, rows, vs] = o

    half = n_chunks // 2
    unroll = min(8, half)

    def scan_first(i, carry):
        scan_step(i, False)
        return carry

    def scan_second(i, carry):
        scan_step(i, True)
        return carry

    lax.fori_loop(0, half, scan_first, 0, unroll=unroll)
    lax.fori_loop(half, n_chunks, scan_second, 0, unroll=unroll)

    if with_state:
        s_out_ref[0] = st_ref[...]


def _gla(p, lay, i, s0, with_output, with_state):
    b, l, _ = p["k"].shape
    seq3 = lambda bi: (bi, 0, 0)
    in_specs = [pl.BlockSpec((1, l, K_W), seq3), pl.BlockSpec((1, l, V_W), seq3),
                pl.BlockSpec((1, l, 2 * K_W), seq3), pl.BlockSpec((1, l, 2 * K_W), seq3),
                pl.BlockSpec((1, 2, DV, K_W), lambda bi: (bi, 0, 0, 0))]
    args = [p["k"], p["v"], p["lg_hi"], p["lg_lo"], s0]
    out_specs, out_shape = [], []
    scratch = [pltpu.VMEM((2, DV, K_W), F32),
               pltpu.VMEM((2, l // CHUNK * DV, K_W), F32), pltpu.VMEM((2, l // CHUNK * 8, K_W), F32)]
    if with_output:
        in_specs += [pl.BlockSpec((1, l, K_W), seq3), pl.BlockSpec((1, l, V_W), seq3),
                     _layer(lay["gla_norm_w"], i)]
        args += [p["q"], p["r"], lay["gla_norm_w"]]
        out_specs.append(pl.BlockSpec((1, l, V_W), seq3))
        out_shape.append(jax.ShapeDtypeStruct((b, l, V_W), BF16))
        scratch += [pltpu.VMEM((2, l, V_W), F32), pltpu.VMEM((2, l, K_W), BF16), pltpu.VMEM((l, V_W), F32)]
    if with_state:
        out_specs.append(pl.BlockSpec((1, 2, DV, K_W), lambda bi: (bi, 0, 0, 0)))
        out_shape.append(jax.ShapeDtypeStruct((b, 2, DV, K_W), F32))
    outs = pl.pallas_call(
        functools.partial(_gla_kernel, seq=l, with_output=with_output, with_state=with_state),
        grid=(b,),
        in_specs=in_specs,
        out_specs=out_specs,
        out_shape=out_shape,
        scratch_shapes=scratch,
        compiler_params=_params(("parallel",), 56),
        name="gla",
    )(*args)
    outs = list(outs)
    y = outs.pop(0) if with_output else None
    s = outs.pop(0) if with_state else None
    return y, s


def _fourier_kernel(uf_ref, chan_ref, pos_ref, yf_ref, ab_scr, *, seq):
    ab = _dot(uf_ref[0], chan_ref[...])
    ab_scr[0:seq, :] = ab[:, :FOUR_W].astype(BF16)
    ab_scr[seq:2 * seq, :] = ab[:, FOUR_W:].astype(BF16)
    rb = min(seq, 512)
    for i in range(seq // rb):
        yf_ref[0, i * rb:(i + 1) * rb, :] = _dot(pos_ref[i * rb:(i + 1) * rb, :], ab_scr[...]).astype(BF16)


FFT_RADIX = 8


def _fourier_fft_kernel(uf_ref, chan_ref, sub_ref, twc_ref, tws_ref, yf_ref, uf_scr, *, seq):
    n = seq // FFT_RADIX
    lanes = uf_scr.shape[2]
    n_lane_tiles = FOUR_W // lanes
    for j in range(n_lane_tiles):
        uf_scr[j] = uf_ref[0, :, j * lanes:(j + 1) * lanes].astype(F32)

    def tokens(r):
        parts = [uf_scr[j, pl.ds(r, n, stride=FFT_RADIX), :] for j in range(n_lane_tiles)]
        return jnp.concatenate(parts, axis=1).astype(BF16)

    ab = [_dot(tokens(r), chan_ref[...]) for r in range(FFT_RADIX)]
    pq = []
    for ab_r in ab:
        a, b = ab_r[:, :FOUR_W].astype(BF16), ab_r[:, FOUR_W:].astype(BF16)
        w = jnp.concatenate([jnp.concatenate([a, b], axis=1), jnp.concatenate([-b, a], axis=1)], axis=0)
        pq.append(_dot(sub_ref[...], w))
    re, im = [pq[0][:, :FOUR_W]], [pq[0][:, FOUR_W:]]
    for r in range(1, FFT_RADIX):
        p, q = pq[r][:, :FOUR_W], pq[r][:, FOUR_W:]
        c, s = twc_ref[r - 1], tws_ref[r - 1]
        re.append(c * p - s * q)
        im.append(s * p + c * q)

    def quarter(z0, z1, z2, z3):
        s02 = (z0[0] + z2[0], z0[1] + z2[1])
        d02 = (z0[0] - z2[0], z0[1] - z2[1])
        s13 = (z1[0] + z3[0], z1[1] + z3[1])
        d13 = (z1[0] - z3[0], z1[1] - z3[1])
        return ((s02[0] + s13[0], s02[1] + s13[1]), (d02[0] - d13[1], d02[1] + d13[0]),
                (s02[0] - s13[0], s02[1] - s13[1]), (d02[0] + d13[1], d02[1] - d13[0]))

    zs = list(zip(re, im))
    even = quarter(zs[0], zs[2], zs[4], zs[6])
    odd = quarter(zs[1], zs[3], zs[5], zs[7])
    half = 0.5 ** 0.5
    odd_re = (odd[0][0], (odd[1][0] - odd[1][1]) * half, -odd[2][1], -(odd[3][0] + odd[3][1]) * half)
    for q in range(4):
        yf_ref[0, q * n:(q + 1) * n, :] = (even[q][0] + odd_re[q]).astype(BF16)
        yf_ref[0, (q + 4) * n:(q + 5) * n, :] = (even[q][0] - odd_re[q]).astype(BF16)


def _fourier(uf, tabs):
    b, l, _ = uf.shape
    tok = pl.BlockSpec((1, l, FOUR_W), lambda bi: (bi, 0, 0))
    if len(tabs) == 4:
        body, name = _fourier_fft_kernel, "fourier_fft"
        scratch = pltpu.VMEM((FOUR_W // 128, l, 128), F32)
    else:
        body, name = _fourier_kernel, "fourier"
        scratch = pltpu.VMEM((2 * l, FOUR_W), BF16)
    return pl.pallas_call(
        functools.partial(body, seq=l),
        grid=(b,),
        in_specs=[tok] + [_resident(t.shape) for t in tabs],
        out_specs=tok,
        out_shape=jax.ShapeDtypeStruct((b, l, FOUR_W), BF16),
        scratch_shapes=[scratch],
        compiler_params=_params(("parallel",), 48),
        name=name,
    )(uf, *tabs)


_FFN_TILES = ((0, 1536), (1536, HIDDEN))


CONV_PAD = 8
HALO_ROWS = 16


def _out_ffn_kernel(*refs, add_pos, n_sub):
    it = iter(refs)
    yg_ref, yf_ref, bg_ref, cg_ref, uc_ref = (next(it) for _ in range(5))
    cg_prev_ref, uc_prev_ref, cg_next_ref, uc_next_ref, cw_ref, cb_ref = (next(it) for _ in range(6))
    wo_ref, x_ref = next(it), next(it)
    pos_ref = next(it) if add_pos else None
    gmix_ref, gatemix_ref, gpre_ref, shift_ref, scale_ref, gate_ref, gpost_ref = (next(it) for _ in range(7))
    win_ref, wout_ref, o_ref, z_scr = next(it), next(it), next(it), next(it)
    tm = x_ref.shape[1]
    subs = [slice(i * tm // n_sub, (i + 1) * tm // n_sub) for i in range(n_sub)]

    t, nt = pl.program_id(1), pl.num_programs(1)
    z_before = (cg_prev_ref[0].astype(F32) * uc_prev_ref[0].astype(F32))[HALO_ROWS - CONV_PAD:]
    z_after = (cg_next_ref[0].astype(F32) * uc_next_ref[0].astype(F32))[:CONV_PAD]
    z_scr[0:CONV_PAD, :] = jnp.where(t > 0, z_before, 0.0)
    z_scr[CONV_PAD + tm:, :] = jnp.where(t < nt - 1, z_after, 0.0)
    z_scr[CONV_PAD:CONV_PAD + tm, :] = cg_ref[0].astype(F32) * uc_ref[0].astype(F32)
    ycs = []
    for r in subs:
        s, n = CONV_PAD + r.start, r.stop - r.start
        y = (cw_ref[0:1] * z_scr[s - 1:s - 1 + n, :] + cw_ref[1:2] * z_scr[s:s + n, :]
             + cw_ref[2:3] * z_scr[s + 1:s + 1 + n, :] + cb_ref[...])
        ycs.append((bg_ref[0, r, :].astype(F32) * y).astype(BF16))

    ys = [(_dot(yg_ref[0, r, :], wo_ref[0:V_W]) + _dot(yf_ref[0, r, :], wo_ref[V_W:V_W + FOUR_W])
           + _dot(yc, wo_ref[V_W + FOUR_W:])) for r, yc in zip(subs, ycs)]
    xs, hs = [], []
    for r, y in zip(subs, ys):
        x = x_ref[0, r, :]
        if add_pos:
            x = x + pos_ref[r, :]
        x = x + gatemix_ref[...] * _rms(y, gmix_ref[...])
        xs.append(x)
        hs.append((_rms(x, gpre_ref[...]) * (1.0 + scale_ref[...]) + shift_ref[...]).astype(BF16))
    ys = []
    for h in hs:
        y = jnp.zeros((h.shape[0], D_MODEL), F32)
        for lo, hi in _FFN_TILES:
            a = _dot(h, win_ref[:, lo:hi])
            u = _dot(h, win_ref[:, HIDDEN + lo:HIDDEN + hi])
            y = y + _dot((_silu(a) * u).astype(BF16), wout_ref[lo:hi, :])
        ys.append(y)
    for r, x, y in zip(subs, xs, ys):
        o_ref[0, r, :] = x + gate_ref[...] * _rms(y, gpost_ref[...])


def _out_ffn(yg, yf, p, x, pos, lay, i, mod_row, tm):
    b, l, d = x.shape
    add_pos = pos is not None
    tok = lambda bi, ti: (bi, ti, 0)
    halo_per_tile, n_halo = tm // HALO_ROWS, l // HALO_ROWS
    before = lambda bi, ti: (bi, jnp.maximum(ti * halo_per_tile - 1, 0), 0)
    after = lambda bi, ti: (bi, jnp.minimum((ti + 1) * halo_per_tile, n_halo - 1), 0)
    conv_tok = pl.BlockSpec((1, tm, CONV_W), tok)
    in_specs = [pl.BlockSpec((1, tm, V_W), tok), pl.BlockSpec((1, tm, FOUR_W), tok), conv_tok, conv_tok, conv_tok,
                pl.BlockSpec((1, HALO_ROWS, CONV_W), before), pl.BlockSpec((1, HALO_ROWS, CONV_W), before),
                pl.BlockSpec((1, HALO_ROWS, CONV_W), after), pl.BlockSpec((1, HALO_ROWS, CONV_W), after),
                _layer(lay["conv_w"], i), _layer(lay["conv_b"], i),
                _layer(lay["w_out"], i), pl.BlockSpec((1, tm, d), tok)]
    args = [yg, yf, p["bg"], p["cg"], p["uc"], p["cg"], p["uc"], p["cg"], p["uc"],
            lay["conv_w"], lay["conv_b"], lay["w_out"], x]
    if add_pos:
        in_specs.append(pl.BlockSpec((tm, d), lambda bi, ti: (ti, 0)))
        args.append(pos)
    in_specs += [_layer(lay["g_mix_post"], i), _mod_spec(i, 2, mod_row), _layer(lay["g_ffn_pre"], i),
                 _mod_spec(i, 3, mod_row), _mod_spec(i, 4, mod_row), _mod_spec(i, 5, mod_row),
                 _layer(lay["g_ffn_post"], i), _layer(lay["w_ffn_in"], i), _layer(lay["w_ffn_out"], i)]
    args += [lay["g_mix_post"], lay["mod"], lay["g_ffn_pre"], lay["mod"], lay["mod"], lay["mod"],
             lay["g_ffn_post"], lay["w_ffn_in"], lay["w_ffn_out"]]
    return pl.pallas_call(
        functools.partial(_out_ffn_kernel, add_pos=add_pos, n_sub=2 if tm >= 512 else 1),
        grid=(b, l // tm),
        in_specs=in_specs,
        out_specs=pl.BlockSpec((1, tm, d), tok),
        out_shape=jax.ShapeDtypeStruct((b, l, d), F32),
        scratch_shapes=[pltpu.VMEM((tm + 2 * CONV_PAD, CONV_W), F32)],
        compiler_params=_params(("parallel", "parallel"), 56),
        name="out_ffn",
    )(*args)


def _arrange_w_in(w):
    main = jnp.concatenate([w[..., _C_K:_C_LR], w[..., _C_Q:_C_END]], axis=-1)
    lr = jnp.pad(w[..., _C_LR:_C_Q], ((0, 0), (0, 0), (0, LR_PAD - 2 * RANK)))
    return jnp.concatenate([main, lr], axis=-1).astype(BF16)


def _arrange_w_dec(w_f, w_b):
    wd = jnp.zeros((DEPTH, LR_PAD, 2 * K_W), F32)
    wd = wd.at[:, 0:RANK, :K_W].set(w_f).at[:, RANK:2 * RANK, K_W:].set(w_b)
    return wd.astype(BF16)


def kernel(x, c, ctx, c_ctx, w_mod, b_mod, g_mix_pre, g_mix_post, g_ffn_pre, g_ffn_post,
           w_in, w_dec_f, b_dec_f, w_dec_b, b_dec_b, gla_norm_w, conv_w, conv_b, w_out,
           w_ffn_in, w_ffn_out):
    b, n_lat, d = x.shape
    n_ctx = ctx.shape[1]
    assert d == D_MODEL and w_in.shape == (DEPTH, D_MODEL, _C_END)
    assert n_lat % 512 == 0 and n_ctx % CHUNK == 0 and b + 1 <= MOD_ROWS

    pos = jnp.asarray(_pos_embed(n_lat, d))
    tabs = {n_lat: _dft_tables(n_lat, FFT_RADIX), n_ctx: _dft_tables(n_ctx, 1)}

    cvec = jnp.zeros((MOD_ROWS, d), F32).at[:b].set(c).at[b].set(c_ctx)
    ctx_row = b
    lay = {
        "mod": _modulation(cvec, w_mod, b_mod).reshape(DEPTH, MOD_ROWS, 1, N_MOD * d),
        "w_in": _arrange_w_in(w_in),
        "w_dec": _arrange_w_dec(w_dec_f, w_dec_b),
        "b_dec": jnp.concatenate([b_dec_f, b_dec_b], axis=-1).reshape(DEPTH, 1, 2 * K_W),
        "gla_norm_w": gla_norm_w.reshape(DEPTH, 1, V_W),
        "conv_w": conv_w,
        "conv_b": conv_b.reshape(DEPTH, 1, CONV_W),
        "w_out": w_out.astype(BF16),
        "w_ffn_in": w_ffn_in.astype(BF16),
        "w_ffn_out": w_ffn_out.astype(BF16),
        "g_mix_pre": g_mix_pre.reshape(DEPTH, 1, d),
        "g_mix_post": g_mix_post.reshape(DEPTH, 1, d),
        "g_ffn_pre": g_ffn_pre.reshape(DEPTH, 1, d),
        "g_ffn_post": g_ffn_post.reshape(DEPTH, 1, d),
    }

    def mixer_and_ffn(i, xs, pos_s, mod_row, s0, want_state, tm):
        p = _inproj(xs, pos_s, lay, i, mod_row, _OUT_GROUPS_FULL, min(2 * tm, xs.shape[1]))
        yg, s = _gla(p, lay, i, s0, True, want_state)
        yf = _fourier(p["uf"], tabs[xs.shape[1]])
        return _out_ffn(yg, yf, p, xs, pos_s, lay, i, mod_row, tm), s

    zero_state = jnp.zeros((b, 2, DV, K_W), F32)
    xc = ctx
    for i in range(DEPTH):
        if i == DEPTH - 1:
            p = _inproj(xc, None, lay, i, ctx_row, _OUT_GROUPS_STATE, n_ctx)
            _, s = _gla(p, lay, i, zero_state, False, True)
        else:
            xc, s = mixer_and_ffn(i, xc, None, ctx_row, zero_state, True, n_ctx)
        x, _ = mixer_and_ffn(i, x, pos if i == 0 else None, None, s, False, 512)
    return x
```

```python
import functools

import numpy as np
import jax
import jax.numpy as jnp
from jax import lax
from jax.experimental import pallas as pl
from jax.experimental.pallas import tpu as pltpu

F32 = jnp.float32
BF16 = jnp.bfloat16

D_MODEL = 1024
DEPTH = 2
GRID_W = 64
N_MOD = 6
HEADS = 4
DK = 64
DV = 128
K_W = HEADS * DK
V_W = HEADS * DV
RANK = 16
TAU = 16.0
CHUNK = 64
FOUR_W = 256
FOUR_G = 4
CONV_W = 256
HIDDEN = 2816
EPS = 1e-6
LR_PAD = 128
MOD_ROWS = 16

_C_K, _C_V, _C_LR, _C_Q, _C_R, _C_FOUR, _C_BG, _C_CG, _C_UC, _C_END = (
    0, 256, 768, 800, 1056, 1568, 1824, 2080, 2336, 2592)
_MAIN_W = 2560
_IN_W = _MAIN_W + LR_PAD
_OUT_GROUPS_FULL = (("k", 0, 256), ("v", 256, 768), ("q", 768, 1024), ("r", 1024, 1536),
                    ("uf", 1536, 1792), ("bg", 1792, 2048), ("cg", 2048, 2304),
                    ("uc", 2304, 2560))
_OUT_GROUPS_STATE = (("k", 0, 256), ("v", 256, 768))

V7X_VMEM_BYTES = 64 * 1024 * 1024


def _params(semantics, vmem_mb):
    assert vmem_mb * 1024 * 1024 < V7X_VMEM_BYTES
    return pltpu.CompilerParams(dimension_semantics=semantics,
                                vmem_limit_bytes=vmem_mb * 1024 * 1024)


def _resident(shape):
    zeros = (0,) * len(shape)
    return pl.BlockSpec(shape, lambda *_: zeros, pipeline_mode=pl.Buffered(1))


def _layer(arr, i):
    shape = tuple(arr.shape[1:])
    zeros = (0,) * len(shape)
    return pl.BlockSpec((None,) + shape, lambda *_: (i,) + zeros, pipeline_mode=pl.Buffered(1))


def _mod_spec(i, j, row):
    if row is None:
        return pl.BlockSpec((None, None, 1, D_MODEL), lambda bi, *_: (i, bi, 0, j))
    return pl.BlockSpec((None, None, 1, D_MODEL), lambda *_: (i, row, 0, j))


def _silu(a):
    return a / (1.0 + jnp.exp(-a))


def _rms(x, g):
    return x * lax.rsqrt(jnp.mean(x * x, axis=-1, keepdims=True) + EPS) * g


def _dot(a, b):
    return jnp.dot(a, b, preferred_element_type=F32)


def _dot_nt(a, b):
    return lax.dot_general(a, b, (((1,), (1,)), ((), ())), preferred_element_type=F32)


def _dot_tn(a, b):
    return lax.dot_general(a, b, (((0,), (0,)), ((), ())), preferred_element_type=F32)


def _pos_embed(n_tokens, dim):
    rows = n_tokens // GRID_W
    row = np.repeat(np.arange(rows, dtype=np.float32), GRID_W)
    col = np.tile(np.arange(GRID_W, dtype=np.float32), rows)
    quarter = dim // 4
    freqs = (1.0 / (10000.0 ** (np.arange(quarter, dtype=np.float32) / quarter))).astype(np.float32)

    def enc(p):
        ang = (p[:, None] * freqs[None, :]).astype(np.float32)
        return np.concatenate([np.sin(ang), np.cos(ang)], axis=-1)

    return np.concatenate([enc(row), enc(col)], axis=-1).astype(np.float32)


def _dft_cos_sin(n):
    idx = np.arange(n, dtype=np.int64)
    ang = 2.0 * np.pi * ((idx[:, None] * idx[None, :]) % n).astype(np.float64) / n
    return np.cos(ang) / np.sqrt(n), np.sin(ang) / np.sqrt(n)


def _dft_tables(seq, radix):
    gw = FOUR_W // FOUR_G
    cc, sc = _dft_cos_sin(gw)
    eye = np.eye(FOUR_G)
    chan_tab = jnp.asarray(np.concatenate([np.kron(eye, cc), np.kron(eye, sc)], axis=1), dtype=F32).astype(BF16)
    if radix == 1:
        cl, sl = _dft_cos_sin(seq)
        return chan_tab, jnp.asarray(np.concatenate([cl, -sl], axis=1), dtype=F32).astype(BF16)
    n = seq // radix
    cn, sn = _dft_cos_sin(n)
    sub_tab = np.concatenate([cn, sn], axis=1) * np.sqrt(n / seq)
    ang = 2.0 * np.pi * (np.arange(1, radix)[:, None] * np.arange(n)[None, :]).astype(np.float64) / seq
    twc = np.broadcast_to(np.cos(ang)[:, :, None], (radix - 1, n, FOUR_W))
    tws = np.broadcast_to(np.sin(ang)[:, :, None], (radix - 1, n, FOUR_W))
    return (chan_tab, jnp.asarray(sub_tab, dtype=F32).astype(BF16),
            jnp.asarray(twc, dtype=F32), jnp.asarray(tws, dtype=F32))


def _mod_kernel(c_ref, w_ref, b_ref, o_ref):
    s = _silu(c_ref[...]).astype(BF16)
    o_ref[0] = _dot(s, w_ref[0].astype(BF16)) + b_ref[0]


def _modulation(cvec, w_mod, b_mod):
    tn = 1536
    n = N_MOD * D_MODEL
    return pl.pallas_call(
        _mod_kernel,
        grid=(DEPTH, n // tn),
        in_specs=[pl.BlockSpec((MOD_ROWS, D_MODEL), lambda i, j: (0, 0)),
                  pl.BlockSpec((1, D_MODEL, tn), lambda i, j: (i, 0, j)),
                  pl.BlockSpec((1, 1, tn), lambda i, j: (i, 0, j))],
        out_specs=pl.BlockSpec((1, MOD_ROWS, tn), lambda i, j: (i, 0, j)),
        out_shape=jax.ShapeDtypeStruct((DEPTH, MOD_ROWS, n), F32),
        compiler_params=_params(("arbitrary", "arbitrary"), 40),
        name="modulation",
    )(cvec, w_mod, b_mod.reshape(DEPTH, 1, n))


def _inproj_kernel(*refs, add_pos, groups):
    it = iter(refs)
    x_ref = next(it)
    pos_ref = next(it) if add_pos else None
    g_ref, shift_ref, scale_ref, w_ref, wd_ref, bd_ref = (next(it) for _ in range(6))
    out_refs = list(it)
    lg_hi_ref, lg_lo_ref = out_refs[-2:]
    tm = x_ref.shape[1]
    n_sub = 2 if tm >= 512 else 1
    subs = [slice(i * tm // n_sub, (i + 1) * tm // n_sub) for i in range(n_sub)]
    hs = []
    for r in subs:
        x = x_ref[0, r, :]
        if add_pos:
            x = x + pos_ref[r, :]
        hs.append((_rms(x, g_ref[...]) * (1.0 + scale_ref[...]) + shift_ref[...]).astype(BF16))
    def project(r, h):
        for o_ref, (_, lo, hi_col) in zip(out_refs, groups):
            o_ref[0, r, :] = _dot(h, w_ref[:, lo:hi_col]).astype(BF16)

    low_rank = [_dot(h, w_ref[:, _MAIN_W:_IN_W]).astype(BF16) for h in hs]
    project(subs[0], hs[0])
    zs = [_dot(lr, wd_ref[...]) + bd_ref[...] for lr in low_rank]
    for r, h in list(zip(subs, hs))[1:]:
        project(r, h)
    for r, z in zip(subs, zs):
        logg = (jnp.minimum(z, 0.0) - jnp.log(1.0 + jnp.exp(-jnp.abs(z)))) * (1.0 / TAU)
        hi = logg.astype(BF16)
        lg_hi_ref[0, r, :] = hi
        lg_lo_ref[0, r, :] = (logg - hi.astype(F32)).astype(BF16)


def _inproj(x, pos, lay, i, mod_row, groups, tm):
    b, l, d = x.shape
    add_pos = pos is not None
    tok = lambda bi, ti: (bi, ti, 0)
    in_specs = [pl.BlockSpec((1, tm, d), tok)]
    args = [x]
    if add_pos:
        in_specs.append(pl.BlockSpec((tm, d), lambda bi, ti: (ti, 0)))
        args.append(pos)
    in_specs += [_layer(lay["g_mix_pre"], i), _mod_spec(i, 0, mod_row), _mod_spec(i, 1, mod_row),
                 _layer(lay["w_in"], i), _layer(lay["w_dec"], i), _layer(lay["b_dec"], i)]
    args += [lay["g_mix_pre"], lay["mod"], lay["mod"], lay["w_in"], lay["w_dec"], lay["b_dec"]]
    widths = [hi - lo for _, lo, hi in groups] + [2 * K_W, 2 * K_W]
    names = [name for name, _, _ in groups] + ["lg_hi", "lg_lo"]
    outs = pl.pallas_call(
        functools.partial(_inproj_kernel, add_pos=add_pos, groups=groups),
        grid=(b, l // tm),
        in_specs=in_specs,
        out_specs=[pl.BlockSpec((1, tm, w), tok) for w in widths],
        out_shape=[jax.ShapeDtypeStruct((b, l, w), BF16) for w in widths],
        compiler_params=_params(("parallel", "parallel"), 48),
        name="inproj",
    )(*args)
    return dict(zip(names, outs))


def _gla_kernel(*refs, seq, with_output, with_state):
    it = iter(refs)
    k_ref, v_ref, lg_hi, lg_lo, s0_ref = (next(it) for _ in range(5))
    if with_output:
        q_ref, r_ref, nw_ref = next(it), next(it), next(it)
        y_ref = next(it)
    if with_state:
        s_out_ref = next(it)
    st_ref = next(it)
    ds_scr = next(it)
    dec_scr = next(it)
    if with_output:
        o_scr = next(it)
        qst_scr = next(it)
        gain_scr = next(it)
    n_chunks = seq // CHUNK

    st_ref[...] = s0_ref[0]

    row = lax.broadcasted_iota(jnp.int32, (CHUNK, CHUNK), 0)
    col = lax.broadcasted_iota(jnp.int32, (CHUNK, CHUNK), 1)
    cum = (jnp.where(row >= col, 1.0, 0.0).astype(BF16), jnp.where(row <= col, 1.0, 0.0).astype(BF16))
    lane_head = lax.broadcasted_iota(jnp.int32, (1, K_W), 1) // DK
    head_mask = [lane_head == h for h in range(HEADS)]
    srow = lax.broadcasted_iota(jnp.int32, (CHUNK, HEADS * CHUNK), 0)
    scol = lax.broadcasted_iota(jnp.int32, (CHUNK, HEADS * CHUNK), 1) % CHUNK
    score_mask = (srow >= scol, srow <= scol)
    zero_v = jnp.zeros((CHUNK, DV), BF16)
    last_row = (CHUNK - 1, 0)
    mid_row = (CHUNK // 2 - 1, CHUNK // 2)

    def stack_heads(a):
        zero = jnp.zeros_like(a)
        return jnp.concatenate([jnp.where(head_mask[h], a, zero) for h in range(HEADS)], axis=0)

    def chunk_rows(n):
        return pl.ds(pl.multiple_of(n * CHUNK, CHUNK), CHUNK)

    per_iter = min(8, n_chunks)

    def chunk_local(i, carry):
        loaded = []
        for u in range(per_iter):
            n = i * per_iter + u
            rows = chunk_rows(n)
            k = k_ref[0, rows, :].astype(F32)
            v = v_ref[0, rows, :]
            v_heads = [v[:, h * DV:(h + 1) * DV] for h in range(HEADS)]
            v_rows = jnp.concatenate(v_heads, axis=0)
            v_diag = jnp.concatenate(
                [jnp.concatenate([v_heads[h] if hh == h else zero_v for hh in range(HEADS)], axis=1)
                 for h in range(HEADS)], axis=0) if with_output else None
            v = (v, v_rows, v_diag)
            qs = q_ref[0, rows, :].astype(F32) * (DK ** -0.5) if with_output else None
            lg = [(lg_hi[0, rows, d * K_W:(d + 1) * K_W], lg_lo[0, rows, d * K_W:(d + 1) * K_W]) for d in range(2)]
            loaded.append((n, rows, k, v, qs, lg))
        chains = [(n, rows, k, v, qs, lg[dirn], dirn) for n, rows, k, v, qs, lg in loaded for dirn in range(2)]
        stores = []
        if with_output:
            for n, rows, *_ in loaded:
                stores.append((gain_scr, (rows, slice(None)),
                               _silu(r_ref[0, rows, :].astype(F32)) * nw_ref[...]))
        gs = [_dot(cum[dirn], hi) + _dot(cum[dirn], lo) for *_, (hi, lo), dirn in chains]
        score_list = []
        for (n, rows, k, (v, v_rows, v_diag), qs, _, dirn), g in zip(chains, gs):
            g_last = g[last_row[dirn]:last_row[dirn] + 1]
            stores.append((dec_scr, (dirn, pl.ds(pl.multiple_of(n * 8, 8), 8), slice(None)),
                           jnp.broadcast_to(jnp.exp(g_last), (8, K_W))))
            if with_output:
                g_mid = g[mid_row[dirn]:mid_row[dirn] + 1]
                q_in = qs * jnp.exp(g - g_mid)
                k_in = k * jnp.exp(g_mid - g)
                score_list.append(_dot_nt(q_in.astype(BF16), stack_heads(k_in.astype(BF16))))
                stores.append((qst_scr, (dirn, rows, slice(None)), (q_in * jnp.exp(g_mid)).astype(BF16)))
                k_upd = stack_heads((k_in * jnp.exp(g_last - g_mid)).astype(BF16))
            else:
                k_upd = stack_heads((k * jnp.exp(g_last - g)).astype(BF16))
            stores.append((ds_scr, (dirn, pl.ds(pl.multiple_of(n * DV, DV), DV), slice(None)),
                           _dot_tn(v_rows, k_upd)))
        if with_output:
            for (n, rows, k, (v, v_rows, v_diag), qs, _, dirn), sc in zip(chains, score_list):
                scores = jnp.where(score_mask[dirn], sc, 0.0).astype(BF16)
                stores.append((o_scr, (dirn, rows, slice(None)), _dot(scores, v_diag)))
        for ref, idx, val in stores:
            ref[idx] = val
        return carry

    lax.fori_loop(0, n_chunks // per_iter, chunk_local, 0)

    def scan_step(i, finish):
        steps = ((0, i), (1, n_chunks - 1 - i))
        new_st, inters = [], []
        for dirn, n in steps:
            st = st_ref[dirn]
            if with_output:
                q_st = stack_heads(qst_scr[dirn, chunk_rows(n), :])
                inters.append(_dot_nt(q_st, st.astype(BF16)))
            dec = dec_scr[dirn, pl.ds(pl.multiple_of(n * 8, 8), 8), :]
            new_st.append(dec[0:1] * st + ds_scr[dirn, pl.ds(pl.multiple_of(n * DV, DV), DV), :])
        for (dirn, n), st in zip(steps, new_st):
            st_ref[dirn] = st
        if with_output:
            for (dirn, n), inter in zip(steps, inters):
                rows = chunk_rows(n)
                for h in range(HEADS):
                    vs = slice(h * DV, (h + 1) * DV)
                    o = o_scr[dirn, rows, vs] + inter[h * CHUNK:(h + 1) * CHUNK]
                    if finish:
                        o = o + o_scr[1 - dirn, rows, vs]
                        y_ref[0, rows, vs] = (_rms(o, gain_scr[rows, vs])).astype(BF16)
                    else:
                        o_scr[dirn, rows, vs] = o

    half = n_chunks // 2
    unroll = min(8, half)

    def scan_first(i, carry):
        scan_step(i, False)
        return carry

    def scan_second(i, carry):
        scan_step(i, True)
        return carry

    lax.fori_loop(0, half, scan_first, 0, unroll=unroll)
    lax.fori_loop(half, n_chunks, scan_second, 0, unroll=unroll)

    if with_state:
        s_out_ref[0] = st_ref[...]


def _gla(p, lay, i, s0, with_output, with_state):
    b, l, _ = p["k"].shape
    seq3 = lambda bi: (bi, 0, 0)
    in_specs = [pl.BlockSpec((1, l, K_W), seq3), pl.BlockSpec((1, l, V_W), seq3),
                pl.BlockSpec((1, l, 2 * K_W), seq3), pl.BlockSpec((1, l, 2 * K_W), seq3),
                pl.BlockSpec((1, 2, DV, K_W), lambda bi: (bi, 0, 0, 0))]
    args = [p["k"], p["v"], p["lg_hi"], p["lg_lo"], s0]
    out_specs, out_shape = [], []
    scratch = [pltpu.VMEM((2, DV, K_W), F32),
               pltpu.VMEM((2, l // CHUNK * DV, K_W), F32), pltpu.VMEM((2, l // CHUNK * 8, K_W), F32)]
    if with_output:
        in_specs += [pl.BlockSpec((1, l, K_W), seq3), pl.BlockSpec((1, l, V_W), seq3),
                     _layer(lay["gla_norm_w"], i)]
        args += [p["q"], p["r"], lay["gla_norm_w"]]
        out_specs.append(pl.BlockSpec((1, l, V_W), seq3))
        out_shape.append(jax.ShapeDtypeStruct((b, l, V_W), BF16))
        scratch += [pltpu.VMEM((2, l, V_W), F32), pltpu.VMEM((2, l, K_W), BF16), pltpu.VMEM((l, V_W), F32)]
    if with_state:
        out_specs.append(pl.BlockSpec((1, 2, DV, K_W), lambda bi: (bi, 0, 0, 0)))
        out_shape.append(jax.ShapeDtypeStruct((b, 2, DV, K_W), F32))
    outs = pl.pallas_call(
        functools.partial(_gla_kernel, seq=l, with_output=with_output, with_state=with_state),
        grid=(b,),
        in_specs=in_specs,
        out_specs=out_specs,
        out_shape=out_shape,
        scratch_shapes=scratch,
        compiler_params=_params(("parallel",), 56),
        name="gla",
    )(*args)
    outs = list(outs)
    y = outs.pop(0) if with_output else None
    s = outs.pop(0) if with_state else None
    return y, s


def _fourier_kernel(uf_ref, chan_ref, pos_ref, yf_ref, ab_scr, *, seq):
    ab = _dot(uf_ref[0], chan_ref[...])
    ab_scr[0:seq, :] = ab[:, :FOUR_W].astype(BF16)
    ab_scr[seq:2 * seq, :] = ab[:, FOUR_W:].astype(BF16)
    rb = min(seq, 512)
    for i in range(seq // rb):
        yf_ref[0, i * rb:(i + 1) * rb, :] = _dot(pos_ref[i * rb:(i + 1) * rb, :], ab_scr[...]).astype(BF16)


FFT_RADIX = 8


def _fourier_fft_kernel(uf_ref, chan_ref, sub_ref, twc_ref, tws_ref, yf_ref, uf_scr, *, seq):
    n = seq // FFT_RADIX
    lanes = uf_scr.shape[2]
    n_lane_tiles = FOUR_W // lanes
    for j in range(n_lane_tiles):
        uf_scr[j] = uf_ref[0, :, j * lanes:(j + 1) * lanes].astype(F32)

    def tokens(r):
        parts = [uf_scr[j, pl.ds(r, n, stride=FFT_RADIX), :] for j in range(n_lane_tiles)]
        return jnp.concatenate(parts, axis=1).astype(BF16)

    ab = [_dot(tokens(r), chan_ref[...]) for r in range(FFT_RADIX)]
    pq = []
    for ab_r in ab:
        a, b = ab_r[:, :FOUR_W].astype(BF16), ab_r[:, FOUR_W:].astype(BF16)
        w = jnp.concatenate([jnp.concatenate([a, b], axis=1), jnp.concatenate([-b, a], axis=1)], axis=0)
        pq.append(_dot(sub_ref[...], w))
    re, im = [pq[0][:, :FOUR_W]], [pq[0][:, FOUR_W:]]
    for r in range(1, FFT_RADIX):
        p, q = pq[r][:, :FOUR_W], pq[r][:, FOUR_W:]
        c, s = twc_ref[r - 1], tws_ref[r - 1]
        re.append(c * p - s * q)
        im.append(s * p + c * q)

    def quarter(z0, z1, z2, z3):
        s02 = (z0[0] + z2[0], z0[1] + z2[1])
        d02 = (z0[0] - z2[0], z0[1] - z2[1])
        s13 = (z1[0] + z3[0], z1[1] + z3[1])
        d13 = (z1[0] - z3[0], z1[1] - z3[1])
        return ((s02[0] + s13[0], s02[1] + s13[1]), (d02[0] - d13[1], d02[1] + d13[0]),
                (s02[0] - s13[0], s02[1] - s13[1]), (d02[0] + d13[1], d02[1] - d13[0]))

    zs = list(zip(re, im))
    even = quarter(zs[0], zs[2], zs[4], zs[6])
    odd = quarter(zs[1], zs[3], zs[5], zs[7])
    half = 0.5 ** 0.5
    odd_re = (odd[0][0], (odd[1][0] - odd[1][1]) * half, -odd[2][1], -(odd[3][0] + odd[3][1]) * half)
    for q in range(4):
        yf_ref[0, q * n:(q + 1) * n, :] = (even[q][0] + odd_re[q]).astype(BF16)
        yf_ref[0, (q + 4) * n:(q + 5) * n, :] = (even[q][0] - odd_re[q]).astype(BF16)


def _fourier(uf, tabs):
    b, l, _ = uf.shape
    tok = pl.BlockSpec((1, l, FOUR_W), lambda bi: (bi, 0, 0))
    if len(tabs) == 4:
        body, name = _fourier_fft_kernel, "fourier_fft"
        scratch = pltpu.VMEM((FOUR_W // 128, l, 128), F32)
    else:
        body, name = _fourier_kernel, "fourier"
        scratch = pltpu.VMEM((2 * l, FOUR_W), BF16)
    return pl.pallas_call(
        functools.partial(body, seq=l),
        grid=(b,),
        in_specs=[tok] + [_resident(t.shape) for t in tabs],
        out_specs=tok,
        out_shape=jax.ShapeDtypeStruct((b, l, FOUR_W), BF16),
        scratch_shapes=[scratch],
        compiler_params=_params(("parallel",), 48),
        name=name,
    )(uf, *tabs)


_FFN_TILES = ((0, 1536), (1536, HIDDEN))


CONV_PAD = 8
HALO_ROWS = 16


def _out_ffn_kernel(*refs, add_pos, n_sub):
    it = iter(refs)
    yg_ref, yf_ref, bg_ref, cg_ref, uc_ref = (next(it) for _ in range(5))
    cg_prev_ref, uc_prev_ref, cg_next_ref, uc_next_ref, cw_ref, cb_ref = (next(it) for _ in range(6))
    wo_ref, x_ref = next(it), next(it)
    pos_ref = next(it) if add_pos else None
    gmix_ref, gatemix_ref, gpre_ref, shift_ref, scale_ref, gate_ref, gpost_ref = (next(it) for _ in range(7))
    win_ref, wout_ref, o_ref, z_scr = next(it), next(it), next(it), next(it)
    tm = x_ref.shape[1]
    subs = [slice(i * tm // n_sub, (i + 1) * tm // n_sub) for i in range(n_sub)]

    t, nt = pl.program_id(1), pl.num_programs(1)
    z_before = (cg_prev_ref[0].astype(F32) * uc_prev_ref[0].astype(F32))[HALO_ROWS - CONV_PAD:]
    z_after = (cg_next_ref[0].astype(F32) * uc_next_ref[0].astype(F32))[:CONV_PAD]
    z_scr[0:CONV_PAD, :] = jnp.where(t > 0, z_before, 0.0)
    z_scr[CONV_PAD + tm:, :] = jnp.where(t < nt - 1, z_after, 0.0)
    z_scr[CONV_PAD:CONV_PAD + tm, :] = cg_ref[0].astype(F32) * uc_ref[0].astype(F32)
    ycs = []
    for r in subs:
        s, n = CONV_PAD + r.start, r.stop - r.start
        y = (cw_ref[0:1] * z_scr[s - 1:s - 1 + n, :] + cw_ref[1:2] * z_scr[s:s + n, :]
             + cw_ref[2:3] * z_scr[s + 1:s + 1 + n, :] + cb_ref[...])
        ycs.append((bg_ref[0, r, :].astype(F32) * y).astype(BF16))

    ys = [(_dot(yg_ref[0, r, :], wo_ref[0:V_W]) + _dot(yf_ref[0, r, :], wo_ref[V_W:V_W + FOUR_W])
           + _dot(yc, wo_ref[V_W + FOUR_W:])) for r, yc in zip(subs, ycs)]
    xs, hs = [], []
    for r, y in zip(subs, ys):
        x = x_ref[0, r, :]
        if add_pos:
            x = x + pos_ref[r, :]
        x = x + gatemix_ref[...] * _rms(y, gmix_ref[...])
        xs.append(x)
        hs.append((_rms(x, gpre_ref[...]) * (1.0 + scale_ref[...]) + shift_ref[...]).astype(BF16))
    ys = []
    for h in hs:
        y = jnp.zeros((h.shape[0], D_MODEL), F32)
        for lo, hi in _FFN_TILES:
            a = _dot(h, win_ref[:, lo:hi])
            u = _dot(h, win_ref[:, HIDDEN + lo:HIDDEN + hi])
            y = y + _dot((_silu(a) * u).astype(BF16), wout_ref[lo:hi, :])
        ys.append(y)
    for r, x, y in zip(subs, xs, ys):
        o_ref[0, r, :] = x + gate_ref[...] * _rms(y, gpost_ref[...])


def _out_ffn(yg, yf, p, x, pos, lay, i, mod_row, tm):
    b, l, d = x.shape
    add_pos = pos is not None
    tok = lambda bi, ti: (bi, ti, 0)
    halo_per_tile, n_halo = tm // HALO_ROWS, l // HALO_ROWS
    before = lambda bi, ti: (bi, jnp.maximum(ti * halo_per_tile - 1, 0), 0)
    after = lambda bi, ti: (bi, jnp.minimum((ti + 1) * halo_per_tile, n_halo - 1), 0)
    conv_tok = pl.BlockSpec((1, tm, CONV_W), tok)
    in_specs = [pl.BlockSpec((1, tm, V_W), tok), pl.BlockSpec((1, tm, FOUR_W), tok), conv_tok, conv_tok, conv_tok,
                pl.BlockSpec((1, HALO_ROWS, CONV_W), before), pl.BlockSpec((1, HALO_ROWS, CONV_W), before),
                pl.BlockSpec((1, HALO_ROWS, CONV_W), after), pl.BlockSpec((1, HALO_ROWS, CONV_W), after),
                _layer(lay["conv_w"], i), _layer(lay["conv_b"], i),
                _layer(lay["w_out"], i), pl.BlockSpec((1, tm, d), tok)]
    args = [yg, yf, p["bg"], p["cg"], p["uc"], p["cg"], p["uc"], p["cg"], p["uc"],
            lay["conv_w"], lay["conv_b"], lay["w_out"], x]
    if add_pos:
        in_specs.append(pl.BlockSpec((tm, d), lambda bi, ti: (ti, 0)))
        args.append(pos)
    in_specs += [_layer(lay["g_mix_post"], i), _mod_spec(i, 2, mod_row), _layer(lay["g_ffn_pre"], i),
                 _mod_spec(i, 3, mod_row), _mod_spec(i, 4, mod_row), _mod_spec(i, 5, mod_row),
                 _layer(lay["g_ffn_post"], i), _layer(lay["w_ffn_in"], i), _layer(lay["w_ffn_out"], i)]
    args += [lay["g_mix_post"], lay["mod"], lay["g_ffn_pre"], lay["mod"], lay["mod"], lay["mod"],
             lay["g_ffn_post"], lay["w_ffn_in"], lay["w_ffn_out"]]
    return pl.pallas_call(
        functools.partial(_out_ffn_kernel, add_pos=add_pos, n_sub=2 if tm >= 512 else 1),
        grid=(b, l // tm),
        in_specs=in_specs,
        out_specs=pl.BlockSpec((1, tm, d), tok),
        out_shape=jax.ShapeDtypeStruct((b, l, d), F32),
        scratch_shapes=[pltpu.VMEM((tm + 2 * CONV_PAD, CONV_W), F32)],
        compiler_params=_params(("parallel", "parallel"), 56),
        name="out_ffn",
    )(*args)


def _arrange_w_in(w):
    w = w.astype(BF16)
    pad = jnp.zeros(w.shape[:-1] + (LR_PAD - 2 * RANK,), BF16)
    return jnp.concatenate([w[..., _C_K:_C_LR], w[..., _C_Q:_C_END], w[..., _C_LR:_C_Q], pad], axis=-1)


def _arrange_w_dec(w_f, w_b):
    wd = jnp.zeros((DEPTH, LR_PAD, 2 * K_W), F32)
    wd = wd.at[:, 0:RANK, :K_W].set(w_f).at[:, RANK:2 * RANK, K_W:].set(w_b)
    return wd.astype(BF16)


def kernel(x, c, ctx, c_ctx, w_mod, b_mod, g_mix_pre, g_mix_post, g_ffn_pre, g_ffn_post,
           w_in, w_dec_f, b_dec_f, w_dec_b, b_dec_b, gla_norm_w, conv_w, conv_b, w_out,
           w_ffn_in, w_ffn_out):
    b, n_lat, d = x.shape
    n_ctx = ctx.shape[1]
    assert d == D_MODEL and w_in.shape == (DEPTH, D_MODEL, _C_END)
    assert n_lat % 512 == 0 and n_ctx % CHUNK == 0 and b + 1 <= MOD_ROWS

    pos = jnp.asarray(_pos_embed(n_lat, d))
    tabs = {n_lat: _dft_tables(n_lat, FFT_RADIX), n_ctx: _dft_tables(n_ctx, 1)}

    cvec = jnp.zeros((MOD_ROWS, d), F32).at[:b].set(c).at[b].set(c_ctx)
    ctx_row = b
    lay = {
        "mod": _modulation(cvec, w_mod, b_mod).reshape(DEPTH, MOD_ROWS, 1, N_MOD * d),
        "w_in": _arrange_w_in(w_in),
        "w_dec": _arrange_w_dec(w_dec_f, w_dec_b),
        "b_dec": jnp.concatenate([b_dec_f, b_dec_b], axis=-1).reshape(DEPTH, 1, 2 * K_W),
        "gla_norm_w": gla_norm_w.reshape(DEPTH, 1, V_W),
        "conv_w": conv_w,
        "conv_b": conv_b.reshape(DEPTH, 1, CONV_W),
        "w_out": w_out.astype(BF16),
        "w_ffn_in": w_ffn_in.astype(BF16),
        "w_ffn_out": w_ffn_out.astype(BF16),
        "g_mix_pre": g_mix_pre.reshape(DEPTH, 1, d),
        "g_mix_post": g_mix_post.reshape(DEPTH, 1, d),
        "g_ffn_pre": g_ffn_pre.reshape(DEPTH, 1, d),
        "g_ffn_post": g_ffn_post.reshape(DEPTH, 1, d),
    }

    def mixer_and_ffn(i, xs, pos_s, mod_row, s0, want_state, tm):
        p = _inproj(xs, pos_s, lay, i, mod_row, _OUT_GROUPS_FULL, min(2 * tm, xs.shape[1]))
        yg, s = _gla(p, lay, i, s0, True, want_state)
        yf = _fourier(p["uf"], tabs[xs.shape[1]])
        return _out_ffn(yg, yf, p, xs, pos_s, lay, i, mod_row, tm), s

    zero_state = jnp.zeros((b, 2, DV, K_W), F32)
    xc = ctx
    for i in range(DEPTH):
        if i == DEPTH - 1:
            p = _inproj(xc, None, lay, i, ctx_row, _OUT_GROUPS_STATE, n_ctx)
            _, s = _gla(p, lay, i, zero_state, False, True)
        else:
            xc, s = mixer_and_ffn(i, xc, None, ctx_row, zero_state, True, n_ctx)
        x, _ = mixer_and_ffn(i, x, pos if i == 0 else None, None, s, False, 512)
    return x
```

```python
import functools

import numpy as np
import jax
import jax.numpy as jnp
from jax import lax
from jax.experimental import pallas as pl
from jax.experimental.pallas import tpu as pltpu

F32 = jnp.float32
BF16 = jnp.bfloat16

D_MODEL = 1024
DEPTH = 2
GRID_W = 64
N_MOD = 6
HEADS = 4
DK = 64
DV = 128
K_W = HEADS * DK
V_W = HEADS * DV
RANK = 16
TAU = 16.0
CHUNK = 64
FOUR_W = 256
FOUR_G = 4
CONV_W = 256
HIDDEN = 2816
EPS = 1e-6
LR_PAD = 128
MOD_ROWS = 16

_C_K, _C_V, _C_LR, _C_Q, _C_R, _C_FOUR, _C_BG, _C_CG, _C_UC, _C_END = (
    0, 256, 768, 800, 1056, 1568, 1824, 2080, 2336, 2592)
_MAIN_W = 2560
_IN_W = _MAIN_W + LR_PAD
_OUT_GROUPS_FULL = (("k", 0, 256), ("v", 256, 768), ("q", 768, 1024), ("r", 1024, 1536),
                    ("uf", 1536, 1792), ("bg", 1792, 2048), ("cg", 2048, 2304),
                    ("uc", 2304, 2560))
_OUT_GROUPS_STATE = (("k", 0, 256), ("v", 256, 768))

V7X_VMEM_BYTES = 64 * 1024 * 1024


def _params(semantics, vmem_mb):
    assert vmem_mb * 1024 * 1024 < V7X_VMEM_BYTES
    return pltpu.CompilerParams(dimension_semantics=semantics,
                                vmem_limit_bytes=vmem_mb * 1024 * 1024)


def _resident(shape):
    zeros = (0,) * len(shape)
    return pl.BlockSpec(shape, lambda *_: zeros, pipeline_mode=pl.Buffered(1))


def _layer(arr, i):
    shape = tuple(arr.shape[1:])
    zeros = (0,) * len(shape)
    return pl.BlockSpec((None,) + shape, lambda *_: (i,) + zeros, pipeline_mode=pl.Buffered(1))


def _mod_spec(i, j, row):
    if row is None:
        return pl.BlockSpec((None, None, 1, D_MODEL), lambda bi, *_: (i, bi, 0, j))
    return pl.BlockSpec((None, None, 1, D_MODEL), lambda *_: (i, row, 0, j))


def _silu(a):
    return a / (1.0 + jnp.exp(-a))


def _rms(x, g):
    return x * lax.rsqrt(jnp.mean(x * x, axis=-1, keepdims=True) + EPS) * g


def _dot(a, b):
    return jnp.dot(a, b, preferred_element_type=F32)


def _dot_nt(a, b):
    return lax.dot_general(a, b, (((1,), (1,)), ((), ())), preferred_element_type=F32)


def _dot_tn(a, b):
    return lax.dot_general(a, b, (((0,), (0,)), ((), ())), preferred_element_type=F32)


def _pos_embed(n_tokens, dim):
    rows = n_tokens // GRID_W
    row = np.repeat(np.arange(rows, dtype=np.float32), GRID_W)
    col = np.tile(np.arange(GRID_W, dtype=np.float32), rows)
    quarter = dim // 4
    freqs = (1.0 / (10000.0 ** (np.arange(quarter, dtype=np.float32) / quarter))).astype(np.float32)

    def enc(p):
        ang = (p[:, None] * freqs[None, :]).astype(np.float32)
        return np.concatenate([np.sin(ang), np.cos(ang)], axis=-1)

    return np.concatenate([enc(row), enc(col)], axis=-1).astype(np.float32)


def _dft_cos_sin(n):
    idx = np.arange(n, dtype=np.int64)
    ang = 2.0 * np.pi * ((idx[:, None] * idx[None, :]) % n).astype(np.float64) / n
    return np.cos(ang) / np.sqrt(n), np.sin(ang) / np.sqrt(n)


def _dft_tables(seq, radix):
    gw = FOUR_W // FOUR_G
    cc, sc = _dft_cos_sin(gw)
    eye = np.eye(FOUR_G)
    chan_tab = jnp.asarray(np.concatenate([np.kron(eye, cc), np.kron(eye, sc)], axis=1), dtype=F32).astype(BF16)
    if radix == 1:
        cl, sl = _dft_cos_sin(seq)
        return chan_tab, jnp.asarray(np.concatenate([cl, -sl], axis=1), dtype=F32).astype(BF16)
    n = seq // radix
    cn, sn = _dft_cos_sin(n)
    sub_tab = np.concatenate([cn, sn], axis=1) * np.sqrt(n / seq)
    ang = 2.0 * np.pi * (np.arange(1, radix)[:, None] * np.arange(n)[None, :]).astype(np.float64) / seq
    twc = np.broadcast_to(np.cos(ang)[:, :, None], (radix - 1, n, FOUR_W))
    tws = np.broadcast_to(np.sin(ang)[:, :, None], (radix - 1, n, FOUR_W))
    return (chan_tab, jnp.asarray(sub_tab, dtype=F32).astype(BF16),
            jnp.asarray(twc, dtype=F32), jnp.asarray(tws, dtype=F32))


def _mod_kernel(c_ref, w_ref, b_ref, o_ref):
    s = _silu(c_ref[...]).astype(BF16)
    o_ref[0] = _dot(s, w_ref[0].astype(BF16)) + b_ref[0]


def _modulation(cvec, w_mod, b_mod):
    tn = 1536
    n = N_MOD * D_MODEL
    return pl.pallas_call(
        _mod_kernel,
        grid=(DEPTH, n // tn),
        in_specs=[pl.BlockSpec((MOD_ROWS, D_MODEL), lambda i, j: (0, 0)),
                  pl.BlockSpec((1, D_MODEL, tn), lambda i, j: (i, 0, j)),
                  pl.BlockSpec((1, 1, tn), lambda i, j: (i, 0, j))],
        out_specs=pl.BlockSpec((1, MOD_ROWS, tn), lambda i, j: (i, 0, j)),
        out_shape=jax.ShapeDtypeStruct((DEPTH, MOD_ROWS, n), F32),
        compiler_params=_params(("arbitrary", "arbitrary"), 40),
        name="modulation",
    )(cvec, w_mod, b_mod.reshape(DEPTH, 1, n))


def _inproj_kernel(*refs, add_pos, groups):
    it = iter(refs)
    x_ref = next(it)
    pos_ref = next(it) if add_pos else None
    g_ref, shift_ref, scale_ref, w_ref, wd_ref, bd_ref = (next(it) for _ in range(6))
    out_refs = list(it)
    lg_hi_ref, lg_lo_ref = out_refs[-2:]
    tm = x_ref.shape[1]
    n_sub = 2 if tm >= 512 else 1
    subs = [slice(i * tm // n_sub, (i + 1) * tm // n_sub) for i in range(n_sub)]
    hs = []
    for r in subs:
        x = x_ref[0, r, :]
        if add_pos:
            x = x + pos_ref[r, :]
        hs.append((_rms(x, g_ref[...]) * (1.0 + scale_ref[...]) + shift_ref[...]).astype(BF16))
    def project(r, h):
        for o_ref, (_, lo, hi_col) in zip(out_refs, groups):
            o_ref[0, r, :] = _dot(h, w_ref[:, lo:hi_col]).astype(BF16)

    low_rank = [_dot(h, w_ref[:, _MAIN_W:_IN_W]).astype(BF16) for h in hs]
    project(subs[0], hs[0])
    zs = [_dot(lr, wd_ref[...]) + bd_ref[...] for lr in low_rank]
    for r, h in list(zip(subs, hs))[1:]:
        project(r, h)
    for r, z in zip(subs, zs):
        logg = (jnp.minimum(z, 0.0) - jnp.log(1.0 + jnp.exp(-jnp.abs(z)))) * (1.0 / TAU)
        hi = logg.astype(BF16)
        lg_hi_ref[0, r, :] = hi
        lg_lo_ref[0, r, :] = (logg - hi.astype(F32)).astype(BF16)


def _inproj(x, pos, lay, i, mod_row, groups, tm):
    b, l, d = x.shape
    add_pos = pos is not None
    tok = lambda bi, ti: (bi, ti, 0)
    in_specs = [pl.BlockSpec((1, tm, d), tok)]
    args = [x]
    if add_pos:
        in_specs.append(pl.BlockSpec((tm, d), lambda bi, ti: (ti, 0)))
        args.append(pos)
    in_specs += [_layer(lay["g_mix_pre"], i), _mod_spec(i, 0, mod_row), _mod_spec(i, 1, mod_row),
                 _layer(lay["w_in"], i), _layer(lay["w_dec"], i), _layer(lay["b_dec"], i)]
    args += [lay["g_mix_pre"], lay["mod"], lay["mod"], lay["w_in"], lay["w_dec"], lay["b_dec"]]
    widths = [hi - lo for _, lo, hi in groups] + [2 * K_W, 2 * K_W]
    names = [name for name, _, _ in groups] + ["lg_hi", "lg_lo"]
    outs = pl.pallas_call(
        functools.partial(_inproj_kernel, add_pos=add_pos, groups=groups),
        grid=(b, l // tm),
        in_specs=in_specs,
        out_specs=[pl.BlockSpec((1, tm, w), tok) for w in widths],
        out_shape=[jax.ShapeDtypeStruct((b, l, w), BF16) for w in widths],
        compiler_params=_params(("parallel", "parallel"), 48),
        name="inproj",
    )(*args)
    return dict(zip(names, outs))


def _gla_kernel(*refs, seq, with_output, with_state):
    it = iter(refs)
    k_ref, v_ref, lg_hi, lg_lo, s0_ref = (next(it) for _ in range(5))
    if with_output:
        q_ref, r_ref, nw_ref = next(it), next(it), next(it)
        y_ref = next(it)
    if with_state:
        s_out_ref = next(it)
    st_ref = next(it)
    ds_scr = next(it)
    dec_scr = next(it)
    if with_output:
        o_scr = next(it)
        qst_scr = next(it)
        gain_scr = next(it)
    n_chunks = seq // CHUNK

    st_ref[...] = s0_ref[0]

    row = lax.broadcasted_iota(jnp.int32, (CHUNK, CHUNK), 0)
    col = lax.broadcasted_iota(jnp.int32, (CHUNK, CHUNK), 1)
    cum = (jnp.where(row >= col, 1.0, 0.0).astype(BF16), jnp.where(row <= col, 1.0, 0.0).astype(BF16))
    lane_head = lax.broadcasted_iota(jnp.int32, (1, K_W), 1) // DK
    head_mask = [lane_head == h for h in range(HEADS)]
    srow = lax.broadcasted_iota(jnp.int32, (CHUNK, HEADS * CHUNK), 0)
    scol = lax.broadcasted_iota(jnp.int32, (CHUNK, HEADS * CHUNK), 1) % CHUNK
    score_mask = (srow >= scol, srow <= scol)
    zero_v = jnp.zeros((CHUNK, DV), BF16)
    last_row = (CHUNK - 1, 0)
    mid_row = (CHUNK // 2 - 1, CHUNK // 2)

    def stack_heads(a):
        zero = jnp.zeros_like(a)
        return jnp.concatenate([jnp.where(head_mask[h], a, zero) for h in range(HEADS)], axis=0)

    def chunk_rows(n):
        return pl.ds(pl.multiple_of(n * CHUNK, CHUNK), CHUNK)

    per_iter = min(8, n_chunks)

    def chunk_local(i, carry):
        loaded = []
        for u in range(per_iter):
            n = i * per_iter + u
            rows = chunk_rows(n)
            k = k_ref[0, rows, :].astype(F32)
            v = v_ref[0, rows, :]
            v_heads = [v[:, h * DV:(h + 1) * DV] for h in range(HEADS)]
            v_rows = jnp.concatenate(v_heads, axis=0)
            v_diag = jnp.concatenate(
                [jnp.concatenate([v_heads[h] if hh == h else zero_v for hh in range(HEADS)], axis=1)
                 for h in range(HEADS)], axis=0) if with_output else None
            v = (v, v_rows, v_diag)
            qs = q_ref[0, rows, :].astype(F32) * (DK ** -0.5) if with_output else None
            lg = [(lg_hi[0, rows, d * K_W:(d + 1) * K_W], lg_lo[0, rows, d * K_W:(d + 1) * K_W]) for d in range(2)]
            loaded.append((n, rows, k, v, qs, lg))
        chains = [(n, rows, k, v, qs, lg[dirn], dirn) for n, rows, k, v, qs, lg in loaded for dirn in range(2)]
        stores = []
        if with_output:
            for n, rows, *_ in loaded:
                stores.append((gain_scr, (rows, slice(None)),
                               _silu(r_ref[0, rows, :].astype(F32)) * nw_ref[...]))
        gs = [_dot(cum[dirn], hi) + _dot(cum[dirn], lo) for *_, (hi, lo), dirn in chains]
        score_list = []
        for (n, rows, k, (v, v_rows, v_diag), qs, _, dirn), g in zip(chains, gs):
            g_last = g[last_row[dirn]:last_row[dirn] + 1]
            stores.append((dec_scr, (dirn, pl.ds(pl.multiple_of(n * 8, 8), 8), slice(None)),
                           jnp.broadcast_to(jnp.exp(g_last), (8, K_W))))
            if with_output:
                g_mid = g[mid_row[dirn]:mid_row[dirn] + 1]
                q_in = qs * jnp.exp(g - g_mid)
                k_in = k * jnp.exp(g_mid - g)
                score_list.append(_dot_nt(q_in.astype(BF16), stack_heads(k_in.astype(BF16))))
                stores.append((qst_scr, (dirn, rows, slice(None)), (q_in * jnp.exp(g_mid)).astype(BF16)))
                k_upd = stack_heads((k_in * jnp.exp(g_last - g_mid)).astype(BF16))
            else:
                k_upd = stack_heads((k * jnp.exp(g_last - g)).astype(BF16))
            stores.append((ds_scr, (dirn, pl.ds(pl.multiple_of(n * DV, DV), DV), slice(None)),
                           _dot_tn(v_rows, k_upd)))
        if with_output:
            for (n, rows, k, (v, v_rows, v_diag), qs, _, dirn), sc in zip(chains, score_list):
                scores = jnp.where(score_mask[dirn], sc, 0.0).astype(BF16)
                stores.append((o_scr, (dirn, rows, slice(None)), _dot(scores, v_diag)))
        for ref, idx, val in stores:
            ref[idx] = val
        return carry

    lax.fori_loop(0, n_chunks // per_iter, chunk_local, 0)

    def scan_step(i, finish):
        steps = ((0, i), (1, n_chunks - 1 - i))
        new_st, inters = [], []
        for dirn, n in steps:
            st = st_ref[dirn]
            if with_output:
                q_st = stack_heads(qst_scr[dirn, chunk_rows(n), :])
                inters.append(_dot_nt(q_st, st.astype(BF16)))
            dec = dec_scr[dirn, pl.ds(pl.multiple_of(n * 8, 8), 8), :]
            new_st.append(dec[0:1] * st + ds_scr[dirn, pl.ds(pl.multiple_of(n * DV, DV), DV), :])
        for (dirn, n), st in zip(steps, new_st):
            st_ref[dirn] = st
        if with_output:
            for (dirn, n), inter in zip(steps, inters):
                rows = chunk_rows(n)
                for h in range(HEADS):
                    vs = slice(h * DV, (h + 1) * DV)
                    o = o_scr[dirn, rows, vs] + inter[h * CHUNK:(h + 1) * CHUNK]
                    if finish:
                        o = o + o_scr[1 - dirn, rows, vs]
                        y_ref[0, rows, vs] = (_rms(o, gain_scr[rows, vs])).astype(BF16)
                    else:
                        o_scr[dirn, rows, vs] = o

    half = n_chunks // 2
    unroll = min(16, half)

    def scan_first(i, carry):
        scan_step(i, False)
        return carry

    def scan_second(i, carry):
        scan_step(i, True)
        return carry

    lax.fori_loop(0, half, scan_first, 0, unroll=unroll)
    lax.fori_loop(half, n_chunks, scan_second, 0, unroll=unroll)

    if with_state:
        s_out_ref[0] = st_ref[...]


def _gla(p, lay, i, s0, with_output, with_state):
    b, l, _ = p["k"].shape
    seq3 = lambda bi: (bi, 0, 0)
    in_specs = [pl.BlockSpec((1, l, K_W), seq3), pl.BlockSpec((1, l, V_W), seq3),
                pl.BlockSpec((1, l, 2 * K_W), seq3), pl.BlockSpec((1, l, 2 * K_W), seq3),
                pl.BlockSpec((1, 2, DV, K_W), lambda bi: (bi, 0, 0, 0))]
    args = [p["k"], p["v"], p["lg_hi"], p["lg_lo"], s0]
    out_specs, out_shape = [], []
    scratch = [pltpu.VMEM((2, DV, K_W), F32),
               pltpu.VMEM((2, l // CHUNK * DV, K_W), F32), pltpu.VMEM((2, l // CHUNK * 8, K_W), F32)]
    if with_output:
        in_specs += [pl.BlockSpec((1, l, K_W), seq3), pl.BlockSpec((1, l, V_W), seq3),
                     _layer(lay["gla_norm_w"], i)]
        args += [p["q"], p["r"], lay["gla_norm_w"]]
        out_specs.append(pl.BlockSpec((1, l, V_W), seq3))
        out_shape.append(jax.ShapeDtypeStruct((b, l, V_W), BF16))
        scratch += [pltpu.VMEM((2, l, V_W), F32), pltpu.VMEM((2, l, K_W), BF16), pltpu.VMEM((l, V_W), F32)]
    if with_state:
        out_specs.append(pl.BlockSpec((1, 2, DV, K_W), lambda bi: (bi, 0, 0, 0)))
        out_shape.append(jax.ShapeDtypeStruct((b, 2, DV, K_W), F32))
    outs = pl.pallas_call(
        functools.partial(_gla_kernel, seq=l, with_output=with_output, with_state=with_state),
        grid=(b,),
        in_specs=in_specs,
        out_specs=out_specs,
        out_shape=out_shape,
        scratch_shapes=scratch,
        compiler_params=_params(("parallel",), 56),
        name="gla",
    )(*args)
    outs = list(outs)
    y = outs.pop(0) if with_output else None
    s = outs.pop(0) if with_state else None
    return y, s


def _fourier_kernel(uf_ref, chan_ref, pos_ref, yf_ref, ab_scr, *, seq):
    ab = _dot(uf_ref[0], chan_ref[...])
    ab_scr[0:seq, :] = ab[:, :FOUR_W].astype(BF16)
    ab_scr[seq:2 * seq, :] = ab[:, FOUR_W:].astype(BF16)
    rb = min(seq, 512)
    for i in range(seq // rb):
        yf_ref[0, i * rb:(i + 1) * rb, :] = _dot(pos_ref[i * rb:(i + 1) * rb, :], ab_scr[...]).astype(BF16)


FFT_RADIX = 8


def _fourier_fft_kernel(uf_ref, chan_ref, sub_ref, twc_ref, tws_ref, yf_ref, uf_scr, *, seq):
    n = seq // FFT_RADIX
    lanes = uf_scr.shape[2]
    n_lane_tiles = FOUR_W // lanes
    for j in range(n_lane_tiles):
        uf_scr[j] = uf_ref[0, :, j * lanes:(j + 1) * lanes].astype(F32)

    def tokens(r):
        parts = [uf_scr[j, pl.ds(r, n, stride=FFT_RADIX), :] for j in range(n_lane_tiles)]
        return jnp.concatenate(parts, axis=1).astype(BF16)

    ab = [_dot(tokens(r), chan_ref[...]) for r in range(FFT_RADIX)]
    pq = []
    for ab_r in ab:
        a, b = ab_r[:, :FOUR_W].astype(BF16), ab_r[:, FOUR_W:].astype(BF16)
        w = jnp.concatenate([jnp.concatenate([a, b], axis=1), jnp.concatenate([-b, a], axis=1)], axis=0)
        pq.append(_dot(sub_ref[...], w))
    re, im = [pq[0][:, :FOUR_W]], [pq[0][:, FOUR_W:]]
    for r in range(1, FFT_RADIX):
        p, q = pq[r][:, :FOUR_W], pq[r][:, FOUR_W:]
        c, s = twc_ref[r - 1], tws_ref[r - 1]
        re.append(c * p - s * q)
        im.append(s * p + c * q)

    def quarter(z0, z1, z2, z3):
        s02 = (z0[0] + z2[0], z0[1] + z2[1])
        d02 = (z0[0] - z2[0], z0[1] - z2[1])
        s13 = (z1[0] + z3[0], z1[1] + z3[1])
        d13 = (z1[0] - z3[0], z1[1] - z3[1])
        return ((s02[0] + s13[0], s02[1] + s13[1]), (d02[0] - d13[1], d02[1] + d13[0]),
                (s02[0] - s13[0], s02[1] - s13[1]), (d02[0] + d13[1], d02[1] - d13[0]))

    zs = list(zip(re, im))
    even = quarter(zs[0], zs[2], zs[4], zs[6])
    odd = quarter(zs[1], zs[3], zs[5], zs[7])
    half = 0.5 ** 0.5
    odd_re = (odd[0][0], (odd[1][0] - odd[1][1]) * half, -odd[2][1], -(odd[3][0] + odd[3][1]) * half)
    for q in range(4):
        yf_ref[0, q * n:(q + 1) * n, :] = (even[q][0] + odd_re[q]).astype(BF16)
        yf_ref[0, (q + 4) * n:(q + 5) * n, :] = (even[q][0] - odd_re[q]).astype(BF16)


def _fourier(uf, tabs):
    b, l, _ = uf.shape
    tok = pl.BlockSpec((1, l, FOUR_W), lambda bi: (bi, 0, 0))
    if len(tabs) == 4:
        body, name = _fourier_fft_kernel, "fourier_fft"
        scratch = pltpu.VMEM((FOUR_W // 128, l, 128), F32)
    else:
        body, name = _fourier_kernel, "fourier"
        scratch = pltpu.VMEM((2 * l, FOUR_W), BF16)
    return pl.pallas_call(
        functools.partial(body, seq=l),
        grid=(b,),
        in_specs=[tok] + [_resident(t.shape) for t in tabs],
        out_specs=tok,
        out_shape=jax.ShapeDtypeStruct((b, l, FOUR_W), BF16),
        scratch_shapes=[scratch],
        compiler_params=_params(("parallel",), 48),
        name=name,
    )(uf, *tabs)


_FFN_TILES = ((0, 1536), (1536, HIDDEN))


CONV_PAD = 8
HALO_ROWS = 16


def _out_ffn_kernel(*refs, add_pos, n_sub):
    it = iter(refs)
    yg_ref, yf_ref, bg_ref, cg_ref, uc_ref = (next(it) for _ in range(5))
    cg_prev_ref, uc_prev_ref, cg_next_ref, uc_next_ref, cw_ref, cb_ref = (next(it) for _ in range(6))
    wo_ref, x_ref = next(it), next(it)
    pos_ref = next(it) if add_pos else None
    gmix_ref, gatemix_ref, gpre_ref, shift_ref, scale_ref, gate_ref, gpost_ref = (next(it) for _ in range(7))
    win_ref, wout_ref, o_ref, z_scr = next(it), next(it), next(it), next(it)
    tm = x_ref.shape[1]
    subs = [slice(i * tm // n_sub, (i + 1) * tm // n_sub) for i in range(n_sub)]

    t, nt = pl.program_id(1), pl.num_programs(1)
    z_before = (cg_prev_ref[0].astype(F32) * uc_prev_ref[0].astype(F32))[HALO_ROWS - CONV_PAD:]
    z_after = (cg_next_ref[0].astype(F32) * uc_next_ref[0].astype(F32))[:CONV_PAD]
    z_scr[0:CONV_PAD, :] = jnp.where(t > 0, z_before, 0.0)
    z_scr[CONV_PAD + tm:, :] = jnp.where(t < nt - 1, z_after, 0.0)
    z_scr[CONV_PAD:CONV_PAD + tm, :] = cg_ref[0].astype(F32) * uc_ref[0].astype(F32)
    ycs = []
    for r in subs:
        s, n = CONV_PAD + r.start, r.stop - r.start
        y = (cw_ref[0:1] * z_scr[s - 1:s - 1 + n, :] + cw_ref[1:2] * z_scr[s:s + n, :]
             + cw_ref[2:3] * z_scr[s + 1:s + 1 + n, :] + cb_ref[...])
        ycs.append((bg_ref[0, r, :].astype(F32) * y).astype(BF16))

    ys = [(_dot(yg_ref[0, r, :], wo_ref[0:V_W]) + _dot(yf_ref[0, r, :], wo_ref[V_W:V_W + FOUR_W])
           + _dot(yc, wo_ref[V_W + FOUR_W:])) for r, yc in zip(subs, ycs)]
    xs, hs = [], []
    for r, y in zip(subs, ys):
        x = x_ref[0, r, :]
        if add_pos:
            x = x + pos_ref[r, :]
        x = x + gatemix_ref[...] * _rms(y, gmix_ref[...])
        xs.append(x)
        hs.append((_rms(x, gpre_ref[...]) * (1.0 + scale_ref[...]) + shift_ref[...]).astype(BF16))
    ys = []
    for h in hs:
        y = jnp.zeros((h.shape[0], D_MODEL), F32)
        for lo, hi in _FFN_TILES:
            a = _dot(h, win_ref[:, lo:hi])
            u = _dot(h, win_ref[:, HIDDEN + lo:HIDDEN + hi])
            y = y + _dot((_silu(a) * u).astype(BF16), wout_ref[lo:hi, :])
        ys.append(y)
    for r, x, y in zip(subs, xs, ys):
        o_ref[0, r, :] = x + gate_ref[...] * _rms(y, gpost_ref[...])


def _out_ffn(yg, yf, p, x, pos, lay, i, mod_row, tm):
    b, l, d = x.shape
    add_pos = pos is not None
    tok = lambda bi, ti: (bi, ti, 0)
    halo_per_tile, n_halo = tm // HALO_ROWS, l // HALO_ROWS
    before = lambda bi, ti: (bi, jnp.maximum(ti * halo_per_tile - 1, 0), 0)
    after = lambda bi, ti: (bi, jnp.minimum((ti + 1) * halo_per_tile, n_halo - 1), 0)
    conv_tok = pl.BlockSpec((1, tm, CONV_W), tok)
    in_specs = [pl.BlockSpec((1, tm, V_W), tok), pl.BlockSpec((1, tm, FOUR_W), tok), conv_tok, conv_tok, conv_tok,
                pl.BlockSpec((1, HALO_ROWS, CONV_W), before), pl.BlockSpec((1, HALO_ROWS, CONV_W), before),
                pl.BlockSpec((1, HALO_ROWS, CONV_W), after), pl.BlockSpec((1, HALO_ROWS, CONV_W), after),
                _layer(lay["conv_w"], i), _layer(lay["conv_b"], i),
                _layer(lay["w_out"], i), pl.BlockSpec((1, tm, d), tok)]
    args = [yg, yf, p["bg"], p["cg"], p["uc"], p["cg"], p["uc"], p["cg"], p["uc"],
            lay["conv_w"], lay["conv_b"], lay["w_out"], x]
    if add_pos:
        in_specs.append(pl.BlockSpec((tm, d), lambda bi, ti: (ti, 0)))
        args.append(pos)
    in_specs += [_layer(lay["g_mix_post"], i), _mod_spec(i, 2, mod_row), _layer(lay["g_ffn_pre"], i),
                 _mod_spec(i, 3, mod_row), _mod_spec(i, 4, mod_row), _mod_spec(i, 5, mod_row),
                 _layer(lay["g_ffn_post"], i), _layer(lay["w_ffn_in"], i), _layer(lay["w_ffn_out"], i)]
    args += [lay["g_mix_post"], lay["mod"], lay["g_ffn_pre"], lay["mod"], lay["mod"], lay["mod"],
             lay["g_ffn_post"], lay["w_ffn_in"], lay["w_ffn_out"]]
    return pl.pallas_call(
        functools.partial(_out_ffn_kernel, add_pos=add_pos, n_sub=2 if tm >= 512 else 1),
        grid=(b, l // tm),
        in_specs=in_specs,
        out_specs=pl.BlockSpec((1, tm, d), tok),
        out_shape=jax.ShapeDtypeStruct((b, l, d), F32),
        scratch_shapes=[pltpu.VMEM((tm + 2 * CONV_PAD, CONV_W), F32)],
        compiler_params=_params(("parallel", "parallel"), 56),
        name="out_ffn",
    )(*args)


def _arrange_w_in(w):
    main = jnp.concatenate([w[..., _C_K:_C_LR], w[..., _C_Q:_C_END]], axis=-1)
    lr = jnp.pad(w[..., _C_LR:_C_Q], ((0, 0), (0, 0), (0, LR_PAD - 2 * RANK)))
    return jnp.concatenate([main, lr], axis=-1).astype(BF16)


def _arrange_w_dec(w_f, w_b):
    wd = jnp.zeros((DEPTH, LR_PAD, 2 * K_W), F32)
    wd = wd.at[:, 0:RANK, :K_W].set(w_f).at[:, RANK:2 * RANK, K_W:].set(w_b)
    return wd.astype(BF16)


def kernel(x, c, ctx, c_ctx, w_mod, b_mod, g_mix_pre, g_mix_post, g_ffn_pre, g_ffn_post,
           w_in, w_dec_f, b_dec_f, w_dec_b, b_dec_b, gla_norm_w, conv_w, conv_b, w_out,
           w_ffn_in, w_ffn_out):
    b, n_lat, d = x.shape
    n_ctx = ctx.shape[1]
    assert d == D_MODEL and w_in.shape == (DEPTH, D_MODEL, _C_END)
    assert n_lat % 512 == 0 and n_ctx % CHUNK == 0 and b + 1 <= MOD_ROWS

    pos = jnp.asarray(_pos_embed(n_lat, d))
    tabs = {n_lat: _dft_tables(n_lat, FFT_RADIX), n_ctx: _dft_tables(n_ctx, 1)}

    cvec = jnp.zeros((MOD_ROWS, d), F32).at[:b].set(c).at[b].set(c_ctx)
    ctx_row = b
    lay = {
        "mod": _modulation(cvec, w_mod, b_mod).reshape(DEPTH, MOD_ROWS, 1, N_MOD * d),
        "w_in": _arrange_w_in(w_in),
        "w_dec": _arrange_w_dec(w_dec_f, w_dec_b),
        "b_dec": jnp.concatenate([b_dec_f, b_dec_b], axis=-1).reshape(DEPTH, 1, 2 * K_W),
        "gla_norm_w": gla_norm_w.reshape(DEPTH, 1, V_W),
        "conv_w": conv_w,
        "conv_b": conv_b.reshape(DEPTH, 1, CONV_W),
        "w_out": w_out.astype(BF16),
        "w_ffn_in": w_ffn_in.astype(BF16),
        "w_ffn_out": w_ffn_out.astype(BF16),
        "g_mix_pre": g_mix_pre.reshape(DEPTH, 1, d),
        "g_mix_post": g_mix_post.reshape(DEPTH, 1, d),
        "g_ffn_pre": g_ffn_pre.reshape(DEPTH, 1, d),
        "g_ffn_post": g_ffn_post.reshape(DEPTH, 1, d),
    }

    def mixer_and_ffn(i, xs, pos_s, mod_row, s0, want_state, tm):
        p = _inproj(xs, pos_s, lay, i, mod_row, _OUT_GROUPS_FULL, min(2 * tm, xs.shape[1]))
        yg, s = _gla(p, lay, i, s0, True, want_state)
        yf = _fourier(p["uf"], tabs[xs.shape[1]])
        return _out_ffn(yg, yf, p, xs, pos_s, lay, i, mod_row, tm), s

    zero_state = jnp.zeros((b, 2, DV, K_W), F32)
    xc = ctx
    for i in range(DEPTH):
        if i == DEPTH - 1:
            p = _inproj(xc, None, lay, i, ctx_row, _OUT_GROUPS_STATE, n_ctx)
            _, s = _gla(p, lay, i, zero_state, False, True)
        else:
            xc, s = mixer_and_ffn(i, xc, None, ctx_row, zero_state, True, n_ctx)
        x, _ = mixer_and_ffn(i, x, pos if i == 0 else None, None, s, False, 512)
    return x
```

```python
import functools

import numpy as np
import jax
import jax.numpy as jnp
from jax import lax
from jax.experimental import pallas as pl
from jax.experimental.pallas import tpu as pltpu

F32 = jnp.float32
BF16 = jnp.bfloat16

D_MODEL = 1024
DEPTH = 2
GRID_W = 64
N_MOD = 6
HEADS = 4
DK = 64
DV = 128
K_W = HEADS * DK
V_W = HEADS * DV
RANK = 16
TAU = 16.0
CHUNK = 64
FOUR_W = 256
FOUR_G = 4
CONV_W = 256
HIDDEN = 2816
EPS = 1e-6
LR_PAD = 128
MOD_ROWS = 16

_C_K, _C_V, _C_LR, _C_Q, _C_R, _C_FOUR, _C_BG, _C_CG, _C_UC, _C_END = (
    0, 256, 768, 800, 1056, 1568, 1824, 2080, 2336, 2592)
_MAIN_W = 2560
_IN_W = _MAIN_W + LR_PAD
_OUT_GROUPS_FULL = (("k", 0, 256), ("v", 256, 768), ("q", 768, 1024), ("r", 1024, 1536),
                    ("uf", 1536, 1792), ("bg", 1792, 2048), ("cg", 2048, 2304),
                    ("uc", 2304, 2560))
_OUT_GROUPS_STATE = (("k", 0, 256), ("v", 256, 768))

V7X_VMEM_BYTES = 64 * 1024 * 1024


def _params(semantics, vmem_mb):
    assert vmem_mb * 1024 * 1024 < V7X_VMEM_BYTES
    return pltpu.CompilerParams(dimension_semantics=semantics,
                                vmem_limit_bytes=vmem_mb * 1024 * 1024)


def _resident(shape):
    zeros = (0,) * len(shape)
    return pl.BlockSpec(shape, lambda *_: zeros, pipeline_mode=pl.Buffered(1))


def _layer(arr, i):
    shape = tuple(arr.shape[1:])
    zeros = (0,) * len(shape)
    return pl.BlockSpec((None,) + shape, lambda *_: (i,) + zeros, pipeline_mode=pl.Buffered(1))


def _mod_spec(i, j, row):
    if row is None:
        return pl.BlockSpec((None, None, 1, D_MODEL), lambda bi, *_: (i, bi, 0, j))
    return pl.BlockSpec((None, None, 1, D_MODEL), lambda *_: (i, row, 0, j))


def _silu(a):
    return a / (1.0 + jnp.exp(-a))


def _rms(x, g):
    return x * lax.rsqrt(jnp.mean(x * x, axis=-1, keepdims=True) + EPS) * g


def _dot(a, b):
    return jnp.dot(a, b, preferred_element_type=F32)


def _dot_nt(a, b):
    return lax.dot_general(a, b, (((1,), (1,)), ((), ())), preferred_element_type=F32)


def _dot_tn(a, b):
    return lax.dot_general(a, b, (((0,), (0,)), ((), ())), preferred_element_type=F32)


def _pos_embed(n_tokens, dim):
    rows = n_tokens // GRID_W
    row = np.repeat(np.arange(rows, dtype=np.float32), GRID_W)
    col = np.tile(np.arange(GRID_W, dtype=np.float32), rows)
    quarter = dim // 4
    freqs = (1.0 / (10000.0 ** (np.arange(quarter, dtype=np.float32) / quarter))).astype(np.float32)

    def enc(p):
        ang = (p[:, None] * freqs[None, :]).astype(np.float32)
        return np.concatenate([np.sin(ang), np.cos(ang)], axis=-1)

    return np.concatenate([enc(row), enc(col)], axis=-1).astype(np.float32)


def _dft_cos_sin(n):
    idx = np.arange(n, dtype=np.int64)
    ang = 2.0 * np.pi * ((idx[:, None] * idx[None, :]) % n).astype(np.float64) / n
    return np.cos(ang) / np.sqrt(n), np.sin(ang) / np.sqrt(n)


def _dft_tables(seq, radix):
    gw = FOUR_W // FOUR_G
    cc, sc = _dft_cos_sin(gw)
    eye = np.eye(FOUR_G)
    chan_tab = jnp.asarray(np.concatenate([np.kron(eye, cc), np.kron(eye, sc)], axis=1), dtype=F32).astype(BF16)
    if radix == 1:
        cl, sl = _dft_cos_sin(seq)
        return chan_tab, jnp.asarray(np.concatenate([cl, -sl], axis=1), dtype=F32).astype(BF16)
    n = seq // radix
    cn, sn = _dft_cos_sin(n)
    sub_tab = np.concatenate([cn, sn], axis=1) * np.sqrt(n / seq)
    ang = 2.0 * np.pi * (np.arange(1, radix)[:, None] * np.arange(n)[None, :]).astype(np.float64) / seq
    twc = np.broadcast_to(np.cos(ang)[:, :, None], (radix - 1, n, FOUR_W))
    tws = np.broadcast_to(np.sin(ang)[:, :, None], (radix - 1, n, FOUR_W))
    return (chan_tab, jnp.asarray(sub_tab, dtype=F32).astype(BF16),
            jnp.asarray(twc, dtype=F32), jnp.asarray(tws, dtype=F32))


def _mod_kernel(c_ref, w_ref, b_ref, o_ref):
    s = _silu(c_ref[...]).astype(BF16)
    o_ref[0] = _dot(s, w_ref[0].astype(BF16)) + b_ref[0]


def _modulation(cvec, w_mod, b_mod):
    tn = 1536
    n = N_MOD * D_MODEL
    return pl.pallas_call(
        _mod_kernel,
        grid=(DEPTH, n // tn),
        in_specs=[pl.BlockSpec((MOD_ROWS, D_MODEL), lambda i, j: (0, 0)),
                  pl.BlockSpec((1, D_MODEL, tn), lambda i, j: (i, 0, j)),
                  pl.BlockSpec((1, 1, tn), lambda i, j: (i, 0, j))],
        out_specs=pl.BlockSpec((1, MOD_ROWS, tn), lambda i, j: (i, 0, j)),
        out_shape=jax.ShapeDtypeStruct((DEPTH, MOD_ROWS, n), F32),
        compiler_params=_params(("arbitrary", "arbitrary"), 40),
        name="modulation",
    )(cvec, w_mod, b_mod.reshape(DEPTH, 1, n))


def _inproj_kernel(*refs, add_pos, groups, n_cast):
    it = iter(refs)
    x_ref = next(it)
    pos_ref = next(it) if add_pos else None
    g_ref, shift_ref, scale_ref, w_ref, wd_ref, bd_ref = (next(it) for _ in range(6))
    cast_in = [next(it) for _ in range(n_cast)]
    out_refs = list(it)
    for src, dst in zip(cast_in, out_refs[len(out_refs) - n_cast:]):
        dst[0] = src[...].astype(BF16)
    out_refs = out_refs[:len(out_refs) - n_cast]
    lg_hi_ref, lg_lo_ref = out_refs[-2:]
    tm = x_ref.shape[1]
    n_sub = 2 if tm >= 512 else 1
    subs = [slice(i * tm // n_sub, (i + 1) * tm // n_sub) for i in range(n_sub)]
    hs = []
    for r in subs:
        x = x_ref[0, r, :]
        if add_pos:
            x = x + pos_ref[r, :]
        hs.append((_rms(x, g_ref[...]) * (1.0 + scale_ref[...]) + shift_ref[...]).astype(BF16))
    def project(r, h):
        for o_ref, (_, lo, hi_col) in zip(out_refs, groups):
            o_ref[0, r, :] = _dot(h, w_ref[:, lo:hi_col]).astype(BF16)

    low_rank = [_dot(h, w_ref[:, _MAIN_W:_IN_W]).astype(BF16) for h in hs]
    project(subs[0], hs[0])
    zs = [_dot(lr, wd_ref[...]) + bd_ref[...] for lr in low_rank]
    for r, h in list(zip(subs, hs))[1:]:
        project(r, h)
    for r, z in zip(subs, zs):
        logg = (jnp.minimum(z, 0.0) - jnp.log(1.0 + jnp.exp(-jnp.abs(z)))) * (1.0 / TAU)
        hi = logg.astype(BF16)
        lg_hi_ref[0, r, :] = hi
        lg_lo_ref[0, r, :] = (logg - hi.astype(F32)).astype(BF16)


def _inproj(x, pos, lay, i, mod_row, groups, tm, cast=()):
    b, l, d = x.shape
    add_pos = pos is not None
    n_t = l // tm
    tok = lambda bi, ti: (bi, ti, 0)
    in_specs = [pl.BlockSpec((1, tm, d), tok)]
    args = [x]
    if add_pos:
        in_specs.append(pl.BlockSpec((tm, d), lambda bi, ti: (ti, 0)))
        args.append(pos)
    in_specs += [_layer(lay["g_mix_pre"], i), _mod_spec(i, 0, mod_row), _mod_spec(i, 1, mod_row),
                 _layer(lay["w_in"], i), _layer(lay["w_dec"], i), _layer(lay["b_dec"], i)]
    args += [lay["g_mix_pre"], lay["mod"], lay["mod"], lay["w_in"], lay["w_dec"], lay["b_dec"]]
    widths = [hi - lo for _, lo, hi in groups] + [2 * K_W, 2 * K_W]
    names = [name for name, _, _ in groups] + ["lg_hi", "lg_lo"]
    out_specs = [pl.BlockSpec((1, tm, w), tok) for w in widths]
    out_shape = [jax.ShapeDtypeStruct((b, l, w), BF16) for w in widths]
    for w in cast:
        _, rows, cols = w.shape
        slab = rows // (b * n_t)
        assert slab * b * n_t == rows and slab % 16 == 0
        in_specs.append(pl.BlockSpec((None, slab, cols), lambda bi, ti: (i, bi * n_t + ti, 0)))
        out_specs.append(pl.BlockSpec((1, slab, cols), lambda bi, ti: (0, bi * n_t + ti, 0)))
        out_shape.append(jax.ShapeDtypeStruct((1, rows, cols), BF16))
        args.append(w)
    outs = pl.pallas_call(
        functools.partial(_inproj_kernel, add_pos=add_pos, groups=groups, n_cast=len(cast)),
        grid=(b, n_t),
        in_specs=in_specs,
        out_specs=out_specs,
        out_shape=out_shape,
        compiler_params=_params(("parallel", "parallel"), 56),
        name="inproj",
    )(*args)
    return dict(zip(names, outs)), list(outs[len(names):])


def _gla_kernel(*refs, seq, with_output, with_state):
    it = iter(refs)
    k_ref, v_ref, lg_hi, lg_lo, s0_ref = (next(it) for _ in range(5))
    if with_output:
        q_ref, r_ref, nw_ref = next(it), next(it), next(it)
        y_ref = next(it)
    if with_state:
        s_out_ref = next(it)
    st_ref = next(it)
    ds_scr = next(it)
    dec_scr = next(it)
    if with_output:
        o_scr = next(it)
        qst_scr = next(it)
        gain_scr = next(it)
    n_chunks = seq // CHUNK

    st_ref[...] = s0_ref[0]

    row = lax.broadcasted_iota(jnp.int32, (CHUNK, CHUNK), 0)
    col = lax.broadcasted_iota(jnp.int32, (CHUNK, CHUNK), 1)
    cum = (jnp.where(row >= col, 1.0, 0.0).astype(BF16), jnp.where(row <= col, 1.0, 0.0).astype(BF16))
    lane_head = lax.broadcasted_iota(jnp.int32, (1, K_W), 1) // DK
    head_mask = [lane_head == h for h in range(HEADS)]
    srow = lax.broadcasted_iota(jnp.int32, (CHUNK, HEADS * CHUNK), 0)
    scol = lax.broadcasted_iota(jnp.int32, (CHUNK, HEADS * CHUNK), 1) % CHUNK
    score_mask = (srow >= scol, srow <= scol)
    zero_v = jnp.zeros((CHUNK, DV), BF16)
    last_row = (CHUNK - 1, 0)
    mid_row = (CHUNK // 2 - 1, CHUNK // 2)

    def stack_heads(a):
        zero = jnp.zeros_like(a)
        return jnp.concatenate([jnp.where(head_mask[h], a, zero) for h in range(HEADS)], axis=0)

    def chunk_rows(n):
        return pl.ds(pl.multiple_of(n * CHUNK, CHUNK), CHUNK)

    per_iter = min(8, n_chunks)

    def chunk_local(i, carry):
        loaded = []
        for u in range(per_iter):
            n = i * per_iter + u
            rows = chunk_rows(n)
            k = k_ref[0, rows, :].astype(F32)
            v = v_ref[0, rows, :]
            v_heads = [v[:, h * DV:(h + 1) * DV] for h in range(HEADS)]
            v_rows = jnp.concatenate(v_heads, axis=0)
            v_diag = jnp.concatenate(
                [jnp.concatenate([v_heads[h] if hh == h else zero_v for hh in range(HEADS)], axis=1)
                 for h in range(HEADS)], axis=0) if with_output else None
            v = (v, v_rows, v_diag)
            qs = q_ref[0, rows, :].astype(F32) * (DK ** -0.5) if with_output else None
            lg = [(lg_hi[0, rows, d * K_W:(d + 1) * K_W], lg_lo[0, rows, d * K_W:(d + 1) * K_W]) for d in range(2)]
            loaded.append((n, rows, k, v, qs, lg))
        chains = [(n, rows, k, v, qs, lg[dirn], dirn) for n, rows, k, v, qs, lg in loaded for dirn in range(2)]
        stores = []
        if with_output:
            for n, rows, *_ in loaded:
                stores.append((gain_scr, (rows, slice(None)),
                               _silu(r_ref[0, rows, :].astype(F32)) * nw_ref[...]))
        gs = [_dot(cum[dirn], hi) + _dot(cum[dirn], lo) for *_, (hi, lo), dirn in chains]
        score_list = []
        for (n, rows, k, (v, v_rows, v_diag), qs, _, dirn), g in zip(chains, gs):
            g_last = g[last_row[dirn]:last_row[dirn] + 1]
            stores.append((dec_scr, (dirn, pl.ds(pl.multiple_of(n * 8, 8), 8), slice(None)),
                           jnp.broadcast_to(jnp.exp(g_last), (8, K_W))))
            if with_output:
                g_mid = g[mid_row[dirn]:mid_row[dirn] + 1]
                q_in = qs * jnp.exp(g - g_mid)
                k_in = k * jnp.exp(g_mid - g)
                score_list.append(_dot_nt(q_in.astype(BF16), stack_heads(k_in.astype(BF16))))
                stores.append((qst_scr, (dirn, rows, slice(None)), (q_in * jnp.exp(g_mid)).astype(BF16)))
                k_upd = stack_heads((k_in * jnp.exp(g_last - g_mid)).astype(BF16))
            else:
                k_upd = stack_heads((k * jnp.exp(g_last - g)).astype(BF16))
            stores.append((ds_scr, (dirn, pl.ds(pl.multiple_of(n * DV, DV), DV), slice(None)),
                           _dot_tn(v_rows, k_upd)))
        if with_output:
            for (n, rows, k, (v, v_rows, v_diag), qs, _, dirn), sc in zip(chains, score_list):
                scores = jnp.where(score_mask[dirn], sc, 0.0).astype(BF16)
                stores.append((o_scr, (dirn, rows, slice(None)), _dot(scores, v_diag)))
        for ref, idx, val in stores:
            ref[idx] = val
        return carry

    lax.fori_loop(0, n_chunks // per_iter, chunk_local, 0)

    def scan_step(i, finish):
        steps = ((0, i), (1, n_chunks - 1 - i))
        new_st, inters = [], []
        for dirn, n in steps:
            st = st_ref[dirn]
            if with_output:
                q_st = stack_heads(qst_scr[dirn, chunk_rows(n), :])
                inters.append(_dot_nt(q_st, st.astype(BF16)))
            dec = dec_scr[dirn, pl.ds(pl.multiple_of(n * 8, 8), 8), :]
            new_st.append(dec[0:1] * st + ds_scr[dirn, pl.ds(pl.multiple_of(n * DV, DV), DV), :])
        for (dirn, n), st in zip(steps, new_st):
            st_ref[dirn] = st
        if with_output:
            for (dirn, n), inter in zip(steps, inters):
                rows = chunk_rows(n)
                for h in range(HEADS):
                    vs = slice(h * DV, (h + 1) * DV)
                    o = o_scr[dirn, rows, vs] + inter[h * CHUNK:(h + 1) * CHUNK]
                    if finish:
                        o = o + o_scr[1 - dirn, rows, vs]
                        y_ref[0, rows, vs] = (_rms(o, gain_scr[rows, vs])).astype(BF16)
                    else:
                        o_scr[dirn, rows, vs] = o

    half = n_chunks // 2
    unroll = min(16, half)

    def scan_first(i, carry):
        scan_step(i, False)
        return carry

    def scan_second(i, carry):
        scan_step(i, True)
        return carry

    lax.fori_loop(0, half, scan_first, 0, unroll=unroll)
    lax.fori_loop(half, n_chunks, scan_second, 0, unroll=unroll)

    if with_state:
        s_out_ref[0] = st_ref[...]


def _gla(p, lay, i, s0, with_output, with_state):
    b, l, _ = p["k"].shape
    seq3 = lambda bi: (bi, 0, 0)
    in_specs = [pl.BlockSpec((1, l, K_W), seq3), pl.BlockSpec((1, l, V_W), seq3),
                pl.BlockSpec((1, l, 2 * K_W), seq3), pl.BlockSpec((1, l, 2 * K_W), seq3),
                pl.BlockSpec((1, 2, DV, K_W), lambda bi: (bi, 0, 0, 0))]
    args = [p["k"], p["v"], p["lg_hi"], p["lg_lo"], s0]
    out_specs, out_shape = [], []
    scratch = [pltpu.VMEM((2, DV, K_W), F32),
               pltpu.VMEM((2, l // CHUNK * DV, K_W), F32), pltpu.VMEM((2, l // CHUNK * 8, K_W), F32)]
    if with_output:
        in_specs += [pl.BlockSpec((1, l, K_W), seq3), pl.BlockSpec((1, l, V_W), seq3),
                     _layer(lay["gla_norm_w"], i)]
        args += [p["q"], p["r"], lay["gla_norm_w"]]
        out_specs.append(pl.BlockSpec((1, l, V_W), seq3))
        out_shape.append(jax.ShapeDtypeStruct((b, l, V_W), BF16))
        scratch += [pltpu.VMEM((2, l, V_W), F32), pltpu.VMEM((2, l, K_W), BF16), pltpu.VMEM((l, V_W), F32)]
    if with_state:
        out_specs.append(pl.BlockSpec((1, 2, DV, K_W), lambda bi: (bi, 0, 0, 0)))
        out_shape.append(jax.ShapeDtypeStruct((b, 2, DV, K_W), F32))
    outs = pl.pallas_call(
        functools.partial(_gla_kernel, seq=l, with_output=with_output, with_state=with_state),
        grid=(b,),
        in_specs=in_specs,
        out_specs=out_specs,
        out_shape=out_shape,
        scratch_shapes=scratch,
        compiler_params=_params(("parallel",), 56),
        name="gla",
    )(*args)
    outs = list(outs)
    y = outs.pop(0) if with_output else None
    s = outs.pop(0) if with_state else None
    return y, s


def _fourier_kernel(uf_ref, chan_ref, pos_ref, yf_ref, ab_scr, *, seq):
    ab = _dot(uf_ref[0], chan_ref[...])
    ab_scr[0:seq, :] = ab[:, :FOUR_W].astype(BF16)
    ab_scr[seq:2 * seq, :] = ab[:, FOUR_W:].astype(BF16)
    rb = min(seq, 512)
    for i in range(seq // rb):
        yf_ref[0, i * rb:(i + 1) * rb, :] = _dot(pos_ref[i * rb:(i + 1) * rb, :], ab_scr[...]).astype(BF16)


FFT_RADIX = 8


def _fourier_fft_kernel(uf_ref, chan_ref, sub_ref, twc_ref, tws_ref, yf_ref, uf_scr, *, seq):
    n = seq // FFT_RADIX
    lanes = uf_scr.shape[2]
    n_lane_tiles = FOUR_W // lanes
    for j in range(n_lane_tiles):
        uf_scr[j] = uf_ref[0, :, j * lanes:(j + 1) * lanes].astype(F32)

    def tokens(r):
        parts = [uf_scr[j, pl.ds(r, n, stride=FFT_RADIX), :] for j in range(n_lane_tiles)]
        return jnp.concatenate(parts, axis=1).astype(BF16)

    ab = [_dot(tokens(r), chan_ref[...]) for r in range(FFT_RADIX)]
    pq = []
    for ab_r in ab:
        a, b = ab_r[:, :FOUR_W].astype(BF16), ab_r[:, FOUR_W:].astype(BF16)
        w = jnp.concatenate([jnp.concatenate([a, b], axis=1), jnp.concatenate([-b, a], axis=1)], axis=0)
        pq.append(_dot(sub_ref[...], w))
    re, im = [pq[0][:, :FOUR_W]], [pq[0][:, FOUR_W:]]
    for r in range(1, FFT_RADIX):
        p, q = pq[r][:, :FOUR_W], pq[r][:, FOUR_W:]
        c, s = twc_ref[r - 1], tws_ref[r - 1]
        re.append(c * p - s * q)
        im.append(s * p + c * q)

    def quarter(z0, z1, z2, z3):
        s02 = (z0[0] + z2[0], z0[1] + z2[1])
        d02 = (z0[0] - z2[0], z0[1] - z2[1])
        s13 = (z1[0] + z3[0], z1[1] + z3[1])
        d13 = (z1[0] - z3[0], z1[1] - z3[1])
        return ((s02[0] + s13[0], s02[1] + s13[1]), (d02[0] - d13[1], d02[1] + d13[0]),
                (s02[0] - s13[0], s02[1] - s13[1]), (d02[0] + d13[1], d02[1] - d13[0]))

    zs = list(zip(re, im))
    even = quarter(zs[0], zs[2], zs[4], zs[6])
    odd = quarter(zs[1], zs[3], zs[5], zs[7])
    half = 0.5 ** 0.5
    odd_re = (odd[0][0], (odd[1][0] - odd[1][1]) * half, -odd[2][1], -(odd[3][0] + odd[3][1]) * half)
    for q in range(4):
        yf_ref[0, q * n:(q + 1) * n, :] = (even[q][0] + odd_re[q]).astype(BF16)
        yf_ref[0, (q + 4) * n:(q + 5) * n, :] = (even[q][0] - odd_re[q]).astype(BF16)


def _fourier(uf, tabs):
    b, l, _ = uf.shape
    tok = pl.BlockSpec((1, l, FOUR_W), lambda bi: (bi, 0, 0))
    if len(tabs) == 4:
        body, name = _fourier_fft_kernel, "fourier_fft"
        scratch = pltpu.VMEM((FOUR_W // 128, l, 128), F32)
    else:
        body, name = _fourier_kernel, "fourier"
        scratch = pltpu.VMEM((2 * l, FOUR_W), BF16)
    return pl.pallas_call(
        functools.partial(body, seq=l),
        grid=(b,),
        in_specs=[tok] + [_resident(t.shape) for t in tabs],
        out_specs=tok,
        out_shape=jax.ShapeDtypeStruct((b, l, FOUR_W), BF16),
        scratch_shapes=[scratch],
        compiler_params=_params(("parallel",), 48),
        name=name,
    )(uf, *tabs)


_FFN_TILES = ((0, 1536), (1536, HIDDEN))


CONV_PAD = 8
HALO_ROWS = 16


def _out_ffn_kernel(*refs, add_pos, n_sub):
    it = iter(refs)
    yg_ref, yf_ref, bg_ref, cg_ref, uc_ref = (next(it) for _ in range(5))
    cg_prev_ref, uc_prev_ref, cg_next_ref, uc_next_ref, cw_ref, cb_ref = (next(it) for _ in range(6))
    wo_ref, x_ref = next(it), next(it)
    pos_ref = next(it) if add_pos else None
    gmix_ref, gatemix_ref, gpre_ref, shift_ref, scale_ref, gate_ref, gpost_ref = (next(it) for _ in range(7))
    win_ref, wout_ref, o_ref, z_scr = next(it), next(it), next(it), next(it)
    tm = x_ref.shape[1]
    subs = [slice(i * tm // n_sub, (i + 1) * tm // n_sub) for i in range(n_sub)]

    t, nt = pl.program_id(1), pl.num_programs(1)
    z_before = (cg_prev_ref[0].astype(F32) * uc_prev_ref[0].astype(F32))[HALO_ROWS - CONV_PAD:]
    z_after = (cg_next_ref[0].astype(F32) * uc_next_ref[0].astype(F32))[:CONV_PAD]
    z_scr[0:CONV_PAD, :] = jnp.where(t > 0, z_before, 0.0)
    z_scr[CONV_PAD + tm:, :] = jnp.where(t < nt - 1, z_after, 0.0)
    z_scr[CONV_PAD:CONV_PAD + tm, :] = cg_ref[0].astype(F32) * uc_ref[0].astype(F32)
    ycs = []
    for r in subs:
        s, n = CONV_PAD + r.start, r.stop - r.start
        y = (cw_ref[0:1] * z_scr[s - 1:s - 1 + n, :] + cw_ref[1:2] * z_scr[s:s + n, :]
             + cw_ref[2:3] * z_scr[s + 1:s + 1 + n, :] + cb_ref[...])
        ycs.append((bg_ref[0, r, :].astype(F32) * y).astype(BF16))

    ys = [(_dot(yg_ref[0, r, :], wo_ref[0:V_W]) + _dot(yf_ref[0, r, :], wo_ref[V_W:V_W + FOUR_W])
           + _dot(yc, wo_ref[V_W + FOUR_W:])) for r, yc in zip(subs, ycs)]
    xs, hs = [], []
    for r, y in zip(subs, ys):
        x = x_ref[0, r, :]
        if add_pos:
            x = x + pos_ref[r, :]
        x = x + gatemix_ref[...] * _rms(y, gmix_ref[...])
        xs.append(x)
        hs.append((_rms(x, gpre_ref[...]) * (1.0 + scale_ref[...]) + shift_ref[...]).astype(BF16))
    ys = []
    for h in hs:
        y = jnp.zeros((h.shape[0], D_MODEL), F32)
        for lo, hi in _FFN_TILES:
            a = _dot(h, win_ref[:, lo:hi])
            u = _dot(h, win_ref[:, HIDDEN + lo:HIDDEN + hi])
            y = y + _dot((_silu(a) * u).astype(BF16), wout_ref[lo:hi, :])
        ys.append(y)
    for r, x, y in zip(subs, xs, ys):
        o_ref[0, r, :] = x + gate_ref[...] * _rms(y, gpost_ref[...])


def _out_ffn(yg, yf, p, x, pos, lay, i, wts, mod_row, tm):
    w_out, w_ffn_in, w_ffn_out = wts
    b, l, d = x.shape
    add_pos = pos is not None
    tok = lambda bi, ti: (bi, ti, 0)
    halo_per_tile, n_halo = tm // HALO_ROWS, l // HALO_ROWS
    before = lambda bi, ti: (bi, jnp.maximum(ti * halo_per_tile - 1, 0), 0)
    after = lambda bi, ti: (bi, jnp.minimum((ti + 1) * halo_per_tile, n_halo - 1), 0)
    conv_tok = pl.BlockSpec((1, tm, CONV_W), tok)
    in_specs = [pl.BlockSpec((1, tm, V_W), tok), pl.BlockSpec((1, tm, FOUR_W), tok), conv_tok, conv_tok, conv_tok,
                pl.BlockSpec((1, HALO_ROWS, CONV_W), before), pl.BlockSpec((1, HALO_ROWS, CONV_W), before),
                pl.BlockSpec((1, HALO_ROWS, CONV_W), after), pl.BlockSpec((1, HALO_ROWS, CONV_W), after),
                _layer(lay["conv_w"], i), _layer(lay["conv_b"], i),
                _layer(w_out, 0), pl.BlockSpec((1, tm, d), tok)]
    args = [yg, yf, p["bg"], p["cg"], p["uc"], p["cg"], p["uc"], p["cg"], p["uc"],
            lay["conv_w"], lay["conv_b"], w_out, x]
    if add_pos:
        in_specs.append(pl.BlockSpec((tm, d), lambda bi, ti: (ti, 0)))
        args.append(pos)
    in_specs += [_layer(lay["g_mix_post"], i), _mod_spec(i, 2, mod_row), _layer(lay["g_ffn_pre"], i),
                 _mod_spec(i, 3, mod_row), _mod_spec(i, 4, mod_row), _mod_spec(i, 5, mod_row),
                 _layer(lay["g_ffn_post"], i), _layer(w_ffn_in, 0), _layer(w_ffn_out, 0)]
    args += [lay["g_mix_post"], lay["mod"], lay["g_ffn_pre"], lay["mod"], lay["mod"], lay["mod"],
             lay["g_ffn_post"], w_ffn_in, w_ffn_out]
    return pl.pallas_call(
        functools.partial(_out_ffn_kernel, add_pos=add_pos, n_sub=2 if tm >= 512 else 1),
        grid=(b, l // tm),
        in_specs=in_specs,
        out_specs=pl.BlockSpec((1, tm, d), tok),
        out_shape=jax.ShapeDtypeStruct((b, l, d), F32),
        scratch_shapes=[pltpu.VMEM((tm + 2 * CONV_PAD, CONV_W), F32)],
        compiler_params=_params(("parallel", "parallel"), 56),
        name="out_ffn",
    )(*args)


def _arrange_w_in(w):
    main = jnp.concatenate([w[..., _C_K:_C_LR], w[..., _C_Q:_C_END]], axis=-1)
    lr = jnp.pad(w[..., _C_LR:_C_Q], ((0, 0), (0, 0), (0, LR_PAD - 2 * RANK)))
    return jnp.concatenate([main, lr], axis=-1).astype(BF16)


def _arrange_w_dec(w_f, w_b):
    wd = jnp.zeros((DEPTH, LR_PAD, 2 * K_W), F32)
    wd = wd.at[:, 0:RANK, :K_W].set(w_f).at[:, RANK:2 * RANK, K_W:].set(w_b)
    return wd.astype(BF16)


def kernel(x, c, ctx, c_ctx, w_mod, b_mod, g_mix_pre, g_mix_post, g_ffn_pre, g_ffn_post,
           w_in, w_dec_f, b_dec_f, w_dec_b, b_dec_b, gla_norm_w, conv_w, conv_b, w_out,
           w_ffn_in, w_ffn_out):
    b, n_lat, d = x.shape
    n_ctx = ctx.shape[1]
    assert d == D_MODEL and w_in.shape == (DEPTH, D_MODEL, _C_END)
    assert n_lat % 512 == 0 and n_ctx % CHUNK == 0 and b + 1 <= MOD_ROWS

    pos = jnp.asarray(_pos_embed(n_lat, d))
    tabs = {n_lat: _dft_tables(n_lat, FFT_RADIX), n_ctx: _dft_tables(n_ctx, 1)}

    cvec = jnp.zeros((MOD_ROWS, d), F32).at[:b].set(c).at[b].set(c_ctx)
    ctx_row = b
    lay = {
        "mod": _modulation(cvec, w_mod, b_mod).reshape(DEPTH, MOD_ROWS, 1, N_MOD * d),
        "w_in": _arrange_w_in(w_in),
        "w_dec": _arrange_w_dec(w_dec_f, w_dec_b),
        "b_dec": jnp.concatenate([b_dec_f, b_dec_b], axis=-1).reshape(DEPTH, 1, 2 * K_W),
        "gla_norm_w": gla_norm_w.reshape(DEPTH, 1, V_W),
        "conv_w": conv_w,
        "conv_b": conv_b.reshape(DEPTH, 1, CONV_W),
        "g_mix_pre": g_mix_pre.reshape(DEPTH, 1, d),
        "g_mix_post": g_mix_post.reshape(DEPTH, 1, d),
        "g_ffn_pre": g_ffn_pre.reshape(DEPTH, 1, d),
        "g_ffn_post": g_ffn_post.reshape(DEPTH, 1, d),
    }

    def mix_and_ffn(i, xs, pos_s, p, wts, mod_row, s0, want_state, tm):
        yg, s = _gla(p, lay, i, s0, True, want_state)
        yf = _fourier(p["uf"], tabs[xs.shape[1]])
        return _out_ffn(yg, yf, p, xs, pos_s, lay, i, wts, mod_row, tm), s

    zero_state = jnp.zeros((b, 2, DV, K_W), F32)
    xc = ctx
    tm = 512
    for i in range(DEPTH):
        pos_i = pos if i == 0 else None
        p_lat, wts = _inproj(x, pos_i, lay, i, None, _OUT_GROUPS_FULL, 2 * tm, cast=(w_out, w_ffn_in, w_ffn_out))
        if i == DEPTH - 1:
            p_ctx, _ = _inproj(xc, None, lay, i, ctx_row, _OUT_GROUPS_STATE, n_ctx)
            _, s = _gla(p_ctx, lay, i, zero_state, False, True)
        else:
            p_ctx, _ = _inproj(xc, None, lay, i, ctx_row, _OUT_GROUPS_FULL, n_ctx)
            xc, s = mix_and_ffn(i, xc, None, p_ctx, wts, ctx_row, zero_state, True, n_ctx)
        x, _ = mix_and_ffn(i, x, pos_i, p_lat, wts, None, s, False, tm)
    return x
```

```python
import functools

import numpy as np
import jax
import jax.numpy as jnp
from jax import lax
from jax.experimental import pallas as pl
from jax.experimental.pallas import tpu as pltpu

F32 = jnp.float32
BF16 = jnp.bfloat16

D_MODEL = 1024
DEPTH = 2
GRID_W = 64
N_MOD = 6
HEADS = 4
DK = 64
DV = 128
K_W = HEADS * DK
V_W = HEADS * DV
RANK = 16
TAU = 16.0
CHUNK = 64
FOUR_W = 256
FOUR_G = 4
CONV_W = 256
HIDDEN = 2816
EPS = 1e-6
LR_PAD = 128
MOD_ROWS = 16

_C_K, _C_V, _C_LR, _C_Q, _C_R, _C_FOUR, _C_BG, _C_CG, _C_UC, _C_END = (
    0, 256, 768, 800, 1056, 1568, 1824, 2080, 2336, 2592)
_MAIN_W = 2560
_IN_W = _MAIN_W + LR_PAD
_OUT_GROUPS_FULL = (("k", 0, 256), ("v", 256, 768), ("q", 768, 1024), ("r", 1024, 1536),
                    ("uf", 1536, 1792), ("bg", 1792, 2048), ("cg", 2048, 2304),
                    ("uc", 2304, 2560))
_OUT_GROUPS_STATE = (("k", 0, 256), ("v", 256, 768))

V7X_VMEM_BYTES = 64 * 1024 * 1024


def _params(semantics, vmem_mb):
    assert vmem_mb * 1024 * 1024 < V7X_VMEM_BYTES
    return pltpu.CompilerParams(dimension_semantics=semantics,
                                vmem_limit_bytes=vmem_mb * 1024 * 1024)


def _resident(shape):
    zeros = (0,) * len(shape)
    return pl.BlockSpec(shape, lambda *_: zeros, pipeline_mode=pl.Buffered(1))


def _layer(arr, i):
    shape = tuple(arr.shape[1:])
    zeros = (0,) * len(shape)
    return pl.BlockSpec((None,) + shape, lambda *_: (i,) + zeros, pipeline_mode=pl.Buffered(1))


def _mod_spec(i, j, row):
    if row is None:
        return pl.BlockSpec((None, None, 1, D_MODEL), lambda bi, *_: (i, bi, 0, j))
    return pl.BlockSpec((None, None, 1, D_MODEL), lambda *_: (i, row, 0, j))


def _silu(a):
    return a / (1.0 + jnp.exp(-a))


def _rms(x, g):
    return x * lax.rsqrt(jnp.mean(x * x, axis=-1, keepdims=True) + EPS) * g


def _dot(a, b):
    return jnp.dot(a, b, preferred_element_type=F32)


def _dot_nt(a, b):
    return lax.dot_general(a, b, (((1,), (1,)), ((), ())), preferred_element_type=F32)


def _dot_tn(a, b):
    return lax.dot_general(a, b, (((0,), (0,)), ((), ())), preferred_element_type=F32)


def _pos_embed(n_tokens, dim):
    rows = n_tokens // GRID_W
    row = np.repeat(np.arange(rows, dtype=np.float32), GRID_W)
    col = np.tile(np.arange(GRID_W, dtype=np.float32), rows)
    quarter = dim // 4
    freqs = (1.0 / (10000.0 ** (np.arange(quarter, dtype=np.float32) / quarter))).astype(np.float32)

    def enc(p):
        ang = (p[:, None] * freqs[None, :]).astype(np.float32)
        return np.concatenate([np.sin(ang), np.cos(ang)], axis=-1)

    return np.concatenate([enc(row), enc(col)], axis=-1).astype(np.float32)


def _dft_cos_sin(n):
    idx = np.arange(n, dtype=np.int64)
    ang = 2.0 * np.pi * ((idx[:, None] * idx[None, :]) % n).astype(np.float64) / n
    return np.cos(ang) / np.sqrt(n), np.sin(ang) / np.sqrt(n)


def _dft_tables(seq, radix):
    gw = FOUR_W // FOUR_G
    cc, sc = _dft_cos_sin(gw)
    eye = np.eye(FOUR_G)
    chan_tab = jnp.asarray(np.concatenate([np.kron(eye, cc), np.kron(eye, sc)], axis=1), dtype=F32).astype(BF16)
    if radix == 1:
        cl, sl = _dft_cos_sin(seq)
        return chan_tab, jnp.asarray(np.concatenate([cl, -sl], axis=1), dtype=F32).astype(BF16)
    n = seq // radix
    cn, sn = _dft_cos_sin(n)
    sub_tab = np.concatenate([cn, sn], axis=1) * np.sqrt(n / seq)
    ang = 2.0 * np.pi * (np.arange(1, radix)[:, None] * np.arange(n)[None, :]).astype(np.float64) / seq
    twc = np.broadcast_to(np.cos(ang)[:, :, None], (radix - 1, n, FOUR_W))
    tws = np.broadcast_to(np.sin(ang)[:, :, None], (radix - 1, n, FOUR_W))
    return (chan_tab, jnp.asarray(sub_tab, dtype=F32).astype(BF16),
            jnp.asarray(twc, dtype=F32), jnp.asarray(tws, dtype=F32))


def _mod_kernel(c_ref, w_ref, b_ref, win_ref, o_ref, wout_ref):
    s = _silu(c_ref[...]).astype(BF16)
    o_ref[0] = _dot(s, w_ref[0].astype(BF16)) + b_ref[0]
    w = win_ref[0]
    wout_ref[0, :, 0:_C_LR] = w[:, _C_K:_C_LR].astype(BF16)
    wout_ref[0, :, _C_LR:_MAIN_W] = w[:, _C_Q:_C_END].astype(BF16)
    lr = jnp.concatenate([w[:, _C_LR:_C_Q], jnp.zeros((w.shape[0], LR_PAD - 2 * RANK), F32)], axis=1)
    wout_ref[0, :, _MAIN_W:_IN_W] = lr.astype(BF16)


def _modulation(cvec, w_mod, b_mod, w_in):
    tn = 1536
    n = N_MOD * D_MODEL
    steps = n // tn
    slab = D_MODEL // steps
    return pl.pallas_call(
        _mod_kernel,
        grid=(DEPTH, steps),
        in_specs=[pl.BlockSpec((MOD_ROWS, D_MODEL), lambda i, j: (0, 0)),
                  pl.BlockSpec((1, D_MODEL, tn), lambda i, j: (i, 0, j)),
                  pl.BlockSpec((1, 1, tn), lambda i, j: (i, 0, j)),
                  pl.BlockSpec((1, slab, _C_END), lambda i, j: (i, j, 0))],
        out_specs=[pl.BlockSpec((1, MOD_ROWS, tn), lambda i, j: (i, 0, j)),
                   pl.BlockSpec((1, slab, _IN_W), lambda i, j: (i, j, 0))],
        out_shape=[jax.ShapeDtypeStruct((DEPTH, MOD_ROWS, n), F32),
                   jax.ShapeDtypeStruct((DEPTH, D_MODEL, _IN_W), BF16)],
        compiler_params=_params(("arbitrary", "arbitrary"), 40),
        name="modulation",
    )(cvec, w_mod, b_mod.reshape(DEPTH, 1, n), w_in)


def _inproj_kernel(*refs, add_pos, groups, n_cast):
    it = iter(refs)
    x_ref = next(it)
    pos_ref = next(it) if add_pos else None
    g_ref, shift_ref, scale_ref, w_ref, wd_ref, bd_ref = (next(it) for _ in range(6))
    cast_in = [next(it) for _ in range(n_cast)]
    out_refs = list(it)
    for src, dst in zip(cast_in, out_refs[len(out_refs) - n_cast:]):
        dst[0] = src[...].astype(BF16)
    out_refs = out_refs[:len(out_refs) - n_cast]
    lg_hi_ref, lg_lo_ref = out_refs[-2:]
    tm = x_ref.shape[1]
    n_sub = 2 if tm >= 512 else 1
    subs = [slice(i * tm // n_sub, (i + 1) * tm // n_sub) for i in range(n_sub)]
    hs = []
    for r in subs:
        x = x_ref[0, r, :]
        if add_pos:
            x = x + pos_ref[r, :]
        hs.append((_rms(x, g_ref[...]) * (1.0 + scale_ref[...]) + shift_ref[...]).astype(BF16))
    def project(r, h):
        for o_ref, (_, lo, hi_col) in zip(out_refs, groups):
            o_ref[0, r, :] = _dot(h, w_ref[:, lo:hi_col]).astype(BF16)

    low_rank = [_dot(h, w_ref[:, _MAIN_W:_IN_W]).astype(BF16) for h in hs]
    project(subs[0], hs[0])
    zs = [_dot(lr, wd_ref[...]) + bd_ref[...] for lr in low_rank]
    for r, h in list(zip(subs, hs))[1:]:
        project(r, h)
    for r, z in zip(subs, zs):
        logg = (jnp.minimum(z, 0.0) - jnp.log(1.0 + jnp.exp(-jnp.abs(z)))) * (1.0 / TAU)
        hi = logg.astype(BF16)
        lg_hi_ref[0, r, :] = hi
        lg_lo_ref[0, r, :] = (logg - hi.astype(F32)).astype(BF16)


def _inproj(x, pos, lay, i, mod_row, groups, tm, cast=()):
    b, l, d = x.shape
    add_pos = pos is not None
    n_t = l // tm
    tok = lambda bi, ti: (bi, ti, 0)
    in_specs = [pl.BlockSpec((1, tm, d), tok)]
    args = [x]
    if add_pos:
        in_specs.append(pl.BlockSpec((tm, d), lambda bi, ti: (ti, 0)))
        args.append(pos)
    in_specs += [_layer(lay["g_mix_pre"], i), _mod_spec(i, 0, mod_row), _mod_spec(i, 1, mod_row),
                 _layer(lay["w_in"], i), _layer(lay["w_dec"], i), _layer(lay["b_dec"], i)]
    args += [lay["g_mix_pre"], lay["mod"], lay["mod"], lay["w_in"], lay["w_dec"], lay["b_dec"]]
    widths = [hi - lo for _, lo, hi in groups] + [2 * K_W, 2 * K_W]
    names = [name for name, _, _ in groups] + ["lg_hi", "lg_lo"]
    out_specs = [pl.BlockSpec((1, tm, w), tok) for w in widths]
    out_shape = [jax.ShapeDtypeStruct((b, l, w), BF16) for w in widths]
    for w in cast:
        _, rows, cols = w.shape
        slab = rows // (b * n_t)
        assert slab * b * n_t == rows and slab % 16 == 0
        in_specs.append(pl.BlockSpec((None, slab, cols), lambda bi, ti: (i, bi * n_t + ti, 0)))
        out_specs.append(pl.BlockSpec((1, slab, cols), lambda bi, ti: (0, bi * n_t + ti, 0)))
        out_shape.append(jax.ShapeDtypeStruct((1, rows, cols), BF16))
        args.append(w)
    outs = pl.pallas_call(
        functools.partial(_inproj_kernel, add_pos=add_pos, groups=groups, n_cast=len(cast)),
        grid=(b, n_t),
        in_specs=in_specs,
        out_specs=out_specs,
        out_shape=out_shape,
        compiler_params=_params(("parallel", "parallel"), 56),
        name="inproj",
    )(*args)
    return dict(zip(names, outs)), list(outs[len(names):])


def _gla_kernel(*refs, seq, with_output, with_state):
    it = iter(refs)
    k_ref, v_ref, lg_hi, lg_lo, s0_ref = (next(it) for _ in range(5))
    if with_output:
        q_ref, r_ref, nw_ref = next(it), next(it), next(it)
        y_ref = next(it)
    if with_state:
        s_out_ref = next(it)
    st_ref = next(it)
    ds_scr = next(it)
    dec_scr = next(it)
    if with_output:
        o_scr = next(it)
        qst_scr = next(it)
        gain_scr = next(it)
    n_chunks = seq // CHUNK

    st_ref[...] = s0_ref[0]

    row = lax.broadcasted_iota(jnp.int32, (CHUNK, CHUNK), 0)
    col = lax.broadcasted_iota(jnp.int32, (CHUNK, CHUNK), 1)
    cum = (jnp.where(row >= col, 1.0, 0.0).astype(BF16), jnp.where(row <= col, 1.0, 0.0).astype(BF16))
    lane_head = lax.broadcasted_iota(jnp.int32, (1, K_W), 1) // DK
    head_mask = [lane_head == h for h in range(HEADS)]
    srow = lax.broadcasted_iota(jnp.int32, (CHUNK, HEADS * CHUNK), 0)
    scol = lax.broadcasted_iota(jnp.int32, (CHUNK, HEADS * CHUNK), 1) % CHUNK
    score_mask = (srow >= scol, srow <= scol)
    zero_v = jnp.zeros((CHUNK, DV), BF16)
    last_row = (CHUNK - 1, 0)
    mid_row = (CHUNK // 2 - 1, CHUNK // 2)

    def stack_heads(a):
        zero = jnp.zeros_like(a)
        return jnp.concatenate([jnp.where(head_mask[h], a, zero) for h in range(HEADS)], axis=0)

    def chunk_rows(n):
        return pl.ds(pl.multiple_of(n * CHUNK, CHUNK), CHUNK)

    per_iter = min(8, n_chunks)

    def chunk_local(i, carry):
        loaded = []
        for u in range(per_iter):
            n = i * per_iter + u
            rows = chunk_rows(n)
            k = k_ref[0, rows, :].astype(F32)
            v = v_ref[0, rows, :]
            v_heads = [v[:, h * DV:(h + 1) * DV] for h in range(HEADS)]
            v_rows = jnp.concatenate(v_heads, axis=0)
            v_diag = jnp.concatenate(
                [jnp.concatenate([v_heads[h] if hh == h else zero_v for hh in range(HEADS)], axis=1)
                 for h in range(HEADS)], axis=0) if with_output else None
            v = (v, v_rows, v_diag)
            qs = q_ref[0, rows, :].astype(F32) * (DK ** -0.5) if with_output else None
            lg = [(lg_hi[0, rows, d * K_W:(d + 1) * K_W], lg_lo[0, rows, d * K_W:(d + 1) * K_W]) for d in range(2)]
            loaded.append((n, rows, k, v, qs, lg))
        chains = [(n, rows, k, v, qs, lg[dirn], dirn) for n, rows, k, v, qs, lg in loaded for dirn in range(2)]
        stores = []
        if with_output:
            for n, rows, *_ in loaded:
                stores.append((gain_scr, (rows, slice(None)),
                               _silu(r_ref[0, rows, :].astype(F32)) * nw_ref[...]))
        gs = [_dot(cum[dirn], hi) + _dot(cum[dirn], lo) for *_, (hi, lo), dirn in chains]
        score_list = []
        for (n, rows, k, (v, v_rows, v_diag), qs, _, dirn), g in zip(chains, gs):
            g_last = g[last_row[dirn]:last_row[dirn] + 1]
            stores.append((dec_scr, (dirn, pl.ds(pl.multiple_of(n * 8, 8), 8), slice(None)),
                           jnp.broadcast_to(jnp.exp(g_last), (8, K_W))))
            if with_output:
                g_mid = g[mid_row[dirn]:mid_row[dirn] + 1]
                q_in = qs * jnp.exp(g - g_mid)
                k_in = k * jnp.exp(g_mid - g)
                score_list.append(_dot_nt(q_in.astype(BF16), stack_heads(k_in.astype(BF16))))
                stores.append((qst_scr, (dirn, rows, slice(None)), (q_in * jnp.exp(g_mid)).astype(BF16)))
                k_upd = stack_heads((k_in * jnp.exp(g_last - g_mid)).astype(BF16))
            else:
                k_upd = stack_heads((k * jnp.exp(g_last - g)).astype(BF16))
            stores.append((ds_scr, (dirn, pl.ds(pl.multiple_of(n * DV, DV), DV), slice(None)),
                           _dot_tn(v_rows, k_upd)))
        if with_output:
            for (n, rows, k, (v, v_rows, v_diag), qs, _, dirn), sc in zip(chains, score_list):
                scores = jnp.where(score_mask[dirn], sc, 0.0).astype(BF16)
                stores.append((o_scr, (dirn, rows, slice(None)), _dot(scores, v_diag)))
        for ref, idx, val in stores:
            ref[idx] = val
        return carry

    lax.fori_loop(0, n_chunks // per_iter, chunk_local, 0)

    def scan_step(i, finish):
        steps = ((0, i), (1, n_chunks - 1 - i))
        new_st, inters = [], []
        for dirn, n in steps:
            st = st_ref[dirn]
            if with_output:
                q_st = stack_heads(qst_scr[dirn, chunk_rows(n), :])
                inters.append(_dot_nt(q_st, st.astype(BF16)))
            dec = dec_scr[dirn, pl.ds(pl.multiple_of(n * 8, 8), 8), :]
            new_st.append(dec[0:1] * st + ds_scr[dirn, pl.ds(pl.multiple_of(n * DV, DV), DV), :])
        for (dirn, n), st in zip(steps, new_st):
            st_ref[dirn] = st
        if with_output:
            for (dirn, n), inter in zip(steps, inters):
                rows = chunk_rows(n)
                for h in range(HEADS):
                    vs = slice(h * DV, (h + 1) * DV)
                    o = o_scr[dirn, rows, vs] + inter[h * CHUNK:(h + 1) * CHUNK]
                    if finish:
                        o = o + o_scr[1 - dirn, rows, vs]
                        y_ref[0, rows, vs] = (_rms(o, gain_scr[rows, vs])).astype(BF16)
                    else:
                        o_scr[dirn, rows, vs] = o

    half = n_chunks // 2
    unroll = min(16, half)

    def scan_first(i, carry):
        scan_step(i, False)
        return carry

    def scan_second(i, carry):
        scan_step(i, True)
        return carry

    lax.fori_loop(0, half, scan_first, 0, unroll=unroll)
    lax.fori_loop(half, n_chunks, scan_second, 0, unroll=unroll)

    if with_state:
        s_out_ref[0] = st_ref[...]


def _gla(p, lay, i, s0, with_output, with_state):
    b, l, _ = p["k"].shape
    seq3 = lambda bi: (bi, 0, 0)
    in_specs = [pl.BlockSpec((1, l, K_W), seq3), pl.BlockSpec((1, l, V_W), seq3),
                pl.BlockSpec((1, l, 2 * K_W), seq3), pl.BlockSpec((1, l, 2 * K_W), seq3),
                pl.BlockSpec((1, 2, DV, K_W), lambda bi: (bi, 0, 0, 0))]
    args = [p["k"], p["v"], p["lg_hi"], p["lg_lo"], s0]
    out_specs, out_shape = [], []
    scratch = [pltpu.VMEM((2, DV, K_W), F32),
               pltpu.VMEM((2, l // CHUNK * DV, K_W), F32), pltpu.VMEM((2, l // CHUNK * 8, K_W), F32)]
    if with_output:
        in_specs += [pl.BlockSpec((1, l, K_W), seq3), pl.BlockSpec((1, l, V_W), seq3),
                     _layer(lay["gla_norm_w"], i)]
        args += [p["q"], p["r"], lay["gla_norm_w"]]
        out_specs.append(pl.BlockSpec((1, l, V_W), seq3))
        out_shape.append(jax.ShapeDtypeStruct((b, l, V_W), BF16))
        scratch += [pltpu.VMEM((2, l, V_W), F32), pltpu.VMEM((2, l, K_W), BF16), pltpu.VMEM((l, V_W), F32)]
    if with_state:
        out_specs.append(pl.BlockSpec((1, 2, DV, K_W), lambda bi: (bi, 0, 0, 0)))
        out_shape.append(jax.ShapeDtypeStruct((b, 2, DV, K_W), F32))
    outs = pl.pallas_call(
        functools.partial(_gla_kernel, seq=l, with_output=with_output, with_state=with_state),
        grid=(b,),
        in_specs=in_specs,
        out_specs=out_specs,
        out_shape=out_shape,
        scratch_shapes=scratch,
        compiler_params=_params(("parallel",), 56),
        name="gla",
    )(*args)
    outs = list(outs)
    y = outs.pop(0) if with_output else None
    s = outs.pop(0) if with_state else None
    return y, s


def _fourier_kernel(uf_ref, chan_ref, pos_ref, yf_ref, ab_scr, *, seq):
    ab = _dot(uf_ref[0], chan_ref[...])
    ab_scr[0:seq, :] = ab[:, :FOUR_W].astype(BF16)
    ab_scr[seq:2 * seq, :] = ab[:, FOUR_W:].astype(BF16)
    rb = min(seq, 512)
    for i in range(seq // rb):
        yf_ref[0, i * rb:(i + 1) * rb, :] = _dot(pos_ref[i * rb:(i + 1) * rb, :], ab_scr[...]).astype(BF16)


FFT_RADIX = 8


def _fourier_fft_kernel(uf_ref, chan_ref, sub_ref, twc_ref, tws_ref, yf_ref, uf_scr, *, seq):
    n = seq // FFT_RADIX
    lanes = uf_scr.shape[2]
    n_lane_tiles = FOUR_W // lanes
    for j in range(n_lane_tiles):
        uf_scr[j] = uf_ref[0, :, j * lanes:(j + 1) * lanes].astype(F32)

    def tokens(r):
        parts = [uf_scr[j, pl.ds(r, n, stride=FFT_RADIX), :] for j in range(n_lane_tiles)]
        return jnp.concatenate(parts, axis=1).astype(BF16)

    ab = [_dot(tokens(r), chan_ref[...]) for r in range(FFT_RADIX)]
    pq = []
    for ab_r in ab:
        a, b = ab_r[:, :FOUR_W].astype(BF16), ab_r[:, FOUR_W:].astype(BF16)
        w = jnp.concatenate([jnp.concatenate([a, b], axis=1), jnp.concatenate([-b, a], axis=1)], axis=0)
        pq.append(_dot(sub_ref[...], w))
    re, im = [pq[0][:, :FOUR_W]], [pq[0][:, FOUR_W:]]
    for r in range(1, FFT_RADIX):
        p, q = pq[r][:, :FOUR_W], pq[r][:, FOUR_W:]
        c, s = twc_ref[r - 1], tws_ref[r - 1]
        re.append(c * p - s * q)
        im.append(s * p + c * q)

    def quarter(z0, z1, z2, z3):
        s02 = (z0[0] + z2[0], z0[1] + z2[1])
        d02 = (z0[0] - z2[0], z0[1] - z2[1])
        s13 = (z1[0] + z3[0], z1[1] + z3[1])
        d13 = (z1[0] - z3[0], z1[1] - z3[1])
        return ((s02[0] + s13[0], s02[1] + s13[1]), (d02[0] - d13[1], d02[1] + d13[0]),
                (s02[0] - s13[0], s02[1] - s13[1]), (d02[0] + d13[1], d02[1] - d13[0]))

    zs = list(zip(re, im))
    even = quarter(zs[0], zs[2], zs[4], zs[6])
    odd = quarter(zs[1], zs[3], zs[5], zs[7])
    half = 0.5 ** 0.5
    odd_re = (odd[0][0], (odd[1][0] - odd[1][1]) * half, -odd[2][1], -(odd[3][0] + odd[3][1]) * half)
    for q in range(4):
        yf_ref[0, q * n:(q + 1) * n, :] = (even[q][0] + odd_re[q]).astype(BF16)
        yf_ref[0, (q + 4) * n:(q + 5) * n, :] = (even[q][0] - odd_re[q]).astype(BF16)


def _fourier(uf, tabs):
    b, l, _ = uf.shape
    tok = pl.BlockSpec((1, l, FOUR_W), lambda bi: (bi, 0, 0))
    if len(tabs) == 4:
        body, name = _fourier_fft_kernel, "fourier_fft"
        scratch = pltpu.VMEM((FOUR_W // 128, l, 128), F32)
    else:
        body, name = _fourier_kernel, "fourier"
        scratch = pltpu.VMEM((2 * l, FOUR_W), BF16)
    return pl.pallas_call(
        functools.partial(body, seq=l),
        grid=(b,),
        in_specs=[tok] + [_resident(t.shape) for t in tabs],
        out_specs=tok,
        out_shape=jax.ShapeDtypeStruct((b, l, FOUR_W), BF16),
        scratch_shapes=[scratch],
        compiler_params=_params(("parallel",), 48),
        name=name,
    )(uf, *tabs)


_FFN_TILES = ((0, 1536), (1536, HIDDEN))


CONV_PAD = 8
HALO_ROWS = 16


def _out_ffn_kernel(*refs, add_pos, n_sub):
    it = iter(refs)
    yg_ref, yf_ref, bg_ref, cg_ref, uc_ref = (next(it) for _ in range(5))
    cg_prev_ref, uc_prev_ref, cg_next_ref, uc_next_ref, cw_ref, cb_ref = (next(it) for _ in range(6))
    wo_ref, x_ref = next(it), next(it)
    pos_ref = next(it) if add_pos else None
    gmix_ref, gatemix_ref, gpre_ref, shift_ref, scale_ref, gate_ref, gpost_ref = (next(it) for _ in range(7))
    win_ref, wout_ref, o_ref, z_scr = next(it), next(it), next(it), next(it)
    tm = x_ref.shape[1]
    subs = [slice(i * tm // n_sub, (i + 1) * tm // n_sub) for i in range(n_sub)]

    t, nt = pl.program_id(1), pl.num_programs(1)
    z_before = (cg_prev_ref[0].astype(F32) * uc_prev_ref[0].astype(F32))[HALO_ROWS - CONV_PAD:]
    z_after = (cg_next_ref[0].astype(F32) * uc_next_ref[0].astype(F32))[:CONV_PAD]
    z_scr[0:CONV_PAD, :] = jnp.where(t > 0, z_before, 0.0)
    z_scr[CONV_PAD + tm:, :] = jnp.where(t < nt - 1, z_after, 0.0)
    z_scr[CONV_PAD:CONV_PAD + tm, :] = cg_ref[0].astype(F32) * uc_ref[0].astype(F32)
    ycs = []
    for r in subs:
        s, n = CONV_PAD + r.start, r.stop - r.start
        y = (cw_ref[0:1] * z_scr[s - 1:s - 1 + n, :] + cw_ref[1:2] * z_scr[s:s + n, :]
             + cw_ref[2:3] * z_scr[s + 1:s + 1 + n, :] + cb_ref[...])
        ycs.append((bg_ref[0, r, :].astype(F32) * y).astype(BF16))

    ys = [(_dot(yg_ref[0, r, :], wo_ref[0:V_W]) + _dot(yf_ref[0, r, :], wo_ref[V_W:V_W + FOUR_W])
           + _dot(yc, wo_ref[V_W + FOUR_W:])) for r, yc in zip(subs, ycs)]
    xs, hs = [], []
    for r, y in zip(subs, ys):
        x = x_ref[0, r, :]
        if add_pos:
            x = x + pos_ref[r, :]
        x = x + gatemix_ref[...] * _rms(y, gmix_ref[...])
        xs.append(x)
        hs.append((_rms(x, gpre_ref[...]) * (1.0 + scale_ref[...]) + shift_ref[...]).astype(BF16))
    ys = []
    for h in hs:
        y = jnp.zeros((h.shape[0], D_MODEL), F32)
        for lo, hi in _FFN_TILES:
            a = _dot(h, win_ref[:, lo:hi])
            u = _dot(h, win_ref[:, HIDDEN + lo:HIDDEN + hi])
            y = y + _dot((_silu(a) * u).astype(BF16), wout_ref[lo:hi, :])
        ys.append(y)
    for r, x, y in zip(subs, xs, ys):
        o_ref[0, r, :] = x + gate_ref[...] * _rms(y, gpost_ref[...])


def _out_ffn(yg, yf, p, x, pos, lay, i, wts, mod_row, tm):
    w_out, w_ffn_in, w_ffn_out = wts
    b, l, d = x.shape
    add_pos = pos is not None
    tok = lambda bi, ti: (bi, ti, 0)
    halo_per_tile, n_halo = tm // HALO_ROWS, l // HALO_ROWS
    before = lambda bi, ti: (bi, jnp.maximum(ti * halo_per_tile - 1, 0), 0)
    after = lambda bi, ti: (bi, jnp.minimum((ti + 1) * halo_per_tile, n_halo - 1), 0)
    conv_tok = pl.BlockSpec((1, tm, CONV_W), tok)
    in_specs = [pl.BlockSpec((1, tm, V_W), tok), pl.BlockSpec((1, tm, FOUR_W), tok), conv_tok, conv_tok, conv_tok,
                pl.BlockSpec((1, HALO_ROWS, CONV_W), before), pl.BlockSpec((1, HALO_ROWS, CONV_W), before),
                pl.BlockSpec((1, HALO_ROWS, CONV_W), after), pl.BlockSpec((1, HALO_ROWS, CONV_W), after),
                _layer(lay["conv_w"], i), _layer(lay["conv_b"], i),
                _layer(w_out, 0), pl.BlockSpec((1, tm, d), tok)]
    args = [yg, yf, p["bg"], p["cg"], p["uc"], p["cg"], p["uc"], p["cg"], p["uc"],
            lay["conv_w"], lay["conv_b"], w_out, x]
    if add_pos:
        in_specs.append(pl.BlockSpec((tm, d), lambda bi, ti: (ti, 0)))
        args.append(pos)
    in_specs += [_layer(lay["g_mix_post"], i), _mod_spec(i, 2, mod_row), _layer(lay["g_ffn_pre"], i),
                 _mod_spec(i, 3, mod_row), _mod_spec(i, 4, mod_row), _mod_spec(i, 5, mod_row),
                 _layer(lay["g_ffn_post"], i), _layer(w_ffn_in, 0), _layer(w_ffn_out, 0)]
    args += [lay["g_mix_post"], lay["mod"], lay["g_ffn_pre"], lay["mod"], lay["mod"], lay["mod"],
             lay["g_ffn_post"], w_ffn_in, w_ffn_out]
    return pl.pallas_call(
        functools.partial(_out_ffn_kernel, add_pos=add_pos, n_sub=2 if tm >= 512 else 1),
        grid=(b, l // tm),
        in_specs=in_specs,
        out_specs=pl.BlockSpec((1, tm, d), tok),
        out_shape=jax.ShapeDtypeStruct((b, l, d), F32),
        scratch_shapes=[pltpu.VMEM((tm + 2 * CONV_PAD, CONV_W), F32)],
        compiler_params=_params(("parallel", "parallel"), 56),
        name="out_ffn",
    )(*args)


def _arrange_w_dec(w_f, w_b):
    wd = jnp.zeros((DEPTH, LR_PAD, 2 * K_W), F32)
    wd = wd.at[:, 0:RANK, :K_W].set(w_f).at[:, RANK:2 * RANK, K_W:].set(w_b)
    return wd.astype(BF16)


def kernel(x, c, ctx, c_ctx, w_mod, b_mod, g_mix_pre, g_mix_post, g_ffn_pre, g_ffn_post,
           w_in, w_dec_f, b_dec_f, w_dec_b, b_dec_b, gla_norm_w, conv_w, conv_b, w_out,
           w_ffn_in, w_ffn_out):
    b, n_lat, d = x.shape
    n_ctx = ctx.shape[1]
    assert d == D_MODEL and w_in.shape == (DEPTH, D_MODEL, _C_END)
    assert n_lat % 512 == 0 and n_ctx % CHUNK == 0 and b + 1 <= MOD_ROWS

    pos = jnp.asarray(_pos_embed(n_lat, d))
    tabs = {n_lat: _dft_tables(n_lat, FFT_RADIX), n_ctx: _dft_tables(n_ctx, 1)}

    cvec = jnp.zeros((MOD_ROWS, d), F32).at[:b].set(c).at[b].set(c_ctx)
    ctx_row = b
    mod, w_in_arranged = _modulation(cvec, w_mod, b_mod, w_in)
    lay = {
        "mod": mod.reshape(DEPTH, MOD_ROWS, 1, N_MOD * d),
        "w_in": w_in_arranged,
        "w_dec": _arrange_w_dec(w_dec_f, w_dec_b),
        "b_dec": jnp.concatenate([b_dec_f, b_dec_b], axis=-1).reshape(DEPTH, 1, 2 * K_W),
        "gla_norm_w": gla_norm_w.reshape(DEPTH, 1, V_W),
        "conv_w": conv_w,
        "conv_b": conv_b.reshape(DEPTH, 1, CONV_W),
        "g_mix_pre": g_mix_pre.reshape(DEPTH, 1, d),
        "g_mix_post": g_mix_post.reshape(DEPTH, 1, d),
        "g_ffn_pre": g_ffn_pre.reshape(DEPTH, 1, d),
        "g_ffn_post": g_ffn_post.reshape(DEPTH, 1, d),
    }

    def mix_and_ffn(i, xs, pos_s, p, wts, mod_row, s0, want_state, tm):
        yg, s = _gla(p, lay, i, s0, True, want_state)
        yf = _fourier(p["uf"], tabs[xs.shape[1]])
        return _out_ffn(yg, yf, p, xs, pos_s, lay, i, wts, mod_row, tm), s

    zero_state = jnp.zeros((b, 2, DV, K_W), F32)
    xc = ctx
    tm = 512
    for i in range(DEPTH):
        pos_i = pos if i == 0 else None
        p_lat, wts = _inproj(x, pos_i, lay, i, None, _OUT_GROUPS_FULL, 2 * tm, cast=(w_out, w_ffn_in, w_ffn_out))
        if i == DEPTH - 1:
            p_ctx, _ = _inproj(xc, None, lay, i, ctx_row, _OUT_GROUPS_STATE, n_ctx)
            _, s = _gla(p_ctx, lay, i, zero_state, False, True)
        else:
            p_ctx, _ = _inproj(xc, None, lay, i, ctx_row, _OUT_GROUPS_FULL, n_ctx)
            xc, s = mix_and_ffn(i, xc, None, p_ctx, wts, ctx_row, zero_state, True, n_ctx)
        x, _ = mix_and_ffn(i, x, pos_i, p_lat, wts, None, s, False, tm)
    return x
```

```python
import functools

import numpy as np
import jax
import jax.numpy as jnp
from jax import lax
from jax.experimental import pallas as pl
from jax.experimental.pallas import tpu as pltpu

F32 = jnp.float32
BF16 = jnp.bfloat16

D_MODEL = 1024
DEPTH = 2
GRID_W = 64
N_MOD = 6
HEADS = 4
DK = 64
DV = 128
K_W = HEADS * DK
V_W = HEADS * DV
RANK = 16
TAU = 16.0
CHUNK = 64
FOUR_W = 256
FOUR_G = 4
CONV_W = 256
HIDDEN = 2816
EPS = 1e-6

V7X_LANES = 128
V7X_SUBLANES_F32 = 8
V7X_SUBLANES_BF16 = 16
V7X_VMEM_BYTES = 64 * 1024 * 1024

LR_PAD = V7X_LANES
MOD_ROWS = V7X_SUBLANES_BF16
TOKEN_TILE = 512
FFN_TILES = ((0, 1536), (1536, HIDDEN))
VMEM_MB = {"modulation": 40, "inproj": 56, "gla": 56, "fourier": 48, "out_ffn": 56}

_C_K, _C_V, _C_LR, _C_Q, _C_R, _C_FOUR, _C_BG, _C_CG, _C_UC, _C_END = (
    0, 256, 768, 800, 1056, 1568, 1824, 2080, 2336, 2592)
_GROUP_WIDTHS = (("k", K_W), ("v", V_W), ("q", K_W), ("r", V_W), ("uf", FOUR_W), ("bg", CONV_W),
                 ("cg", CONV_W), ("uc", CONV_W))
_MAIN_W = sum(w for _, w in _GROUP_WIDTHS)
_IN_W = _MAIN_W + LR_PAD


def _groups(names):
    out, lo = [], 0
    for name, w in _GROUP_WIDTHS:
        if name in names:
            out.append((name, lo, lo + w))
        lo += w
    return tuple(out)


_OUT_GROUPS_FULL = _groups([name for name, _ in _GROUP_WIDTHS])
_OUT_GROUPS_STATE = _groups(("k", "v"))


def _params(semantics, name):
    assert VMEM_MB[name] * 1024 * 1024 < V7X_VMEM_BYTES
    return pltpu.CompilerParams(dimension_semantics=semantics,
                                vmem_limit_bytes=VMEM_MB[name] * 1024 * 1024)


def _resident(shape):
    zeros = (0,) * len(shape)
    return pl.BlockSpec(shape, lambda *_: zeros, pipeline_mode=pl.Buffered(1))


def _layer(arr, i):
    shape = tuple(arr.shape[1:])
    zeros = (0,) * len(shape)
    return pl.BlockSpec((None,) + shape, lambda *_: (i,) + zeros, pipeline_mode=pl.Buffered(1))


def _mod_spec(i, j, row):
    if row is None:
        return pl.BlockSpec((None, None, 1, D_MODEL), lambda bi, *_: (i, bi, 0, j))
    return pl.BlockSpec((None, None, 1, D_MODEL), lambda *_: (i, row, 0, j))


def _row(ref, layer):
    return ref[layer:layer + 1, :]


def _silu(a):
    return a / (1.0 + jnp.exp(-a))


def _rms(x, g):
    return x * lax.rsqrt(jnp.mean(x * x, axis=-1, keepdims=True) + EPS) * g


def _dot(a, b):
    return jnp.dot(a, b, preferred_element_type=F32)


def _dot_nt(a, b):
    return lax.dot_general(a, b, (((1,), (1,)), ((), ())), preferred_element_type=F32)


def _dot_tn(a, b):
    return lax.dot_general(a, b, (((0,), (0,)), ((), ())), preferred_element_type=F32)


def _pos_embed(n_tokens, dim):
    rows = n_tokens // GRID_W
    row = np.repeat(np.arange(rows, dtype=np.float32), GRID_W)
    col = np.tile(np.arange(GRID_W, dtype=np.float32), rows)
    quarter = dim // 4
    freqs = (1.0 / (10000.0 ** (np.arange(quarter, dtype=np.float32) / quarter))).astype(np.float32)

    def enc(p):
        ang = (p[:, None] * freqs[None, :]).astype(np.float32)
        return np.concatenate([np.sin(ang), np.cos(ang)], axis=-1)

    return np.concatenate([enc(row), enc(col)], axis=-1).astype(np.float32)


def _dft_cos_sin(n):
    idx = np.arange(n, dtype=np.int64)
    ang = 2.0 * np.pi * ((idx[:, None] * idx[None, :]) % n).astype(np.float64) / n
    return np.cos(ang) / np.sqrt(n), np.sin(ang) / np.sqrt(n)


def _dft_tables(seq, radix):
    gw = FOUR_W // FOUR_G
    cc, sc = _dft_cos_sin(gw)
    eye = np.eye(FOUR_G)
    chan_tab = jnp.asarray(np.concatenate([np.kron(eye, cc), np.kron(eye, sc)], axis=1), dtype=F32).astype(BF16)
    if radix == 1:
        cl, sl = _dft_cos_sin(seq)
        return chan_tab, jnp.asarray(np.concatenate([cl, -sl], axis=1), dtype=F32).astype(BF16)
    n = seq // radix
    cn, sn = _dft_cos_sin(n)
    sub_tab = np.concatenate([cn, sn], axis=1) * np.sqrt(n / seq)
    ang = 2.0 * np.pi * (np.arange(1, radix)[:, None] * np.arange(n)[None, :]).astype(np.float64) / seq
    twc = np.broadcast_to(np.cos(ang)[:, :, None], (radix - 1, n, FOUR_W))
    tws = np.broadcast_to(np.sin(ang)[:, :, None], (radix - 1, n, FOUR_W))
    return (chan_tab, jnp.asarray(sub_tab, dtype=F32).astype(BF16),
            jnp.asarray(twc, dtype=F32), jnp.asarray(tws, dtype=F32))


def _mod_kernel(c_ref, w_ref, b_ref, win_ref, o_ref, wout_ref):
    s = _silu(c_ref[...]).astype(BF16)
    o_ref[0] = _dot(s, w_ref[0].astype(BF16)) + b_ref[0]
    w = win_ref[0]
    wout_ref[0, :, 0:_C_LR] = w[:, _C_K:_C_LR].astype(BF16)
    wout_ref[0, :, _C_LR:_MAIN_W] = w[:, _C_Q:_C_END].astype(BF16)
    lr = jnp.concatenate([w[:, _C_LR:_C_Q], jnp.zeros((w.shape[0], LR_PAD - 2 * RANK), F32)], axis=1)
    wout_ref[0, :, _MAIN_W:_IN_W] = lr.astype(BF16)


def _modulation(cvec, w_mod, b_mod, w_in):
    steps = 4
    n = N_MOD * D_MODEL
    tn = n // steps
    slab = D_MODEL // steps
    assert tn % V7X_LANES == 0 and slab % V7X_SUBLANES_BF16 == 0
    return pl.pallas_call(
        _mod_kernel,
        grid=(DEPTH, steps),
        in_specs=[pl.BlockSpec((MOD_ROWS, D_MODEL), lambda i, j: (0, 0)),
                  pl.BlockSpec((1, D_MODEL, tn), lambda i, j: (i, 0, j)),
                  pl.BlockSpec((1, 1, tn), lambda i, j: (i, 0, j)),
                  pl.BlockSpec((1, slab, _C_END), lambda i, j: (i, j, 0))],
        out_specs=[pl.BlockSpec((1, MOD_ROWS, tn), lambda i, j: (i, 0, j)),
                   pl.BlockSpec((1, slab, _IN_W), lambda i, j: (i, j, 0))],
        out_shape=[jax.ShapeDtypeStruct((DEPTH, MOD_ROWS, n), F32),
                   jax.ShapeDtypeStruct((DEPTH, D_MODEL, _IN_W), BF16)],
        compiler_params=_params(("arbitrary", "arbitrary"), "modulation"),
        name="modulation",
    )(cvec, w_mod, b_mod.reshape(DEPTH, 1, n), w_in)


def _inproj_kernel(*refs, layer, add_pos, groups, n_cast):
    it = iter(refs)
    x_ref = next(it)
    pos_ref = next(it) if add_pos else None
    g_ref, shift_ref, scale_ref, w_ref, wd_ref, bd_ref = (next(it) for _ in range(6))
    cast_in = [next(it) for _ in range(n_cast)]
    out_refs = list(it)
    for src, dst in zip(cast_in, out_refs[len(out_refs) - n_cast:]):
        dst[0] = src[...].astype(BF16)
    out_refs = out_refs[:len(out_refs) - n_cast]
    lg_hi_ref, lg_lo_ref = out_refs[-2:]
    tm = x_ref.shape[1]
    n_sub = 2 if tm >= 512 else 1
    subs = [slice(i * tm // n_sub, (i + 1) * tm // n_sub) for i in range(n_sub)]
    hs = []
    for r in subs:
        x = x_ref[0, r, :]
        if add_pos:
            x = x + pos_ref[r, :]
        hs.append((_rms(x, _row(g_ref, layer)) * (1.0 + scale_ref[...]) + shift_ref[...]).astype(BF16))
    def project(r, h):
        for o_ref, (_, lo, hi_col) in zip(out_refs, groups):
            o_ref[0, r, :] = _dot(h, w_ref[:, lo:hi_col]).astype(BF16)

    low_rank = [_dot(h, w_ref[:, _MAIN_W:_IN_W]).astype(BF16) for h in hs]
    project(subs[0], hs[0])
    zs = [_dot(lr, wd_ref[...]) + bd_ref[...] for lr in low_rank]
    for r, h in list(zip(subs, hs))[1:]:
        project(r, h)
    for r, z in zip(subs, zs):
        logg = (jnp.minimum(z, 0.0) - jnp.log(1.0 + jnp.exp(-jnp.abs(z)))) * (1.0 / TAU)
        hi = logg.astype(BF16)
        lg_hi_ref[0, r, :] = hi
        lg_lo_ref[0, r, :] = (logg - hi.astype(F32)).astype(BF16)


def _inproj(x, pos, lay, i, mod_row, groups, tm, cast=()):
    b, l, d = x.shape
    add_pos = pos is not None
    n_t = l // tm
    tok = lambda bi, ti: (bi, ti, 0)
    in_specs = [pl.BlockSpec((1, tm, d), tok)]
    args = [x]
    if add_pos:
        in_specs.append(pl.BlockSpec((tm, d), lambda bi, ti: (ti, 0)))
        args.append(pos)
    in_specs += [_resident(lay["g_mix_pre"].shape), _mod_spec(i, 0, mod_row), _mod_spec(i, 1, mod_row),
                 _layer(lay["w_in"], i), _layer(lay["w_dec"], i), _layer(lay["b_dec"], i)]
    args += [lay["g_mix_pre"], lay["mod"], lay["mod"], lay["w_in"], lay["w_dec"], lay["b_dec"]]
    widths = [hi - lo for _, lo, hi in groups] + [2 * K_W, 2 * K_W]
    names = [name for name, _, _ in groups] + ["lg_hi", "lg_lo"]
    out_specs = [pl.BlockSpec((1, tm, w), tok) for w in widths]
    out_shape = [jax.ShapeDtypeStruct((b, l, w), BF16) for w in widths]
    for w in cast:
        _, rows, cols = w.shape
        slab = rows // (b * n_t)
        assert slab * b * n_t == rows and slab % V7X_SUBLANES_BF16 == 0
        in_specs.append(pl.BlockSpec((None, slab, cols), lambda bi, ti: (i, bi * n_t + ti, 0)))
        out_specs.append(pl.BlockSpec((1, slab, cols), lambda bi, ti: (0, bi * n_t + ti, 0)))
        out_shape.append(jax.ShapeDtypeStruct((1, rows, cols), BF16))
        args.append(w)
    outs = pl.pallas_call(
        functools.partial(_inproj_kernel, layer=i, add_pos=add_pos, groups=groups, n_cast=len(cast)),
        grid=(b, n_t),
        in_specs=in_specs,
        out_specs=out_specs,
        out_shape=out_shape,
        compiler_params=_params(("parallel", "parallel"), "inproj"),
        name="inproj",
    )(*args)
    return dict(zip(names, outs)), list(outs[len(names):])


def _gla_kernel(*refs, layer, seq, with_output, with_state):
    it = iter(refs)
    k_ref, v_ref, lg_hi, lg_lo, s0_ref = (next(it) for _ in range(5))
    if with_output:
        q_ref, r_ref, nw_ref = next(it), next(it), next(it)
        y_ref = next(it)
    if with_state:
        s_out_ref = next(it)
    st_ref = next(it)
    ds_scr = next(it)
    dec_scr = next(it)
    if with_output:
        o_scr = next(it)
        qst_scr = next(it)
        gain_scr = next(it)
    n_chunks = seq // CHUNK

    st_ref[...] = s0_ref[0]

    row = lax.broadcasted_iota(jnp.int32, (CHUNK, CHUNK), 0)
    col = lax.broadcasted_iota(jnp.int32, (CHUNK, CHUNK), 1)
    cum = (jnp.where(row >= col, 1.0, 0.0).astype(BF16), jnp.where(row <= col, 1.0, 0.0).astype(BF16))
    lane_head = lax.broadcasted_iota(jnp.int32, (1, K_W), 1) // DK
    head_mask = [lane_head == h for h in range(HEADS)]
    srow = lax.broadcasted_iota(jnp.int32, (CHUNK, HEADS * CHUNK), 0)
    scol = lax.broadcasted_iota(jnp.int32, (CHUNK, HEADS * CHUNK), 1) % CHUNK
    score_mask = (srow >= scol, srow <= scol)
    zero_v = jnp.zeros((CHUNK, DV), BF16)
    last_row = (CHUNK - 1, 0)
    mid_row = (CHUNK // 2 - 1, CHUNK // 2)

    def stack_heads(a):
        zero = jnp.zeros_like(a)
        return jnp.concatenate([jnp.where(head_mask[h], a, zero) for h in range(HEADS)], axis=0)

    def chunk_rows(n):
        return pl.ds(pl.multiple_of(n * CHUNK, CHUNK), CHUNK)

    per_iter = min(8, n_chunks)

    def chunk_local(i, carry):
        loaded = []
        for u in range(per_iter):
            n = i * per_iter + u
            rows = chunk_rows(n)
            k = k_ref[0, rows, :].astype(F32)
            v = v_ref[0, rows, :]
            v_heads = [v[:, h * DV:(h + 1) * DV] for h in range(HEADS)]
            v_rows = jnp.concatenate(v_heads, axis=0)
            v_diag = jnp.concatenate(
                [jnp.concatenate([v_heads[h] if hh == h else zero_v for hh in range(HEADS)], axis=1)
                 for h in range(HEADS)], axis=0) if with_output else None
            v = (v, v_rows, v_diag)
            qs = q_ref[0, rows, :].astype(F32) * (DK ** -0.5) if with_output else None
            lg = [(lg_hi[0, rows, d * K_W:(d + 1) * K_W], lg_lo[0, rows, d * K_W:(d + 1) * K_W]) for d in range(2)]
            loaded.append((n, rows, k, v, qs, lg))
        chains = [(n, rows, k, v, qs, lg[dirn], dirn) for n, rows, k, v, qs, lg in loaded for dirn in range(2)]
        stores = []
        if with_output:
            for n, rows, *_ in loaded:
                stores.append((gain_scr, (rows, slice(None)),
                               _silu(r_ref[0, rows, :].astype(F32)) * _row(nw_ref, layer)))
        gs = [_dot(cum[dirn], hi) + _dot(cum[dirn], lo) for *_, (hi, lo), dirn in chains]
        score_list = []
        for (n, rows, k, (v, v_rows, v_diag), qs, _, dirn), g in zip(chains, gs):
            g_last = g[last_row[dirn]:last_row[dirn] + 1]
            stores.append((dec_scr, (dirn, pl.ds(pl.multiple_of(n * 8, 8), 8), slice(None)),
                           jnp.broadcast_to(jnp.exp(g_last), (8, K_W))))
            if with_output:
                g_mid = g[mid_row[dirn]:mid_row[dirn] + 1]
                q_in = qs * jnp.exp(g - g_mid)
                k_in = k * jnp.exp(g_mid - g)
                score_list.append(_dot_nt(q_in.astype(BF16), stack_heads(k_in.astype(BF16))))
                stores.append((qst_scr, (dirn, rows, slice(None)), (q_in * jnp.exp(g_mid)).astype(BF16)))
                k_upd = stack_heads((k_in * jnp.exp(g_last - g_mid)).astype(BF16))
            else:
                k_upd = stack_heads((k * jnp.exp(g_last - g)).astype(BF16))
            stores.append((ds_scr, (dirn, pl.ds(pl.multiple_of(n * DV, DV), DV), slice(None)),
                           _dot_tn(v_rows, k_upd)))
        if with_output:
            for (n, rows, k, (v, v_rows, v_diag), qs, _, dirn), sc in zip(chains, score_list):
                scores = jnp.where(score_mask[dirn], sc, 0.0).astype(BF16)
                stores.append((o_scr, (dirn, rows, slice(None)), _dot(scores, v_diag)))
        for ref, idx, val in stores:
            ref[idx] = val
        return carry

    lax.fori_loop(0, n_chunks // per_iter, chunk_local, 0)

    def scan_step(i, finish):
        steps = ((0, i), (1, n_chunks - 1 - i))
        new_st, inters = [], []
        for dirn, n in steps:
            st = st_ref[dirn]
            if with_output:
                q_st = stack_heads(qst_scr[dirn, chunk_rows(n), :])
                inters.append(_dot_nt(q_st, st.astype(BF16)))
            dec = dec_scr[dirn, pl.ds(pl.multiple_of(n * 8, 8), 8), :]
            new_st.append(dec[0:1] * st + ds_scr[dirn, pl.ds(pl.multiple_of(n * DV, DV), DV), :])
        for (dirn, n), st in zip(steps, new_st):
            st_ref[dirn] = st
        if with_output:
            for (dirn, n), inter in zip(steps, inters):
                rows = chunk_rows(n)
                for h in range(HEADS):
                    vs = slice(h * DV, (h + 1) * DV)
                    o = o_scr[dirn, rows, vs] + inter[h * CHUNK:(h + 1) * CHUNK]
                    if finish:
                        o = o + o_scr[1 - dirn, rows, vs]
                        y_ref[0, rows, vs] = (_rms(o, gain_scr[rows, vs])).astype(BF16)
                    else:
                        o_scr[dirn, rows, vs] = o

    half = n_chunks // 2
    unroll = min(16, half)

    def scan_first(i, carry):
        scan_step(i, False)
        return carry

    def scan_second(i, carry):
        scan_step(i, True)
        return carry

    lax.fori_loop(0, half, scan_first, 0, unroll=unroll)
    lax.fori_loop(half, n_chunks, scan_second, 0, unroll=unroll)

    if with_state:
        s_out_ref[0] = st_ref[...]


def _gla(p, lay, i, s0, with_output, with_state):
    b, l, _ = p["k"].shape
    seq3 = lambda bi: (bi, 0, 0)
    in_specs = [pl.BlockSpec((1, l, K_W), seq3), pl.BlockSpec((1, l, V_W), seq3),
                pl.BlockSpec((1, l, 2 * K_W), seq3), pl.BlockSpec((1, l, 2 * K_W), seq3),
                pl.BlockSpec((1, 2, DV, K_W), lambda bi: (bi, 0, 0, 0))]
    args = [p["k"], p["v"], p["lg_hi"], p["lg_lo"], s0]
    out_specs, out_shape = [], []
    scratch = [pltpu.VMEM((2, DV, K_W), F32),
               pltpu.VMEM((2, l // CHUNK * DV, K_W), F32), pltpu.VMEM((2, l // CHUNK * 8, K_W), F32)]
    if with_output:
        in_specs += [pl.BlockSpec((1, l, K_W), seq3), pl.BlockSpec((1, l, V_W), seq3),
                     _resident(lay["gla_norm_w"].shape)]
        args += [p["q"], p["r"], lay["gla_norm_w"]]
        out_specs.append(pl.BlockSpec((1, l, V_W), seq3))
        out_shape.append(jax.ShapeDtypeStruct((b, l, V_W), BF16))
        scratch += [pltpu.VMEM((2, l, V_W), F32), pltpu.VMEM((2, l, K_W), BF16), pltpu.VMEM((l, V_W), F32)]
    if with_state:
        out_specs.append(pl.BlockSpec((1, 2, DV, K_W), lambda bi: (bi, 0, 0, 0)))
        out_shape.append(jax.ShapeDtypeStruct((b, 2, DV, K_W), F32))
    outs = pl.pallas_call(
        functools.partial(_gla_kernel, layer=i, seq=l, with_output=with_output, with_state=with_state),
        grid=(b,),
        in_specs=in_specs,
        out_specs=out_specs,
        out_shape=out_shape,
        scratch_shapes=scratch,
        compiler_params=_params(("parallel",), "gla"),
        name="gla",
    )(*args)
    outs = list(outs)
    y = outs.pop(0) if with_output else None
    s = outs.pop(0) if with_state else None
    return y, s


def _fourier_kernel(uf_ref, chan_ref, pos_ref, yf_ref, ab_scr, *, seq):
    ab = _dot(uf_ref[0], chan_ref[...])
    ab_scr[0:seq, :] = ab[:, :FOUR_W].astype(BF16)
    ab_scr[seq:2 * seq, :] = ab[:, FOUR_W:].astype(BF16)
    rb = min(seq, 512)
    for i in range(seq // rb):
        yf_ref[0, i * rb:(i + 1) * rb, :] = _dot(pos_ref[i * rb:(i + 1) * rb, :], ab_scr[...]).astype(BF16)


FFT_RADIX = 8


def _fourier_fft_kernel(uf_ref, chan_ref, sub_ref, twc_ref, tws_ref, yf_ref, uf_scr, *, seq):
    n = seq // FFT_RADIX
    lanes = uf_scr.shape[2]
    n_lane_tiles = FOUR_W // lanes
    for j in range(n_lane_tiles):
        uf_scr[j] = uf_ref[0, :, j * lanes:(j + 1) * lanes].astype(F32)

    def tokens(r):
        parts = [uf_scr[j, pl.ds(r, n, stride=FFT_RADIX), :] for j in range(n_lane_tiles)]
        return jnp.concatenate(parts, axis=1).astype(BF16)

    ab = [_dot(tokens(r), chan_ref[...]) for r in range(FFT_RADIX)]
    pq = []
    for ab_r in ab:
        a, b = ab_r[:, :FOUR_W].astype(BF16), ab_r[:, FOUR_W:].astype(BF16)
        w = jnp.concatenate([jnp.concatenate([a, b], axis=1), jnp.concatenate([-b, a], axis=1)], axis=0)
        pq.append(_dot(sub_ref[...], w))
    re, im = [pq[0][:, :FOUR_W]], [pq[0][:, FOUR_W:]]
    for r in range(1, FFT_RADIX):
        p, q = pq[r][:, :FOUR_W], pq[r][:, FOUR_W:]
        c, s = twc_ref[r - 1], tws_ref[r - 1]
        re.append(c * p - s * q)
        im.append(s * p + c * q)

    def quarter(z0, z1, z2, z3):
        s02 = (z0[0] + z2[0], z0[1] + z2[1])
        d02 = (z0[0] - z2[0], z0[1] - z2[1])
        s13 = (z1[0] + z3[0], z1[1] + z3[1])
        d13 = (z1[0] - z3[0], z1[1] - z3[1])
        return ((s02[0] + s13[0], s02[1] + s13[1]), (d02[0] - d13[1], d02[1] + d13[0]),
                (s02[0] - s13[0], s02[1] - s13[1]), (d02[0] + d13[1], d02[1] - d13[0]))

    zs = list(zip(re, im))
    even = quarter(zs[0], zs[2], zs[4], zs[6])
    odd = quarter(zs[1], zs[3], zs[5], zs[7])
    half = 0.5 ** 0.5
    odd_re = (odd[0][0], (odd[1][0] - odd[1][1]) * half, -odd[2][1], -(odd[3][0] + odd[3][1]) * half)
    for q in range(4):
        yf_ref[0, q * n:(q + 1) * n, :] = (even[q][0] + odd_re[q]).astype(BF16)
        yf_ref[0, (q + 4) * n:(q + 5) * n, :] = (even[q][0] - odd_re[q]).astype(BF16)


def _fourier(uf, tabs):
    b, l, _ = uf.shape
    tok = pl.BlockSpec((1, l, FOUR_W), lambda bi: (bi, 0, 0))
    if len(tabs) == 4:
        body, name = _fourier_fft_kernel, "fourier_fft"
        scratch = pltpu.VMEM((FOUR_W // V7X_LANES, l, V7X_LANES), F32)
    else:
        body, name = _fourier_kernel, "fourier"
        scratch = pltpu.VMEM((2 * l, FOUR_W), BF16)
    return pl.pallas_call(
        functools.partial(body, seq=l),
        grid=(b,),
        in_specs=[tok] + [_resident(t.shape) for t in tabs],
        out_specs=tok,
        out_shape=jax.ShapeDtypeStruct((b, l, FOUR_W), BF16),
        scratch_shapes=[scratch],
        compiler_params=_params(("parallel",), "fourier"),
        name=name,
    )(uf, *tabs)


CONV_PAD = V7X_SUBLANES_F32
HALO_ROWS = V7X_SUBLANES_BF16


def _out_ffn_kernel(*refs, layer, add_pos, n_sub):
    it = iter(refs)
    yg_ref, yf_ref, bg_ref, cg_ref, uc_ref = (next(it) for _ in range(5))
    cg_prev_ref, uc_prev_ref, cg_next_ref, uc_next_ref, cw_ref, cb_ref = (next(it) for _ in range(6))
    wo_ref, x_ref = next(it), next(it)
    pos_ref = next(it) if add_pos else None
    gmix_ref, gatemix_ref, gpre_ref, shift_ref, scale_ref, gate_ref, gpost_ref = (next(it) for _ in range(7))
    win_ref, wout_ref, o_ref, z_scr = next(it), next(it), next(it), next(it)
    tm = x_ref.shape[1]
    subs = [slice(i * tm // n_sub, (i + 1) * tm // n_sub) for i in range(n_sub)]

    t, nt = pl.program_id(1), pl.num_programs(1)
    z_before = (cg_prev_ref[0].astype(F32) * uc_prev_ref[0].astype(F32))[HALO_ROWS - CONV_PAD:]
    z_after = (cg_next_ref[0].astype(F32) * uc_next_ref[0].astype(F32))[:CONV_PAD]
    z_scr[0:CONV_PAD, :] = jnp.where(t > 0, z_before, 0.0)
    z_scr[CONV_PAD + tm:, :] = jnp.where(t < nt - 1, z_after, 0.0)
    z_scr[CONV_PAD:CONV_PAD + tm, :] = cg_ref[0].astype(F32) * uc_ref[0].astype(F32)
    ycs = []
    for r in subs:
        s, n = CONV_PAD + r.start, r.stop - r.start
        y = (cw_ref[0:1] * z_scr[s - 1:s - 1 + n, :] + cw_ref[1:2] * z_scr[s:s + n, :]
             + cw_ref[2:3] * z_scr[s + 1:s + 1 + n, :] + _row(cb_ref, layer))
        ycs.append((bg_ref[0, r, :].astype(F32) * y).astype(BF16))

    ys = [(_dot(yg_ref[0, r, :], wo_ref[0:V_W]) + _dot(yf_ref[0, r, :], wo_ref[V_W:V_W + FOUR_W])
           + _dot(yc, wo_ref[V_W + FOUR_W:])) for r, yc in zip(subs, ycs)]
    xs, hs = [], []
    for r, y in zip(subs, ys):
        x = x_ref[0, r, :]
        if add_pos:
            x = x + pos_ref[r, :]
        x = x + gatemix_ref[...] * _rms(y, _row(gmix_ref, layer))
        xs.append(x)
        hs.append((_rms(x, _row(gpre_ref, layer)) * (1.0 + scale_ref[...]) + shift_ref[...]).astype(BF16))
    ys = []
    for h in hs:
        y = jnp.zeros((h.shape[0], D_MODEL), F32)
        for lo, hi in FFN_TILES:
            a = _dot(h, win_ref[:, lo:hi])
            u = _dot(h, win_ref[:, HIDDEN + lo:HIDDEN + hi])
            y = y + _dot((_silu(a) * u).astype(BF16), wout_ref[lo:hi, :])
        ys.append(y)
    for r, x, y in zip(subs, xs, ys):
        o_ref[0, r, :] = x + gate_ref[...] * _rms(y, _row(gpost_ref, layer))


def _out_ffn(yg, yf, p, x, pos, lay, i, wts, mod_row, tm):
    w_out, w_ffn_in, w_ffn_out = wts
    b, l, d = x.shape
    add_pos = pos is not None
    tok = lambda bi, ti: (bi, ti, 0)
    halo_per_tile, n_halo = tm // HALO_ROWS, l // HALO_ROWS
    before = lambda bi, ti: (bi, jnp.maximum(ti * halo_per_tile - 1, 0), 0)
    after = lambda bi, ti: (bi, jnp.minimum((ti + 1) * halo_per_tile, n_halo - 1), 0)
    conv_tok = pl.BlockSpec((1, tm, CONV_W), tok)
    in_specs = [pl.BlockSpec((1, tm, V_W), tok), pl.BlockSpec((1, tm, FOUR_W), tok), conv_tok, conv_tok, conv_tok,
                pl.BlockSpec((1, HALO_ROWS, CONV_W), before), pl.BlockSpec((1, HALO_ROWS, CONV_W), before),
                pl.BlockSpec((1, HALO_ROWS, CONV_W), after), pl.BlockSpec((1, HALO_ROWS, CONV_W), after),
                _layer(lay["conv_w"], i), _resident(lay["conv_b"].shape),
                _layer(w_out, 0), pl.BlockSpec((1, tm, d), tok)]
    args = [yg, yf, p["bg"], p["cg"], p["uc"], p["cg"], p["uc"], p["cg"], p["uc"],
            lay["conv_w"], lay["conv_b"], w_out, x]
    if add_pos:
        in_specs.append(pl.BlockSpec((tm, d), lambda bi, ti: (ti, 0)))
        args.append(pos)
    in_specs += [_resident(lay["g_mix_post"].shape), _mod_spec(i, 2, mod_row), _resident(lay["g_ffn_pre"].shape),
                 _mod_spec(i, 3, mod_row), _mod_spec(i, 4, mod_row), _mod_spec(i, 5, mod_row),
                 _resident(lay["g_ffn_post"].shape), _layer(w_ffn_in, 0), _layer(w_ffn_out, 0)]
    args += [lay["g_mix_post"], lay["mod"], lay["g_ffn_pre"], lay["mod"], lay["mod"], lay["mod"],
             lay["g_ffn_post"], w_ffn_in, w_ffn_out]
    return pl.pallas_call(
        functools.partial(_out_ffn_kernel, layer=i, add_pos=add_pos, n_sub=2 if tm >= TOKEN_TILE else 1),
        grid=(b, l // tm),
        in_specs=in_specs,
        out_specs=pl.BlockSpec((1, tm, d), tok),
        out_shape=jax.ShapeDtypeStruct((b, l, d), F32),
        scratch_shapes=[pltpu.VMEM((tm + 2 * CONV_PAD, CONV_W), F32)],
        compiler_params=_params(("parallel", "parallel"), "out_ffn"),
        name="out_ffn",
    )(*args)


def _arrange_w_dec(w_f, w_b):
    wd = jnp.zeros((DEPTH, LR_PAD, 2 * K_W), F32)
    wd = wd.at[:, 0:RANK, :K_W].set(w_f).at[:, RANK:2 * RANK, K_W:].set(w_b)
    return wd.astype(BF16)


def kernel(x, c, ctx, c_ctx, w_mod, b_mod, g_mix_pre, g_mix_post, g_ffn_pre, g_ffn_post,
           w_in, w_dec_f, b_dec_f, w_dec_b, b_dec_b, gla_norm_w, conv_w, conv_b, w_out,
           w_ffn_in, w_ffn_out):
    b, n_lat, d = x.shape
    n_ctx = ctx.shape[1]
    assert d == D_MODEL and w_in.shape == (DEPTH, D_MODEL, _C_END)
    assert n_lat % (2 * TOKEN_TILE) == 0 and n_lat % (FFT_RADIX * V7X_SUBLANES_BF16) == 0
    assert n_ctx % CHUNK == 0 and n_ctx <= TOKEN_TILE and b + 1 <= MOD_ROWS

    pos = jnp.asarray(_pos_embed(n_lat, d))
    tabs = {n_lat: _dft_tables(n_lat, FFT_RADIX), n_ctx: _dft_tables(n_ctx, 1)}

    cvec = jnp.zeros((MOD_ROWS, d), F32).at[:b].set(c).at[b].set(c_ctx)
    ctx_row = b
    mod, w_in_arranged = _modulation(cvec, w_mod, b_mod, w_in)
    lay = {
        "mod": mod.reshape(DEPTH, MOD_ROWS, 1, N_MOD * d),
        "w_in": w_in_arranged,
        "w_dec": _arrange_w_dec(w_dec_f, w_dec_b),
        "b_dec": jnp.concatenate([b_dec_f, b_dec_b], axis=-1).reshape(DEPTH, 1, 2 * K_W),
        "gla_norm_w": gla_norm_w,
        "conv_w": conv_w,
        "conv_b": conv_b,
        "g_mix_pre": g_mix_pre,
        "g_mix_post": g_mix_post,
        "g_ffn_pre": g_ffn_pre,
        "g_ffn_post": g_ffn_post,
    }

    def mix_and_ffn(i, xs, pos_s, p, wts, mod_row, s0, want_state, tm):
        yg, s = _gla(p, lay, i, s0, True, want_state)
        yf = _fourier(p["uf"], tabs[xs.shape[1]])
        return _out_ffn(yg, yf, p, xs, pos_s, lay, i, wts, mod_row, tm), s

    zero_state = jnp.zeros((b, 2, DV, K_W), F32)
    xc = ctx
    tm = TOKEN_TILE
    for i in range(DEPTH):
        pos_i = pos if i == 0 else None
        p_lat, wts = _inproj(x, pos_i, lay, i, None, _OUT_GROUPS_FULL, 2 * tm, cast=(w_out, w_ffn_in, w_ffn_out))
        if i == DEPTH - 1:
            p_ctx, _ = _inproj(xc, None, lay, i, ctx_row, _OUT_GROUPS_STATE, n_ctx)
            _, s = _gla(p_ctx, lay, i, zero_state, False, True)
        else:
            p_ctx, _ = _inproj(xc, None, lay, i, ctx_row, _OUT_GROUPS_FULL, n_ctx)
            xc, s = mix_and_ffn(i, xc, None, p_ctx, wts, ctx_row, zero_state, True, n_ctx)
        x, _ = mix_and_ffn(i, x, pos_i, p_lat, wts, None, s, False, tm)
    return x
```

```python
import functools

import numpy as np
import jax
import jax.numpy as jnp
from jax import lax
from jax.experimental import pallas as pl
from jax.experimental.pallas import tpu as pltpu

F32 = jnp.float32
BF16 = jnp.bfloat16

D_MODEL = 1024
DEPTH = 2
GRID_W = 64
N_MOD = 6
HEADS = 4
DK = 64
DV = 128
K_W = HEADS * DK
V_W = HEADS * DV
RANK = 16
TAU = 16.0
CHUNK = 64
FOUR_W = 256
FOUR_G = 4
CONV_W = 256
HIDDEN = 2816
EPS = 1e-6

V7X_LANES = 128
V7X_SUBLANES_F32 = 8
V7X_SUBLANES_BF16 = 16
V7X_VMEM_BYTES = 64 * 1024 * 1024

LR_PAD = V7X_LANES
MOD_ROWS = V7X_SUBLANES_BF16
TOKEN_TILE = 512
FFN_TILES = ((0, 1536), (1536, HIDDEN))
VMEM_MB = {"modulation": 56, "inproj": 56, "gla": 56, "fourier": 48, "out_ffn": 56}

_C_K, _C_V, _C_LR, _C_Q, _C_R, _C_FOUR, _C_BG, _C_CG, _C_UC, _C_END = (
    0, 256, 768, 800, 1056, 1568, 1824, 2080, 2336, 2592)
_GROUP_WIDTHS = (("k", K_W), ("v", V_W), ("q", K_W), ("r", V_W), ("uf", FOUR_W), ("bg", CONV_W),
                 ("cg", CONV_W), ("uc", CONV_W))
_MAIN_W = sum(w for _, w in _GROUP_WIDTHS)
_IN_W = _MAIN_W + LR_PAD


def _groups(names):
    out, lo = [], 0
    for name, w in _GROUP_WIDTHS:
        if name in names:
            out.append((name, lo, lo + w))
        lo += w
    return tuple(out)


_OUT_GROUPS_FULL = _groups([name for name, _ in _GROUP_WIDTHS])
_OUT_GROUPS_STATE = _groups(("k", "v"))


def _params(semantics, name):
    assert VMEM_MB[name] * 1024 * 1024 < V7X_VMEM_BYTES
    return pltpu.CompilerParams(dimension_semantics=semantics,
                                vmem_limit_bytes=VMEM_MB[name] * 1024 * 1024)


def _resident(shape):
    zeros = (0,) * len(shape)
    return pl.BlockSpec(shape, lambda *_: zeros, pipeline_mode=pl.Buffered(1))


def _layer(arr, i):
    shape = tuple(arr.shape[1:])
    zeros = (0,) * len(shape)
    return pl.BlockSpec((None,) + shape, lambda *_: (i,) + zeros, pipeline_mode=pl.Buffered(1))


def _mod_spec(i, j, row):
    if row is None:
        return pl.BlockSpec((None, None, 1, D_MODEL), lambda bi, *_: (i, bi, 0, j))
    return pl.BlockSpec((None, None, 1, D_MODEL), lambda *_: (i, row, 0, j))


def _row(ref, layer):
    return ref[layer:layer + 1, :]


def _silu(a):
    return a / (1.0 + jnp.exp(-a))


def _rms(x, g):
    return x * lax.rsqrt(jnp.mean(x * x, axis=-1, keepdims=True) + EPS) * g


def _dot(a, b):
    return jnp.dot(a, b, preferred_element_type=F32)


def _dot_nt(a, b):
    return lax.dot_general(a, b, (((1,), (1,)), ((), ())), preferred_element_type=F32)


def _dot_tn(a, b):
    return lax.dot_general(a, b, (((0,), (0,)), ((), ())), preferred_element_type=F32)


def _pos_embed(n_tokens, dim):
    rows = n_tokens // GRID_W
    row = np.repeat(np.arange(rows, dtype=np.float32), GRID_W)
    col = np.tile(np.arange(GRID_W, dtype=np.float32), rows)
    quarter = dim // 4
    freqs = (1.0 / (10000.0 ** (np.arange(quarter, dtype=np.float32) / quarter))).astype(np.float32)

    def enc(p):
        ang = (p[:, None] * freqs[None, :]).astype(np.float32)
        return np.concatenate([np.sin(ang), np.cos(ang)], axis=-1)

    return np.concatenate([enc(row), enc(col)], axis=-1).astype(np.float32)


def _dft_cos_sin(n):
    idx = np.arange(n, dtype=np.int64)
    ang = 2.0 * np.pi * ((idx[:, None] * idx[None, :]) % n).astype(np.float64) / n
    return np.cos(ang) / np.sqrt(n), np.sin(ang) / np.sqrt(n)


def _dft_tables(seq, radix):
    gw = FOUR_W // FOUR_G
    cc, sc = _dft_cos_sin(gw)
    eye = np.eye(FOUR_G)
    chan_tab = jnp.asarray(np.concatenate([np.kron(eye, cc), np.kron(eye, sc)], axis=1), dtype=F32).astype(BF16)
    if radix == 1:
        cl, sl = _dft_cos_sin(seq)
        return chan_tab, jnp.asarray(np.concatenate([cl, -sl], axis=1), dtype=F32).astype(BF16)
    n = seq // radix
    cn, sn = _dft_cos_sin(n)
    sub_tab = np.concatenate([cn, sn], axis=1) * np.sqrt(n / seq)
    ang = 2.0 * np.pi * (np.arange(1, radix)[:, None] * np.arange(n)[None, :]).astype(np.float64) / seq
    twc = np.broadcast_to(np.cos(ang)[:, :, None], (radix - 1, n, FOUR_W))
    tws = np.broadcast_to(np.sin(ang)[:, :, None], (radix - 1, n, FOUR_W))
    return (chan_tab, jnp.asarray(sub_tab, dtype=F32).astype(BF16),
            jnp.asarray(twc, dtype=F32), jnp.asarray(tws, dtype=F32))


def _mod_kernel(c_ref, w_ref, b_ref, wint_ref, o_ref, wout_ref):
    s = _silu(c_ref[...]).astype(BF16)
    o_ref[0] = _dot(s, w_ref[0].astype(BF16)) + b_ref[0]

    @pl.when(pl.program_id(1) == 0)
    def _():
        for col in range(0, _IN_W, V7X_LANES):
            if col < _C_LR:
                rows = wint_ref[0, col:col + V7X_LANES, :]
            elif col < _MAIN_W:
                src = col + _C_Q - _C_LR
                rows = wint_ref[0, src:src + V7X_LANES, :]
            else:
                rows = jnp.concatenate([wint_ref[0, _C_LR:_C_Q, :],
                                        jnp.zeros((LR_PAD - 2 * RANK, D_MODEL), F32)], axis=0)
            wout_ref[0, :, col:col + V7X_LANES] = rows.T.astype(BF16)


def _modulation(cvec, w_mod, b_mod, w_in_t):
    steps = 4
    n = N_MOD * D_MODEL
    tn = n // steps
    assert tn % V7X_LANES == 0
    return pl.pallas_call(
        _mod_kernel,
        grid=(DEPTH, steps),
        in_specs=[pl.BlockSpec((MOD_ROWS, D_MODEL), lambda i, j: (0, 0)),
                  pl.BlockSpec((1, D_MODEL, tn), lambda i, j: (i, 0, j)),
                  pl.BlockSpec((1, 1, tn), lambda i, j: (i, 0, j)),
                  pl.BlockSpec((1, _C_END, D_MODEL), lambda i, j: (i, 0, 0))],
        out_specs=[pl.BlockSpec((1, MOD_ROWS, tn), lambda i, j: (i, 0, j)),
                   pl.BlockSpec((1, D_MODEL, _IN_W), lambda i, j: (i, 0, 0))],
        out_shape=[jax.ShapeDtypeStruct((DEPTH, MOD_ROWS, n), F32),
                   jax.ShapeDtypeStruct((DEPTH, D_MODEL, _IN_W), BF16)],
        compiler_params=_params(("arbitrary", "arbitrary"), "modulation"),
        name="modulation",
    )(cvec, w_mod, b_mod.reshape(DEPTH, 1, n), w_in_t)


def _inproj_kernel(*refs, layer, add_pos, groups, n_cast):
    it = iter(refs)
    x_ref = next(it)
    pos_ref = next(it) if add_pos else None
    g_ref, shift_ref, scale_ref, w_ref, wd_ref, bd_ref = (next(it) for _ in range(6))
    cast_in = [next(it) for _ in range(n_cast)]
    out_refs = list(it)
    for src, dst in zip(cast_in, out_refs[len(out_refs) - n_cast:]):
        dst[0] = src[...].astype(BF16)
    out_refs = out_refs[:len(out_refs) - n_cast]
    lg_hi_ref, lg_lo_ref = out_refs[-2:]
    tm = x_ref.shape[1]
    n_sub = 2 if tm >= 512 else 1
    subs = [slice(i * tm // n_sub, (i + 1) * tm // n_sub) for i in range(n_sub)]
    hs = []
    for r in subs:
        x = x_ref[0, r, :]
        if add_pos:
            x = x + pos_ref[r, :]
        hs.append((_rms(x, _row(g_ref, layer)) * (1.0 + scale_ref[...]) + shift_ref[...]).astype(BF16))
    def project(r, h):
        for o_ref, (_, lo, hi_col) in zip(out_refs, groups):
            o_ref[0, r, :] = _dot(h, w_ref[:, lo:hi_col]).astype(BF16)

    low_rank = [_dot(h, w_ref[:, _MAIN_W:_IN_W]).astype(BF16) for h in hs]
    project(subs[0], hs[0])
    zs = [_dot(lr, wd_ref[...]) + bd_ref[...] for lr in low_rank]
    for r, h in list(zip(subs, hs))[1:]:
        project(r, h)
    for r, z in zip(subs, zs):
        logg = (jnp.minimum(z, 0.0) - jnp.log(1.0 + jnp.exp(-jnp.abs(z)))) * (1.0 / TAU)
        hi = logg.astype(BF16)
        lg_hi_ref[0, r, :] = hi
        lg_lo_ref[0, r, :] = (logg - hi.astype(F32)).astype(BF16)


def _inproj(x, pos, lay, i, mod_row, groups, tm, cast=()):
    b, l, d = x.shape
    add_pos = pos is not None
    n_t = l // tm
    tok = lambda bi, ti: (bi, ti, 0)
    in_specs = [pl.BlockSpec((1, tm, d), tok)]
    args = [x]
    if add_pos:
        in_specs.append(pl.BlockSpec((tm, d), lambda bi, ti: (ti, 0)))
        args.append(pos)
    in_specs += [_resident(lay["g_mix_pre"].shape), _mod_spec(i, 0, mod_row), _mod_spec(i, 1, mod_row),
                 _layer(lay["w_in"], i), _layer(lay["w_dec"], i), _layer(lay["b_dec"], i)]
    args += [lay["g_mix_pre"], lay["mod"], lay["mod"], lay["w_in"], lay["w_dec"], lay["b_dec"]]
    widths = [hi - lo for _, lo, hi in groups] + [2 * K_W, 2 * K_W]
    names = [name for name, _, _ in groups] + ["lg_hi", "lg_lo"]
    out_specs = [pl.BlockSpec((1, tm, w), tok) for w in widths]
    out_shape = [jax.ShapeDtypeStruct((b, l, w), BF16) for w in widths]
    for w in cast:
        _, rows, cols = w.shape
        slab = rows // (b * n_t)
        assert slab * b * n_t == rows and slab % V7X_SUBLANES_BF16 == 0
        in_specs.append(pl.BlockSpec((None, slab, cols), lambda bi, ti: (i, bi * n_t + ti, 0)))
        out_specs.append(pl.BlockSpec((1, slab, cols), lambda bi, ti: (0, bi * n_t + ti, 0)))
        out_shape.append(jax.ShapeDtypeStruct((1, rows, cols), BF16))
        args.append(w)
    outs = pl.pallas_call(
        functools.partial(_inproj_kernel, layer=i, add_pos=add_pos, groups=groups, n_cast=len(cast)),
        grid=(b, n_t),
        in_specs=in_specs,
        out_specs=out_specs,
        out_shape=out_shape,
        compiler_params=_params(("parallel", "parallel"), "inproj"),
        name="inproj",
    )(*args)
    return dict(zip(names, outs)), list(outs[len(names):])


def _gla_kernel(*refs, layer, seq, with_output, with_state):
    it = iter(refs)
    k_ref, v_ref, lg_hi, lg_lo, s0_ref = (next(it) for _ in range(5))
    if with_output:
        q_ref, r_ref, nw_ref = next(it), next(it), next(it)
        y_ref = next(it)
    if with_state:
        s_out_ref = next(it)
    st_ref = next(it)
    ds_scr = next(it)
    dec_scr = next(it)
    if with_output:
        o_scr = next(it)
        qst_scr = next(it)
        gain_scr = next(it)
    n_chunks = seq // CHUNK

    st_ref[...] = s0_ref[0]

    row = lax.broadcasted_iota(jnp.int32, (CHUNK, CHUNK), 0)
    col = lax.broadcasted_iota(jnp.int32, (CHUNK, CHUNK), 1)
    cum = (jnp.where(row >= col, 1.0, 0.0).astype(BF16), jnp.where(row <= col, 1.0, 0.0).astype(BF16))
    lane_head = lax.broadcasted_iota(jnp.int32, (1, K_W), 1) // DK
    head_mask = [lane_head == h for h in range(HEADS)]
    srow = lax.broadcasted_iota(jnp.int32, (CHUNK, HEADS * CHUNK), 0)
    scol = lax.broadcasted_iota(jnp.int32, (CHUNK, HEADS * CHUNK), 1) % CHUNK
    score_mask = (srow >= scol, srow <= scol)
    zero_v = jnp.zeros((CHUNK, DV), BF16)
    last_row = (CHUNK - 1, 0)
    mid_row = (CHUNK // 2 - 1, CHUNK // 2)

    def stack_heads(a):
        zero = jnp.zeros_like(a)
        return jnp.concatenate([jnp.where(head_mask[h], a, zero) for h in range(HEADS)], axis=0)

    def chunk_rows(n):
        return pl.ds(pl.multiple_of(n * CHUNK, CHUNK), CHUNK)

    per_iter = min(8, n_chunks)

    def chunk_local(i, carry):
        loaded = []
        for u in range(per_iter):
            n = i * per_iter + u
            rows = chunk_rows(n)
            k = k_ref[0, rows, :].astype(F32)
            v = v_ref[0, rows, :]
            v_heads = [v[:, h * DV:(h + 1) * DV] for h in range(HEADS)]
            v_rows = jnp.concatenate(v_heads, axis=0)
            v_diag = jnp.concatenate(
                [jnp.concatenate([v_heads[h] if hh == h else zero_v for hh in range(HEADS)], axis=1)
                 for h in range(HEADS)], axis=0) if with_output else None
            v = (v, v_rows, v_diag)
            qs = q_ref[0, rows, :].astype(F32) * (DK ** -0.5) if with_output else None
            lg = [(lg_hi[0, rows, d * K_W:(d + 1) * K_W], lg_lo[0, rows, d * K_W:(d + 1) * K_W]) for d in range(2)]
            loaded.append((n, rows, k, v, qs, lg))
        chains = [(n, rows, k, v, qs, lg[dirn], dirn) for n, rows, k, v, qs, lg in loaded for dirn in range(2)]
        stores = []
        if with_output:
            for n, rows, *_ in loaded:
                stores.append((gain_scr, (rows, slice(None)),
                               _silu(r_ref[0, rows, :].astype(F32)) * _row(nw_ref, layer)))
        gs = [_dot(cum[dirn], hi) + _dot(cum[dirn], lo) for *_, (hi, lo), dirn in chains]
        score_list = []
        for (n, rows, k, (v, v_rows, v_diag), qs, _, dirn), g in zip(chains, gs):
            g_last = g[last_row[dirn]:last_row[dirn] + 1]
            stores.append((dec_scr, (dirn, pl.ds(pl.multiple_of(n * 8, 8), 8), slice(None)),
                           jnp.broadcast_to(jnp.exp(g_last), (8, K_W))))
            if with_output:
                g_mid = g[mid_row[dirn]:mid_row[dirn] + 1]
                q_in = qs * jnp.exp(g - g_mid)
                k_in = k * jnp.exp(g_mid - g)
                score_list.append(_dot_nt(q_in.astype(BF16), stack_heads(k_in.astype(BF16))))
                stores.append((qst_scr, (dirn, rows, slice(None)), (q_in * jnp.exp(g_mid)).astype(BF16)))
                k_upd = stack_heads((k_in * jnp.exp(g_last - g_mid)).astype(BF16))
            else:
                k_upd = stack_heads((k * jnp.exp(g_last - g)).astype(BF16))
            stores.append((ds_scr, (dirn, pl.ds(pl.multiple_of(n * DV, DV), DV), slice(None)),
                           _dot_tn(v_rows, k_upd)))
        if with_output:
            for (n, rows, k, (v, v_rows, v_diag), qs, _, dirn), sc in zip(chains, score_list):
                scores = jnp.where(score_mask[dirn], sc, 0.0).astype(BF16)
                stores.append((o_scr, (dirn, rows, slice(None)), _dot(scores, v_diag)))
        for ref, idx, val in stores:
            ref[idx] = val
        return carry

    lax.fori_loop(0, n_chunks // per_iter, chunk_local, 0)

    def scan_step(i, finish):
        steps = ((0, i), (1, n_chunks - 1 - i))
        new_st, inters = [], []
        for dirn, n in steps:
            st = st_ref[dirn]
            if with_output:
                q_st = stack_heads(qst_scr[dirn, chunk_rows(n), :])
                inters.append(_dot_nt(q_st, st.astype(BF16)))
            dec = dec_scr[dirn, pl.ds(pl.multiple_of(n * 8, 8), 8), :]
            new_st.append(dec[0:1] * st + ds_scr[dirn, pl.ds(pl.multiple_of(n * DV, DV), DV), :])
        for (dirn, n), st in zip(steps, new_st):
            st_ref[dirn] = st
        if with_output:
            for (dirn, n), inter in zip(steps, inters):
                rows = chunk_rows(n)
                for h in range(HEADS):
                    vs = slice(h * DV, (h + 1) * DV)
                    o = o_scr[dirn, rows, vs] + inter[h * CHUNK:(h + 1) * CHUNK]
                    if finish:
                        o = o + o_scr[1 - dirn, rows, vs]
                        y_ref[0, rows, vs] = (_rms(o, gain_scr[rows, vs])).astype(BF16)
                    else:
                        o_scr[dirn, rows, vs] = o

    half = n_chunks // 2
    unroll = min(16, half)

    def scan_first(i, carry):
        scan_step(i, False)
        return carry

    def scan_second(i, carry):
        scan_step(i, True)
        return carry

    lax.fori_loop(0, half, scan_first, 0, unroll=unroll)
    lax.fori_loop(half, n_chunks, scan_second, 0, unroll=unroll)

    if with_state:
        s_out_ref[0] = st_ref[...]


def _gla(p, lay, i, s0, with_output, with_state):
    b, l, _ = p["k"].shape
    seq3 = lambda bi: (bi, 0, 0)
    in_specs = [pl.BlockSpec((1, l, K_W), seq3), pl.BlockSpec((1, l, V_W), seq3),
                pl.BlockSpec((1, l, 2 * K_W), seq3), pl.BlockSpec((1, l, 2 * K_W), seq3),
                pl.BlockSpec((1, 2, DV, K_W), lambda bi: (bi, 0, 0, 0))]
    args = [p["k"], p["v"], p["lg_hi"], p["lg_lo"], s0]
    out_specs, out_shape = [], []
    scratch = [pltpu.VMEM((2, DV, K_W), F32),
               pltpu.VMEM((2, l // CHUNK * DV, K_W), F32), pltpu.VMEM((2, l // CHUNK * 8, K_W), F32)]
    if with_output:
        in_specs += [pl.BlockSpec((1, l, K_W), seq3), pl.BlockSpec((1, l, V_W), seq3),
                     _resident(lay["gla_norm_w"].shape)]
        args += [p["q"], p["r"], lay["gla_norm_w"]]
        out_specs.append(pl.BlockSpec((1, l, V_W), seq3))
        out_shape.append(jax.ShapeDtypeStruct((b, l, V_W), BF16))
        scratch += [pltpu.VMEM((2, l, V_W), F32), pltpu.VMEM((2, l, K_W), BF16), pltpu.VMEM((l, V_W), F32)]
    if with_state:
        out_specs.append(pl.BlockSpec((1, 2, DV, K_W), lambda bi: (bi, 0, 0, 0)))
        out_shape.append(jax.ShapeDtypeStruct((b, 2, DV, K_W), F32))
    outs = pl.pallas_call(
        functools.partial(_gla_kernel, layer=i, seq=l, with_output=with_output, with_state=with_state),
        grid=(b,),
        in_specs=in_specs,
        out_specs=out_specs,
        out_shape=out_shape,
        scratch_shapes=scratch,
        compiler_params=_params(("parallel",), "gla"),
        name="gla",
    )(*args)
    outs = list(outs)
    y = outs.pop(0) if with_output else None
    s = outs.pop(0) if with_state else None
    return y, s


def _fourier_kernel(uf_ref, chan_ref, pos_ref, yf_ref, ab_scr, *, seq):
    ab = _dot(uf_ref[0], chan_ref[...])
    ab_scr[0:seq, :] = ab[:, :FOUR_W].astype(BF16)
    ab_scr[seq:2 * seq, :] = ab[:, FOUR_W:].astype(BF16)
    rb = min(seq, 512)
    for i in range(seq // rb):
        yf_ref[0, i * rb:(i + 1) * rb, :] = _dot(pos_ref[i * rb:(i + 1) * rb, :], ab_scr[...]).astype(BF16)


FFT_RADIX = 8


def _fourier_fft_kernel(uf_ref, chan_ref, sub_ref, twc_ref, tws_ref, yf_ref, uf_scr, *, seq):
    n = seq // FFT_RADIX
    lanes = uf_scr.shape[2]
    n_lane_tiles = FOUR_W // lanes
    for j in range(n_lane_tiles):
        uf_scr[j] = uf_ref[0, :, j * lanes:(j + 1) * lanes].astype(F32)

    def tokens(r):
        parts = [uf_scr[j, pl.ds(r, n, stride=FFT_RADIX), :] for j in range(n_lane_tiles)]
        return jnp.concatenate(parts, axis=1).astype(BF16)

    ab = [_dot(tokens(r), chan_ref[...]) for r in range(FFT_RADIX)]
    pq = []
    for ab_r in ab:
        a, b = ab_r[:, :FOUR_W].astype(BF16), ab_r[:, FOUR_W:].astype(BF16)
        w = jnp.concatenate([jnp.concatenate([a, b], axis=1), jnp.concatenate([-b, a], axis=1)], axis=0)
        pq.append(_dot(sub_ref[...], w))
    re, im = [pq[0][:, :FOUR_W]], [pq[0][:, FOUR_W:]]
    for r in range(1, FFT_RADIX):
        p, q = pq[r][:, :FOUR_W], pq[r][:, FOUR_W:]
        c, s = twc_ref[r - 1], tws_ref[r - 1]
        re.append(c * p - s * q)
        im.append(s * p + c * q)

    def quarter(z0, z1, z2, z3):
        s02 = (z0[0] + z2[0], z0[1] + z2[1])
        d02 = (z0[0] - z2[0], z0[1] - z2[1])
        s13 = (z1[0] + z3[0], z1[1] + z3[1])
        d13 = (z1[0] - z3[0], z1[1] - z3[1])
        return ((s02[0] + s13[0], s02[1] + s13[1]), (d02[0] - d13[1], d02[1] + d13[0]),
                (s02[0] - s13[0], s02[1] - s13[1]), (d02[0] + d13[1], d02[1] - d13[0]))

    zs = list(zip(re, im))
    even = quarter(zs[0], zs[2], zs[4], zs[6])
    odd = quarter(zs[1], zs[3], zs[5], zs[7])
    half = 0.5 ** 0.5
    odd_re = (odd[0][0], (odd[1][0] - odd[1][1]) * half, -odd[2][1], -(odd[3][0] + odd[3][1]) * half)
    for q in range(4):
        yf_ref[0, q * n:(q + 1) * n, :] = (even[q][0] + odd_re[q]).astype(BF16)
        yf_ref[0, (q + 4) * n:(q + 5) * n, :] = (even[q][0] - odd_re[q]).astype(BF16)


def _fourier(uf, tabs):
    b, l, _ = uf.shape
    tok = pl.BlockSpec((1, l, FOUR_W), lambda bi: (bi, 0, 0))
    if len(tabs) == 4:
        body, name = _fourier_fft_kernel, "fourier_fft"
        scratch = pltpu.VMEM((FOUR_W // V7X_LANES, l, V7X_LANES), F32)
    else:
        body, name = _fourier_kernel, "fourier"
        scratch = pltpu.VMEM((2 * l, FOUR_W), BF16)
    return pl.pallas_call(
        functools.partial(body, seq=l),
        grid=(b,),
        in_specs=[tok] + [_resident(t.shape) for t in tabs],
        out_specs=tok,
        out_shape=jax.ShapeDtypeStruct((b, l, FOUR_W), BF16),
        scratch_shapes=[scratch],
        compiler_params=_params(("parallel",), "fourier"),
        name=name,
    )(uf, *tabs)


CONV_PAD = V7X_SUBLANES_F32
HALO_ROWS = V7X_SUBLANES_BF16


def _out_ffn_kernel(*refs, layer, add_pos, n_sub):
    it = iter(refs)
    yg_ref, yf_ref, bg_ref, cg_ref, uc_ref = (next(it) for _ in range(5))
    cg_prev_ref, uc_prev_ref, cg_next_ref, uc_next_ref, cw_ref, cb_ref = (next(it) for _ in range(6))
    wo_ref, x_ref = next(it), next(it)
    pos_ref = next(it) if add_pos else None
    gmix_ref, gatemix_ref, gpre_ref, shift_ref, scale_ref, gate_ref, gpost_ref = (next(it) for _ in range(7))
    win_ref, wout_ref, o_ref, z_scr = next(it), next(it), next(it), next(it)
    tm = x_ref.shape[1]
    subs = [slice(i * tm // n_sub, (i + 1) * tm // n_sub) for i in range(n_sub)]

    t, nt = pl.program_id(1), pl.num_programs(1)
    z_before = (cg_prev_ref[0].astype(F32) * uc_prev_ref[0].astype(F32))[HALO_ROWS - CONV_PAD:]
    z_after = (cg_next_ref[0].astype(F32) * uc_next_ref[0].astype(F32))[:CONV_PAD]
    z_scr[0:CONV_PAD, :] = jnp.where(t > 0, z_before, 0.0)
    z_scr[CONV_PAD + tm:, :] = jnp.where(t < nt - 1, z_after, 0.0)
    z_scr[CONV_PAD:CONV_PAD + tm, :] = cg_ref[0].astype(F32) * uc_ref[0].astype(F32)
    ycs = []
    for r in subs:
        s, n = CONV_PAD + r.start, r.stop - r.start
        y = (cw_ref[0:1] * z_scr[s - 1:s - 1 + n, :] + cw_ref[1:2] * z_scr[s:s + n, :]
             + cw_ref[2:3] * z_scr[s + 1:s + 1 + n, :] + _row(cb_ref, layer))
        ycs.append((bg_ref[0, r, :].astype(F32) * y).astype(BF16))

    ys = [(_dot(yg_ref[0, r, :], wo_ref[0:V_W]) + _dot(yf_ref[0, r, :], wo_ref[V_W:V_W + FOUR_W])
           + _dot(yc, wo_ref[V_W + FOUR_W:])) for r, yc in zip(subs, ycs)]
    xs, hs = [], []
    for r, y in zip(subs, ys):
        x = x_ref[0, r, :]
        if add_pos:
            x = x + pos_ref[r, :]
        x = x + gatemix_ref[...] * _rms(y, _row(gmix_ref, layer))
        xs.append(x)
        hs.append((_rms(x, _row(gpre_ref, layer)) * (1.0 + scale_ref[...]) + shift_ref[...]).astype(BF16))
    ys = []
    for h in hs:
        y = jnp.zeros((h.shape[0], D_MODEL), F32)
        for lo, hi in FFN_TILES:
            a = _dot(h, win_ref[:, lo:hi])
            u = _dot(h, win_ref[:, HIDDEN + lo:HIDDEN + hi])
            y = y + _dot((_silu(a) * u).astype(BF16), wout_ref[lo:hi, :])
        ys.append(y)
    for r, x, y in zip(subs, xs, ys):
        o_ref[0, r, :] = x + gate_ref[...] * _rms(y, _row(gpost_ref, layer))


def _out_ffn(yg, yf, p, x, pos, lay, i, wts, mod_row, tm):
    w_out, w_ffn_in, w_ffn_out = wts
    b, l, d = x.shape
    add_pos = pos is not None
    tok = lambda bi, ti: (bi, ti, 0)
    halo_per_tile, n_halo = tm // HALO_ROWS, l // HALO_ROWS
    before = lambda bi, ti: (bi, jnp.maximum(ti * halo_per_tile - 1, 0), 0)
    after = lambda bi, ti: (bi, jnp.minimum((ti + 1) * halo_per_tile, n_halo - 1), 0)
    conv_tok = pl.BlockSpec((1, tm, CONV_W), tok)
    in_specs = [pl.BlockSpec((1, tm, V_W), tok), pl.BlockSpec((1, tm, FOUR_W), tok), conv_tok, conv_tok, conv_tok,
                pl.BlockSpec((1, HALO_ROWS, CONV_W), before), pl.BlockSpec((1, HALO_ROWS, CONV_W), before),
                pl.BlockSpec((1, HALO_ROWS, CONV_W), after), pl.BlockSpec((1, HALO_ROWS, CONV_W), after),
                _layer(lay["conv_w"], i), _resident(lay["conv_b"].shape),
                _layer(w_out, 0), pl.BlockSpec((1, tm, d), tok)]
    args = [yg, yf, p["bg"], p["cg"], p["uc"], p["cg"], p["uc"], p["cg"], p["uc"],
            lay["conv_w"], lay["conv_b"], w_out, x]
    if add_pos:
        in_specs.append(pl.BlockSpec((tm, d), lambda bi, ti: (ti, 0)))
        args.append(pos)
    in_specs += [_resident(lay["g_mix_post"].shape), _mod_spec(i, 2, mod_row), _resident(lay["g_ffn_pre"].shape),
                 _mod_spec(i, 3, mod_row), _mod_spec(i, 4, mod_row), _mod_spec(i, 5, mod_row),
                 _resident(lay["g_ffn_post"].shape), _layer(w_ffn_in, 0), _layer(w_ffn_out, 0)]
    args += [lay["g_mix_post"], lay["mod"], lay["g_ffn_pre"], lay["mod"], lay["mod"], lay["mod"],
             lay["g_ffn_post"], w_ffn_in, w_ffn_out]
    return pl.pallas_call(
        functools.partial(_out_ffn_kernel, layer=i, add_pos=add_pos, n_sub=2 if tm >= TOKEN_TILE else 1),
        grid=(b, l // tm),
        in_specs=in_specs,
        out_specs=pl.BlockSpec((1, tm, d), tok),
        out_shape=jax.ShapeDtypeStruct((b, l, d), F32),
        scratch_shapes=[pltpu.VMEM((tm + 2 * CONV_PAD, CONV_W), F32)],
        compiler_params=_params(("parallel", "parallel"), "out_ffn"),
        name="out_ffn",
    )(*args)


def _arrange_w_dec(w_f, w_b):
    wd = jnp.zeros((DEPTH, LR_PAD, 2 * K_W), F32)
    wd = wd.at[:, 0:RANK, :K_W].set(w_f).at[:, RANK:2 * RANK, K_W:].set(w_b)
    return wd.astype(BF16)


def kernel(x, c, ctx, c_ctx, w_mod, b_mod, g_mix_pre, g_mix_post, g_ffn_pre, g_ffn_post,
           w_in, w_dec_f, b_dec_f, w_dec_b, b_dec_b, gla_norm_w, conv_w, conv_b, w_out,
           w_ffn_in, w_ffn_out):
    b, n_lat, d = x.shape
    n_ctx = ctx.shape[1]
    assert d == D_MODEL and w_in.shape == (DEPTH, D_MODEL, _C_END)
    assert n_lat % (2 * TOKEN_TILE) == 0 and n_lat % (FFT_RADIX * V7X_SUBLANES_BF16) == 0
    assert n_ctx % CHUNK == 0 and n_ctx <= TOKEN_TILE and b + 1 <= MOD_ROWS

    pos = jnp.asarray(_pos_embed(n_lat, d))
    tabs = {n_lat: _dft_tables(n_lat, FFT_RADIX), n_ctx: _dft_tables(n_ctx, 1)}

    cvec = jnp.zeros((MOD_ROWS, d), F32).at[:b].set(c).at[b].set(c_ctx)
    ctx_row = b
    mod, w_in_arranged = _modulation(cvec, w_mod, b_mod, jnp.swapaxes(w_in, 1, 2))
    lay = {
        "mod": mod.reshape(DEPTH, MOD_ROWS, 1, N_MOD * d),
        "w_in": w_in_arranged,
        "w_dec": _arrange_w_dec(w_dec_f, w_dec_b),
        "b_dec": jnp.concatenate([b_dec_f, b_dec_b], axis=-1).reshape(DEPTH, 1, 2 * K_W),
        "gla_norm_w": gla_norm_w,
        "conv_w": conv_w,
        "conv_b": conv_b,
        "g_mix_pre": g_mix_pre,
        "g_mix_post": g_mix_post,
        "g_ffn_pre": g_ffn_pre,
        "g_ffn_post": g_ffn_post,
    }

    def mix_and_ffn(i, xs, pos_s, p, wts, mod_row, s0, want_state, tm):
        yg, s = _gla(p, lay, i, s0, True, want_state)
        yf = _fourier(p["uf"], tabs[xs.shape[1]])
        return _out_ffn(yg, yf, p, xs, pos_s, lay, i, wts, mod_row, tm), s

    zero_state = jnp.zeros((b, 2, DV, K_W), F32)
    xc = ctx
    tm = TOKEN_TILE
    for i in range(DEPTH):
        pos_i = pos if i == 0 else None
        p_lat, wts = _inproj(x, pos_i, lay, i, None, _OUT_GROUPS_FULL, 2 * tm, cast=(w_out, w_ffn_in, w_ffn_out))
        if i == DEPTH - 1:
            p_ctx, _ = _inproj(xc, None, lay, i, ctx_row, _OUT_GROUPS_STATE, n_ctx)
            _, s = _gla(p_ctx, lay, i, zero_state, False, True)
        else:
            p_ctx, _ = _inproj(xc, None, lay, i, ctx_row, _OUT_GROUPS_FULL, n_ctx)
            xc, s = mix_and_ffn(i, xc, None, p_ctx, wts, ctx_row, zero_state, True, n_ctx)
        x, _ = mix_and_ffn(i, x, pos_i, p_lat, wts, None, s, False, tm)
    return x
```

```python
import functools

import numpy as np
import jax
import jax.numpy as jnp
from jax import lax
from jax.experimental import pallas as pl
from jax.experimental.pallas import tpu as pltpu

F32 = jnp.float32
BF16 = jnp.bfloat16

D_MODEL = 1024
DEPTH = 2
GRID_W = 64
N_MOD = 6
HEADS = 4
DK = 64
DV = 128
K_W = HEADS * DK
V_W = HEADS * DV
RANK = 16
TAU = 16.0
CHUNK = 64
FOUR_W = 256
FOUR_G = 4
CONV_W = 256
HIDDEN = 2816
EPS = 1e-6

V7X_LANES = 128
V7X_SUBLANES_F32 = 8
V7X_SUBLANES_BF16 = 16
V7X_VMEM_BYTES = 64 * 1024 * 1024

LR_PAD = V7X_LANES
MOD_ROWS = V7X_SUBLANES_BF16
TOKEN_TILE = 512
FFN_TILES = ((0, 1536), (1536, HIDDEN))
VMEM_MB = {"modulation": 56, "inproj": 56, "gla": 56, "fourier": 48, "out_ffn": 56}

_C_K, _C_V, _C_LR, _C_Q, _C_R, _C_FOUR, _C_BG, _C_CG, _C_UC, _C_END = (
    0, 256, 768, 800, 1056, 1568, 1824, 2080, 2336, 2592)
_GROUP_WIDTHS = (("k", K_W), ("v", V_W), ("q", K_W), ("r", V_W), ("uf", FOUR_W), ("bg", CONV_W),
                 ("cg", CONV_W), ("uc", CONV_W))
_MAIN_W = sum(w for _, w in _GROUP_WIDTHS)
_IN_W = _MAIN_W + LR_PAD


def _groups(names):
    out, lo = [], 0
    for name, w in _GROUP_WIDTHS:
        if name in names:
            out.append((name, lo, lo + w))
        lo += w
    return tuple(out)


_OUT_GROUPS_FULL = _groups([name for name, _ in _GROUP_WIDTHS])
_OUT_GROUPS_STATE = _groups(("k", "v"))


def _params(semantics, name):
    assert VMEM_MB[name] * 1024 * 1024 < V7X_VMEM_BYTES
    return pltpu.CompilerParams(dimension_semantics=semantics,
                                vmem_limit_bytes=VMEM_MB[name] * 1024 * 1024)


def _resident(shape):
    zeros = (0,) * len(shape)
    return pl.BlockSpec(shape, lambda *_: zeros, pipeline_mode=pl.Buffered(1))


def _layer(arr, i):
    shape = tuple(arr.shape[1:])
    zeros = (0,) * len(shape)
    return pl.BlockSpec((None,) + shape, lambda *_: (i,) + zeros, pipeline_mode=pl.Buffered(1))


def _mod_spec(i, j, row):
    if row is None:
        return pl.BlockSpec((None, None, 1, D_MODEL), lambda bi, *_: (i, bi, 0, j))
    return pl.BlockSpec((None, None, 1, D_MODEL), lambda *_: (i, row, 0, j))


def _row(ref, layer):
    return ref[layer:layer + 1, :]


def _silu(a):
    return a / (1.0 + jnp.exp(-a))


def _rms(x, g):
    return x * lax.rsqrt(jnp.mean(x * x, axis=-1, keepdims=True) + EPS) * g


def _dot(a, b):
    return jnp.dot(a, b, preferred_element_type=F32)


def _dot_nt(a, b):
    return lax.dot_general(a, b, (((1,), (1,)), ((), ())), preferred_element_type=F32)


def _dot_tn(a, b):
    return lax.dot_general(a, b, (((0,), (0,)), ((), ())), preferred_element_type=F32)


def _pos_embed(n_tokens, dim):
    rows = n_tokens // GRID_W
    row = np.repeat(np.arange(rows, dtype=np.float32), GRID_W)
    col = np.tile(np.arange(GRID_W, dtype=np.float32), rows)
    quarter = dim // 4
    freqs = (1.0 / (10000.0 ** (np.arange(quarter, dtype=np.float32) / quarter))).astype(np.float32)

    def enc(p):
        ang = (p[:, None] * freqs[None, :]).astype(np.float32)
        return np.concatenate([np.sin(ang), np.cos(ang)], axis=-1)

    return np.concatenate([enc(row), enc(col)], axis=-1).astype(np.float32)


def _dft_cos_sin(n):
    idx = np.arange(n, dtype=np.int64)
    ang = 2.0 * np.pi * ((idx[:, None] * idx[None, :]) % n).astype(np.float64) / n
    return np.cos(ang) / np.sqrt(n), np.sin(ang) / np.sqrt(n)


def _dft_tables(seq, radix):
    gw = FOUR_W // FOUR_G
    cc, sc = _dft_cos_sin(gw)
    eye = np.eye(FOUR_G)
    chan_tab = jnp.asarray(np.concatenate([np.kron(eye, cc), np.kron(eye, sc)], axis=1), dtype=F32).astype(BF16)
    if radix == 1:
        cl, sl = _dft_cos_sin(seq)
        return chan_tab, jnp.asarray(np.concatenate([cl, -sl], axis=1), dtype=F32).astype(BF16)
    n = seq // radix
    cn, sn = _dft_cos_sin(n)
    sub_tab = np.concatenate([cn, sn], axis=1) * np.sqrt(n / seq)
    ang = 2.0 * np.pi * (np.arange(1, radix)[:, None] * np.arange(n)[None, :]).astype(np.float64) / seq
    twc = np.broadcast_to(np.cos(ang)[:, :, None], (radix - 1, n, FOUR_W))
    tws = np.broadcast_to(np.sin(ang)[:, :, None], (radix - 1, n, FOUR_W))
    return (chan_tab, jnp.asarray(sub_tab, dtype=F32).astype(BF16),
            jnp.asarray(twc, dtype=F32), jnp.asarray(tws, dtype=F32))


def _mod_kernel(c_ref, w_ref, b_ref, wint_ref, o_ref, wout_ref, *, steps):
    s = _silu(c_ref[...]).astype(BF16)
    o_ref[0] = _dot(s, w_ref[0].astype(BF16)) + b_ref[0]

    def arrange(col):
        if col < _C_LR:
            rows = wint_ref[0, col:col + V7X_LANES, :]
        elif col < _MAIN_W:
            src = col + _C_Q - _C_LR
            rows = wint_ref[0, src:src + V7X_LANES, :]
        else:
            rows = jnp.concatenate([wint_ref[0, _C_LR:_C_Q, :],
                                    jnp.zeros((LR_PAD - 2 * RANK, D_MODEL), F32)], axis=0)
        wout_ref[0, :, col:col + V7X_LANES] = rows.T.astype(BF16)

    tiles = list(range(0, _IN_W, V7X_LANES))
    for step in range(steps):
        @pl.when(pl.program_id(1) == step)
        def _(step=step):
            for col in tiles[step::steps]:
                arrange(col)


def _modulation(cvec, w_mod, b_mod, w_in_t):
    steps = 4
    n = N_MOD * D_MODEL
    tn = n // steps
    assert tn % V7X_LANES == 0
    return pl.pallas_call(
        functools.partial(_mod_kernel, steps=steps),
        grid=(DEPTH, steps),
        in_specs=[pl.BlockSpec((MOD_ROWS, D_MODEL), lambda i, j: (0, 0)),
                  pl.BlockSpec((1, D_MODEL, tn), lambda i, j: (i, 0, j)),
                  pl.BlockSpec((1, 1, tn), lambda i, j: (i, 0, j)),
                  pl.BlockSpec((1, _C_END, D_MODEL), lambda i, j: (i, 0, 0))],
        out_specs=[pl.BlockSpec((1, MOD_ROWS, tn), lambda i, j: (i, 0, j)),
                   pl.BlockSpec((1, D_MODEL, _IN_W), lambda i, j: (i, 0, 0))],
        out_shape=[jax.ShapeDtypeStruct((DEPTH, MOD_ROWS, n), F32),
                   jax.ShapeDtypeStruct((DEPTH, D_MODEL, _IN_W), BF16)],
        compiler_params=_params(("arbitrary", "arbitrary"), "modulation"),
        name="modulation",
    )(cvec, w_mod, b_mod.reshape(DEPTH, 1, n), w_in_t)


def _inproj_kernel(*refs, layer, add_pos, groups, n_cast):
    it = iter(refs)
    x_ref = next(it)
    pos_ref = next(it) if add_pos else None
    g_ref, shift_ref, scale_ref, w_ref, wd_ref, bd_ref = (next(it) for _ in range(6))
    cast_in = [next(it) for _ in range(n_cast)]
    out_refs = list(it)
    for src, dst in zip(cast_in, out_refs[len(out_refs) - n_cast:]):
        dst[0] = src[...].astype(BF16)
    out_refs = out_refs[:len(out_refs) - n_cast]
    lg_hi_ref, lg_lo_ref = out_refs[-2:]
    tm = x_ref.shape[1]
    n_sub = 2 if tm >= 512 else 1
    subs = [slice(i * tm // n_sub, (i + 1) * tm // n_sub) for i in range(n_sub)]
    hs = []
    for r in subs:
        x = x_ref[0, r, :]
        if add_pos:
            x = x + pos_ref[r, :]
        hs.append((_rms(x, _row(g_ref, layer)) * (1.0 + scale_ref[...]) + shift_ref[...]).astype(BF16))
    def project(r, h):
        for o_ref, (_, lo, hi_col) in zip(out_refs, groups):
            o_ref[0, r, :] = _dot(h, w_ref[:, lo:hi_col]).astype(BF16)

    low_rank = [_dot(h, w_ref[:, _MAIN_W:_IN_W]).astype(BF16) for h in hs]
    project(subs[0], hs[0])
    zs = [_dot(lr, wd_ref[...]) + bd_ref[...] for lr in low_rank]
    for r, h in list(zip(subs, hs))[1:]:
        project(r, h)
    for r, z in zip(subs, zs):
        logg = (jnp.minimum(z, 0.0) - jnp.log(1.0 + jnp.exp(-jnp.abs(z)))) * (1.0 / TAU)
        hi = logg.astype(BF16)
        lg_hi_ref[0, r, :] = hi
        lg_lo_ref[0, r, :] = (logg - hi.astype(F32)).astype(BF16)


def _inproj(x, pos, lay, i, mod_row, groups, tm, cast=()):
    b, l, d = x.shape
    add_pos = pos is not None
    n_t = l // tm
    tok = lambda bi, ti: (bi, ti, 0)
    in_specs = [pl.BlockSpec((1, tm, d), tok)]
    args = [x]
    if add_pos:
        in_specs.append(pl.BlockSpec((tm, d), lambda bi, ti: (ti, 0)))
        args.append(pos)
    in_specs += [_resident(lay["g_mix_pre"].shape), _mod_spec(i, 0, mod_row), _mod_spec(i, 1, mod_row),
                 _layer(lay["w_in"], i), _layer(lay["w_dec"], i), _layer(lay["b_dec"], i)]
    args += [lay["g_mix_pre"], lay["mod"], lay["mod"], lay["w_in"], lay["w_dec"], lay["b_dec"]]
    widths = [hi - lo for _, lo, hi in groups] + [2 * K_W, 2 * K_W]
    names = [name for name, _, _ in groups] + ["lg_hi", "lg_lo"]
    out_specs = [pl.BlockSpec((1, tm, w), tok) for w in widths]
    out_shape = [jax.ShapeDtypeStruct((b, l, w), BF16) for w in widths]
    for w in cast:
        _, rows, cols = w.shape
        slab = rows // (b * n_t)
        assert slab * b * n_t == rows and slab % V7X_SUBLANES_BF16 == 0
        in_specs.append(pl.BlockSpec((None, slab, cols), lambda bi, ti: (i, bi * n_t + ti, 0)))
        out_specs.append(pl.BlockSpec((1, slab, cols), lambda bi, ti: (0, bi * n_t + ti, 0)))
        out_shape.append(jax.ShapeDtypeStruct((1, rows, cols), BF16))
        args.append(w)
    outs = pl.pallas_call(
        functools.partial(_inproj_kernel, layer=i, add_pos=add_pos, groups=groups, n_cast=len(cast)),
        grid=(b, n_t),
        in_specs=in_specs,
        out_specs=out_specs,
        out_shape=out_shape,
        compiler_params=_params(("parallel", "parallel"), "inproj"),
        name="inproj",
    )(*args)
    return dict(zip(names, outs)), list(outs[len(names):])


def _gla_kernel(*refs, layer, seq, with_output, with_state):
    it = iter(refs)
    k_ref, v_ref, lg_hi, lg_lo, s0_ref = (next(it) for _ in range(5))
    if with_output:
        q_ref, r_ref, nw_ref = next(it), next(it), next(it)
        y_ref = next(it)
    if with_state:
        s_out_ref = next(it)
    st_ref = next(it)
    ds_scr = next(it)
    dec_scr = next(it)
    if with_output:
        o_scr = next(it)
        qst_scr = next(it)
        gain_scr = next(it)
    n_chunks = seq // CHUNK

    st_ref[...] = s0_ref[0]

    row = lax.broadcasted_iota(jnp.int32, (CHUNK, CHUNK), 0)
    col = lax.broadcasted_iota(jnp.int32, (CHUNK, CHUNK), 1)
    cum = (jnp.where(row >= col, 1.0, 0.0).astype(BF16), jnp.where(row <= col, 1.0, 0.0).astype(BF16))
    lane_head = lax.broadcasted_iota(jnp.int32, (1, K_W), 1) // DK
    head_mask = [lane_head == h for h in range(HEADS)]
    srow = lax.broadcasted_iota(jnp.int32, (CHUNK, HEADS * CHUNK), 0)
    scol = lax.broadcasted_iota(jnp.int32, (CHUNK, HEADS * CHUNK), 1) % CHUNK
    score_mask = (srow >= scol, srow <= scol)
    zero_v = jnp.zeros((CHUNK, DV), BF16)
    last_row = (CHUNK - 1, 0)
    mid_row = (CHUNK // 2 - 1, CHUNK // 2)

    def stack_heads(a):
        zero = jnp.zeros_like(a)
        return jnp.concatenate([jnp.where(head_mask[h], a, zero) for h in range(HEADS)], axis=0)

    def chunk_rows(n):
        return pl.ds(pl.multiple_of(n * CHUNK, CHUNK), CHUNK)

    per_iter = min(8, n_chunks)

    def chunk_local(i, carry):
        loaded = []
        for u in range(per_iter):
            n = i * per_iter + u
            rows = chunk_rows(n)
            k = k_ref[0, rows, :].astype(F32)
            v = v_ref[0, rows, :]
            v_heads = [v[:, h * DV:(h + 1) * DV] for h in range(HEADS)]
            v_rows = jnp.concatenate(v_heads, axis=0)
            v_diag = jnp.concatenate(
                [jnp.concatenate([v_heads[h] if hh == h else zero_v for hh in range(HEADS)], axis=1)
                 for h in range(HEADS)], axis=0) if with_output else None
            v = (v, v_rows, v_diag)
            qs = q_ref[0, rows, :].astype(F32) * (DK ** -0.5) if with_output else None
            lg = [(lg_hi[0, rows, d * K_W:(d + 1) * K_W], lg_lo[0, rows, d * K_W:(d + 1) * K_W]) for d in range(2)]
            loaded.append((n, rows, k, v, qs, lg))
        chains = [(n, rows, k, v, qs, lg[dirn], dirn) for n, rows, k, v, qs, lg in loaded for dirn in range(2)]
        stores = []
        if with_output:
            for n, rows, *_ in loaded:
                stores.append((gain_scr, (rows, slice(None)),
                               _silu(r_ref[0, rows, :].astype(F32)) * _row(nw_ref, layer)))
        gs = [_dot(cum[dirn], hi) + _dot(cum[dirn], lo) for *_, (hi, lo), dirn in chains]
        score_list = []
        for (n, rows, k, (v, v_rows, v_diag), qs, _, dirn), g in zip(chains, gs):
            g_last = g[last_row[dirn]:last_row[dirn] + 1]
            stores.append((dec_scr, (dirn, pl.ds(pl.multiple_of(n * 8, 8), 8), slice(None)),
                           jnp.broadcast_to(jnp.exp(g_last), (8, K_W))))
            if with_output:
                g_mid = g[mid_row[dirn]:mid_row[dirn] + 1]
                q_in = qs * jnp.exp(g - g_mid)
                k_in = k * jnp.exp(g_mid - g)
                score_list.append(_dot_nt(q_in.astype(BF16), stack_heads(k_in.astype(BF16))))
                stores.append((qst_scr, (dirn, rows, slice(None)), (q_in * jnp.exp(g_mid)).astype(BF16)))
                k_upd = stack_heads((k_in * jnp.exp(g_last - g_mid)).astype(BF16))
            else:
                k_upd = stack_heads((k * jnp.exp(g_last - g)).astype(BF16))
            stores.append((ds_scr, (dirn, pl.ds(pl.multiple_of(n * DV, DV), DV), slice(None)),
                           _dot_tn(v_rows, k_upd)))
        if with_output:
            for (n, rows, k, (v, v_rows, v_diag), qs, _, dirn), sc in zip(chains, score_list):
                scores = jnp.where(score_mask[dirn], sc, 0.0).astype(BF16)
                stores.append((o_scr, (dirn, rows, slice(None)), _dot(scores, v_diag)))
        for ref, idx, val in stores:
            ref[idx] = val
        return carry

    lax.fori_loop(0, n_chunks // per_iter, chunk_local, 0)

    def scan_step(i, finish):
        steps = ((0, i), (1, n_chunks - 1 - i))
        new_st, inters = [], []
        for dirn, n in steps:
            st = st_ref[dirn]
            if with_output:
                q_st = stack_heads(qst_scr[dirn, chunk_rows(n), :])
                inters.append(_dot_nt(q_st, st.astype(BF16)))
            dec = dec_scr[dirn, pl.ds(pl.multiple_of(n * 8, 8), 8), :]
            new_st.append(dec[0:1] * st + ds_scr[dirn, pl.ds(pl.multiple_of(n * DV, DV), DV), :])
        for (dirn, n), st in zip(steps, new_st):
            st_ref[dirn] = st
        if with_output:
            for (dirn, n), inter in zip(steps, inters):
                rows = chunk_rows(n)
                for h in range(HEADS):
                    vs = slice(h * DV, (h + 1) * DV)
                    o = o_scr[dirn, rows, vs] + inter[h * CHUNK:(h + 1) * CHUNK]
                    if finish:
                        o = o + o_scr[1 - dirn, rows, vs]
                        y_ref[0, rows, vs] = (_rms(o, gain_scr[rows, vs])).astype(BF16)
                    else:
                        o_scr[dirn, rows, vs] = o

    half = n_chunks // 2
    unroll = min(16, half)

    def scan_first(i, carry):
        scan_step(i, False)
        return carry

    def scan_second(i, carry):
        scan_step(i, True)
        return carry

    lax.fori_loop(0, half, scan_first, 0, unroll=unroll)
    lax.fori_loop(half, n_chunks, scan_second, 0, unroll=unroll)

    if with_state:
        s_out_ref[0] = st_ref[...]


def _gla(p, lay, i, s0, with_output, with_state):
    b, l, _ = p["k"].shape
    seq3 = lambda bi: (bi, 0, 0)
    in_specs = [pl.BlockSpec((1, l, K_W), seq3), pl.BlockSpec((1, l, V_W), seq3),
                pl.BlockSpec((1, l, 2 * K_W), seq3), pl.BlockSpec((1, l, 2 * K_W), seq3),
                pl.BlockSpec((1, 2, DV, K_W), lambda bi: (bi, 0, 0, 0))]
    args = [p["k"], p["v"], p["lg_hi"], p["lg_lo"], s0]
    out_specs, out_shape = [], []
    scratch = [pltpu.VMEM((2, DV, K_W), F32),
               pltpu.VMEM((2, l // CHUNK * DV, K_W), F32), pltpu.VMEM((2, l // CHUNK * 8, K_W), F32)]
    if with_output:
        in_specs += [pl.BlockSpec((1, l, K_W), seq3), pl.BlockSpec((1, l, V_W), seq3),
                     _resident(lay["gla_norm_w"].shape)]
        args += [p["q"], p["r"], lay["gla_norm_w"]]
        out_specs.append(pl.BlockSpec((1, l, V_W), seq3))
        out_shape.append(jax.ShapeDtypeStruct((b, l, V_W), BF16))
        scratch += [pltpu.VMEM((2, l, V_W), F32), pltpu.VMEM((2, l, K_W), BF16), pltpu.VMEM((l, V_W), F32)]
    if with_state:
        out_specs.append(pl.BlockSpec((1, 2, DV, K_W), lambda bi: (bi, 0, 0, 0)))
        out_shape.append(jax.ShapeDtypeStruct((b, 2, DV, K_W), F32))
    outs = pl.pallas_call(
        functools.partial(_gla_kernel, layer=i, seq=l, with_output=with_output, with_state=with_state),
        grid=(b,),
        in_specs=in_specs,
        out_specs=out_specs,
        out_shape=out_shape,
        scratch_shapes=scratch,
        compiler_params=_params(("parallel",), "gla"),
        name="gla",
    )(*args)
    outs = list(outs)
    y = outs.pop(0) if with_output else None
    s = outs.pop(0) if with_state else None
    return y, s


def _fourier_kernel(uf_ref, chan_ref, pos_ref, yf_ref, ab_scr, *, seq):
    ab = _dot(uf_ref[0], chan_ref[...])
    ab_scr[0:seq, :] = ab[:, :FOUR_W].astype(BF16)
    ab_scr[seq:2 * seq, :] = ab[:, FOUR_W:].astype(BF16)
    rb = min(seq, 512)
    for i in range(seq // rb):
        yf_ref[0, i * rb:(i + 1) * rb, :] = _dot(pos_ref[i * rb:(i + 1) * rb, :], ab_scr[...]).astype(BF16)


FFT_RADIX = 8


def _fourier_fft_kernel(uf_ref, chan_ref, sub_ref, twc_ref, tws_ref, yf_ref, uf_scr, *, seq):
    n = seq // FFT_RADIX
    lanes = uf_scr.shape[2]
    n_lane_tiles = FOUR_W // lanes
    for j in range(n_lane_tiles):
        uf_scr[j] = uf_ref[0, :, j * lanes:(j + 1) * lanes].astype(F32)

    def tokens(r):
        parts = [uf_scr[j, pl.ds(r, n, stride=FFT_RADIX), :] for j in range(n_lane_tiles)]
        return jnp.concatenate(parts, axis=1).astype(BF16)

    ab = [_dot(tokens(r), chan_ref[...]) for r in range(FFT_RADIX)]
    pq = []
    for ab_r in ab:
        a, b = ab_r[:, :FOUR_W].astype(BF16), ab_r[:, FOUR_W:].astype(BF16)
        w = jnp.concatenate([jnp.concatenate([a, b], axis=1), jnp.concatenate([-b, a], axis=1)], axis=0)
        pq.append(_dot(sub_ref[...], w))
    re, im = [pq[0][:, :FOUR_W]], [pq[0][:, FOUR_W:]]
    for r in range(1, FFT_RADIX):
        p, q = pq[r][:, :FOUR_W], pq[r][:, FOUR_W:]
        c, s = twc_ref[r - 1], tws_ref[r - 1]
        re.append(c * p - s * q)
        im.append(s * p + c * q)

    def quarter(z0, z1, z2, z3):
        s02 = (z0[0] + z2[0], z0[1] + z2[1])
        d02 = (z0[0] - z2[0], z0[1] - z2[1])
        s13 = (z1[0] + z3[0], z1[1] + z3[1])
        d13 = (z1[0] - z3[0], z1[1] - z3[1])
        return ((s02[0] + s13[0], s02[1] + s13[1]), (d02[0] - d13[1], d02[1] + d13[0]),
                (s02[0] - s13[0], s02[1] - s13[1]), (d02[0] + d13[1], d02[1] - d13[0]))

    zs = list(zip(re, im))
    even = quarter(zs[0], zs[2], zs[4], zs[6])
    odd = quarter(zs[1], zs[3], zs[5], zs[7])
    half = 0.5 ** 0.5
    odd_re = (odd[0][0], (odd[1][0] - odd[1][1]) * half, -odd[2][1], -(odd[3][0] + odd[3][1]) * half)
    for q in range(4):
        yf_ref[0, q * n:(q + 1) * n, :] = (even[q][0] + odd_re[q]).astype(BF16)
        yf_ref[0, (q + 4) * n:(q + 5) * n, :] = (even[q][0] - odd_re[q]).astype(BF16)


def _fourier(uf, tabs):
    b, l, _ = uf.shape
    tok = pl.BlockSpec((1, l, FOUR_W), lambda bi: (bi, 0, 0))
    if len(tabs) == 4:
        body, name = _fourier_fft_kernel, "fourier_fft"
        scratch = pltpu.VMEM((FOUR_W // V7X_LANES, l, V7X_LANES), F32)
    else:
        body, name = _fourier_kernel, "fourier"
        scratch = pltpu.VMEM((2 * l, FOUR_W), BF16)
    return pl.pallas_call(
        functools.partial(body, seq=l),
        grid=(b,),
        in_specs=[tok] + [_resident(t.shape) for t in tabs],
        out_specs=tok,
        out_shape=jax.ShapeDtypeStruct((b, l, FOUR_W), BF16),
        scratch_shapes=[scratch],
        compiler_params=_params(("parallel",), "fourier"),
        name=name,
    )(uf, *tabs)


CONV_PAD = V7X_SUBLANES_F32
HALO_ROWS = V7X_SUBLANES_BF16


def _out_ffn_kernel(*refs, layer, add_pos, n_sub):
    it = iter(refs)
    yg_ref, yf_ref, bg_ref, cg_ref, uc_ref = (next(it) for _ in range(5))
    cg_prev_ref, uc_prev_ref, cg_next_ref, uc_next_ref, cw_ref, cb_ref = (next(it) for _ in range(6))
    wo_ref, x_ref = next(it), next(it)
    pos_ref = next(it) if add_pos else None
    gmix_ref, gatemix_ref, gpre_ref, shift_ref, scale_ref, gate_ref, gpost_ref = (next(it) for _ in range(7))
    win_ref, wout_ref, o_ref, z_scr = next(it), next(it), next(it), next(it)
    tm = x_ref.shape[1]
    subs = [slice(i * tm // n_sub, (i + 1) * tm // n_sub) for i in range(n_sub)]

    t, nt = pl.program_id(1), pl.num_programs(1)
    z_before = (cg_prev_ref[0].astype(F32) * uc_prev_ref[0].astype(F32))[HALO_ROWS - CONV_PAD:]
    z_after = (cg_next_ref[0].astype(F32) * uc_next_ref[0].astype(F32))[:CONV_PAD]
    z_scr[0:CONV_PAD, :] = jnp.where(t > 0, z_before, 0.0)
    z_scr[CONV_PAD + tm:, :] = jnp.where(t < nt - 1, z_after, 0.0)
    z_scr[CONV_PAD:CONV_PAD + tm, :] = cg_ref[0].astype(F32) * uc_ref[0].astype(F32)
    ycs = []
    for r in subs:
        s, n = CONV_PAD + r.start, r.stop - r.start
        y = (cw_ref[0:1] * z_scr[s - 1:s - 1 + n, :] + cw_ref[1:2] * z_scr[s:s + n, :]
             + cw_ref[2:3] * z_scr[s + 1:s + 1 + n, :] + _row(cb_ref, layer))
        ycs.append((bg_ref[0, r, :].astype(F32) * y).astype(BF16))

    ys = [(_dot(yg_ref[0, r, :], wo_ref[0:V_W]) + _dot(yf_ref[0, r, :], wo_ref[V_W:V_W + FOUR_W])
           + _dot(yc, wo_ref[V_W + FOUR_W:])) for r, yc in zip(subs, ycs)]
    xs, hs = [], []
    for r, y in zip(subs, ys):
        x = x_ref[0, r, :]
        if add_pos:
            x = x + pos_ref[r, :]
        x = x + gatemix_ref[...] * _rms(y, _row(gmix_ref, layer))
        xs.append(x)
        hs.append((_rms(x, _row(gpre_ref, layer)) * (1.0 + scale_ref[...]) + shift_ref[...]).astype(BF16))
    ys = []
    for h in hs:
        y = jnp.zeros((h.shape[0], D_MODEL), F32)
        for lo, hi in FFN_TILES:
            a = _dot(h, win_ref[:, lo:hi])
            u = _dot(h, win_ref[:, HIDDEN + lo:HIDDEN + hi])
            y = y + _dot((_silu(a) * u).astype(BF16), wout_ref[lo:hi, :])
        ys.append(y)
    for r, x, y in zip(subs, xs, ys):
        o_ref[0, r, :] = x + gate_ref[...] * _rms(y, _row(gpost_ref, layer))


def _out_ffn(yg, yf, p, x, pos, lay, i, wts, mod_row, tm):
    w_out, w_ffn_in, w_ffn_out = wts
    b, l, d = x.shape
    add_pos = pos is not None
    tok = lambda bi, ti: (bi, ti, 0)
    halo_per_tile, n_halo = tm // HALO_ROWS, l // HALO_ROWS
    before = lambda bi, ti: (bi, jnp.maximum(ti * halo_per_tile - 1, 0), 0)
    after = lambda bi, ti: (bi, jnp.minimum((ti + 1) * halo_per_tile, n_halo - 1), 0)
    conv_tok = pl.BlockSpec((1, tm, CONV_W), tok)
    in_specs = [pl.BlockSpec((1, tm, V_W), tok), pl.BlockSpec((1, tm, FOUR_W), tok), conv_tok, conv_tok, conv_tok,
                pl.BlockSpec((1, HALO_ROWS, CONV_W), before), pl.BlockSpec((1, HALO_ROWS, CONV_W), before),
                pl.BlockSpec((1, HALO_ROWS, CONV_W), after), pl.BlockSpec((1, HALO_ROWS, CONV_W), after),
                _layer(lay["conv_w"], i), _resident(lay["conv_b"].shape),
                _layer(w_out, 0), pl.BlockSpec((1, tm, d), tok)]
    args = [yg, yf, p["bg"], p["cg"], p["uc"], p["cg"], p["uc"], p["cg"], p["uc"],
            lay["conv_w"], lay["conv_b"], w_out, x]
    if add_pos:
        in_specs.append(pl.BlockSpec((tm, d), lambda bi, ti: (ti, 0)))
        args.append(pos)
    in_specs += [_resident(lay["g_mix_post"].shape), _mod_spec(i, 2, mod_row), _resident(lay["g_ffn_pre"].shape),
                 _mod_spec(i, 3, mod_row), _mod_spec(i, 4, mod_row), _mod_spec(i, 5, mod_row),
                 _resident(lay["g_ffn_post"].shape), _layer(w_ffn_in, 0), _layer(w_ffn_out, 0)]
    args += [lay["g_mix_post"], lay["mod"], lay["g_ffn_pre"], lay["mod"], lay["mod"], lay["mod"],
             lay["g_ffn_post"], w_ffn_in, w_ffn_out]
    return pl.pallas_call(
        functools.partial(_out_ffn_kernel, layer=i, add_pos=add_pos, n_sub=2 if tm >= TOKEN_TILE else 1),
        grid=(b, l // tm),
        in_specs=in_specs,
        out_specs=pl.BlockSpec((1, tm, d), tok),
        out_shape=jax.ShapeDtypeStruct((b, l, d), F32),
        scratch_shapes=[pltpu.VMEM((tm + 2 * CONV_PAD, CONV_W), F32)],
        compiler_params=_params(("parallel", "parallel"), "out_ffn"),
        name="out_ffn",
    )(*args)


def _arrange_w_dec(w_f, w_b):
    wd = jnp.zeros((DEPTH, LR_PAD, 2 * K_W), F32)
    wd = wd.at[:, 0:RANK, :K_W].set(w_f).at[:, RANK:2 * RANK, K_W:].set(w_b)
    return wd.astype(BF16)


def kernel(x, c, ctx, c_ctx, w_mod, b_mod, g_mix_pre, g_mix_post, g_ffn_pre, g_ffn_post,
           w_in, w_dec_f, b_dec_f, w_dec_b, b_dec_b, gla_norm_w, conv_w, conv_b, w_out,
           w_ffn_in, w_ffn_out):
    b, n_lat, d = x.shape
    n_ctx = ctx.shape[1]
    assert d == D_MODEL and w_in.shape == (DEPTH, D_MODEL, _C_END)
    assert n_lat % (2 * TOKEN_TILE) == 0 and n_lat % (FFT_RADIX * V7X_SUBLANES_BF16) == 0
    assert n_ctx % CHUNK == 0 and n_ctx <= TOKEN_TILE and b + 1 <= MOD_ROWS

    pos = jnp.asarray(_pos_embed(n_lat, d))
    tabs = {n_lat: _dft_tables(n_lat, FFT_RADIX), n_ctx: _dft_tables(n_ctx, 1)}

    cvec = jnp.zeros((MOD_ROWS, d), F32).at[:b].set(c).at[b].set(c_ctx)
    ctx_row = b
    mod, w_in_arranged = _modulation(cvec, w_mod, b_mod, jnp.swapaxes(w_in, 1, 2))
    lay = {
        "mod": mod.reshape(DEPTH, MOD_ROWS, 1, N_MOD * d),
        "w_in": w_in_arranged,
        "w_dec": _arrange_w_dec(w_dec_f, w_dec_b),
        "b_dec": jnp.concatenate([b_dec_f, b_dec_b], axis=-1).reshape(DEPTH, 1, 2 * K_W),
        "gla_norm_w": gla_norm_w,
        "conv_w": conv_w,
        "conv_b": conv_b,
        "g_mix_pre": g_mix_pre,
        "g_mix_post": g_mix_post,
        "g_ffn_pre": g_ffn_pre,
        "g_ffn_post": g_ffn_post,
    }

    def mix_and_ffn(i, xs, pos_s, p, wts, mod_row, s0, want_state, tm):
        yg, s = _gla(p, lay, i, s0, True, want_state)
        yf = _fourier(p["uf"], tabs[xs.shape[1]])
        return _out_ffn(yg, yf, p, xs, pos_s, lay, i, wts, mod_row, tm), s

    zero_state = jnp.zeros((b, 2, DV, K_W), F32)
    xc = ctx
    tm = TOKEN_TILE
    for i in range(DEPTH):
        pos_i = pos if i == 0 else None
        p_lat, wts = _inproj(x, pos_i, lay, i, None, _OUT_GROUPS_FULL, 2 * tm, cast=(w_out, w_ffn_in, w_ffn_out))
        if i == DEPTH - 1:
            p_ctx, _ = _inproj(xc, None, lay, i, ctx_row, _OUT_GROUPS_STATE, n_ctx)
            _, s = _gla(p_ctx, lay, i, zero_state, False, True)
        else:
            p_ctx, _ = _inproj(xc, None, lay, i, ctx_row, _OUT_GROUPS_FULL, n_ctx)
            xc, s = mix_and_ffn(i, xc, None, p_ctx, wts, ctx_row, zero_state, True, n_ctx)
        x, _ = mix_and_ffn(i, x, pos_i, p_lat, wts, None, s, False, tm)
    return x
```

```python
import functools

import numpy as np
import jax
import jax.numpy as jnp
from jax import lax
from jax.experimental import pallas as pl
from jax.experimental.pallas import tpu as pltpu

F32 = jnp.float32
BF16 = jnp.bfloat16

D_MODEL = 1024
DEPTH = 2
GRID_W = 64
N_MOD = 6
HEADS = 4
DK = 64
DV = 128
K_W = HEADS * DK
V_W = HEADS * DV
RANK = 16
TAU = 16.0
CHUNK = 64
FOUR_W = 256
FOUR_G = 4
CONV_W = 256
HIDDEN = 2816
EPS = 1e-6

V7X_LANES = 128
V7X_SUBLANES_F32 = 8
V7X_SUBLANES_BF16 = 16
V7X_VMEM_BYTES = 64 * 1024 * 1024

LR_PAD = V7X_LANES
MOD_ROWS = V7X_SUBLANES_BF16
TOKEN_TILE = 512
FFN_TILES = ((0, 1536), (1536, HIDDEN))
VMEM_MB = {"modulation": 56, "inproj": 56, "gla": 56, "fourier": 48, "out_ffn": 56}

_C_K, _C_V, _C_LR, _C_Q, _C_R, _C_FOUR, _C_BG, _C_CG, _C_UC, _C_END = (
    0, 256, 768, 800, 1056, 1568, 1824, 2080, 2336, 2592)
_GROUP_WIDTHS = (("k", K_W), ("v", V_W), ("q", K_W), ("r", V_W), ("uf", FOUR_W), ("bg", CONV_W),
                 ("cg", CONV_W), ("uc", CONV_W))
_MAIN_W = sum(w for _, w in _GROUP_WIDTHS)
_IN_W = _MAIN_W + LR_PAD


def _groups(names):
    out, lo = [], 0
    for name, w in _GROUP_WIDTHS:
        if name in names:
            out.append((name, lo, lo + w))
        lo += w
    return tuple(out)


_OUT_GROUPS_FULL = _groups([name for name, _ in _GROUP_WIDTHS])
_OUT_GROUPS_STATE = _groups(("k", "v"))


def _params(semantics, name):
    assert VMEM_MB[name] * 1024 * 1024 < V7X_VMEM_BYTES
    return pltpu.CompilerParams(dimension_semantics=semantics,
                                vmem_limit_bytes=VMEM_MB[name] * 1024 * 1024)


def _resident(shape):
    zeros = (0,) * len(shape)
    return pl.BlockSpec(shape, lambda *_: zeros, pipeline_mode=pl.Buffered(1))


def _layer(arr, i):
    shape = tuple(arr.shape[1:])
    zeros = (0,) * len(shape)
    return pl.BlockSpec((None,) + shape, lambda *_: (i,) + zeros, pipeline_mode=pl.Buffered(1))


def _mod_spec(i, j, row):
    if row is None:
        return pl.BlockSpec((None, None, 1, D_MODEL), lambda bi, *_: (i, bi, 0, j))
    return pl.BlockSpec((None, None, 1, D_MODEL), lambda *_: (i, row, 0, j))


def _row(ref, layer):
    return ref[layer:layer + 1, :]


def _silu(a):
    return a / (1.0 + jnp.exp(-a))


def _rms(x, g):
    return x * lax.rsqrt(jnp.mean(x * x, axis=-1, keepdims=True) + EPS) * g


def _dot(a, b):
    return jnp.dot(a, b, preferred_element_type=F32)


def _dot_nt(a, b):
    return lax.dot_general(a, b, (((1,), (1,)), ((), ())), preferred_element_type=F32)


def _dot_tn(a, b):
    return lax.dot_general(a, b, (((0,), (0,)), ((), ())), preferred_element_type=F32)


def _pos_embed(n_tokens, dim):
    rows = n_tokens // GRID_W
    row = np.repeat(np.arange(rows, dtype=np.float32), GRID_W)
    col = np.tile(np.arange(GRID_W, dtype=np.float32), rows)
    quarter = dim // 4
    freqs = (1.0 / (10000.0 ** (np.arange(quarter, dtype=np.float32) / quarter))).astype(np.float32)

    def enc(p):
        ang = (p[:, None] * freqs[None, :]).astype(np.float32)
        return np.concatenate([np.sin(ang), np.cos(ang)], axis=-1)

    return np.concatenate([enc(row), enc(col)], axis=-1).astype(np.float32)


def _dft_cos_sin(n):
    idx = np.arange(n, dtype=np.int64)
    ang = 2.0 * np.pi * ((idx[:, None] * idx[None, :]) % n).astype(np.float64) / n
    return np.cos(ang) / np.sqrt(n), np.sin(ang) / np.sqrt(n)


def _dft_tables(seq, radix):
    gw = FOUR_W // FOUR_G
    cc, sc = _dft_cos_sin(gw)
    eye = np.eye(FOUR_G)
    chan_tab = jnp.asarray(np.concatenate([np.kron(eye, cc), np.kron(eye, sc)], axis=1), dtype=F32).astype(BF16)
    if radix == 1:
        cl, sl = _dft_cos_sin(seq)
        return chan_tab, jnp.asarray(np.concatenate([cl, -sl], axis=1), dtype=F32).astype(BF16)
    n = seq // radix
    cn, sn = _dft_cos_sin(n)
    sub_tab = np.concatenate([cn, sn], axis=1) * np.sqrt(n / seq)
    ang = 2.0 * np.pi * (np.arange(1, radix)[:, None] * np.arange(n)[None, :]).astype(np.float64) / seq
    twc = np.broadcast_to(np.cos(ang)[:, :, None], (radix - 1, n, FOUR_W))
    tws = np.broadcast_to(np.sin(ang)[:, :, None], (radix - 1, n, FOUR_W))
    return (chan_tab, jnp.asarray(sub_tab, dtype=F32).astype(BF16),
            jnp.asarray(twc, dtype=F32), jnp.asarray(tws, dtype=F32))


def _mod_kernel(c_ref, w_ref, b_ref, wint_ref, o_ref, wout_ref):
    s = _silu(c_ref[...]).astype(BF16)
    o_ref[0] = _dot(s, w_ref[0].astype(BF16)) + b_ref[0]

    @pl.when(pl.program_id(1) == 0)
    def _():
        for col in range(0, _IN_W, V7X_LANES):
            if col < _C_LR:
                rows = wint_ref[0, col:col + V7X_LANES, :]
            elif col < _MAIN_W:
                src = col + _C_Q - _C_LR
                rows = wint_ref[0, src:src + V7X_LANES, :]
            else:
                rows = jnp.concatenate([wint_ref[0, _C_LR:_C_Q, :],
                                        jnp.zeros((LR_PAD - 2 * RANK, D_MODEL), F32)], axis=0)
            wout_ref[0, :, col:col + V7X_LANES] = rows.T.astype(BF16)


def _modulation(cvec, w_mod, b_mod, w_in_t):
    steps = 4
    n = N_MOD * D_MODEL
    tn = n // steps
    assert tn % V7X_LANES == 0
    return pl.pallas_call(
        _mod_kernel,
        grid=(DEPTH, steps),
        in_specs=[pl.BlockSpec((MOD_ROWS, D_MODEL), lambda i, j: (0, 0)),
                  pl.BlockSpec((1, D_MODEL, tn), lambda i, j: (i, 0, j)),
                  pl.BlockSpec((1, 1, tn), lambda i, j: (i, 0, j)),
                  pl.BlockSpec((1, _C_END, D_MODEL), lambda i, j: (i, 0, 0))],
        out_specs=[pl.BlockSpec((1, MOD_ROWS, tn), lambda i, j: (i, 0, j)),
                   pl.BlockSpec((1, D_MODEL, _IN_W), lambda i, j: (i, 0, 0))],
        out_shape=[jax.ShapeDtypeStruct((DEPTH, MOD_ROWS, n), F32),
                   jax.ShapeDtypeStruct((DEPTH, D_MODEL, _IN_W), BF16)],
        compiler_params=_params(("arbitrary", "arbitrary"), "modulation"),
        name="modulation",
    )(cvec, w_mod, b_mod.reshape(DEPTH, 1, n), w_in_t)


def _inproj_kernel(*refs, layer, add_pos, groups, n_cast):
    it = iter(refs)
    x_ref = next(it)
    pos_ref = next(it) if add_pos else None
    g_ref, shift_ref, scale_ref, w_ref, wd_ref, bd_ref = (next(it) for _ in range(6))
    cast_in = [next(it) for _ in range(n_cast)]
    out_refs = list(it)
    for src, dst in zip(cast_in, out_refs[len(out_refs) - n_cast:]):
        dst[0] = src[...].astype(BF16)
    out_refs = out_refs[:len(out_refs) - n_cast]
    lg_hi_ref, lg_lo_ref = out_refs[-2:]
    tm = x_ref.shape[1]
    n_sub = 2 if tm >= TOKEN_TILE else 1
    subs = [slice(i * tm // n_sub, (i + 1) * tm // n_sub) for i in range(n_sub)]
    hs = []
    for r in subs:
        x = x_ref[0, r, :]
        if add_pos:
            x = x + pos_ref[r, :]
        hs.append((_rms(x, _row(g_ref, layer)) * (1.0 + scale_ref[...]) + shift_ref[...]).astype(BF16))
    def project(r, h):
        for o_ref, (_, lo, hi_col) in zip(out_refs, groups):
            o_ref[0, r, :] = _dot(h, w_ref[:, lo:hi_col]).astype(BF16)

    low_rank = [_dot(h, w_ref[:, _MAIN_W:_IN_W]).astype(BF16) for h in hs]
    project(subs[0], hs[0])
    zs = [_dot(lr, wd_ref[...]) + bd_ref[...] for lr in low_rank]
    for r, h in list(zip(subs, hs))[1:]:
        project(r, h)
    for r, z in zip(subs, zs):
        logg = (jnp.minimum(z, 0.0) - jnp.log(1.0 + jnp.exp(-jnp.abs(z)))) * (1.0 / TAU)
        hi = logg.astype(BF16)
        lg_hi_ref[0, r, :] = hi
        lg_lo_ref[0, r, :] = (logg - hi.astype(F32)).astype(BF16)


def _inproj(x, pos, lay, i, mod_row, groups, tm, cast=()):
    b, l, d = x.shape
    add_pos = pos is not None
    n_t = l // tm
    tok = lambda bi, ti: (bi, ti, 0)
    in_specs = [pl.BlockSpec((1, tm, d), tok)]
    args = [x]
    if add_pos:
        in_specs.append(pl.BlockSpec((tm, d), lambda bi, ti: (ti, 0)))
        args.append(pos)
    in_specs += [_resident(lay["g_mix_pre"].shape), _mod_spec(i, 0, mod_row), _mod_spec(i, 1, mod_row),
                 _layer(lay["w_in"], i), _layer(lay["w_dec"], i), _layer(lay["b_dec"], i)]
    args += [lay["g_mix_pre"], lay["mod"], lay["mod"], lay["w_in"], lay["w_dec"], lay["b_dec"]]
    widths = [hi - lo for _, lo, hi in groups] + [2 * K_W, 2 * K_W]
    names = [name for name, _, _ in groups] + ["lg_hi", "lg_lo"]
    out_specs = [pl.BlockSpec((1, tm, w), tok) for w in widths]
    out_shape = [jax.ShapeDtypeStruct((b, l, w), BF16) for w in widths]
    for w in cast:
        _, rows, cols = w.shape
        slab = rows // (b * n_t)
        assert slab * b * n_t == rows and slab % V7X_SUBLANES_BF16 == 0
        in_specs.append(pl.BlockSpec((None, slab, cols), lambda bi, ti: (i, bi * n_t + ti, 0)))
        out_specs.append(pl.BlockSpec((1, slab, cols), lambda bi, ti: (0, bi * n_t + ti, 0)))
        out_shape.append(jax.ShapeDtypeStruct((1, rows, cols), BF16))
        args.append(w)
    outs = pl.pallas_call(
        functools.partial(_inproj_kernel, layer=i, add_pos=add_pos, groups=groups, n_cast=len(cast)),
        grid=(b, n_t),
        in_specs=in_specs,
        out_specs=out_specs,
        out_shape=out_shape,
        compiler_params=_params(("parallel", "parallel"), "inproj"),
        name="inproj",
    )(*args)
    return dict(zip(names, outs)), list(outs[len(names):])


def _gla_kernel(*refs, layer, seq, with_output, with_state):
    it = iter(refs)
    k_ref, v_ref, lg_hi, lg_lo, s0_ref = (next(it) for _ in range(5))
    if with_output:
        q_ref, r_ref, nw_ref = next(it), next(it), next(it)
        y_ref = next(it)
    if with_state:
        s_out_ref = next(it)
    st_ref = next(it)
    ds_scr = next(it)
    dec_scr = next(it)
    if with_output:
        o_scr = next(it)
        qst_scr = next(it)
        gain_scr = next(it)
    n_chunks = seq // CHUNK

    st_ref[...] = s0_ref[0]

    row = lax.broadcasted_iota(jnp.int32, (CHUNK, CHUNK), 0)
    col = lax.broadcasted_iota(jnp.int32, (CHUNK, CHUNK), 1)
    cum = (jnp.where(row >= col, 1.0, 0.0).astype(BF16), jnp.where(row <= col, 1.0, 0.0).astype(BF16))
    lane_head = lax.broadcasted_iota(jnp.int32, (1, K_W), 1) // DK
    head_mask = [lane_head == h for h in range(HEADS)]
    srow = lax.broadcasted_iota(jnp.int32, (CHUNK, HEADS * CHUNK), 0)
    scol = lax.broadcasted_iota(jnp.int32, (CHUNK, HEADS * CHUNK), 1) % CHUNK
    score_mask = (srow >= scol, srow <= scol)
    zero_v = jnp.zeros((CHUNK, DV), BF16)
    last_row = (CHUNK - 1, 0)
    mid_row = (CHUNK // 2 - 1, CHUNK // 2)

    def dec_rows(n):
        return pl.ds(pl.multiple_of(n * V7X_SUBLANES_F32, V7X_SUBLANES_F32), V7X_SUBLANES_F32)

    def stack_heads(a):
        zero = jnp.zeros_like(a)
        return jnp.concatenate([jnp.where(head_mask[h], a, zero) for h in range(HEADS)], axis=0)

    def chunk_rows(n):
        return pl.ds(pl.multiple_of(n * CHUNK, CHUNK), CHUNK)

    per_iter = min(8, n_chunks)

    def chunk_local(i, carry):
        loaded = []
        for u in range(per_iter):
            n = i * per_iter + u
            rows = chunk_rows(n)
            k = k_ref[0, rows, :].astype(F32)
            v = v_ref[0, rows, :]
            v_heads = [v[:, h * DV:(h + 1) * DV] for h in range(HEADS)]
            v_rows = jnp.concatenate(v_heads, axis=0)
            v_diag = jnp.concatenate(
                [jnp.concatenate([v_heads[h] if hh == h else zero_v for hh in range(HEADS)], axis=1)
                 for h in range(HEADS)], axis=0) if with_output else None
            v = (v, v_rows, v_diag)
            qs = q_ref[0, rows, :].astype(F32) * (DK ** -0.5) if with_output else None
            lg = [(lg_hi[0, rows, d * K_W:(d + 1) * K_W], lg_lo[0, rows, d * K_W:(d + 1) * K_W]) for d in range(2)]
            loaded.append((n, rows, k, v, qs, lg))
        chains = [(n, rows, k, v, qs, lg[dirn], dirn) for n, rows, k, v, qs, lg in loaded for dirn in range(2)]
        stores = []
        if with_output:
            for n, rows, *_ in loaded:
                stores.append((gain_scr, (rows, slice(None)),
                               _silu(r_ref[0, rows, :].astype(F32)) * _row(nw_ref, layer)))
        gs = [_dot(cum[dirn], hi) + _dot(cum[dirn], lo) for *_, (hi, lo), dirn in chains]
        score_list = []
        for (n, rows, k, (v, v_rows, v_diag), qs, _, dirn), g in zip(chains, gs):
            g_last = g[last_row[dirn]:last_row[dirn] + 1]
            stores.append((dec_scr, (dirn, dec_rows(n), slice(None)),
                           jnp.broadcast_to(jnp.exp(g_last), (V7X_SUBLANES_F32, K_W))))
            if with_output:
                g_mid = g[mid_row[dirn]:mid_row[dirn] + 1]
                q_in = qs * jnp.exp(g - g_mid)
                k_in = k * jnp.exp(g_mid - g)
                score_list.append(_dot_nt(q_in.astype(BF16), stack_heads(k_in.astype(BF16))))
                stores.append((qst_scr, (dirn, rows, slice(None)), (q_in * jnp.exp(g_mid)).astype(BF16)))
                k_upd = stack_heads((k_in * jnp.exp(g_last - g_mid)).astype(BF16))
            else:
                k_upd = stack_heads((k * jnp.exp(g_last - g)).astype(BF16))
            stores.append((ds_scr, (dirn, pl.ds(pl.multiple_of(n * DV, DV), DV), slice(None)),
                           _dot_tn(v_rows, k_upd)))
        if with_output:
            for (n, rows, k, (v, v_rows, v_diag), qs, _, dirn), sc in zip(chains, score_list):
                scores = jnp.where(score_mask[dirn], sc, 0.0).astype(BF16)
                stores.append((o_scr, (dirn, rows, slice(None)), _dot(scores, v_diag)))
        for ref, idx, val in stores:
            ref[idx] = val
        return carry

    lax.fori_loop(0, n_chunks // per_iter, chunk_local, 0)

    def scan_step(i, finish):
        steps = ((0, i), (1, n_chunks - 1 - i))
        new_st, inters = [], []
        for dirn, n in steps:
            st = st_ref[dirn]
            if with_output:
                q_st = stack_heads(qst_scr[dirn, chunk_rows(n), :])
                inters.append(_dot_nt(q_st, st.astype(BF16)))
            dec = dec_scr[dirn, dec_rows(n), :]
            new_st.append(dec[0:1] * st + ds_scr[dirn, pl.ds(pl.multiple_of(n * DV, DV), DV), :])
        for (dirn, n), st in zip(steps, new_st):
            st_ref[dirn] = st
        if with_output:
            for (dirn, n), inter in zip(steps, inters):
                rows = chunk_rows(n)
                for h in range(HEADS):
                    vs = slice(h * DV, (h + 1) * DV)
                    o = o_scr[dirn, rows, vs] + inter[h * CHUNK:(h + 1) * CHUNK]
                    if finish:
                        o = o + o_scr[1 - dirn, rows, vs]
                        y_ref[0, rows, vs] = (_rms(o, gain_scr[rows, vs])).astype(BF16)
                    else:
                        o_scr[dirn, rows, vs] = o

    half = n_chunks // 2
    unroll = min(16, half)

    def scan_first(i, carry):
        scan_step(i, False)
        return carry

    def scan_second(i, carry):
        scan_step(i, True)
        return carry

    lax.fori_loop(0, half, scan_first, 0, unroll=unroll)
    lax.fori_loop(half, n_chunks, scan_second, 0, unroll=unroll)

    if with_state:
        s_out_ref[0] = st_ref[...]


def _gla(p, lay, i, s0, with_output, with_state):
    b, l, _ = p["k"].shape
    seq3 = lambda bi: (bi, 0, 0)
    in_specs = [pl.BlockSpec((1, l, K_W), seq3), pl.BlockSpec((1, l, V_W), seq3),
                pl.BlockSpec((1, l, 2 * K_W), seq3), pl.BlockSpec((1, l, 2 * K_W), seq3),
                pl.BlockSpec((1, 2, DV, K_W), lambda bi: (bi, 0, 0, 0))]
    args = [p["k"], p["v"], p["lg_hi"], p["lg_lo"], s0]
    out_specs, out_shape = [], []
    scratch = [pltpu.VMEM((2, DV, K_W), F32),
               pltpu.VMEM((2, l // CHUNK * DV, K_W), F32),
               pltpu.VMEM((2, l // CHUNK * V7X_SUBLANES_F32, K_W), F32)]
    if with_output:
        in_specs += [pl.BlockSpec((1, l, K_W), seq3), pl.BlockSpec((1, l, V_W), seq3),
                     _resident(lay["gla_norm_w"].shape)]
        args += [p["q"], p["r"], lay["gla_norm_w"]]
        out_specs.append(pl.BlockSpec((1, l, V_W), seq3))
        out_shape.append(jax.ShapeDtypeStruct((b, l, V_W), BF16))
        scratch += [pltpu.VMEM((2, l, V_W), F32), pltpu.VMEM((2, l, K_W), BF16), pltpu.VMEM((l, V_W), F32)]
    if with_state:
        out_specs.append(pl.BlockSpec((1, 2, DV, K_W), lambda bi: (bi, 0, 0, 0)))
        out_shape.append(jax.ShapeDtypeStruct((b, 2, DV, K_W), F32))
    outs = pl.pallas_call(
        functools.partial(_gla_kernel, layer=i, seq=l, with_output=with_output, with_state=with_state),
        grid=(b,),
        in_specs=in_specs,
        out_specs=out_specs,
        out_shape=out_shape,
        scratch_shapes=scratch,
        compiler_params=_params(("parallel",), "gla"),
        name="gla",
    )(*args)
    outs = list(outs)
    y = outs.pop(0) if with_output else None
    s = outs.pop(0) if with_state else None
    return y, s


def _fourier_kernel(uf_ref, chan_ref, pos_ref, yf_ref, ab_scr, *, seq):
    ab = _dot(uf_ref[0], chan_ref[...])
    ab_scr[0:seq, :] = ab[:, :FOUR_W].astype(BF16)
    ab_scr[seq:2 * seq, :] = ab[:, FOUR_W:].astype(BF16)
    rb = min(seq, 512)
    for i in range(seq // rb):
        yf_ref[0, i * rb:(i + 1) * rb, :] = _dot(pos_ref[i * rb:(i + 1) * rb, :], ab_scr[...]).astype(BF16)


FFT_RADIX = 8


def _fourier_fft_kernel(uf_ref, chan_ref, sub_ref, twc_ref, tws_ref, yf_ref, uf_scr, *, seq):
    n = seq // FFT_RADIX
    lanes = uf_scr.shape[2]
    n_lane_tiles = FOUR_W // lanes
    for j in range(n_lane_tiles):
        uf_scr[j] = uf_ref[0, :, j * lanes:(j + 1) * lanes].astype(F32)

    def tokens(r):
        parts = [uf_scr[j, pl.ds(r, n, stride=FFT_RADIX), :] for j in range(n_lane_tiles)]
        return jnp.concatenate(parts, axis=1).astype(BF16)

    ab = [_dot(tokens(r), chan_ref[...]) for r in range(FFT_RADIX)]
    pq = []
    for ab_r in ab:
        a, b = ab_r[:, :FOUR_W].astype(BF16), ab_r[:, FOUR_W:].astype(BF16)
        w = jnp.concatenate([jnp.concatenate([a, b], axis=1), jnp.concatenate([-b, a], axis=1)], axis=0)
        pq.append(_dot(sub_ref[...], w))
    re, im = [pq[0][:, :FOUR_W]], [pq[0][:, FOUR_W:]]
    for r in range(1, FFT_RADIX):
        p, q = pq[r][:, :FOUR_W], pq[r][:, FOUR_W:]
        c, s = twc_ref[r - 1], tws_ref[r - 1]
        re.append(c * p - s * q)
        im.append(s * p + c * q)

    def quarter(z0, z1, z2, z3):
        s02 = (z0[0] + z2[0], z0[1] + z2[1])
        d02 = (z0[0] - z2[0], z0[1] - z2[1])
        s13 = (z1[0] + z3[0], z1[1] + z3[1])
        d13 = (z1[0] - z3[0], z1[1] - z3[1])
        return ((s02[0] + s13[0], s02[1] + s13[1]), (d02[0] - d13[1], d02[1] + d13[0]),
                (s02[0] - s13[0], s02[1] - s13[1]), (d02[0] + d13[1], d02[1] - d13[0]))

    zs = list(zip(re, im))
    even = quarter(zs[0], zs[2], zs[4], zs[6])
    odd = quarter(zs[1], zs[3], zs[5], zs[7])
    half = 0.5 ** 0.5
    odd_re = (odd[0][0], (odd[1][0] - odd[1][1]) * half, -odd[2][1], -(odd[3][0] + odd[3][1]) * half)
    for q in range(4):
        yf_ref[0, q * n:(q + 1) * n, :] = (even[q][0] + odd_re[q]).astype(BF16)
        yf_ref[0, (q + 4) * n:(q + 5) * n, :] = (even[q][0] - odd_re[q]).astype(BF16)


def _fourier(uf, tabs):
    b, l, _ = uf.shape
    tok = pl.BlockSpec((1, l, FOUR_W), lambda bi: (bi, 0, 0))
    if len(tabs) == 4:
        body, name = _fourier_fft_kernel, "fourier_fft"
        scratch = pltpu.VMEM((FOUR_W // V7X_LANES, l, V7X_LANES), F32)
    else:
        body, name = _fourier_kernel, "fourier"
        scratch = pltpu.VMEM((2 * l, FOUR_W), BF16)
    return pl.pallas_call(
        functools.partial(body, seq=l),
        grid=(b,),
        in_specs=[tok] + [_resident(t.shape) for t in tabs],
        out_specs=tok,
        out_shape=jax.ShapeDtypeStruct((b, l, FOUR_W), BF16),
        scratch_shapes=[scratch],
        compiler_params=_params(("parallel",), "fourier"),
        name=name,
    )(uf, *tabs)


CONV_PAD = V7X_SUBLANES_F32
HALO_ROWS = V7X_SUBLANES_BF16


def _out_ffn_kernel(*refs, layer, add_pos, n_sub):
    it = iter(refs)
    yg_ref, yf_ref, bg_ref, cg_ref, uc_ref = (next(it) for _ in range(5))
    cg_prev_ref, uc_prev_ref, cg_next_ref, uc_next_ref, cw_ref, cb_ref = (next(it) for _ in range(6))
    wo_ref, x_ref = next(it), next(it)
    pos_ref = next(it) if add_pos else None
    gmix_ref, gatemix_ref, gpre_ref, shift_ref, scale_ref, gate_ref, gpost_ref = (next(it) for _ in range(7))
    win_ref, wout_ref, o_ref, z_scr = next(it), next(it), next(it), next(it)
    tm = x_ref.shape[1]
    subs = [slice(i * tm // n_sub, (i + 1) * tm // n_sub) for i in range(n_sub)]

    t, nt = pl.program_id(1), pl.num_programs(1)
    z_before = (cg_prev_ref[0].astype(F32) * uc_prev_ref[0].astype(F32))[HALO_ROWS - CONV_PAD:]
    z_after = (cg_next_ref[0].astype(F32) * uc_next_ref[0].astype(F32))[:CONV_PAD]
    z_scr[0:CONV_PAD, :] = jnp.where(t > 0, z_before, 0.0)
    z_scr[CONV_PAD + tm:, :] = jnp.where(t < nt - 1, z_after, 0.0)
    z_scr[CONV_PAD:CONV_PAD + tm, :] = cg_ref[0].astype(F32) * uc_ref[0].astype(F32)
    ycs = []
    for r in subs:
        s, n = CONV_PAD + r.start, r.stop - r.start
        y = (cw_ref[0:1] * z_scr[s - 1:s - 1 + n, :] + cw_ref[1:2] * z_scr[s:s + n, :]
             + cw_ref[2:3] * z_scr[s + 1:s + 1 + n, :] + _row(cb_ref, layer))
        ycs.append((bg_ref[0, r, :].astype(F32) * y).astype(BF16))

    ys = [(_dot(yg_ref[0, r, :], wo_ref[0:V_W]) + _dot(yf_ref[0, r, :], wo_ref[V_W:V_W + FOUR_W])
           + _dot(yc, wo_ref[V_W + FOUR_W:])) for r, yc in zip(subs, ycs)]
    xs, hs = [], []
    for r, y in zip(subs, ys):
        x = x_ref[0, r, :]
        if add_pos:
            x = x + pos_ref[r, :]
        x = x + gatemix_ref[...] * _rms(y, _row(gmix_ref, layer))
        xs.append(x)
        hs.append((_rms(x, _row(gpre_ref, layer)) * (1.0 + scale_ref[...]) + shift_ref[...]).astype(BF16))
    ys = []
    for h in hs:
        y = jnp.zeros((h.shape[0], D_MODEL), F32)
        for lo, hi in FFN_TILES:
            a = _dot(h, win_ref[:, lo:hi])
            u = _dot(h, win_ref[:, HIDDEN + lo:HIDDEN + hi])
            y = y + _dot((_silu(a) * u).astype(BF16), wout_ref[lo:hi, :])
        ys.append(y)
    for r, x, y in zip(subs, xs, ys):
        o_ref[0, r, :] = x + gate_ref[...] * _rms(y, _row(gpost_ref, layer))


def _out_ffn(yg, yf, p, x, pos, lay, i, wts, mod_row, tm):
    w_out, w_ffn_in, w_ffn_out = wts
    b, l, d = x.shape
    add_pos = pos is not None
    tok = lambda bi, ti: (bi, ti, 0)
    halo_per_tile, n_halo = tm // HALO_ROWS, l // HALO_ROWS
    before = lambda bi, ti: (bi, jnp.maximum(ti * halo_per_tile - 1, 0), 0)
    after = lambda bi, ti: (bi, jnp.minimum((ti + 1) * halo_per_tile, n_halo - 1), 0)
    conv_tok = pl.BlockSpec((1, tm, CONV_W), tok)
    in_specs = [pl.BlockSpec((1, tm, V_W), tok), pl.BlockSpec((1, tm, FOUR_W), tok), conv_tok, conv_tok, conv_tok,
                pl.BlockSpec((1, HALO_ROWS, CONV_W), before), pl.BlockSpec((1, HALO_ROWS, CONV_W), before),
                pl.BlockSpec((1, HALO_ROWS, CONV_W), after), pl.BlockSpec((1, HALO_ROWS, CONV_W), after),
                _layer(lay["conv_w"], i), _resident(lay["conv_b"].shape),
                _layer(w_out, 0), pl.BlockSpec((1, tm, d), tok)]
    args = [yg, yf, p["bg"], p["cg"], p["uc"], p["cg"], p["uc"], p["cg"], p["uc"],
            lay["conv_w"], lay["conv_b"], w_out, x]
    if add_pos:
        in_specs.append(pl.BlockSpec((tm, d), lambda bi, ti: (ti, 0)))
        args.append(pos)
    in_specs += [_resident(lay["g_mix_post"].shape), _mod_spec(i, 2, mod_row), _resident(lay["g_ffn_pre"].shape),
                 _mod_spec(i, 3, mod_row), _mod_spec(i, 4, mod_row), _mod_spec(i, 5, mod_row),
                 _resident(lay["g_ffn_post"].shape), _layer(w_ffn_in, 0), _layer(w_ffn_out, 0)]
    args += [lay["g_mix_post"], lay["mod"], lay["g_ffn_pre"], lay["mod"], lay["mod"], lay["mod"],
             lay["g_ffn_post"], w_ffn_in, w_ffn_out]
    return pl.pallas_call(
        functools.partial(_out_ffn_kernel, layer=i, add_pos=add_pos, n_sub=2 if tm >= TOKEN_TILE else 1),
        grid=(b, l // tm),
        in_specs=in_specs,
        out_specs=pl.BlockSpec((1, tm, d), tok),
        out_shape=jax.ShapeDtypeStruct((b, l, d), F32),
        scratch_shapes=[pltpu.VMEM((tm + 2 * CONV_PAD, CONV_W), F32)],
        compiler_params=_params(("parallel", "parallel"), "out_ffn"),
        name="out_ffn",
    )(*args)


def _arrange_w_dec(w_f, w_b):
    wd = jnp.zeros((DEPTH, LR_PAD, 2 * K_W), F32)
    wd = wd.at[:, 0:RANK, :K_W].set(w_f).at[:, RANK:2 * RANK, K_W:].set(w_b)
    return wd.astype(BF16)


def kernel(x, c, ctx, c_ctx, w_mod, b_mod, g_mix_pre, g_mix_post, g_ffn_pre, g_ffn_post,
           w_in, w_dec_f, b_dec_f, w_dec_b, b_dec_b, gla_norm_w, conv_w, conv_b, w_out,
           w_ffn_in, w_ffn_out):
    b, n_lat, d = x.shape
    n_ctx = ctx.shape[1]
    assert d == D_MODEL and w_in.shape == (DEPTH, D_MODEL, _C_END)
    assert n_lat % (2 * TOKEN_TILE) == 0 and n_lat % (FFT_RADIX * V7X_SUBLANES_BF16) == 0
    assert n_ctx % CHUNK == 0 and n_ctx <= TOKEN_TILE and b + 1 <= MOD_ROWS

    pos = jnp.asarray(_pos_embed(n_lat, d))
    tabs = {n_lat: _dft_tables(n_lat, FFT_RADIX), n_ctx: _dft_tables(n_ctx, 1)}

    cvec = jnp.zeros((MOD_ROWS, d), F32).at[:b].set(c).at[b].set(c_ctx)
    ctx_row = b
    mod, w_in_arranged = _modulation(cvec, w_mod, b_mod, jnp.swapaxes(w_in, 1, 2))
    lay = {
        "mod": mod.reshape(DEPTH, MOD_ROWS, 1, N_MOD * d),
        "w_in": w_in_arranged,
        "w_dec": _arrange_w_dec(w_dec_f, w_dec_b),
        "b_dec": jnp.concatenate([b_dec_f, b_dec_b], axis=-1).reshape(DEPTH, 1, 2 * K_W),
        "gla_norm_w": gla_norm_w,
        "conv_w": conv_w,
        "conv_b": conv_b,
        "g_mix_pre": g_mix_pre,
        "g_mix_post": g_mix_post,
        "g_ffn_pre": g_ffn_pre,
        "g_ffn_post": g_ffn_post,
    }

    def mix_and_ffn(i, xs, pos_s, p, wts, mod_row, s0, want_state, tm):
        yg, s = _gla(p, lay, i, s0, True, want_state)
        yf = _fourier(p["uf"], tabs[xs.shape[1]])
        return _out_ffn(yg, yf, p, xs, pos_s, lay, i, wts, mod_row, tm), s

    zero_state = jnp.zeros((b, 2, DV, K_W), F32)
    xc = ctx
    tm = TOKEN_TILE
    for i in range(DEPTH):
        pos_i = pos if i == 0 else None
        p_lat, wts = _inproj(x, pos_i, lay, i, None, _OUT_GROUPS_FULL, 2 * tm, cast=(w_out, w_ffn_in, w_ffn_out))
        if i == DEPTH - 1:
            p_ctx, _ = _inproj(xc, None, lay, i, ctx_row, _OUT_GROUPS_STATE, n_ctx)
            _, s = _gla(p_ctx, lay, i, zero_state, False, True)
        else:
            p_ctx, _ = _inproj(xc, None, lay, i, ctx_row, _OUT_GROUPS_FULL, n_ctx)
            xc, s = mix_and_ffn(i, xc, None, p_ctx, wts, ctx_row, zero_state, True, n_ctx)
        x, _ = mix_and_ffn(i, x, pos_i, p_lat, wts, None, s, False, tm)
    return x
```

```python
import functools

import numpy as np
import jax
import jax.numpy as jnp
from jax import lax
from jax.experimental import pallas as pl
from jax.experimental.pallas import tpu as pltpu

F32 = jnp.float32
BF16 = jnp.bfloat16

D_MODEL = 1024
DEPTH = 2
GRID_W = 64
N_MOD = 6
HEADS = 4
DK = 64
DV = 128
K_W = HEADS * DK
V_W = HEADS * DV
RANK = 16
TAU = 16.0
CHUNK = 64
FOUR_W = 256
FOUR_G = 4
CONV_W = 256
HIDDEN = 2816
EPS = 1e-6

V7X_LANES = 128
V7X_SUBLANES_F32 = 8
V7X_SUBLANES_BF16 = 16
V7X_VMEM_BYTES = 64 * 1024 * 1024

LR_PAD = V7X_LANES
MOD_ROWS = V7X_SUBLANES_BF16
TOKEN_TILE = 512
FFN_TILES = ((0, 1536), (1536, HIDDEN))
VMEM_MB = {"modulation": 56, "inproj": 56, "gla": 56, "fourier": 48, "out_ffn": 56}

_C_K, _C_V, _C_LR, _C_Q, _C_R, _C_FOUR, _C_BG, _C_CG, _C_UC, _C_END = (
    0, 256, 768, 800, 1056, 1568, 1824, 2080, 2336, 2592)
_GROUP_WIDTHS = (("k", K_W), ("v", V_W), ("q", K_W), ("r", V_W), ("uf", FOUR_W), ("bg", CONV_W),
                 ("cg", CONV_W), ("uc", CONV_W))
_MAIN_W = sum(w for _, w in _GROUP_WIDTHS)
_IN_W = _MAIN_W + LR_PAD


def _groups(names):
    out, lo = [], 0
    for name, w in _GROUP_WIDTHS:
        if name in names:
            out.append((name, lo, lo + w))
        lo += w
    return tuple(out)


_OUT_GROUPS_FULL = _groups([name for name, _ in _GROUP_WIDTHS])
_OUT_GROUPS_STATE = _groups(("k", "v"))


def _params(semantics, name):
    assert VMEM_MB[name] * 1024 * 1024 < V7X_VMEM_BYTES
    return pltpu.CompilerParams(dimension_semantics=semantics,
                                vmem_limit_bytes=VMEM_MB[name] * 1024 * 1024)


def _resident(shape):
    zeros = (0,) * len(shape)
    return pl.BlockSpec(shape, lambda *_: zeros, pipeline_mode=pl.Buffered(1))


def _layer(arr, i):
    shape = tuple(arr.shape[1:])
    zeros = (0,) * len(shape)
    return pl.BlockSpec((None,) + shape, lambda *_: (i,) + zeros, pipeline_mode=pl.Buffered(1))


def _mod_spec(i, j, row):
    if row is None:
        return pl.BlockSpec((None, None, 1, D_MODEL), lambda bi, *_: (i, bi, 0, j))
    return pl.BlockSpec((None, None, 1, D_MODEL), lambda *_: (i, row, 0, j))


def _row(ref, layer):
    return ref[layer:layer + 1, :]


def _silu(a):
    return a / (1.0 + jnp.exp(-a))


def _rms(x, g):
    return x * lax.rsqrt(jnp.mean(x * x, axis=-1, keepdims=True) + EPS) * g


def _dot(a, b):
    return jnp.dot(a, b, preferred_element_type=F32)


def _dot_nt(a, b):
    return lax.dot_general(a, b, (((1,), (1,)), ((), ())), preferred_element_type=F32)


def _dot_tn(a, b):
    return lax.dot_general(a, b, (((0,), (0,)), ((), ())), preferred_element_type=F32)


def _pos_embed(n_tokens, dim):
    rows = n_tokens // GRID_W
    row = np.repeat(np.arange(rows, dtype=np.float32), GRID_W)
    col = np.tile(np.arange(GRID_W, dtype=np.float32), rows)
    quarter = dim // 4
    freqs = (1.0 / (10000.0 ** (np.arange(quarter, dtype=np.float32) / quarter))).astype(np.float32)

    def enc(p):
        ang = (p[:, None] * freqs[None, :]).astype(np.float32)
        return np.concatenate([np.sin(ang), np.cos(ang)], axis=-1)

    return np.concatenate([enc(row), enc(col)], axis=-1).astype(np.float32)


def _dft_cos_sin(n):
    idx = np.arange(n, dtype=np.int64)
    ang = 2.0 * np.pi * ((idx[:, None] * idx[None, :]) % n).astype(np.float64) / n
    return np.cos(ang) / np.sqrt(n), np.sin(ang) / np.sqrt(n)


def _dft_tables(seq, radix):
    gw = FOUR_W // FOUR_G
    cc, sc = _dft_cos_sin(gw)
    eye = np.eye(FOUR_G)
    chan_tab = jnp.asarray(np.concatenate([np.kron(eye, cc), np.kron(eye, sc)], axis=1), dtype=F32).astype(BF16)
    if radix == 1:
        cl, sl = _dft_cos_sin(seq)
        return chan_tab, jnp.asarray(np.concatenate([cl, -sl], axis=1), dtype=F32).astype(BF16)
    n = seq // radix
    cn, sn = _dft_cos_sin(n)
    sub_tab = np.concatenate([cn, sn], axis=1) * np.sqrt(n / seq)
    ang = 2.0 * np.pi * (np.arange(1, radix)[:, None] * np.arange(n)[None, :]).astype(np.float64) / seq
    twc = np.broadcast_to(np.cos(ang)[:, :, None], (radix - 1, n, FOUR_W))
    tws = np.broadcast_to(np.sin(ang)[:, :, None], (radix - 1, n, FOUR_W))
    return (chan_tab, jnp.asarray(sub_tab, dtype=F32).astype(BF16),
            jnp.asarray(twc, dtype=F32), jnp.asarray(tws, dtype=F32))


def _mod_kernel(c_ref, w_ref, b_ref, wint_ref, o_ref, wout_ref):
    s = _silu(c_ref[...]).astype(BF16)
    o_ref[0] = _dot(s, w_ref[0].astype(BF16)) + b_ref[0]

    @pl.when(pl.program_id(1) == 0)
    def _():
        for col in range(0, _IN_W, V7X_LANES):
            if col < _C_LR:
                rows = wint_ref[0, col:col + V7X_LANES, :]
            elif col < _MAIN_W:
                src = col + _C_Q - _C_LR
                rows = wint_ref[0, src:src + V7X_LANES, :]
            else:
                rows = jnp.concatenate([wint_ref[0, _C_LR:_C_Q, :],
                                        jnp.zeros((LR_PAD - 2 * RANK, D_MODEL), F32)], axis=0)
            wout_ref[0, :, col:col + V7X_LANES] = rows.T.astype(BF16)


def _modulation(cvec, w_mod, b_mod, w_in_t):
    steps = 4
    n = N_MOD * D_MODEL
    tn = n // steps
    assert tn % V7X_LANES == 0
    return pl.pallas_call(
        _mod_kernel,
        grid=(DEPTH, steps),
        in_specs=[pl.BlockSpec((MOD_ROWS, D_MODEL), lambda i, j: (0, 0)),
                  pl.BlockSpec((1, D_MODEL, tn), lambda i, j: (i, 0, j)),
                  pl.BlockSpec((1, 1, tn), lambda i, j: (i, 0, j)),
                  pl.BlockSpec((1, _C_END, D_MODEL), lambda i, j: (i, 0, 0))],
        out_specs=[pl.BlockSpec((1, MOD_ROWS, tn), lambda i, j: (i, 0, j)),
                   pl.BlockSpec((1, D_MODEL, _IN_W), lambda i, j: (i, 0, 0))],
        out_shape=[jax.ShapeDtypeStruct((DEPTH, MOD_ROWS, n), F32),
                   jax.ShapeDtypeStruct((DEPTH, D_MODEL, _IN_W), BF16)],
        compiler_params=_params(("arbitrary", "arbitrary"), "modulation"),
        name="modulation",
    )(cvec, w_mod, b_mod.reshape(DEPTH, 1, n), w_in_t)


def _inproj_kernel(*refs, layer, add_pos, groups, n_cast):
    it = iter(refs)
    x_ref = next(it)
    pos_ref = next(it) if add_pos else None
    g_ref, shift_ref, scale_ref, w_ref, wd_ref, bd_ref = (next(it) for _ in range(6))
    cast_in = [next(it) for _ in range(n_cast)]
    out_refs = list(it)
    for src, dst in zip(cast_in, out_refs[len(out_refs) - n_cast:]):
        dst[0] = src[...].astype(BF16)
    out_refs = out_refs[:len(out_refs) - n_cast]
    lg_hi_ref, lg_lo_ref = out_refs[-2:]
    tm = x_ref.shape[1]
    n_sub = 2 if tm >= TOKEN_TILE else 1
    subs = [slice(i * tm // n_sub, (i + 1) * tm // n_sub) for i in range(n_sub)]
    hs = []
    for r in subs:
        x = x_ref[0, r, :]
        if add_pos:
            x = x + pos_ref[r, :]
        hs.append((_rms(x, _row(g_ref, layer)) * (1.0 + scale_ref[...]) + shift_ref[...]).astype(BF16))
    def project(r, h):
        for o_ref, (_, lo, hi_col) in zip(out_refs, groups):
            o_ref[0, r, :] = _dot(h, w_ref[:, lo:hi_col]).astype(BF16)

    low_rank = [_dot(h, w_ref[:, _MAIN_W:_IN_W]).astype(BF16) for h in hs]
    project(subs[0], hs[0])
    zs = [_dot(lr, wd_ref[...]) + bd_ref[...] for lr in low_rank]
    for r, h in list(zip(subs, hs))[1:]:
        project(r, h)
    for r, z in zip(subs, zs):
        logg = (jnp.minimum(z, 0.0) - jnp.log(1.0 + jnp.exp(-jnp.abs(z)))) * (1.0 / TAU)
        hi = logg.astype(BF16)
        lg_hi_ref[0, r, :] = hi
        lg_lo_ref[0, r, :] = (logg - hi.astype(F32)).astype(BF16)


def _inproj(x, pos, lay, i, mod_row, groups, tm, cast=()):
    b, l, d = x.shape
    add_pos = pos is not None
    n_t = l // tm
    tok = lambda bi, ti: (bi, ti, 0)
    in_specs = [pl.BlockSpec((1, tm, d), tok)]
    args = [x]
    if add_pos:
        in_specs.append(pl.BlockSpec((tm, d), lambda bi, ti: (ti, 0)))
        args.append(pos)
    in_specs += [_resident(lay["g_mix_pre"].shape), _mod_spec(i, 0, mod_row), _mod_spec(i, 1, mod_row),
                 _layer(lay["w_in"], i), _layer(lay["w_dec"], i), _layer(lay["b_dec"], i)]
    args += [lay["g_mix_pre"], lay["mod"], lay["mod"], lay["w_in"], lay["w_dec"], lay["b_dec"]]
    widths = [hi - lo for _, lo, hi in groups] + [2 * K_W, 2 * K_W]
    names = [name for name, _, _ in groups] + ["lg_hi", "lg_lo"]
    out_specs = [pl.BlockSpec((1, tm, w), tok) for w in widths]
    out_shape = [jax.ShapeDtypeStruct((b, l, w), BF16) for w in widths]
    for w in cast:
        _, rows, cols = w.shape
        slab = rows // (b * n_t)
        assert slab * b * n_t == rows and slab % V7X_SUBLANES_BF16 == 0
        in_specs.append(pl.BlockSpec((None, slab, cols), lambda bi, ti: (i, bi * n_t + ti, 0)))
        out_specs.append(pl.BlockSpec((1, slab, cols), lambda bi, ti: (0, bi * n_t + ti, 0)))
        out_shape.append(jax.ShapeDtypeStruct((1, rows, cols), BF16))
        args.append(w)
    outs = pl.pallas_call(
        functools.partial(_inproj_kernel, layer=i, add_pos=add_pos, groups=groups, n_cast=len(cast)),
        grid=(b, n_t),
        in_specs=in_specs,
        out_specs=out_specs,
        out_shape=out_shape,
        compiler_params=_params(("parallel", "parallel"), "inproj"),
        name="inproj",
    )(*args)
    return dict(zip(names, outs)), list(outs[len(names):])


def _gla_kernel(*refs, layer, seq, with_output, with_state):
    it = iter(refs)
    k_ref, v_ref, lg_hi, lg_lo, s0_ref = (next(it) for _ in range(5))
    if with_output:
        q_ref, r_ref, nw_ref = next(it), next(it), next(it)
        y_ref = next(it)
    if with_state:
        s_out_ref = next(it)
    st_ref = next(it)
    ds_scr = next(it)
    dec_scr = next(it)
    if with_output:
        o_scr = next(it)
        qst_scr = next(it)
        gain_scr = next(it)
    n_chunks = seq // CHUNK

    st_ref[...] = s0_ref[0]

    row = lax.broadcasted_iota(jnp.int32, (CHUNK, CHUNK), 0)
    col = lax.broadcasted_iota(jnp.int32, (CHUNK, CHUNK), 1)
    cum = (jnp.where(row >= col, 1.0, 0.0).astype(BF16), jnp.where(row <= col, 1.0, 0.0).astype(BF16))
    lane_head = lax.broadcasted_iota(jnp.int32, (1, K_W), 1) // DK
    head_mask = [lane_head == h for h in range(HEADS)]
    srow = lax.broadcasted_iota(jnp.int32, (CHUNK, HEADS * CHUNK), 0)
    scol = lax.broadcasted_iota(jnp.int32, (CHUNK, HEADS * CHUNK), 1) % CHUNK
    score_mask = (srow >= scol, srow <= scol)
    zero_v = jnp.zeros((CHUNK, DV), BF16)
    last_row = (CHUNK - 1, 0)
    mid_row = (CHUNK // 2 - 1, CHUNK // 2)

    def dec_rows(n):
        return pl.ds(pl.multiple_of(n * V7X_SUBLANES_F32, V7X_SUBLANES_F32), V7X_SUBLANES_F32)

    def stack_heads(a):
        zero = jnp.zeros_like(a)
        return jnp.concatenate([jnp.where(head_mask[h], a, zero) for h in range(HEADS)], axis=0)

    def chunk_rows(n):
        return pl.ds(pl.multiple_of(n * CHUNK, CHUNK), CHUNK)

    per_iter = min(8, n_chunks)

    def chunk_local(i, carry):
        loaded = []
        for u in range(per_iter):
            n = i * per_iter + u
            rows = chunk_rows(n)
            k = k_ref[0, rows, :].astype(F32)
            v = v_ref[0, rows, :]
            v_heads = [v[:, h * DV:(h + 1) * DV] for h in range(HEADS)]
            v_rows = jnp.concatenate(v_heads, axis=0)
            v_diag = jnp.concatenate(
                [jnp.concatenate([v_heads[h] if hh == h else zero_v for hh in range(HEADS)], axis=1)
                 for h in range(HEADS)], axis=0) if with_output else None
            v = (v, v_rows, v_diag)
            qs = q_ref[0, rows, :].astype(F32) * (DK ** -0.5) if with_output else None
            lg = [(lg_hi[0, rows, d * K_W:(d + 1) * K_W], lg_lo[0, rows, d * K_W:(d + 1) * K_W]) for d in range(2)]
            loaded.append((n, rows, k, v, qs, lg))
        chains = [(n, rows, k, v, qs, lg[dirn], dirn) for n, rows, k, v, qs, lg in loaded for dirn in range(2)]
        stores = []
        if with_output:
            for n, rows, *_ in loaded:
                stores.append((gain_scr, (rows, slice(None)),
                               _silu(r_ref[0, rows, :].astype(F32)) * _row(nw_ref, layer)))
        gs = [_dot(cum[dirn], hi) + _dot(cum[dirn], lo) for *_, (hi, lo), dirn in chains]
        score_list = []
        for (n, rows, k, (v, v_rows, v_diag), qs, _, dirn), g in zip(chains, gs):
            g_last = g[last_row[dirn]:last_row[dirn] + 1]
            stores.append((dec_scr, (dirn, dec_rows(n), slice(None)),
                           jnp.broadcast_to(jnp.exp(g_last), (V7X_SUBLANES_F32, K_W))))
            if with_output:
                g_mid = g[mid_row[dirn]:mid_row[dirn] + 1]
                q_in = qs * jnp.exp(g - g_mid)
                k_in = k * jnp.exp(g_mid - g)
                score_list.append(_dot_nt(q_in.astype(BF16), stack_heads(k_in.astype(BF16))))
                stores.append((qst_scr, (dirn, rows, slice(None)), (q_in * jnp.exp(g_mid)).astype(BF16)))
                k_upd = stack_heads((k_in * jnp.exp(g_last - g_mid)).astype(BF16))
            else:
                k_upd = stack_heads((k * jnp.exp(g_last - g)).astype(BF16))
            stores.append((ds_scr, (dirn, pl.ds(pl.multiple_of(n * DV, DV), DV), slice(None)),
                           _dot_tn(v_rows, k_upd)))
        if with_output:
            for (n, rows, k, (v, v_rows, v_diag), qs, _, dirn), sc in zip(chains, score_list):
                scores = jnp.where(score_mask[dirn], sc, 0.0).astype(BF16)
                stores.append((o_scr, (dirn, rows, slice(None)), _dot(scores, v_diag)))
        for ref, idx, val in stores:
            ref[idx] = val
        return carry

    lax.fori_loop(0, n_chunks // per_iter, chunk_local, 0)

    def scan_step(i, finish):
        steps = ((0, i), (1, n_chunks - 1 - i))
        new_st, inters = [], []
        for dirn, n in steps:
            st = st_ref[dirn]
            if with_output:
                q_st = stack_heads(qst_scr[dirn, chunk_rows(n), :])
                inters.append(_dot_nt(q_st, st.astype(BF16)))
            dec = dec_scr[dirn, dec_rows(n), :]
            new_st.append(dec[0:1] * st + ds_scr[dirn, pl.ds(pl.multiple_of(n * DV, DV), DV), :])
        for (dirn, n), st in zip(steps, new_st):
            st_ref[dirn] = st
        if with_output:
            for (dirn, n), inter in zip(steps, inters):
                rows = chunk_rows(n)
                for h in range(HEADS):
                    vs = slice(h * DV, (h + 1) * DV)
                    o = o_scr[dirn, rows, vs] + inter[h * CHUNK:(h + 1) * CHUNK]
                    if finish:
                        o = o + o_scr[1 - dirn, rows, vs]
                        y_ref[0, rows, vs] = (_rms(o, gain_scr[rows, vs])).astype(BF16)
                    else:
                        o_scr[dirn, rows, vs] = o

    half = n_chunks // 2
    unroll = min(16, half)

    def scan_first(i, carry):
        scan_step(i, False)
        return carry

    def scan_second(i, carry):
        scan_step(i, True)
        return carry

    lax.fori_loop(0, half, scan_first, 0, unroll=unroll)
    lax.fori_loop(half, n_chunks, scan_second, 0, unroll=unroll)

    if with_state:
        s_out_ref[0] = st_ref[...]


def _gla(p, lay, i, s0, with_output, with_state):
    b, l, _ = p["k"].shape
    seq3 = lambda bi: (bi, 0, 0)
    in_specs = [pl.BlockSpec((1, l, K_W), seq3), pl.BlockSpec((1, l, V_W), seq3),
                pl.BlockSpec((1, l, 2 * K_W), seq3), pl.BlockSpec((1, l, 2 * K_W), seq3),
                pl.BlockSpec((1, 2, DV, K_W), lambda bi: (bi, 0, 0, 0))]
    args = [p["k"], p["v"], p["lg_hi"], p["lg_lo"], s0]
    out_specs, out_shape = [], []
    scratch = [pltpu.VMEM((2, DV, K_W), F32),
               pltpu.VMEM((2, l // CHUNK * DV, K_W), F32),
               pltpu.VMEM((2, l // CHUNK * V7X_SUBLANES_F32, K_W), F32)]
    if with_output:
        in_specs += [pl.BlockSpec((1, l, K_W), seq3), pl.BlockSpec((1, l, V_W), seq3),
                     _resident(lay["gla_norm_w"].shape)]
        args += [p["q"], p["r"], lay["gla_norm_w"]]
        out_specs.append(pl.BlockSpec((1, l, V_W), seq3))
        out_shape.append(jax.ShapeDtypeStruct((b, l, V_W), BF16))
        scratch += [pltpu.VMEM((2, l, V_W), F32), pltpu.VMEM((2, l, K_W), BF16), pltpu.VMEM((l, V_W), F32)]
    if with_state:
        out_specs.append(pl.BlockSpec((1, 2, DV, K_W), lambda bi: (bi, 0, 0, 0)))
        out_shape.append(jax.ShapeDtypeStruct((b, 2, DV, K_W), F32))
    outs = pl.pallas_call(
        functools.partial(_gla_kernel, layer=i, seq=l, with_output=with_output, with_state=with_state),
        grid=(b,),
        in_specs=in_specs,
        out_specs=out_specs,
        out_shape=out_shape,
        scratch_shapes=scratch,
        compiler_params=_params(("parallel",), "gla"),
        name="gla",
    )(*args)
    outs = list(outs)
    y = outs.pop(0) if with_output else None
    s = outs.pop(0) if with_state else None
    return y, s


def _fourier_kernel(uf_ref, chan_ref, pos_ref, yf_ref, ab_scr, *, seq):
    ab = _dot(uf_ref[0], chan_ref[...])
    ab_scr[0:seq, :] = ab[:, :FOUR_W].astype(BF16)
    ab_scr[seq:2 * seq, :] = ab[:, FOUR_W:].astype(BF16)
    rb = min(seq, 512)
    for i in range(seq // rb):
        yf_ref[0, i * rb:(i + 1) * rb, :] = _dot(pos_ref[i * rb:(i + 1) * rb, :], ab_scr[...]).astype(BF16)


FFT_RADIX = 8


def _fourier_fft_kernel(uf_ref, chan_ref, sub_ref, twc_ref, tws_ref, yf_ref, uf_scr, *, seq):
    n = seq // FFT_RADIX
    lanes = uf_scr.shape[2]
    n_lane_tiles = FOUR_W // lanes
    for j in range(n_lane_tiles):
        uf_scr[j] = uf_ref[0, :, j * lanes:(j + 1) * lanes].astype(F32)

    def tokens(r):
        parts = [uf_scr[j, pl.ds(r, n, stride=FFT_RADIX), :] for j in range(n_lane_tiles)]
        return jnp.concatenate(parts, axis=1).astype(BF16)

    ab = [_dot(tokens(r), chan_ref[...]) for r in range(FFT_RADIX)]
    pq = []
    for ab_r in ab:
        a, b = ab_r[:, :FOUR_W].astype(BF16), ab_r[:, FOUR_W:].astype(BF16)
        w = jnp.concatenate([jnp.concatenate([a, b], axis=1), jnp.concatenate([-b, a], axis=1)], axis=0)
        pq.append(_dot(sub_ref[...], w))
    re, im = [pq[0][:, :FOUR_W]], [pq[0][:, FOUR_W:]]
    for r in range(1, FFT_RADIX):
        p, q = pq[r][:, :FOUR_W], pq[r][:, FOUR_W:]
        c, s = twc_ref[r - 1], tws_ref[r - 1]
        re.append(c * p - s * q)
        im.append(s * p + c * q)

    def quarter(z0, z1, z2, z3):
        s02 = (z0[0] + z2[0], z0[1] + z2[1])
        d02 = (z0[0] - z2[0], z0[1] - z2[1])
        s13 = (z1[0] + z3[0], z1[1] + z3[1])
        d13 = (z1[0] - z3[0], z1[1] - z3[1])
        return ((s02[0] + s13[0], s02[1] + s13[1]), (d02[0] - d13[1], d02[1] + d13[0]),
                (s02[0] - s13[0], s02[1] - s13[1]), (d02[0] + d13[1], d02[1] - d13[0]))

    zs = list(zip(re, im))
    even = quarter(zs[0], zs[2], zs[4], zs[6])
    odd = quarter(zs[1], zs[3], zs[5], zs[7])
    half = 0.5 ** 0.5
    odd_re = (odd[0][0], (odd[1][0] - odd[1][1]) * half, -odd[2][1], -(odd[3][0] + odd[3][1]) * half)
    for q in range(4):
        yf_ref[0, q * n:(q + 1) * n, :] = (even[q][0] + odd_re[q]).astype(BF16)
        yf_ref[0, (q + 4) * n:(q + 5) * n, :] = (even[q][0] - odd_re[q]).astype(BF16)


def _fourier(uf, tabs):
    b, l, _ = uf.shape
    tok = pl.BlockSpec((1, l, FOUR_W), lambda bi: (bi, 0, 0))
    if len(tabs) == 4:
        body, name = _fourier_fft_kernel, "fourier_fft"
        scratch = pltpu.VMEM((FOUR_W // V7X_LANES, l, V7X_LANES), F32)
    else:
        body, name = _fourier_kernel, "fourier"
        scratch = pltpu.VMEM((2 * l, FOUR_W), BF16)
    return pl.pallas_call(
        functools.partial(body, seq=l),
        grid=(b,),
        in_specs=[tok] + [_resident(t.shape) for t in tabs],
        out_specs=tok,
        out_shape=jax.ShapeDtypeStruct((b, l, FOUR_W), BF16),
        scratch_shapes=[scratch],
        compiler_params=_params(("parallel",), "fourier"),
        name=name,
    )(uf, *tabs)


CONV_PAD = V7X_SUBLANES_F32
HALO_ROWS = V7X_SUBLANES_BF16


def _out_ffn_kernel(*refs, layer, add_pos, n_sub):
    it = iter(refs)
    yg_ref, yf_ref, bg_ref, cg_ref, uc_ref = (next(it) for _ in range(5))
    cg_prev_ref, uc_prev_ref, cg_next_ref, uc_next_ref, cw_ref, cb_ref = (next(it) for _ in range(6))
    wo_ref, x_ref = next(it), next(it)
    pos_ref = next(it) if add_pos else None
    gmix_ref, gatemix_ref, gpre_ref, shift_ref, scale_ref, gate_ref, gpost_ref = (next(it) for _ in range(7))
    win_ref, wout_ref, o_ref, z_scr = next(it), next(it), next(it), next(it)
    tm = x_ref.shape[1]
    subs = [slice(i * tm // n_sub, (i + 1) * tm // n_sub) for i in range(n_sub)]

    t, nt = pl.program_id(1), pl.num_programs(1)
    z_before = (cg_prev_ref[0].astype(F32) * uc_prev_ref[0].astype(F32))[HALO_ROWS - CONV_PAD:]
    z_after = (cg_next_ref[0].astype(F32) * uc_next_ref[0].astype(F32))[:CONV_PAD]
    z_scr[0:CONV_PAD, :] = jnp.where(t > 0, z_before, 0.0)
    z_scr[CONV_PAD + tm:, :] = jnp.where(t < nt - 1, z_after, 0.0)
    z_scr[CONV_PAD:CONV_PAD + tm, :] = cg_ref[0].astype(F32) * uc_ref[0].astype(F32)
    ycs = []
    for r in subs:
        s, n = CONV_PAD + r.start, r.stop - r.start
        y = (cw_ref[0:1] * z_scr[s - 1:s - 1 + n, :] + cw_ref[1:2] * z_scr[s:s + n, :]
             + cw_ref[2:3] * z_scr[s + 1:s + 1 + n, :] + _row(cb_ref, layer))
        ycs.append((bg_ref[0, r, :].astype(F32) * y).astype(BF16))

    ys = [(_dot(yg_ref[0, r, :], wo_ref[0:V_W]) + _dot(yf_ref[0, r, :], wo_ref[V_W:V_W + FOUR_W])
           + _dot(yc, wo_ref[V_W + FOUR_W:])) for r, yc in zip(subs, ycs)]
    xs, hs = [], []
    for r, y in zip(subs, ys):
        x = x_ref[0, r, :]
        if add_pos:
            x = x + pos_ref[r, :]
        x = x + gatemix_ref[...] * _rms(y, _row(gmix_ref, layer))
        xs.append(x)
        hs.append((_rms(x, _row(gpre_ref, layer)) * (1.0 + scale_ref[...]) + shift_ref[...]).astype(BF16))
    ys = []
    for h in hs:
        y = jnp.zeros((h.shape[0], D_MODEL), F32)
        for lo, hi in FFN_TILES:
            a = _dot(h, win_ref[:, lo:hi])
            u = _dot(h, win_ref[:, HIDDEN + lo:HIDDEN + hi])
            y = y + _dot((_silu(a) * u).astype(BF16), wout_ref[lo:hi, :])
        ys.append(y)
    for r, x, y in zip(subs, xs, ys):
        o_ref[0, r, :] = x + gate_ref[...] * _rms(y, _row(gpost_ref, layer))


def _out_ffn(yg, yf, p, x, pos, lay, i, wts, mod_row, tm):
    w_out, w_ffn_in, w_ffn_out = wts
    b, l, d = x.shape
    add_pos = pos is not None
    tok = lambda bi, ti: (bi, ti, 0)
    halo_per_tile, n_halo = tm // HALO_ROWS, l // HALO_ROWS
    before = lambda bi, ti: (bi, jnp.maximum(ti * halo_per_tile - 1, 0), 0)
    after = lambda bi, ti: (bi, jnp.minimum((ti + 1) * halo_per_tile, n_halo - 1), 0)
    conv_tok = pl.BlockSpec((1, tm, CONV_W), tok)
    in_specs = [pl.BlockSpec((1, tm, V_W), tok), pl.BlockSpec((1, tm, FOUR_W), tok), conv_tok, conv_tok, conv_tok,
                pl.BlockSpec((1, HALO_ROWS, CONV_W), before), pl.BlockSpec((1, HALO_ROWS, CONV_W), before),
                pl.BlockSpec((1, HALO_ROWS, CONV_W), after), pl.BlockSpec((1, HALO_ROWS, CONV_W), after),
                _layer(lay["conv_w"], i), _resident(lay["conv_b"].shape),
                _layer(w_out, 0), pl.BlockSpec((1, tm, d), tok)]
    args = [yg, yf, p["bg"], p["cg"], p["uc"], p["cg"], p["uc"], p["cg"], p["uc"],
            lay["conv_w"], lay["conv_b"], w_out, x]
    if add_pos:
        in_specs.append(pl.BlockSpec((tm, d), lambda bi, ti: (ti, 0)))
        args.append(pos)
    in_specs += [_resident(lay["g_mix_post"].shape), _mod_spec(i, 2, mod_row), _resident(lay["g_ffn_pre"].shape),
                 _mod_spec(i, 3, mod_row), _mod_spec(i, 4, mod_row), _mod_spec(i, 5, mod_row),
                 _resident(lay["g_ffn_post"].shape), _layer(w_ffn_in, 0), _layer(w_ffn_out, 0)]
    args += [lay["g_mix_post"], lay["mod"], lay["g_ffn_pre"], lay["mod"], lay["mod"], lay["mod"],
             lay["g_ffn_post"], w_ffn_in, w_ffn_out]
    return pl.pallas_call(
        functools.partial(_out_ffn_kernel, layer=i, add_pos=add_pos, n_sub=2),
        grid=(b, l // tm),
        in_specs=in_specs,
        out_specs=pl.BlockSpec((1, tm, d), tok),
        out_shape=jax.ShapeDtypeStruct((b, l, d), F32),
        scratch_shapes=[pltpu.VMEM((tm + 2 * CONV_PAD, CONV_W), F32)],
        compiler_params=_params(("parallel", "parallel"), "out_ffn"),
        name="out_ffn",
    )(*args)


def _arrange_w_dec(w_f, w_b):
    wd = jnp.zeros((DEPTH, LR_PAD, 2 * K_W), F32)
    wd = wd.at[:, 0:RANK, :K_W].set(w_f).at[:, RANK:2 * RANK, K_W:].set(w_b)
    return wd.astype(BF16)


def kernel(x, c, ctx, c_ctx, w_mod, b_mod, g_mix_pre, g_mix_post, g_ffn_pre, g_ffn_post,
           w_in, w_dec_f, b_dec_f, w_dec_b, b_dec_b, gla_norm_w, conv_w, conv_b, w_out,
           w_ffn_in, w_ffn_out):
    b, n_lat, d = x.shape
    n_ctx = ctx.shape[1]
    assert d == D_MODEL and w_in.shape == (DEPTH, D_MODEL, _C_END)
    assert n_lat % (2 * TOKEN_TILE) == 0 and n_lat % (FFT_RADIX * V7X_SUBLANES_BF16) == 0
    assert n_ctx % CHUNK == 0 and n_ctx <= TOKEN_TILE and b + 1 <= MOD_ROWS

    pos = jnp.asarray(_pos_embed(n_lat, d))
    tabs = {n_lat: _dft_tables(n_lat, FFT_RADIX), n_ctx: _dft_tables(n_ctx, 1)}

    cvec = jnp.zeros((MOD_ROWS, d), F32).at[:b].set(c).at[b].set(c_ctx)
    ctx_row = b
    mod, w_in_arranged = _modulation(cvec, w_mod, b_mod, jnp.swapaxes(w_in, 1, 2))
    lay = {
        "mod": mod.reshape(DEPTH, MOD_ROWS, 1, N_MOD * d),
        "w_in": w_in_arranged,
        "w_dec": _arrange_w_dec(w_dec_f, w_dec_b),
        "b_dec": jnp.concatenate([b_dec_f, b_dec_b], axis=-1).reshape(DEPTH, 1, 2 * K_W),
        "gla_norm_w": gla_norm_w,
        "conv_w": conv_w,
        "conv_b": conv_b,
        "g_mix_pre": g_mix_pre,
        "g_mix_post": g_mix_post,
        "g_ffn_pre": g_ffn_pre,
        "g_ffn_post": g_ffn_post,
    }

    def mix_and_ffn(i, xs, pos_s, p, wts, mod_row, s0, want_state, tm):
        yg, s = _gla(p, lay, i, s0, True, want_state)
        yf = _fourier(p["uf"], tabs[xs.shape[1]])
        return _out_ffn(yg, yf, p, xs, pos_s, lay, i, wts, mod_row, tm), s

    zero_state = jnp.zeros((b, 2, DV, K_W), F32)
    xc = ctx
    tm = TOKEN_TILE
    for i in range(DEPTH):
        pos_i = pos if i == 0 else None
        p_lat, wts = _inproj(x, pos_i, lay, i, None, _OUT_GROUPS_FULL, 2 * tm, cast=(w_out, w_ffn_in, w_ffn_out))
        if i == DEPTH - 1:
            p_ctx, _ = _inproj(xc, None, lay, i, ctx_row, _OUT_GROUPS_STATE, n_ctx)
            _, s = _gla(p_ctx, lay, i, zero_state, False, True)
        else:
            p_ctx, _ = _inproj(xc, None, lay, i, ctx_row, _OUT_GROUPS_FULL, n_ctx)
            xc, s = mix_and_ffn(i, xc, None, p_ctx, wts, ctx_row, zero_state, True, n_ctx)
        x, _ = mix_and_ffn(i, x, pos_i, p_lat, wts, None, s, False, tm if pos_i is not None else 2 * tm)
    return x
```

```python
import functools

import numpy as np
import jax
import jax.numpy as jnp
from jax import lax
from jax.experimental import pallas as pl
from jax.experimental.pallas import tpu as pltpu

F32 = jnp.float32
BF16 = jnp.bfloat16

D_MODEL = 1024
DEPTH = 2
GRID_W = 64
N_MOD = 6
HEADS = 4
DK = 64
DV = 128
K_W = HEADS * DK
V_W = HEADS * DV
RANK = 16
TAU = 16.0
CHUNK = 64
FOUR_W = 256
FOUR_G = 4
CONV_W = 256
HIDDEN = 2816
EPS = 1e-6

V7X_LANES = 128
V7X_SUBLANES_F32 = 8
V7X_SUBLANES_BF16 = 16
V7X_VMEM_BYTES = 64 * 1024 * 1024

LR_PAD = V7X_LANES
MOD_ROWS = V7X_SUBLANES_BF16
TOKEN_TILE = 512
FFN_TILES = ((0, 1536), (1536, HIDDEN))
VMEM_MB = {"modulation": 56, "inproj": 56, "gla": 56, "fourier": 48, "out_ffn": 56}

_C_K, _C_V, _C_LR, _C_Q, _C_R, _C_FOUR, _C_BG, _C_CG, _C_UC, _C_END = (
    0, 256, 768, 800, 1056, 1568, 1824, 2080, 2336, 2592)
_GROUP_WIDTHS = (("k", K_W), ("v", V_W), ("q", K_W), ("r", V_W), ("uf", FOUR_W), ("bg", CONV_W),
                 ("cg", CONV_W), ("uc", CONV_W))
_MAIN_W = sum(w for _, w in _GROUP_WIDTHS)
_IN_W = _MAIN_W + LR_PAD


def _groups(names):
    out, lo = [], 0
    for name, w in _GROUP_WIDTHS:
        if name in names:
            out.append((name, lo, lo + w))
        lo += w
    return tuple(out)


_OUT_GROUPS_FULL = _groups([name for name, _ in _GROUP_WIDTHS])
_OUT_GROUPS_STATE = _groups(("k", "v"))


def _params(semantics, name):
    assert VMEM_MB[name] * 1024 * 1024 < V7X_VMEM_BYTES
    return pltpu.CompilerParams(dimension_semantics=semantics,
                                vmem_limit_bytes=VMEM_MB[name] * 1024 * 1024)


def _resident(shape):
    zeros = (0,) * len(shape)
    return pl.BlockSpec(shape, lambda *_: zeros, pipeline_mode=pl.Buffered(1))


def _layer(arr, i):
    shape = tuple(arr.shape[1:])
    zeros = (0,) * len(shape)
    return pl.BlockSpec((None,) + shape, lambda *_: (i,) + zeros, pipeline_mode=pl.Buffered(1))


def _mod_spec(i, j, row):
    if row is None:
        return pl.BlockSpec((None, None, 1, D_MODEL), lambda bi, *_: (i, bi, 0, j))
    return pl.BlockSpec((None, None, 1, D_MODEL), lambda *_: (i, row, 0, j))


def _row(ref, layer):
    return ref[layer:layer + 1, :]


def _silu(a):
    return a / (1.0 + jnp.exp(-a))


def _rms(x, g):
    return x * lax.rsqrt(jnp.mean(x * x, axis=-1, keepdims=True) + EPS) * g


def _dot(a, b):
    return jnp.dot(a, b, preferred_element_type=F32)


def _dot_nt(a, b):
    return lax.dot_general(a, b, (((1,), (1,)), ((), ())), preferred_element_type=F32)


def _dot_tn(a, b):
    return lax.dot_general(a, b, (((0,), (0,)), ((), ())), preferred_element_type=F32)


def _pos_embed(n_tokens, dim):
    rows = n_tokens // GRID_W
    row = np.repeat(np.arange(rows, dtype=np.float32), GRID_W)
    col = np.tile(np.arange(GRID_W, dtype=np.float32), rows)
    quarter = dim // 4
    freqs = (1.0 / (10000.0 ** (np.arange(quarter, dtype=np.float32) / quarter))).astype(np.float32)

    def enc(p):
        ang = (p[:, None] * freqs[None, :]).astype(np.float32)
        return np.concatenate([np.sin(ang), np.cos(ang)], axis=-1)

    return np.concatenate([enc(row), enc(col)], axis=-1).astype(np.float32)


def _dft_cos_sin(n):
    idx = np.arange(n, dtype=np.int64)
    ang = 2.0 * np.pi * ((idx[:, None] * idx[None, :]) % n).astype(np.float64) / n
    return np.cos(ang) / np.sqrt(n), np.sin(ang) / np.sqrt(n)


def _dft_tables(seq, radix):
    gw = FOUR_W // FOUR_G
    cc, sc = _dft_cos_sin(gw)
    eye = np.eye(FOUR_G)
    chan_tab = jnp.asarray(np.concatenate([np.kron(eye, cc), np.kron(eye, sc)], axis=1), dtype=F32).astype(BF16)
    if radix == 1:
        cl, sl = _dft_cos_sin(seq)
        return chan_tab, jnp.asarray(np.concatenate([cl, -sl], axis=1), dtype=F32).astype(BF16)
    n = seq // radix
    cn, sn = _dft_cos_sin(n)
    sub_tab = np.concatenate([cn, sn], axis=1) * np.sqrt(n / seq)
    ang = 2.0 * np.pi * (np.arange(1, radix)[:, None] * np.arange(n)[None, :]).astype(np.float64) / seq
    twc = np.broadcast_to(np.cos(ang)[:, :, None], (radix - 1, n, FOUR_W))
    tws = np.broadcast_to(np.sin(ang)[:, :, None], (radix - 1, n, FOUR_W))
    return (chan_tab, jnp.asarray(sub_tab, dtype=F32).astype(BF16),
            jnp.asarray(twc, dtype=F32), jnp.asarray(tws, dtype=F32))


def _mod_kernel(c_ref, w_ref, b_ref, wint_ref, o_ref, wout_ref):
    s = _silu(c_ref[...]).astype(BF16)
    o_ref[0] = _dot(s, w_ref[0].astype(BF16)) + b_ref[0]

    @pl.when(pl.program_id(1) == 0)
    def _():
        for col in range(0, _IN_W, V7X_LANES):
            if col < _C_LR:
                rows = wint_ref[0, col:col + V7X_LANES, :]
            elif col < _MAIN_W:
                src = col + _C_Q - _C_LR
                rows = wint_ref[0, src:src + V7X_LANES, :]
            else:
                rows = jnp.concatenate([wint_ref[0, _C_LR:_C_Q, :],
                                        jnp.zeros((LR_PAD - 2 * RANK, D_MODEL), F32)], axis=0)
            q_cols = K_W + V_W <= col < 2 * K_W + V_W
            wout_ref[0, :, col:col + V7X_LANES] = (rows * (DK ** -0.5) if q_cols else rows).T.astype(BF16)


def _modulation(cvec, w_mod, b_mod, w_in_t):
    steps = 4
    n = N_MOD * D_MODEL
    tn = n // steps
    assert tn % V7X_LANES == 0
    return pl.pallas_call(
        _mod_kernel,
        grid=(DEPTH, steps),
        in_specs=[pl.BlockSpec((MOD_ROWS, D_MODEL), lambda i, j: (0, 0)),
                  pl.BlockSpec((1, D_MODEL, tn), lambda i, j: (i, 0, j)),
                  pl.BlockSpec((1, 1, tn), lambda i, j: (i, 0, j)),
                  pl.BlockSpec((1, _C_END, D_MODEL), lambda i, j: (i, 0, 0))],
        out_specs=[pl.BlockSpec((1, MOD_ROWS, tn), lambda i, j: (i, 0, j)),
                   pl.BlockSpec((1, D_MODEL, _IN_W), lambda i, j: (i, 0, 0))],
        out_shape=[jax.ShapeDtypeStruct((DEPTH, MOD_ROWS, n), F32),
                   jax.ShapeDtypeStruct((DEPTH, D_MODEL, _IN_W), BF16)],
        compiler_params=_params(("arbitrary", "arbitrary"), "modulation"),
        name="modulation",
    )(cvec, w_mod, b_mod.reshape(DEPTH, 1, n), w_in_t)


def _inproj_kernel(*refs, layer, add_pos, groups, n_cast):
    it = iter(refs)
    x_ref = next(it)
    pos_ref = next(it) if add_pos else None
    g_ref, shift_ref, scale_ref, w_ref, wd_ref, bd_ref = (next(it) for _ in range(6))
    cast_in = [next(it) for _ in range(n_cast)]
    out_refs = list(it)
    for src, dst in zip(cast_in, out_refs[len(out_refs) - n_cast:]):
        dst[0] = src[...].astype(BF16)
    out_refs = out_refs[:len(out_refs) - n_cast]
    lg_hi_ref, lg_lo_ref = out_refs[-2:]
    tm = x_ref.shape[1]
    n_sub = 2 if tm >= TOKEN_TILE else 1
    subs = [slice(i * tm // n_sub, (i + 1) * tm // n_sub) for i in range(n_sub)]
    hs = []
    for r in subs:
        x = x_ref[0, r, :]
        if add_pos:
            x = x + pos_ref[r, :]
        hs.append((_rms(x, _row(g_ref, layer)) * (1.0 + scale_ref[...]) + shift_ref[...]).astype(BF16))
    def project(r, h):
        for o_ref, (_, lo, hi_col) in zip(out_refs, groups):
            o_ref[0, r, :] = _dot(h, w_ref[:, lo:hi_col]).astype(BF16)

    low_rank = [_dot(h, w_ref[:, _MAIN_W:_IN_W]).astype(BF16) for h in hs]
    project(subs[0], hs[0])
    zs = [_dot(lr, wd_ref[...]) + bd_ref[...] for lr in low_rank]
    for r, h in list(zip(subs, hs))[1:]:
        project(r, h)
    for r, z in zip(subs, zs):
        logg = (jnp.minimum(z, 0.0) - jnp.log(1.0 + jnp.exp(-jnp.abs(z)))) * (1.0 / TAU)
        hi = logg.astype(BF16)
        lg_hi_ref[0, r, :] = hi
        lg_lo_ref[0, r, :] = (logg - hi.astype(F32)).astype(BF16)


def _inproj(x, pos, lay, i, mod_row, groups, tm, cast=()):
    b, l, d = x.shape
    add_pos = pos is not None
    n_t = l // tm
    tok = lambda bi, ti: (bi, ti, 0)
    in_specs = [pl.BlockSpec((1, tm, d), tok)]
    args = [x]
    if add_pos:
        in_specs.append(pl.BlockSpec((tm, d), lambda bi, ti: (ti, 0)))
        args.append(pos)
    in_specs += [_resident(lay["g_mix_pre"].shape), _mod_spec(i, 0, mod_row), _mod_spec(i, 1, mod_row),
                 _layer(lay["w_in"], i), _layer(lay["w_dec"], i), _layer(lay["b_dec"], i)]
    args += [lay["g_mix_pre"], lay["mod"], lay["mod"], lay["w_in"], lay["w_dec"], lay["b_dec"]]
    widths = [hi - lo for _, lo, hi in groups] + [2 * K_W, 2 * K_W]
    names = [name for name, _, _ in groups] + ["lg_hi", "lg_lo"]
    out_specs = [pl.BlockSpec((1, tm, w), tok) for w in widths]
    out_shape = [jax.ShapeDtypeStruct((b, l, w), BF16) for w in widths]
    for w in cast:
        _, rows, cols = w.shape
        slab = rows // (b * n_t)
        assert slab * b * n_t == rows and slab % V7X_SUBLANES_BF16 == 0
        in_specs.append(pl.BlockSpec((None, slab, cols), lambda bi, ti: (i, bi * n_t + ti, 0)))
        out_specs.append(pl.BlockSpec((1, slab, cols), lambda bi, ti: (0, bi * n_t + ti, 0)))
        out_shape.append(jax.ShapeDtypeStruct((1, rows, cols), BF16))
        args.append(w)
    outs = pl.pallas_call(
        functools.partial(_inproj_kernel, layer=i, add_pos=add_pos, groups=groups, n_cast=len(cast)),
        grid=(b, n_t),
        in_specs=in_specs,
        out_specs=out_specs,
        out_shape=out_shape,
        compiler_params=_params(("parallel", "parallel"), "inproj"),
        name="inproj",
    )(*args)
    return dict(zip(names, outs)), list(outs[len(names):])


def _gla_kernel(*refs, layer, seq, with_output, with_state):
    it = iter(refs)
    k_ref, v_ref, lg_hi, lg_lo, s0_ref = (next(it) for _ in range(5))
    if with_output:
        q_ref, r_ref, nw_ref = next(it), next(it), next(it)
        y_ref = next(it)
    if with_state:
        s_out_ref = next(it)
    st_ref = next(it)
    ds_scr = next(it)
    dec_scr = next(it)
    if with_output:
        o_scr = next(it)
        qst_scr = next(it)
        gain_scr = next(it)
    n_chunks = seq // CHUNK

    st_ref[...] = s0_ref[0]

    row = lax.broadcasted_iota(jnp.int32, (CHUNK, CHUNK), 0)
    col = lax.broadcasted_iota(jnp.int32, (CHUNK, CHUNK), 1)
    cum = (jnp.where(row >= col, 1.0, 0.0).astype(BF16), jnp.where(row <= col, 1.0, 0.0).astype(BF16))
    lane_head = lax.broadcasted_iota(jnp.int32, (1, K_W), 1) // DK
    head_mask = [lane_head == h for h in range(HEADS)]
    srow = lax.broadcasted_iota(jnp.int32, (CHUNK, HEADS * CHUNK), 0)
    scol = lax.broadcasted_iota(jnp.int32, (CHUNK, HEADS * CHUNK), 1) % CHUNK
    score_mask = (srow >= scol, srow <= scol)
    zero_v = jnp.zeros((CHUNK, DV), BF16)
    last_row = (CHUNK - 1, 0)
    mid_row = (CHUNK // 2 - 1, CHUNK // 2)

    def dec_rows(n):
        return pl.ds(pl.multiple_of(n * V7X_SUBLANES_F32, V7X_SUBLANES_F32), V7X_SUBLANES_F32)

    def stack_heads(a):
        zero = jnp.zeros_like(a)
        return jnp.concatenate([jnp.where(head_mask[h], a, zero) for h in range(HEADS)], axis=0)

    def chunk_rows(n):
        return pl.ds(pl.multiple_of(n * CHUNK, CHUNK), CHUNK)

    per_iter = min(8, n_chunks)

    def chunk_local(i, carry):
        loaded = []
        for u in range(per_iter):
            n = i * per_iter + u
            rows = chunk_rows(n)
            k = k_ref[0, rows, :].astype(F32)
            v = v_ref[0, rows, :]
            v_heads = [v[:, h * DV:(h + 1) * DV] for h in range(HEADS)]
            v_rows = jnp.concatenate(v_heads, axis=0)
            v_diag = jnp.concatenate(
                [jnp.concatenate([v_heads[h] if hh == h else zero_v for hh in range(HEADS)], axis=1)
                 for h in range(HEADS)], axis=0) if with_output else None
            v = (v, v_rows, v_diag)
            qs = q_ref[0, rows, :].astype(F32) if with_output else None
            lg = [(lg_hi[0, rows, d * K_W:(d + 1) * K_W], lg_lo[0, rows, d * K_W:(d + 1) * K_W]) for d in range(2)]
            loaded.append((n, rows, k, v, qs, lg))
        chains = [(n, rows, k, v, qs, lg[dirn], dirn) for n, rows, k, v, qs, lg in loaded for dirn in range(2)]
        stores = []
        if with_output:
            for n, rows, *_ in loaded:
                stores.append((gain_scr, (rows, slice(None)),
                               _silu(r_ref[0, rows, :].astype(F32)) * _row(nw_ref, layer)))
        gs = [_dot(cum[dirn], hi) + _dot(cum[dirn], lo) for *_, (hi, lo), dirn in chains]
        score_list = []
        for (n, rows, k, (v, v_rows, v_diag), qs, _, dirn), g in zip(chains, gs):
            g_last = g[last_row[dirn]:last_row[dirn] + 1]
            stores.append((dec_scr, (dirn, dec_rows(n), slice(None)),
                           jnp.broadcast_to(jnp.exp(g_last), (V7X_SUBLANES_F32, K_W))))
            if with_output:
                g_mid = g[mid_row[dirn]:mid_row[dirn] + 1]
                q_in = qs * jnp.exp(g - g_mid)
                k_in = k * jnp.exp(g_mid - g)
                score_list.append(_dot_nt(q_in.astype(BF16), stack_heads(k_in.astype(BF16))))
                stores.append((qst_scr, (dirn, rows, slice(None)), (q_in * jnp.exp(g_mid)).astype(BF16)))
                k_upd = stack_heads((k_in * jnp.exp(g_last - g_mid)).astype(BF16))
            else:
                k_upd = stack_heads((k * jnp.exp(g_last - g)).astype(BF16))
            stores.append((ds_scr, (dirn, pl.ds(pl.multiple_of(n * DV, DV), DV), slice(None)),
                           _dot_tn(v_rows, k_upd)))
        if with_output:
            for (n, rows, k, (v, v_rows, v_diag), qs, _, dirn), sc in zip(chains, score_list):
                scores = jnp.where(score_mask[dirn], sc, 0.0).astype(BF16)
                stores.append((o_scr, (dirn, rows, slice(None)), _dot(scores, v_diag)))
        for ref, idx, val in stores:
            ref[idx] = val
        return carry

    lax.fori_loop(0, n_chunks // per_iter, chunk_local, 0)

    def scan_step(i, finish):
        steps = ((0, i), (1, n_chunks - 1 - i))
        new_st, inters = [], []
        for dirn, n in steps:
            st = st_ref[dirn]
            if with_output:
                q_st = stack_heads(qst_scr[dirn, chunk_rows(n), :])
                inters.append(_dot_nt(q_st, st.astype(BF16)))
            dec = dec_scr[dirn, dec_rows(n), :]
            new_st.append(dec[0:1] * st + ds_scr[dirn, pl.ds(pl.multiple_of(n * DV, DV), DV), :])
        for (dirn, n), st in zip(steps, new_st):
            st_ref[dirn] = st
        if with_output:
            for (dirn, n), inter in zip(steps, inters):
                rows = chunk_rows(n)
                for h in range(HEADS):
                    vs = slice(h * DV, (h + 1) * DV)
                    o = o_scr[dirn, rows, vs] + inter[h * CHUNK:(h + 1) * CHUNK]
                    if finish:
                        o = o + o_scr[1 - dirn, rows, vs]
                        y_ref[0, rows, vs] = (_rms(o, gain_scr[rows, vs])).astype(BF16)
                    else:
                        o_scr[dirn, rows, vs] = o

    half = n_chunks // 2
    unroll = min(16, half)

    def scan_first(i, carry):
        scan_step(i, False)
        return carry

    def scan_second(i, carry):
        scan_step(i, True)
        return carry

    lax.fori_loop(0, half, scan_first, 0, unroll=unroll)
    lax.fori_loop(half, n_chunks, scan_second, 0, unroll=unroll)

    if with_state:
        s_out_ref[0] = st_ref[...]


def _gla(p, lay, i, s0, with_output, with_state):
    b, l, _ = p["k"].shape
    seq3 = lambda bi: (bi, 0, 0)
    in_specs = [pl.BlockSpec((1, l, K_W), seq3), pl.BlockSpec((1, l, V_W), seq3),
                pl.BlockSpec((1, l, 2 * K_W), seq3), pl.BlockSpec((1, l, 2 * K_W), seq3),
                pl.BlockSpec((1, 2, DV, K_W), lambda bi: (bi, 0, 0, 0))]
    args = [p["k"], p["v"], p["lg_hi"], p["lg_lo"], s0]
    out_specs, out_shape = [], []
    scratch = [pltpu.VMEM((2, DV, K_W), F32),
               pltpu.VMEM((2, l // CHUNK * DV, K_W), F32),
               pltpu.VMEM((2, l // CHUNK * V7X_SUBLANES_F32, K_W), F32)]
    if with_output:
        in_specs += [pl.BlockSpec((1, l, K_W), seq3), pl.BlockSpec((1, l, V_W), seq3),
                     _resident(lay["gla_norm_w"].shape)]
        args += [p["q"], p["r"], lay["gla_norm_w"]]
        out_specs.append(pl.BlockSpec((1, l, V_W), seq3))
        out_shape.append(jax.ShapeDtypeStruct((b, l, V_W), BF16))
        scratch += [pltpu.VMEM((2, l, V_W), F32), pltpu.VMEM((2, l, K_W), BF16), pltpu.VMEM((l, V_W), F32)]
    if with_state:
        out_specs.append(pl.BlockSpec((1, 2, DV, K_W), lambda bi: (bi, 0, 0, 0)))
        out_shape.append(jax.ShapeDtypeStruct((b, 2, DV, K_W), F32))
    outs = pl.pallas_call(
        functools.partial(_gla_kernel, layer=i, seq=l, with_output=with_output, with_state=with_state),
        grid=(b,),
        in_specs=in_specs,
        out_specs=out_specs,
        out_shape=out_shape,
        scratch_shapes=scratch,
        compiler_params=_params(("parallel",), "gla"),
        name="gla",
    )(*args)
    outs = list(outs)
    y = outs.pop(0) if with_output else None
    s = outs.pop(0) if with_state else None
    return y, s


def _fourier_kernel(uf_ref, chan_ref, pos_ref, yf_ref, ab_scr, *, seq):
    ab = _dot(uf_ref[0], chan_ref[...])
    ab_scr[0:seq, :] = ab[:, :FOUR_W].astype(BF16)
    ab_scr[seq:2 * seq, :] = ab[:, FOUR_W:].astype(BF16)
    rb = min(seq, 512)
    for i in range(seq // rb):
        yf_ref[0, i * rb:(i + 1) * rb, :] = _dot(pos_ref[i * rb:(i + 1) * rb, :], ab_scr[...]).astype(BF16)


FFT_RADIX = 8


def _fourier_fft_kernel(uf_ref, chan_ref, sub_ref, twc_ref, tws_ref, yf_ref, uf_scr, *, seq):
    n = seq // FFT_RADIX
    lanes = uf_scr.shape[2]
    n_lane_tiles = FOUR_W // lanes
    for j in range(n_lane_tiles):
        uf_scr[j] = uf_ref[0, :, j * lanes:(j + 1) * lanes].astype(F32)

    def tokens(r):
        parts = [uf_scr[j, pl.ds(r, n, stride=FFT_RADIX), :] for j in range(n_lane_tiles)]
        return jnp.concatenate(parts, axis=1).astype(BF16)

    ab = [_dot(tokens(r), chan_ref[...]) for r in range(FFT_RADIX)]
    pq = []
    for ab_r in ab:
        a, b = ab_r[:, :FOUR_W].astype(BF16), ab_r[:, FOUR_W:].astype(BF16)
        w = jnp.concatenate([jnp.concatenate([a, b], axis=1), jnp.concatenate([-b, a], axis=1)], axis=0)
        pq.append(_dot(sub_ref[...], w))
    re, im = [pq[0][:, :FOUR_W]], [pq[0][:, FOUR_W:]]
    for r in range(1, FFT_RADIX):
        p, q = pq[r][:, :FOUR_W], pq[r][:, FOUR_W:]
        c, s = twc_ref[r - 1], tws_ref[r - 1]
        re.append(c * p - s * q)
        im.append(s * p + c * q)

    def quarter(z0, z1, z2, z3):
        s02 = (z0[0] + z2[0], z0[1] + z2[1])
        d02 = (z0[0] - z2[0], z0[1] - z2[1])
        s13 = (z1[0] + z3[0], z1[1] + z3[1])
        d13 = (z1[0] - z3[0], z1[1] - z3[1])
        return ((s02[0] + s13[0], s02[1] + s13[1]), (d02[0] - d13[1], d02[1] + d13[0]),
                (s02[0] - s13[0], s02[1] - s13[1]), (d02[0] + d13[1], d02[1] - d13[0]))

    zs = list(zip(re, im))
    even = quarter(zs[0], zs[2], zs[4], zs[6])
    odd = quarter(zs[1], zs[3], zs[5], zs[7])
    half = 0.5 ** 0.5
    odd_re = (odd[0][0], (odd[1][0] - odd[1][1]) * half, -odd[2][1], -(odd[3][0] + odd[3][1]) * half)
    for q in range(4):
        yf_ref[0, q * n:(q + 1) * n, :] = (even[q][0] + odd_re[q]).astype(BF16)
        yf_ref[0, (q + 4) * n:(q + 5) * n, :] = (even[q][0] - odd_re[q]).astype(BF16)


def _fourier(uf, tabs):
    b, l, _ = uf.shape
    tok = pl.BlockSpec((1, l, FOUR_W), lambda bi: (bi, 0, 0))
    if len(tabs) == 4:
        body, name = _fourier_fft_kernel, "fourier_fft"
        scratch = pltpu.VMEM((FOUR_W // V7X_LANES, l, V7X_LANES), F32)
    else:
        body, name = _fourier_kernel, "fourier"
        scratch = pltpu.VMEM((2 * l, FOUR_W), BF16)
    return pl.pallas_call(
        functools.partial(body, seq=l),
        grid=(b,),
        in_specs=[tok] + [_resident(t.shape) for t in tabs],
        out_specs=tok,
        out_shape=jax.ShapeDtypeStruct((b, l, FOUR_W), BF16),
        scratch_shapes=[scratch],
        compiler_params=_params(("parallel",), "fourier"),
        name=name,
    )(uf, *tabs)


CONV_PAD = V7X_SUBLANES_F32
HALO_ROWS = V7X_SUBLANES_BF16


def _out_ffn_kernel(*refs, layer, add_pos, n_sub):
    it = iter(refs)
    yg_ref, yf_ref, bg_ref, cg_ref, uc_ref = (next(it) for _ in range(5))
    cg_prev_ref, uc_prev_ref, cg_next_ref, uc_next_ref, cw_ref, cb_ref = (next(it) for _ in range(6))
    wo_ref, x_ref = next(it), next(it)
    pos_ref = next(it) if add_pos else None
    gmix_ref, gatemix_ref, gpre_ref, shift_ref, scale_ref, gate_ref, gpost_ref = (next(it) for _ in range(7))
    win_ref, wout_ref, o_ref, z_scr = next(it), next(it), next(it), next(it)
    tm = x_ref.shape[1]
    subs = [slice(i * tm // n_sub, (i + 1) * tm // n_sub) for i in range(n_sub)]

    t, nt = pl.program_id(1), pl.num_programs(1)
    z_before = (cg_prev_ref[0].astype(F32) * uc_prev_ref[0].astype(F32))[HALO_ROWS - CONV_PAD:]
    z_after = (cg_next_ref[0].astype(F32) * uc_next_ref[0].astype(F32))[:CONV_PAD]
    z_scr[0:CONV_PAD, :] = jnp.where(t > 0, z_before, 0.0)
    z_scr[CONV_PAD + tm:, :] = jnp.where(t < nt - 1, z_after, 0.0)
    z_scr[CONV_PAD:CONV_PAD + tm, :] = cg_ref[0].astype(F32) * uc_ref[0].astype(F32)
    ycs = []
    for r in subs:
        s, n = CONV_PAD + r.start, r.stop - r.start
        y = (cw_ref[0:1] * z_scr[s - 1:s - 1 + n, :] + cw_ref[1:2] * z_scr[s:s + n, :]
             + cw_ref[2:3] * z_scr[s + 1:s + 1 + n, :] + _row(cb_ref, layer))
        ycs.append((bg_ref[0, r, :].astype(F32) * y).astype(BF16))

    ys = [(_dot(yg_ref[0, r, :], wo_ref[0:V_W]) + _dot(yf_ref[0, r, :], wo_ref[V_W:V_W + FOUR_W])
           + _dot(yc, wo_ref[V_W + FOUR_W:])) for r, yc in zip(subs, ycs)]
    xs, hs = [], []
    for r, y in zip(subs, ys):
        x = x_ref[0, r, :]
        if add_pos:
            x = x + pos_ref[r, :]
        x = x + gatemix_ref[...] * _rms(y, _row(gmix_ref, layer))
        xs.append(x)
        hs.append((_rms(x, _row(gpre_ref, layer)) * (1.0 + scale_ref[...]) + shift_ref[...]).astype(BF16))
    ys = []
    for h in hs:
        y = jnp.zeros((h.shape[0], D_MODEL), F32)
        for lo, hi in FFN_TILES:
            a = _dot(h, win_ref[:, lo:hi])
            u = _dot(h, win_ref[:, HIDDEN + lo:HIDDEN + hi])
            y = y + _dot((_silu(a) * u).astype(BF16), wout_ref[lo:hi, :])
        ys.append(y)
    for r, x, y in zip(subs, xs, ys):
        o_ref[0, r, :] = x + gate_ref[...] * _rms(y, _row(gpost_ref, layer))


def _out_ffn(yg, yf, p, x, pos, lay, i, wts, mod_row, tm):
    w_out, w_ffn_in, w_ffn_out = wts
    b, l, d = x.shape
    add_pos = pos is not None
    tok = lambda bi, ti: (bi, ti, 0)
    halo_per_tile, n_halo = tm // HALO_ROWS, l // HALO_ROWS
    before = lambda bi, ti: (bi, jnp.maximum(ti * halo_per_tile - 1, 0), 0)
    after = lambda bi, ti: (bi, jnp.minimum((ti + 1) * halo_per_tile, n_halo - 1), 0)
    conv_tok = pl.BlockSpec((1, tm, CONV_W), tok)
    in_specs = [pl.BlockSpec((1, tm, V_W), tok), pl.BlockSpec((1, tm, FOUR_W), tok), conv_tok, conv_tok, conv_tok,
                pl.BlockSpec((1, HALO_ROWS, CONV_W), before), pl.BlockSpec((1, HALO_ROWS, CONV_W), before),
                pl.BlockSpec((1, HALO_ROWS, CONV_W), after), pl.BlockSpec((1, HALO_ROWS, CONV_W), after),
                _layer(lay["conv_w"], i), _resident(lay["conv_b"].shape),
                _layer(w_out, 0), pl.BlockSpec((1, tm, d), tok)]
    args = [yg, yf, p["bg"], p["cg"], p["uc"], p["cg"], p["uc"], p["cg"], p["uc"],
            lay["conv_w"], lay["conv_b"], w_out, x]
    if add_pos:
        in_specs.append(pl.BlockSpec((tm, d), lambda bi, ti: (ti, 0)))
        args.append(pos)
    in_specs += [_resident(lay["g_mix_post"].shape), _mod_spec(i, 2, mod_row), _resident(lay["g_ffn_pre"].shape),
                 _mod_spec(i, 3, mod_row), _mod_spec(i, 4, mod_row), _mod_spec(i, 5, mod_row),
                 _resident(lay["g_ffn_post"].shape), _layer(w_ffn_in, 0), _layer(w_ffn_out, 0)]
    args += [lay["g_mix_post"], lay["mod"], lay["g_ffn_pre"], lay["mod"], lay["mod"], lay["mod"],
             lay["g_ffn_post"], w_ffn_in, w_ffn_out]
    return pl.pallas_call(
        functools.partial(_out_ffn_kernel, layer=i, add_pos=add_pos, n_sub=2 if tm >= TOKEN_TILE else 1),
        grid=(b, l // tm),
        in_specs=in_specs,
        out_specs=pl.BlockSpec((1, tm, d), tok),
        out_shape=jax.ShapeDtypeStruct((b, l, d), F32),
        scratch_shapes=[pltpu.VMEM((tm + 2 * CONV_PAD, CONV_W), F32)],
        compiler_params=_params(("parallel", "parallel"), "out_ffn"),
        name="out_ffn",
    )(*args)


def _arrange_w_dec(w_f, w_b):
    wd = jnp.zeros((DEPTH, LR_PAD, 2 * K_W), F32)
    wd = wd.at[:, 0:RANK, :K_W].set(w_f).at[:, RANK:2 * RANK, K_W:].set(w_b)
    return wd.astype(BF16)


def kernel(x, c, ctx, c_ctx, w_mod, b_mod, g_mix_pre, g_mix_post, g_ffn_pre, g_ffn_post,
           w_in, w_dec_f, b_dec_f, w_dec_b, b_dec_b, gla_norm_w, conv_w, conv_b, w_out,
           w_ffn_in, w_ffn_out):
    b, n_lat, d = x.shape
    n_ctx = ctx.shape[1]
    assert d == D_MODEL and w_in.shape == (DEPTH, D_MODEL, _C_END)
    assert n_lat % (2 * TOKEN_TILE) == 0 and n_lat % (FFT_RADIX * V7X_SUBLANES_BF16) == 0
    assert n_ctx % CHUNK == 0 and n_ctx <= TOKEN_TILE and b + 1 <= MOD_ROWS

    pos = jnp.asarray(_pos_embed(n_lat, d))
    tabs = {n_lat: _dft_tables(n_lat, FFT_RADIX), n_ctx: _dft_tables(n_ctx, 1)}

    cvec = jnp.zeros((MOD_ROWS, d), F32).at[:b].set(c).at[b].set(c_ctx)
    ctx_row = b
    mod, w_in_arranged = _modulation(cvec, w_mod, b_mod, jnp.swapaxes(w_in, 1, 2))
    lay = {
        "mod": mod.reshape(DEPTH, MOD_ROWS, 1, N_MOD * d),
        "w_in": w_in_arranged,
        "w_dec": _arrange_w_dec(w_dec_f, w_dec_b),
        "b_dec": jnp.concatenate([b_dec_f, b_dec_b], axis=-1).reshape(DEPTH, 1, 2 * K_W),
        "gla_norm_w": gla_norm_w,
        "conv_w": conv_w,
        "conv_b": conv_b,
        "g_mix_pre": g_mix_pre,
        "g_mix_post": g_mix_post,
        "g_ffn_pre": g_ffn_pre,
        "g_ffn_post": g_ffn_post,
    }

    def mix_and_ffn(i, xs, pos_s, p, wts, mod_row, s0, want_state, tm):
        yg, s = _gla(p, lay, i, s0, True, want_state)
        yf = _fourier(p["uf"], tabs[xs.shape[1]])
        return _out_ffn(yg, yf, p, xs, pos_s, lay, i, wts, mod_row, tm), s

    zero_state = jnp.zeros((b, 2, DV, K_W), F32)
    xc = ctx
    tm = TOKEN_TILE
    for i in range(DEPTH):
        pos_i = pos if i == 0 else None
        p_lat, wts = _inproj(x, pos_i, lay, i, None, _OUT_GROUPS_FULL, 2 * tm, cast=(w_out, w_ffn_in, w_ffn_out))
        if i == DEPTH - 1:
            p_ctx, _ = _inproj(xc, None, lay, i, ctx_row, _OUT_GROUPS_STATE, n_ctx)
            _, s = _gla(p_ctx, lay, i, zero_state, False, True)
        else:
            p_ctx, _ = _inproj(xc, None, lay, i, ctx_row, _OUT_GROUPS_FULL, n_ctx)
            xc, s = mix_and_ffn(i, xc, None, p_ctx, wts, ctx_row, zero_state, True, n_ctx)
        x, _ = mix_and_ffn(i, x, pos_i, p_lat, wts, None, s, False, tm)
    return x
```
